```python
import jax, jax.numpy as jnp
from jax import lax
import numpy as np

D_MODEL = 1024
BATCH = 8
SEQ = 2048
DEPTH = 2
DEC_BATCH = 128
DEC_SEQ = 1
PAST_LEN = 16384
PAGE_SIZE = 128

D_MIX = D_MODEL
POOL_WIDTH = D_MIX // 4
POOL_WINDOWS = (2, 4, 8, 16)
POOL_GROUPS = len(POOL_WINDOWS)
POOL_GROUP_DIM = POOL_WIDTH // POOL_GROUPS
POOL_STATE = max(POOL_WINDOWS) - 1
MLSTM_WIDTH = D_MIX // 2
MLSTM_HEADS = 4
MLSTM_HEAD_DIM = MLSTM_WIDTH // MLSTM_HEADS
MLSTM_CHUNK = 128
GMLP_WIDTH = D_MIX // 4
GMLP_GROUPS = 4
GMLP_GROUP_DIM = GMLP_WIDTH // GMLP_GROUPS
GMLP_CHUNK = 128
D_IN = POOL_WIDTH + 4 * MLSTM_WIDTH + 2 * MLSTM_HEADS + 2 * GMLP_WIDTH
D_FF = 2816
N_EXPERTS = 8
TOP_K = 2
D_FF_EXPERT = 1408
PLE_DIM = 256
N_DENSE = (DEPTH + 1) // 2
N_MOE = DEPTH // 2
RMS_EPS = 1e-6

kernel_name = "hybrid_pool_mlstm_gmlp_decoder_step"


def _rms_norm(x, w):
    xf = x.astype(jnp.float32)
    y = xf * lax.rsqrt(jnp.mean(xf * xf, axis=-1, keepdims=True) + RMS_EPS)
    return (y * w.astype(jnp.float32)).astype(x.dtype)


def _pool_mixer(u_prev, u, start_pos, pool_w, pool_scale):
    B, L, _ = u.shape
    n_prev = u_prev.shape[1]
    ext = jnp.concatenate([u_prev.astype(u.dtype), u], axis=1)
    L_ext = ext.shape[1]
    pos = start_pos - n_prev + jnp.arange(L_ext)
    ext_f = ext.astype(jnp.float32)
    cs = jnp.cumsum(ext_f, axis=1)
    outs = []
    for g, w in enumerate(POOL_WINDOWS):
        sl = slice(g * POOL_GROUP_DIM, (g + 1) * POOL_GROUP_DIM)
        csg = cs[..., sl]
        shifted = jnp.pad(csg, ((0, 0), (w, 0), (0, 0)))[:, :L_ext]
        cnt = jnp.minimum(w, pos + 1).astype(jnp.float32)
        outs.append((csg - shifted) / cnt[None, :, None] - ext_f[..., sl])
    pooled = jnp.stack(outs, axis=2)[:, n_prev:]
    y = jnp.einsum('blgc,gcd->blgd', pooled, pool_w.astype(jnp.float32)).reshape(B, L, POOL_WIDTH)
    y = y * pool_scale.astype(jnp.float32)
    return y.astype(u.dtype), ext[:, -POOL_STATE:]


def _mlstm_chunk(carry, xs):
    C, n, m = carry
    q, k, v, ig, lf = xs
    c = q.shape[1]
    b = jnp.cumsum(lf, axis=1)
    causal = jnp.tril(jnp.ones((c, c), dtype=bool))
    dmat = b[:, :, None, :] - b[:, None, :, :] + ig[:, None, :, :]
    dmat = jnp.where(causal[None, :, :, None], dmat, -jnp.inf)
    g = b + m[:, None, :]
    m_t = jnp.maximum(g, jnp.max(dmat, axis=2))
    wts = jnp.exp(dmat - m_t[:, :, None, :]) * jnp.einsum('bthd,bshd->btsh', q, k)
    inter = jnp.exp(g - m_t)
    num = inter[..., None] * jnp.einsum('bthk,bhkv->bthv', q, C) + jnp.einsum('btsh,bshv->bthv', wts, v)
    den = inter * jnp.einsum('bthk,bhk->bth', q, n) + jnp.sum(wts, axis=2)
    h = num / jnp.maximum(jnp.abs(den), jnp.exp(-m_t))[..., None]
    m_new = m_t[:, -1]
    decay = jnp.exp(b[:, -1] + m - m_new)
    w_s = jnp.exp(b[:, -1:] - b + ig - m_new[:, None])
    C_new = decay[..., None, None] * C + jnp.einsum('bsh,bshk,bshv->bhkv', w_s, k, v)
    n_new = decay[..., None] * n + jnp.einsum('bsh,bshk->bhk', w_s, k)
    return (C_new, n_new, m_new), h


def _mlstm(q, k, v, ig, lf, C, n, m):
    B, L, H, Dh = q.shape
    c = MLSTM_CHUNK if L % MLSTM_CHUNK == 0 else L
    nc = L // c

    def to_chunks(a):
        return jnp.moveaxis(a.reshape((B, nc, c) + a.shape[2:]), 1, 0)

    (C, n, m), h = lax.scan(_mlstm_chunk, (C, n, m), tuple(to_chunks(a) for a in (q, k, v, ig, lf)))
    h = jnp.moveaxis(h, 0, 1).reshape(B, L, H, Dh)
    return h, C, n, m


def _gmlp(u, v, norm_w, ws, bs):
    B, L, _ = u.shape
    vn = _rms_norm(v.reshape(B, L, GMLP_GROUPS, GMLP_GROUP_DIM), norm_w.reshape(GMLP_GROUPS, GMLP_GROUP_DIM))
    c = min(GMLP_CHUNK, L)
    vc = vn.reshape(B, L // c, c, GMLP_GROUPS, GMLP_GROUP_DIM)
    wm = jnp.tril(ws[:, :c, :c])
    mixed = jnp.einsum('gts,bnsgd->bntgd', wm, vc) + bs[:, :c].T[None, None, :, :, None]
    y = u.reshape(B, L // c, c, GMLP_GROUPS, GMLP_GROUP_DIM) * mixed
    return y.reshape(B, L, GMLP_WIDTH).astype(u.dtype), vn.reshape(B, L, GMLP_WIDTH)


def _swiglu(h, wg, wu, wd):
    return (jax.nn.silu(h @ wg) * (h @ wu)) @ wd


def _moe(h, router_w, router_b, wg, wu, wd):
    logits = (h @ router_w).astype(jnp.float32) + router_b.astype(jnp.float32)
    probs = jax.nn.softmax(logits, axis=-1)
    top_p, top_i = lax.top_k(probs, TOP_K)
    top_p = top_p / jnp.sum(top_p, axis=-1, keepdims=True)
    gates = jnp.sum(jax.nn.one_hot(top_i, N_EXPERTS, dtype=jnp.float32) * top_p[..., None], axis=-2)
    y = jnp.zeros(h.shape, jnp.float32)
    for e in range(N_EXPERTS):
        y = y + gates[..., e:e + 1] * _swiglu(h, wg[e], wu[e], wd[e]).astype(jnp.float32)
    return y.astype(h.dtype)


def _mixer(h, i, W, pool_prev, C, n, m, start_pos):
    B, L, _ = h.shape
    z = h @ W['w_in'][i]
    sizes = [POOL_WIDTH, MLSTM_WIDTH, MLSTM_WIDTH, MLSTM_WIDTH, MLSTM_WIDTH,
             MLSTM_HEADS, MLSTM_HEADS, GMLP_WIDTH, GMLP_WIDTH]
    cuts = [int(s) for s in np.cumsum(sizes)[:-1]]
    u_pool, q, k, v, o, ig, fg, gu, gv = jnp.split(z, cuts, axis=-1)
    y_pool, pool_new = _pool_mixer(pool_prev, u_pool, start_pos, W['pool_w'][i], W['pool_scale'][i])
    f32 = jnp.float32
    shp = (B, L, MLSTM_HEADS, MLSTM_HEAD_DIM)
    qh = q.astype(f32).reshape(shp)
    kh = k.astype(f32).reshape(shp) * (MLSTM_HEAD_DIM ** -0.5)
    vh = v.astype(f32).reshape(shp)
    ig_t = ig.astype(f32) + W['mlstm_b_i'][i].astype(f32)
    lf_t = jax.nn.log_sigmoid(fg.astype(f32) + W['mlstm_b_f'][i].astype(f32))
    hm, C_new, n_new, m_new = _mlstm(qh, kh, vh, ig_t, lf_t, C.astype(f32), n.astype(f32), m.astype(f32))
    hm = _rms_norm(hm, W['mlstm_norm_w'][i].reshape(MLSTM_HEADS, MLSTM_HEAD_DIM)).reshape(B, L, MLSTM_WIDTH)
    y_m = (jax.nn.sigmoid(o.astype(f32)) * hm).astype(h.dtype)
    y_g, gv_rows = _gmlp(gu, gv, W['gmlp_norm_w'][i], W['gmlp_ws'][i], W['gmlp_bs'][i])
    out = jnp.concatenate([y_pool, y_m, y_g], axis=-1) @ W['w_out'][i]
    return out, pool_new, C_new, n_new, m_new, gv_rows


def _trunk(x, p, pool_prev, C0, n0, m0, start_pos, W):
    pools, Cs, ns, ms, gvs = [], [], [], [], []
    for i in range(DEPTH):
        h = _rms_norm(x, W['norm_mix_pre'][i])
        mix, pool_new, C_new, n_new, m_new, gv_rows = _mixer(h, i, W, pool_prev[i], C0[i], n0[i], m0[i], start_pos)
        x = x + _rms_norm(mix, W['norm_mix_post'][i])
        h = _rms_norm(x, W['norm_ffn_pre'][i])
        if i % 2 == 0:
            j = i // 2
            f = _swiglu(h, W['ffn_w_gate'][j], W['ffn_w_up'][j], W['ffn_w_down'][j])
        else:
            j = i // 2
            f = _moe(h, W['moe_router_w'][j], W['moe_router_b'][j], W['moe_w_gate'][j], W['moe_w_up'][j], W['moe_w_down'][j])
        x = x + _rms_norm(f, W['norm_ffn_post'][i])
        gate = jax.nn.sigmoid(_rms_norm(x, W['norm_ple'][i]) @ W['ple_w_gate'][i])
        x = x + gate * (p[i].astype(x.dtype) @ W['ple_w_proj'][i])
        pools.append(pool_new); Cs.append(C_new); ns.append(n_new); ms.append(m_new); gvs.append(gv_rows)
    return x, jnp.stack(pools), jnp.stack(Cs), jnp.stack(ns), jnp.stack(ms), jnp.stack(gvs)


def setup_inputs(seed: int = 0) -> dict:
    key = jax.random.key(seed)
    ks = iter(jax.random.split(key, 64))
    nrm = lambda shape, s=1.0: jax.random.normal(next(ks), shape, jnp.float32) * s
    gain = lambda shape: 1.0 + nrm(shape, 0.1)
    H, Dh = MLSTM_HEADS, MLSTM_HEAD_DIM
    return {
        'x_prompt': nrm((BATCH, SEQ, D_MODEL)),
        'x_sample': nrm((DEC_BATCH, DEC_SEQ, D_MODEL)),
        'state_pool': nrm((DEPTH, DEC_BATCH, POOL_STATE, POOL_WIDTH)),
        'state_mlstm_C': nrm((DEPTH, DEC_BATCH, H, Dh, Dh), 0.1),
        'state_mlstm_n': nrm((DEPTH, DEC_BATCH, H, Dh), 0.3),
        'state_mlstm_m': nrm((DEPTH, DEC_BATCH, H), 0.5),
        'p_prompt': nrm((DEPTH, BATCH, SEQ, PLE_DIM)),
        'p_sample': nrm((DEPTH, DEC_BATCH, DEC_SEQ, PLE_DIM)),
        'norm_mix_pre': gain((DEPTH, D_MODEL)),
        'norm_mix_post': gain((DEPTH, D_MODEL)),
        'norm_ffn_pre': gain((DEPTH, D_MODEL)),
        'norm_ffn_post': gain((DEPTH, D_MODEL)),
        'norm_ple': gain((DEPTH, D_MODEL)),
        'w_in': nrm((DEPTH, D_MODEL, D_IN), D_MODEL ** -0.5),
        'pool_w': nrm((DEPTH, POOL_GROUPS, POOL_GROUP_DIM, POOL_GROUP_DIM), POOL_GROUP_DIM ** -0.5),
        'pool_scale': gain((DEPTH, POOL_WIDTH)),
        'mlstm_b_i': nrm((DEPTH, H), 0.1),
        'mlstm_b_f': jnp.linspace(3.0, 6.0, H, dtype=jnp.float32)[None, :] + nrm((DEPTH, H), 0.1),
        'mlstm_norm_w': gain((DEPTH, MLSTM_WIDTH)),
        'gmlp_norm_w': gain((DEPTH, GMLP_WIDTH)),
        'gmlp_ws': nrm((DEPTH, GMLP_GROUPS, GMLP_CHUNK, GMLP_CHUNK), 0.5 * GMLP_CHUNK ** -0.5),
        'gmlp_bs': 1.0 + nrm((DEPTH, GMLP_GROUPS, GMLP_CHUNK), 0.1),
        'w_out': nrm((DEPTH, D_MIX, D_MODEL), D_MIX ** -0.5),
        'ffn_w_gate': nrm((N_DENSE, D_MODEL, D_FF), D_MODEL ** -0.5),
        'ffn_w_up': nrm((N_DENSE, D_MODEL, D_FF), D_MODEL ** -0.5),
        'ffn_w_down': nrm((N_DENSE, D_FF, D_MODEL), D_FF ** -0.5),
        'moe_router_w': nrm((N_MOE, D_MODEL, N_EXPERTS), D_MODEL ** -0.5),
        'moe_router_b': nrm((N_MOE, N_EXPERTS), 0.01),
        'moe_w_gate': nrm((N_MOE, N_EXPERTS, D_MODEL, D_FF_EXPERT), D_MODEL ** -0.5),
        'moe_w_up': nrm((N_MOE, N_EXPERTS, D_MODEL, D_FF_EXPERT), D_MODEL ** -0.5),
        'moe_w_down': nrm((N_MOE, N_EXPERTS, D_FF_EXPERT, D_MODEL), D_FF_EXPERT ** -0.5),
        'ple_w_gate': nrm((DEPTH, D_MODEL, D_MODEL), D_MODEL ** -0.5),
        'ple_w_proj': nrm((DEPTH, PLE_DIM, D_MODEL), PLE_DIM ** -0.5),
    }


def reference(x_prompt, x_sample, state_pool, state_mlstm_C, state_mlstm_n, state_mlstm_m, p_prompt, p_sample,
              norm_mix_pre, norm_mix_post, norm_ffn_pre, norm_ffn_post, norm_ple, w_in, pool_w, pool_scale,
              mlstm_b_i, mlstm_b_f, mlstm_norm_w, gmlp_norm_w, gmlp_ws, gmlp_bs, w_out,
              ffn_w_gate, ffn_w_up, ffn_w_down, moe_router_w, moe_router_b, moe_w_gate, moe_w_up, moe_w_down,
              ple_w_gate, ple_w_proj):
    W = dict(norm_mix_pre=norm_mix_pre, norm_mix_post=norm_mix_post, norm_ffn_pre=norm_ffn_pre,
             norm_ffn_post=norm_ffn_post, norm_ple=norm_ple, w_in=w_in, pool_w=pool_w, pool_scale=pool_scale,
             mlstm_b_i=mlstm_b_i, mlstm_b_f=mlstm_b_f, mlstm_norm_w=mlstm_norm_w, gmlp_norm_w=gmlp_norm_w,
             gmlp_ws=gmlp_ws, gmlp_bs=gmlp_bs, w_out=w_out, ffn_w_gate=ffn_w_gate, ffn_w_up=ffn_w_up,
             ffn_w_down=ffn_w_down, moe_router_w=moe_router_w, moe_router_b=moe_router_b,
             moe_w_gate=moe_w_gate, moe_w_up=moe_w_up, moe_w_down=moe_w_down,
             ple_w_gate=ple_w_gate, ple_w_proj=ple_w_proj)
    H, Dh = MLSTM_HEADS, MLSTM_HEAD_DIM
    B = x_prompt.shape[0]
    pool0 = jnp.zeros((DEPTH, B, 0, POOL_WIDTH), x_prompt.dtype)
    C0 = jnp.zeros((DEPTH, B, H, Dh, Dh), jnp.float32)
    n0 = jnp.zeros((DEPTH, B, H, Dh), jnp.float32)
    m0 = jnp.zeros((DEPTH, B, H), jnp.float32)
    y_prompt, pool_p, C_p, n_p, m_p, _gv_p = _trunk(x_prompt, p_prompt, pool0, C0, n0, m0, 0, W)
    y_sample, pool_s, C_s, n_s, m_s, gv_s = _trunk(x_sample, p_sample, state_pool, state_mlstm_C,
                                                  state_mlstm_n, state_mlstm_m, PAST_LEN, W)
    return (y_prompt, y_sample, pool_p, C_p, n_p, m_p, pool_s, C_s, n_s, m_s, gv_s)
```

```python
import functools

import jax
import jax.numpy as jnp
from jax import lax
from jax.experimental import pallas as pl
from jax.experimental.pallas import tpu as pltpu

F32 = jnp.float32
BF16 = jnp.bfloat16

D_MODEL = 1024
DEPTH = 2
POOL_WIDTH = 256
POOL_WINDOWS = (2, 4, 8, 16)
POOL_GROUP_DIM = 64
POOL_STATE = 15
MLSTM_WIDTH = 512
MLSTM_HEADS = 4
MLSTM_HEAD_DIM = 128
CHUNK = 128
GMLP_WIDTH = 256
GMLP_GROUPS = 4
GMLP_GROUP_DIM = 64
D_FF = 2816
N_EXPERTS = 8
D_FF_EXPERT = 1408
PLE_DIM = 256
RMS_EPS = 1e-6
PAST_LEN = 16384

LANES = 128
SUBLANES = 8
VMEM_LIMIT = 48 * 1024 * 1024

Z_POOL = 0
Z_Q = 256
Z_K = 768
Z_V = 1280
Z_O = 1792
Z_GU = 2304
Z_GV = 2560
Z_IG = 2816
Z_FG = 2944
Z_WIDTH = 3072
Z_CHUNK = 512

TM_PROMPT = 512
TM_FFN = 1024
TF_DENSE = 256
TM_MOE = 512
FF_EXPERT_CHUNKS = ((0, 512), (512, 512), (1024, 384))
SAMPLE_BLOCK = 8


def _params(*semantics):
    return pltpu.CompilerParams(dimension_semantics=semantics, vmem_limit_bytes=VMEM_LIMIT)


def _rms(x, w):
    return x * lax.rsqrt(jnp.mean(x * x, axis=-1, keepdims=True) + RMS_EPS) * w


def _log_sigmoid(x):
    return jnp.minimum(x, 0.0) - jnp.log1p(jnp.exp(-jnp.abs(x)))


def _dot(a, b):
    return jnp.dot(a, b, preferred_element_type=F32)


def _dot_f32(a, b):
    return jnp.dot(a, b, preferred_element_type=F32, precision=lax.Precision.HIGHEST)


def _norm_matmul_kernel(x_ref, nw_ref, w_ref, o_ref, h_ref):
    h_ref[...] = _rms(x_ref[...], nw_ref[...]).astype(BF16)
    for n0 in range(0, Z_WIDTH, Z_CHUNK):
        o_ref[:, n0:n0 + Z_CHUNK] = _dot(h_ref[...], w_ref[:, n0:n0 + Z_CHUNK])


def _norm_matmul(x, nw, w, tm):
    t = x.shape[0]
    return pl.pallas_call(
        _norm_matmul_kernel,
        grid=(t // tm,),
        in_specs=[pl.BlockSpec((tm, D_MODEL), lambda i: (i, 0)),
                  pl.BlockSpec((1, D_MODEL), lambda i: (0, 0)),
                  pl.BlockSpec((D_MODEL, Z_WIDTH), lambda i: (0, 0))],
        out_specs=pl.BlockSpec((tm, Z_WIDTH), lambda i: (i, 0)),
        out_shape=jax.ShapeDtypeStruct((t, Z_WIDTH), F32),
        scratch_shapes=[pltpu.VMEM((tm, D_MODEL), BF16)],
        compiler_params=_params("parallel"),
        name="norm_in_proj",
    )(x, nw, w)


def _group_rms(v, gmean, w):
    ms = _dot_f32(v * v, gmean)
    return v * lax.rsqrt(ms + RMS_EPS) * w


def _pool_tile(ext_ref, u_tile, col0, w_lo, w_hi, pos):
    acc = u_tile
    sums = {}
    for shift in range(1, w_hi):
        acc = acc + ext_ref[pl.ds(16 - shift, CHUNK), col0:col0 + LANES]
        if shift + 1 in (w_lo, w_hi):
            sums[shift + 1] = acc
    cnt_lo = jnp.minimum(w_lo, pos + 1).astype(F32)
    cnt_hi = jnp.minimum(w_hi, pos + 1).astype(F32)
    lane = lax.broadcasted_iota(jnp.int32, (CHUNK, LANES), 1)
    return jnp.where(lane < POOL_GROUP_DIM, sums[w_lo] / cnt_lo, sums[w_hi] / cnt_hi) - u_tile


def _mixer_prompt_kernel(z_ref, poolw_ref, pscale_ref, bi_ref, bf_ref, mnorm_ref, gnorm_ref,
                         gws_ref, gbs_ref, gmean_ref,
                         y_ref, c_ref, n_ref, m_ref, ext_ref):
    chunk = pl.program_id(1)

    @pl.when(chunk == 0)
    def _():
        ext_ref[0:16, :] = jnp.zeros((16, POOL_WIDTH), F32)
        c_ref[...] = jnp.zeros(c_ref.shape, F32)
        n_ref[...] = jnp.zeros(n_ref.shape, F32)
        m_ref[...] = jnp.zeros(m_ref.shape, F32)

    row = lax.broadcasted_iota(jnp.int32, (CHUNK, CHUNK), 0)
    col = lax.broadcasted_iota(jnp.int32, (CHUNK, CHUNK), 1)
    causal = col <= row
    lane = col

    ext_ref[16:16 + CHUNK, :] = z_ref[:, Z_POOL:Z_POOL + POOL_WIDTH]
    pos = chunk * CHUNK + lax.broadcasted_iota(jnp.int32, (CHUNK, 1), 0)
    pooled = []
    for tile in range(2):
        col0 = tile * LANES
        u_tile = z_ref[:, Z_POOL + col0:Z_POOL + col0 + LANES]
        pooled.append(_pool_tile(ext_ref, u_tile, col0, POOL_WINDOWS[2 * tile],
                                 POOL_WINDOWS[2 * tile + 1], pos))
    pooled = jnp.concatenate(pooled, axis=1).astype(BF16)
    y_pool = _dot(pooled, poolw_ref[...]) * pscale_ref[...]
    y_ref[:, 0:POOL_WIDTH] = y_pool.astype(BF16)
    ext_ref[0:16, :] = ext_ref[CHUNK:CHUNK + 16, :]

    vn = _group_rms(z_ref[:, Z_GV:Z_GV + GMLP_WIDTH], gmean_ref[...], gnorm_ref[...]).astype(BF16)
    for tile in range(2):
        col0 = tile * LANES
        vt = vn[:, col0:col0 + LANES]
        w_a = jnp.where(causal, gws_ref[2 * tile], 0.0).astype(BF16)
        w_b = jnp.where(causal, gws_ref[2 * tile + 1], 0.0).astype(BF16)
        mixed = jnp.where(lane < GMLP_GROUP_DIM, _dot(w_a, vt), _dot(w_b, vt))
        gu = z_ref[:, Z_GU + col0:Z_GU + col0 + LANES]
        y_g = gu * (mixed + gbs_ref[:, col0:col0 + LANES])
        y_ref[:, 768 + col0:768 + col0 + LANES] = y_g.astype(BF16)

    ig = z_ref[:, Z_IG:Z_IG + LANES] + bi_ref[...]
    lf = _log_sigmoid(z_ref[:, Z_FG:Z_FG + LANES] + bf_ref[...])
    tri = jnp.where(causal, 1.0, 0.0).astype(F32)
    b = _dot_f32(tri, lf)
    m_prev = m_ref[...]
    g = b + m_prev
    r_t = jnp.transpose(ig - b)
    b_last = b[CHUNK - 1:CHUNK, :]
    m_new_row = m_prev
    for h in range(MLSTM_HEADS):
        c0 = h * MLSTM_HEAD_DIM
        q = z_ref[:, Z_Q + c0:Z_Q + c0 + MLSTM_HEAD_DIM]
        k = z_ref[:, Z_K + c0:Z_K + c0 + MLSTM_HEAD_DIM] * (MLSTM_HEAD_DIM ** -0.5)
        v = z_ref[:, Z_V + c0:Z_V + c0 + MLSTM_HEAD_DIM].astype(BF16)
        o = z_ref[:, Z_O + c0:Z_O + c0 + MLSTM_HEAD_DIM]
        qb = q.astype(BF16)
        b_col = b[:, h:h + 1]
        dmat = jnp.where(causal, b_col + r_t[h:h + 1, :], -jnp.inf)
        g_col = g[:, h:h + 1]
        m_t = jnp.maximum(g_col, jnp.max(dmat, axis=1, keepdims=True))
        scores = lax.dot_general(qb, k.astype(BF16), (((1,), (1,)), ((), ())),
                                 preferred_element_type=F32)
        wts = jnp.exp(dmat - m_t) * scores
        inter = jnp.exp(g_col - m_t)
        c_h = c_ref[h]
        n_h = n_ref[h:h + 1, :]
        num = inter * _dot(qb, c_h.astype(BF16)) + _dot(wts.astype(BF16), v)
        den = inter * jnp.sum(q * n_h, axis=1, keepdims=True) + jnp.sum(wts, axis=1, keepdims=True)
        hid = num / jnp.maximum(jnp.abs(den), jnp.exp(-m_t))
        hid = _rms(hid, mnorm_ref[:, c0:c0 + MLSTM_HEAD_DIM])
        y_ref[:, POOL_WIDTH + c0:POOL_WIDTH + c0 + MLSTM_HEAD_DIM] = (jax.nn.sigmoid(o) * hid).astype(BF16)
        m_new = m_t[CHUNK - 1:CHUNK, :]
        bl = b_last[:, h:h + 1]
        decay = jnp.exp(bl + m_prev[:, h:h + 1] - m_new)
        w_s = jnp.exp(bl - b_col + ig[:, h:h + 1] - m_new)
        kw = k * w_s
        c_ref[h] = decay * c_h + lax.dot_general(kw.astype(BF16), v, (((0,), (0,)), ((), ())),
                                                 preferred_element_type=F32)
        n_ref[h:h + 1, :] = decay * n_h + jnp.sum(kw, axis=0, keepdims=True)
        m_new_row = jnp.where(lane[0:1, :] == h, m_new, m_new_row)
    m_ref[...] = m_new_row


def _mixer_prompt(z, consts, batch, seq):
    nc = seq // CHUNK
    z3 = z.reshape(batch, seq, Z_WIDTH)
    const_specs = [pl.BlockSpec(a.shape, lambda b, c, nd=a.ndim: (0,) * nd) for a in consts]
    return pl.pallas_call(
        _mixer_prompt_kernel,
        grid=(batch, nc),
        in_specs=[pl.BlockSpec((None, CHUNK, Z_WIDTH), lambda b, c: (b, c, 0))] + const_specs,
        out_specs=[pl.BlockSpec((None, CHUNK, D_MODEL), lambda b, c: (b, c, 0)),
                   pl.BlockSpec((None, MLSTM_HEADS, MLSTM_HEAD_DIM, MLSTM_HEAD_DIM), lambda b, c: (b, 0, 0, 0)),
                   pl.BlockSpec((None, MLSTM_HEADS, MLSTM_HEAD_DIM), lambda b, c: (b, 0, 0)),
                   pl.BlockSpec((None, 1, LANES), lambda b, c: (b, 0, 0))],
        out_shape=[jax.ShapeDtypeStruct((batch, seq, D_MODEL), BF16),
                   jax.ShapeDtypeStruct((batch, MLSTM_HEADS, MLSTM_HEAD_DIM, MLSTM_HEAD_DIM), F32),
                   jax.ShapeDtypeStruct((batch, MLSTM_HEADS, MLSTM_HEAD_DIM), F32),
                   jax.ShapeDtypeStruct((batch, 1, LANES), F32)],
        scratch_shapes=[pltpu.VMEM((16 + CHUNK, POOL_WIDTH), F32)],
        compiler_params=_params("parallel", "arbitrary"),
        name="mixer_prompt",
    )(z3, *consts)


def _mixer_sample_kernel(z_ref, sp_ref, c_ref, n_ref, m_ref,
                         poolw_ref, pscale_ref, bi_ref, bf_ref, mnorm_ref, gnorm_ref,
                         gw0_ref, gb0_ref, gmean_ref,
                         y_ref, cn_ref, nn_ref, mn_ref, gv_ref, tq_ref, tk_ref):
    nb = SAMPLE_BLOCK
    lane = lax.broadcasted_iota(jnp.int32, (nb, LANES), 1)

    pooled = []
    for tile in range(2):
        col0 = tile * LANES
        u_tile = z_ref[:, Z_POOL + col0:Z_POOL + col0 + LANES]
        w_lo, w_hi = POOL_WINDOWS[2 * tile], POOL_WINDOWS[2 * tile + 1]
        acc = u_tile
        sums = {}
        for shift in range(1, w_hi):
            acc = acc + sp_ref[POOL_STATE - shift, :, col0:col0 + LANES]
            if shift + 1 in (w_lo, w_hi):
                sums[shift + 1] = acc
        pooled.append(jnp.where(lane < POOL_GROUP_DIM, sums[w_lo] / float(w_lo), sums[w_hi] / float(w_hi)) - u_tile)
    pooled = jnp.concatenate(pooled, axis=1).astype(BF16)
    y_ref[:, 0:POOL_WIDTH] = (_dot(pooled, poolw_ref[...]) * pscale_ref[...]).astype(BF16)

    vn = _group_rms(z_ref[:, Z_GV:Z_GV + GMLP_WIDTH], gmean_ref[...], gnorm_ref[...])
    gv_ref[...] = vn
    y_g = z_ref[:, Z_GU:Z_GU + GMLP_WIDTH] * (gw0_ref[...] * vn + gb0_ref[...])
    y_ref[:, 768:768 + GMLP_WIDTH] = y_g.astype(BF16)

    ig = z_ref[:, Z_IG:Z_IG + LANES] + bi_ref[...]
    lf = _log_sigmoid(z_ref[:, Z_FG:Z_FG + LANES] + bf_ref[...])
    m_prev = m_ref[...]
    g = lf + m_prev
    m_t = jnp.maximum(g, ig)
    inter = jnp.exp(g - m_t)
    e_ig = jnp.exp(ig - m_t)
    floor = jnp.exp(-m_t)
    mn_ref[...] = m_t
    for h in range(MLSTM_HEADS):
        c0 = h * MLSTM_HEAD_DIM
        tq_ref[...] = jnp.zeros((LANES, LANES), F32)
        tk_ref[...] = jnp.zeros((LANES, LANES), F32)
        tq_ref[0:nb, :] = z_ref[:, Z_Q + c0:Z_Q + c0 + MLSTM_HEAD_DIM]
        tk_ref[0:nb, :] = z_ref[:, Z_K + c0:Z_K + c0 + MLSTM_HEAD_DIM] * (MLSTM_HEAD_DIM ** -0.5)
        q_t = jnp.transpose(tq_ref[...])
        k_t = jnp.transpose(tk_ref[...])
        for s in range(nb):
            q_row = tq_ref[s:s + 1, :]
            k_row = tk_ref[s:s + 1, :]
            v_row = z_ref[s:s + 1, Z_V + c0:Z_V + c0 + MLSTM_HEAD_DIM]
            o_row = z_ref[s:s + 1, Z_O + c0:Z_O + c0 + MLSTM_HEAD_DIM]
            c_sh = c_ref[s, h]
            n_row = n_ref[s, h:h + 1, :]
            inter_s = inter[s:s + 1, h:h + 1]
            wts = e_ig[s:s + 1, h:h + 1] * jnp.sum(q_row * k_row, axis=1, keepdims=True)
            q_c = jnp.sum(q_t[:, s:s + 1] * c_sh, axis=0, keepdims=True)
            num = inter_s * q_c + wts * v_row
            den = inter_s * jnp.sum(q_row * n_row, axis=1, keepdims=True) + wts
            hid = num / jnp.maximum(jnp.abs(den), floor[s:s + 1, h:h + 1])
            hid = _rms(hid, mnorm_ref[:, c0:c0 + MLSTM_HEAD_DIM])
            y_ref[s:s + 1, POOL_WIDTH + c0:POOL_WIDTH + c0 + MLSTM_HEAD_DIM] = (
                jax.nn.sigmoid(o_row) * hid).astype(BF16)
            w_s = e_ig[s:s + 1, h:h + 1]
            cn_ref[s, h] = inter_s * c_sh + (k_t[:, s:s + 1] * w_s) * v_row
            nn_ref[s, h:h + 1, :] = inter_s * n_row + w_s * k_row


def _mixer_sample(z, sp_t, c_state, n_state, m_pad, consts):
    nseq = z.shape[0]
    nb = SAMPLE_BLOCK
    hd = MLSTM_HEAD_DIM
    const_specs = [pl.BlockSpec(a.shape, lambda j, nd=a.ndim: (0,) * nd) for a in consts]
    return pl.pallas_call(
        _mixer_sample_kernel,
        grid=(nseq // nb,),
        in_specs=[pl.BlockSpec((nb, Z_WIDTH), lambda j: (j, 0)),
                  pl.BlockSpec((POOL_STATE, nb, POOL_WIDTH), lambda j: (0, j, 0)),
                  pl.BlockSpec((nb, MLSTM_HEADS, hd, hd), lambda j: (j, 0, 0, 0)),
                  pl.BlockSpec((nb, MLSTM_HEADS, hd), lambda j: (j, 0, 0)),
                  pl.BlockSpec((nb, LANES), lambda j: (j, 0))] + const_specs,
        out_specs=[pl.BlockSpec((nb, D_MODEL), lambda j: (j, 0)),
                   pl.BlockSpec((nb, MLSTM_HEADS, hd, hd), lambda j: (j, 0, 0, 0)),
                   pl.BlockSpec((nb, MLSTM_HEADS, hd), lambda j: (j, 0, 0)),
                   pl.BlockSpec((nb, LANES), lambda j: (j, 0)),
                   pl.BlockSpec((nb, GMLP_WIDTH), lambda j: (j, 0))],
        out_shape=[jax.ShapeDtypeStruct((nseq, D_MODEL), BF16),
                   jax.ShapeDtypeStruct((nseq, MLSTM_HEADS, hd, hd), F32),
                   jax.ShapeDtypeStruct((nseq, MLSTM_HEADS, hd), F32),
                   jax.ShapeDtypeStruct((nseq, LANES), F32),
                   jax.ShapeDtypeStruct((nseq, GMLP_WIDTH), F32)],
        scratch_shapes=[pltpu.VMEM((LANES, LANES), F32), pltpu.VMEM((LANES, LANES), F32)],
        compiler_params=_params("parallel"),
        name="mixer_sample",
    )(z, sp_t, c_state, n_state, m_pad, *consts)


def _proj_norm_res_kernel(y_ref, x_ref, w_ref, nw_ref, o_ref):
    o_ref[...] = x_ref[...] + _rms(_dot(y_ref[...], w_ref[...]), nw_ref[...])


def _proj_norm_res(y, x, w, nw, tm):
    t = x.shape[0]
    return pl.pallas_call(
        _proj_norm_res_kernel,
        grid=(t // tm,),
        in_specs=[pl.BlockSpec((tm, D_MODEL), lambda i: (i, 0)),
                  pl.BlockSpec((tm, D_MODEL), lambda i: (i, 0)),
                  pl.BlockSpec((D_MODEL, D_MODEL), lambda i: (0, 0)),
                  pl.BlockSpec((1, D_MODEL), lambda i: (0, 0))],
        out_specs=pl.BlockSpec((tm, D_MODEL), lambda i: (i, 0)),
        out_shape=jax.ShapeDtypeStruct((t, D_MODEL), F32),
        compiler_params=_params("parallel"),
        name="out_proj",
    )(y, x, w, nw)


def _ffn_dense_kernel(x_ref, npre_ref, npost_ref, wg_ref, wu_ref, wd_ref, o_ref, h_ref, acc_ref):
    j = pl.program_id(1)

    @pl.when(j == 0)
    def _():
        h_ref[...] = _rms(x_ref[...], npre_ref[...]).astype(BF16)
        acc_ref[...] = jnp.zeros(acc_ref.shape, F32)

    h = h_ref[...]
    gate = _dot(h, wg_ref[...].astype(BF16))
    up = _dot(h, wu_ref[...].astype(BF16))
    act = (gate * jax.nn.sigmoid(gate) * up).astype(BF16)
    acc_ref[...] += _dot(act, wd_ref[...].astype(BF16))

    @pl.when(j == pl.num_programs(1) - 1)
    def _():
        o_ref[...] = x_ref[...] + _rms(acc_ref[...], npost_ref[...])


def _ffn_dense(x, npre, npost, wg, wu, wd, tm):
    t = x.shape[0]
    return pl.pallas_call(
        _ffn_dense_kernel,
        grid=(t // tm, D_FF // TF_DENSE),
        in_specs=[pl.BlockSpec((tm, D_MODEL), lambda i, j: (i, 0)),
                  pl.BlockSpec((1, D_MODEL), lambda i, j: (0, 0)),
                  pl.BlockSpec((1, D_MODEL), lambda i, j: (0, 0)),
                  pl.BlockSpec((D_MODEL, TF_DENSE), lambda i, j: (0, j)),
                  pl.BlockSpec((D_MODEL, TF_DENSE), lambda i, j: (0, j)),
                  pl.BlockSpec((TF_DENSE, D_MODEL), lambda i, j: (j, 0))],
        out_specs=pl.BlockSpec((tm, D_MODEL), lambda i, j: (i, 0)),
        out_shape=jax.ShapeDtypeStruct((t, D_MODEL), F32),
        scratch_shapes=[pltpu.VMEM((tm, D_MODEL), BF16), pltpu.VMEM((tm, D_MODEL), F32)],
        compiler_params=_params("parallel", "arbitrary"),
        name="ffn_dense",
    )(x, npre, npost, wg, wu, wd)


def _router_gates(h, rw_ref, rb_ref):
    shape = (h.shape[0], LANES)
    lane = lax.broadcasted_iota(jnp.int32, shape, 1)
    lane_f = lane.astype(F32)
    logits = jnp.where(lane < N_EXPERTS, _dot(h, rw_ref[...]) + rb_ref[...], -jnp.inf)
    e = jnp.exp(logits - jnp.max(logits, axis=-1, keepdims=True))
    probs = e / jnp.sum(e, axis=-1, keepdims=True)
    p1 = jnp.max(probs, axis=-1, keepdims=True)
    i1 = jnp.min(jnp.where(probs == p1, lane_f, float(LANES)), axis=-1, keepdims=True)
    rest = jnp.where(lane_f == i1, -1.0, probs)
    p2 = jnp.max(rest, axis=-1, keepdims=True)
    i2 = jnp.min(jnp.where(rest == p2, lane_f, float(LANES)), axis=-1, keepdims=True)
    total = p1 + p2
    return jnp.where(lane_f == i1, p1 / total, 0.0) + jnp.where(lane_f == i2, p2 / total, 0.0)


def _ffn_moe_kernel(x_ref, npre_ref, npost_ref, rw_ref, rb_ref, wg_ref, wu_ref, wd_ref,
                    o_ref, h_ref, acc_ref, gates_ref):
    e = pl.program_id(1)

    @pl.when(e == 0)
    def _():
        h_ref[...] = _rms(x_ref[...], npre_ref[...]).astype(BF16)
        acc_ref[...] = jnp.zeros(acc_ref.shape, F32)
        gates_ref[...] = _router_gates(h_ref[...], rw_ref, rb_ref)

    h = h_ref[...]
    lane = lax.broadcasted_iota(jnp.int32, gates_ref.shape, 1)
    gate_col = jnp.sum(jnp.where(lane == e, gates_ref[...], 0.0), axis=-1, keepdims=True)
    y = None
    for f0, fw in FF_EXPERT_CHUNKS:
        gate = _dot(h, wg_ref[:, f0:f0 + fw])
        up = _dot(h, wu_ref[:, f0:f0 + fw])
        act = (gate * jax.nn.sigmoid(gate) * up).astype(BF16)
        part = _dot(act, wd_ref[f0:f0 + fw, :])
        y = part if y is None else y + part
    acc_ref[...] += gate_col * y

    @pl.when(e == pl.num_programs(1) - 1)
    def _():
        o_ref[...] = x_ref[...] + _rms(acc_ref[...], npost_ref[...])


def _ffn_moe(x, npre, npost, rw, rb, wg, wu, wd, tm):
    t = x.shape[0]
    return pl.pallas_call(
        _ffn_moe_kernel,
        grid=(t // tm, N_EXPERTS),
        in_specs=[pl.BlockSpec((tm, D_MODEL), lambda i, e: (i, 0)),
                  pl.BlockSpec((1, D_MODEL), lambda i, e: (0, 0)),
                  pl.BlockSpec((1, D_MODEL), lambda i, e: (0, 0)),
                  pl.BlockSpec((D_MODEL, LANES), lambda i, e: (0, 0)),
                  pl.BlockSpec((1, LANES), lambda i, e: (0, 0)),
                  pl.BlockSpec((None, D_MODEL, D_FF_EXPERT), lambda i, e: (e, 0, 0)),
                  pl.BlockSpec((None, D_MODEL, D_FF_EXPERT), lambda i, e: (e, 0, 0)),
                  pl.BlockSpec((None, D_FF_EXPERT, D_MODEL), lambda i, e: (e, 0, 0))],
        out_specs=pl.BlockSpec((tm, D_MODEL), lambda i, e: (i, 0)),
        out_shape=jax.ShapeDtypeStruct((t, D_MODEL), F32),
        scratch_shapes=[pltpu.VMEM((tm, D_MODEL), BF16), pltpu.VMEM((tm, D_MODEL), F32),
                        pltpu.VMEM((tm, LANES), F32)],
        compiler_params=_params("parallel", "arbitrary"),
        name="ffn_moe",
    )(x, npre, npost, rw, rb, wg, wu, wd)


def _ple_kernel(x_ref, p_ref, nw_ref, wg_ref, wp_ref, o_ref):
    x = x_ref[...]
    gate = jax.nn.sigmoid(_dot(_rms(x, nw_ref[...]).astype(BF16), wg_ref[...]))
    o_ref[...] = x + gate * _dot(p_ref[...].astype(BF16), wp_ref[...])


def _ple(x, p, nw, wg, wp, tm):
    t = x.shape[0]
    return pl.pallas_call(
        _ple_kernel,
        grid=(t // tm,),
        in_specs=[pl.BlockSpec((tm, D_MODEL), lambda i: (i, 0)),
                  pl.BlockSpec((tm, PLE_DIM), lambda i: (i, 0)),
                  pl.BlockSpec((1, D_MODEL), lambda i: (0, 0)),
                  pl.BlockSpec((D_MODEL, D_MODEL), lambda i: (0, 0)),
                  pl.BlockSpec((PLE_DIM, D_MODEL), lambda i: (0, 0))],
        out_specs=pl.BlockSpec((tm, D_MODEL), lambda i: (i, 0)),
        out_shape=jax.ShapeDtypeStruct((t, D_MODEL), F32),
        compiler_params=_params("parallel"),
        name="ple",
    )(x, p, nw, wg, wp)


def _pad_lanes(a, width=LANES):
    return jnp.pad(a, [(0, 0)] * (a.ndim - 1) + [(0, width - a.shape[-1])])


def _permute_w_in(w):
    main = w[:, 0:2304]
    ig = w[:, 2304:2308]
    fg = w[:, 2308:2312]
    gu_gv = w[:, 2312:2824]
    return jnp.concatenate([main, gu_gv, _pad_lanes(ig), _pad_lanes(fg)], axis=1).astype(BF16)


def _block_diag(blocks):
    g, d, _ = blocks.shape
    out = jnp.zeros((g * d, g * d), blocks.dtype)
    for i in range(g):
        out = out.at[i * d:(i + 1) * d, i * d:(i + 1) * d].set(blocks[i])
    return out


def _row(a):
    return a.reshape(1, -1).astype(F32)


def kernel(x_prompt, x_sample, state_pool, state_mlstm_C, state_mlstm_n, state_mlstm_m, p_prompt, p_sample,
           norm_mix_pre, norm_mix_post, norm_ffn_pre, norm_ffn_post, norm_ple, w_in, pool_w, pool_scale,
           mlstm_b_i, mlstm_b_f, mlstm_norm_w, gmlp_norm_w, gmlp_ws, gmlp_bs, w_out,
           ffn_w_gate, ffn_w_up, ffn_w_down, moe_router_w, moe_router_b, moe_w_gate, moe_w_up, moe_w_down,
           ple_w_gate, ple_w_proj):
    batch, seq, _ = x_prompt.shape
    nseq = x_sample.shape[0]
    xp = x_prompt.reshape(batch * seq, D_MODEL)
    xs = x_sample.reshape(nseq, D_MODEL)
    gmean = _block_diag(jnp.full((GMLP_GROUPS, GMLP_GROUP_DIM, GMLP_GROUP_DIM), 1.0 / GMLP_GROUP_DIM, F32))

    pools_p, cs_p, ns_p, ms_p = [], [], [], []
    pools_s, cs_s, ns_s, ms_s, gvs_s = [], [], [], [], []
    for i in range(DEPTH):
        w_in_p = _permute_w_in(w_in[i])
        w_out_b = w_out[i].astype(BF16)
        poolw = _block_diag(pool_w[i]).astype(BF16)
        shared = [poolw, _row(pool_scale[i]), _pad_lanes(_row(mlstm_b_i[i])), _pad_lanes(_row(mlstm_b_f[i])),
                  _row(mlstm_norm_w[i]), _row(gmlp_norm_w[i])]
        gbs_full = jnp.repeat(gmlp_bs[i].T, GMLP_GROUP_DIM, axis=1)
        consts_p = shared + [gmlp_ws[i], gbs_full, gmean]
        gw0 = jnp.repeat(gmlp_ws[i][:, 0, 0], GMLP_GROUP_DIM).reshape(1, GMLP_WIDTH)
        consts_s = shared + [gw0, gbs_full[0:1, :], gmean]
        ple_g = ple_w_gate[i].astype(BF16)
        ple_p = ple_w_proj[i].astype(BF16)
        j = i // 2
        if i % 2 == 1:
            rw = _pad_lanes(moe_router_w[j]).astype(BF16)
            rb = _pad_lanes(_row(moe_router_b[j]))
            moe_g, moe_u, moe_d = (moe_w_gate[j].astype(BF16), moe_w_up[j].astype(BF16),
                                   moe_w_down[j].astype(BF16))

        z = _norm_matmul(xp, _row(norm_mix_pre[i]), w_in_p, TM_PROMPT)
        y, c_new, n_new, m_new = _mixer_prompt(z, consts_p, batch, seq)
        pools_p.append(z.reshape(batch, seq, Z_WIDTH)[:, seq - POOL_STATE:, 0:POOL_WIDTH])
        cs_p.append(c_new)
        ns_p.append(n_new)
        ms_p.append(m_new[:, 0, 0:MLSTM_HEADS])
        xp = _proj_norm_res(y.reshape(batch * seq, D_MODEL), xp, w_out_b, _row(norm_mix_post[i]), TM_PROMPT)
        if i % 2 == 0:
            xp = _ffn_dense(xp, _row(norm_ffn_pre[i]), _row(norm_ffn_post[i]),
                            ffn_w_gate[j], ffn_w_up[j], ffn_w_down[j], TM_FFN)
        else:
            xp = _ffn_moe(xp, _row(norm_ffn_pre[i]), _row(norm_ffn_post[i]), rw, rb, moe_g, moe_u, moe_d, TM_MOE)
        xp = _ple(xp, p_prompt[i].reshape(batch * seq, PLE_DIM), _row(norm_ple[i]), ple_g, ple_p, TM_PROMPT)

        z = _norm_matmul(xs, _row(norm_mix_pre[i]), w_in_p, nseq)
        sp_t = jnp.transpose(state_pool[i], (1, 0, 2))
        y, c_new, n_new, m_new, gv = _mixer_sample(z, sp_t, state_mlstm_C[i], state_mlstm_n[i],
                                                   _pad_lanes(state_mlstm_m[i]), consts_s)
        pools_s.append(jnp.concatenate([state_pool[i][:, 1:], z[:, None, 0:POOL_WIDTH]], axis=1))
        cs_s.append(c_new)
        ns_s.append(n_new)
        ms_s.append(m_new[:, 0:MLSTM_HEADS])
        gvs_s.append(gv[:, None, :])
        xs = _proj_norm_res(y, xs, w_out_b, _row(norm_mix_post[i]), nseq)
        if i % 2 == 0:
            xs = _ffn_dense(xs, _row(norm_ffn_pre[i]), _row(norm_ffn_post[i]),
                            ffn_w_gate[j], ffn_w_up[j], ffn_w_down[j], nseq)
        else:
            xs = _ffn_moe(xs, _row(norm_ffn_pre[i]), _row(norm_ffn_post[i]), rw, rb, moe_g, moe_u, moe_d, nseq)
        xs = _ple(xs, p_sample[i].reshape(nseq, PLE_DIM), _row(norm_ple[i]), ple_g, ple_p, nseq)

    return (xp.reshape(batch, seq, D_MODEL), xs.reshape(nseq, 1, D_MODEL),
            jnp.stack(pools_p), jnp.stack(cs_p), jnp.stack(ns_p), jnp.stack(ms_p),
            jnp.stack(pools_s), jnp.stack(cs_s), jnp.stack(ns_s), jnp.stack(ms_s), jnp.stack(gvs_s))
```

```python
import functools

import jax
import jax.numpy as jnp
from jax import lax
from jax.experimental import pallas as pl
from jax.experimental.pallas import tpu as pltpu

F32 = jnp.float32
BF16 = jnp.bfloat16

D_MODEL = 1024
DEPTH = 2
POOL_WIDTH = 256
POOL_WINDOWS = (2, 4, 8, 16)
POOL_GROUP_DIM = 64
POOL_STATE = 15
MLSTM_WIDTH = 512
MLSTM_HEADS = 4
MLSTM_HEAD_DIM = 128
CHUNK = 128
GMLP_WIDTH = 256
GMLP_GROUPS = 4
GMLP_GROUP_DIM = 64
D_FF = 2816
N_EXPERTS = 8
D_FF_EXPERT = 1408
PLE_DIM = 256
RMS_EPS = 1e-6
PAST_LEN = 16384

LANES = 128
SUBLANES = 8
VMEM_LIMIT = 48 * 1024 * 1024

Z_POOL = 0
Z_Q = 256
Z_K = 768
Z_V = 1280
Z_O = 1792
Z_GU = 2304
Z_GV = 2560
Z_IG = 2816
Z_FG = 2944
Z_WIDTH = 3072
Z_CHUNK = 512

TM_PROMPT = 512
TM_FFN = 1024
TF_DENSE = 256
TM_MOE = 512
FF_EXPERT_CHUNKS = ((0, 512), (512, 512), (1024, 384))
SAMPLE_BLOCK = 8
PROMPT_SEQ_PER_STEP = 2


def _params(*semantics):
    return pltpu.CompilerParams(dimension_semantics=semantics, vmem_limit_bytes=VMEM_LIMIT)


def _rms(x, w):
    return x * lax.rsqrt(jnp.mean(x * x, axis=-1, keepdims=True) + RMS_EPS) * w


def _log_sigmoid(x):
    return jnp.minimum(x, 0.0) - jnp.log1p(jnp.exp(-jnp.abs(x)))


def _dot(a, b):
    return jnp.dot(a, b, preferred_element_type=F32)


def _split3(x):
    hi = x.astype(BF16)
    rest = x - hi.astype(F32)
    mid = rest.astype(BF16)
    lo = (rest - mid.astype(F32)).astype(BF16)
    return hi, mid, lo


def _norm_matmul_kernel(x_ref, nw_ref, w_ref, o_ref, h_ref):
    h_ref[...] = _rms(x_ref[...], nw_ref[...]).astype(BF16)
    for n0 in range(0, Z_WIDTH, Z_CHUNK):
        o_ref[:, n0:n0 + Z_CHUNK] = _dot(h_ref[...], w_ref[:, n0:n0 + Z_CHUNK])


def _norm_matmul(x, nw, w, tm):
    t = x.shape[0]
    return pl.pallas_call(
        _norm_matmul_kernel,
        grid=(t // tm,),
        in_specs=[pl.BlockSpec((tm, D_MODEL), lambda i: (i, 0)),
                  pl.BlockSpec((1, D_MODEL), lambda i: (0, 0)),
                  pl.BlockSpec((D_MODEL, Z_WIDTH), lambda i: (0, 0))],
        out_specs=pl.BlockSpec((tm, Z_WIDTH), lambda i: (i, 0)),
        out_shape=jax.ShapeDtypeStruct((t, Z_WIDTH), F32),
        scratch_shapes=[pltpu.VMEM((tm, D_MODEL), BF16)],
        compiler_params=_params("parallel"),
        name="norm_in_proj",
    )(x, nw, w)


def _group_rms(v, gmean, w):
    hi, mid, lo = _split3(v * v)
    ms = _dot(hi, gmean) + _dot(mid, gmean) + _dot(lo, gmean)
    return v * lax.rsqrt(ms + RMS_EPS) * w


def _pool_tile(ext_ref, u_tile, col0, w_lo, w_hi, pos):
    acc = u_tile
    sums = {}
    for shift in range(1, w_hi):
        acc = acc + ext_ref[pl.ds(16 - shift, CHUNK), col0:col0 + LANES]
        if shift + 1 in (w_lo, w_hi):
            sums[shift + 1] = acc
    cnt_lo = jnp.minimum(w_lo, pos + 1).astype(F32)
    cnt_hi = jnp.minimum(w_hi, pos + 1).astype(F32)
    lane = lax.broadcasted_iota(jnp.int32, (CHUNK, LANES), 1)
    return jnp.where(lane < POOL_GROUP_DIM, sums[w_lo] / cnt_lo, sums[w_hi] / cnt_hi) - u_tile


def _mixer_prompt_kernel(z_ref, *refs):
    consts = refs[:9]
    y_ref, cn_ref, m_ref, ext_ref = refs[9:]

    @pl.when(pl.program_id(1) == 0)
    def _():
        ext_ref[:, 0:16, :] = jnp.zeros((PROMPT_SEQ_PER_STEP, 16, POOL_WIDTH), F32)
        cn_ref[...] = jnp.zeros(cn_ref.shape, F32)
        m_ref[...] = jnp.zeros(m_ref.shape, F32)

    for i in range(PROMPT_SEQ_PER_STEP):
        _mixer_prompt_body(z_ref.at[i], *consts, y_ref.at[i], cn_ref.at[i], m_ref.at[i], ext_ref.at[i])


def _mixer_prompt_body(z_ref, poolw_ref, pscale_ref, bi_ref, bf_ref, mnorm_ref, gnorm_ref,
                       gws_ref, gbs_ref, gmean_ref,
                       y_ref, cn_ref, m_ref, ext_ref):
    chunk = pl.program_id(1)
    row = lax.broadcasted_iota(jnp.int32, (CHUNK, CHUNK), 0)
    col = lax.broadcasted_iota(jnp.int32, (CHUNK, CHUNK), 1)
    causal = col <= row
    lane = col

    ext_ref[16:16 + CHUNK, :] = z_ref[:, Z_POOL:Z_POOL + POOL_WIDTH]
    pos = chunk * CHUNK + lax.broadcasted_iota(jnp.int32, (CHUNK, 1), 0)
    pooled = []
    for tile in range(2):
        col0 = tile * LANES
        u_tile = z_ref[:, Z_POOL + col0:Z_POOL + col0 + LANES]
        pooled.append(_pool_tile(ext_ref, u_tile, col0, POOL_WINDOWS[2 * tile],
                                 POOL_WINDOWS[2 * tile + 1], pos))
    pooled = jnp.concatenate(pooled, axis=1).astype(BF16)
    y_pool = _dot(pooled, poolw_ref[...]) * pscale_ref[...]
    y_ref[:, 0:POOL_WIDTH] = y_pool.astype(BF16)
    ext_ref[0:16, :] = ext_ref[CHUNK:CHUNK + 16, :]

    vn = _group_rms(z_ref[:, Z_GV:Z_GV + GMLP_WIDTH], gmean_ref[...], gnorm_ref[...]).astype(BF16)
    for tile in range(2):
        col0 = tile * LANES
        vt = vn[:, col0:col0 + LANES]
        w_a = jnp.where(causal, gws_ref[2 * tile], 0.0).astype(BF16)
        w_b = jnp.where(causal, gws_ref[2 * tile + 1], 0.0).astype(BF16)
        mixed = jnp.where(lane < GMLP_GROUP_DIM, _dot(w_a, vt), _dot(w_b, vt))
        gu = z_ref[:, Z_GU + col0:Z_GU + col0 + LANES]
        y_g = gu * (mixed + gbs_ref[:, col0:col0 + LANES])
        y_ref[:, 768 + col0:768 + col0 + LANES] = y_g.astype(BF16)

    ig = z_ref[:, Z_IG:Z_IG + LANES] + bi_ref[...]
    lf = _log_sigmoid(z_ref[:, Z_FG:Z_FG + LANES] + bf_ref[...])
    tri = jnp.where(causal, 1.0, 0.0).astype(BF16)
    lf_hi, lf_mid, lf_lo = _split3(lf)
    b = _dot(tri, lf_hi) + _dot(tri, lf_mid) + _dot(tri, lf_lo)
    m_prev = m_ref[...]
    g = b + m_prev
    r_t = jnp.transpose(ig - b)
    b_last = b[CHUNK - 1:CHUNK, :]
    ones_col = jnp.where(lane == 0, 1.0, 0.0).astype(BF16)
    m_new_row = m_prev
    for h in range(MLSTM_HEADS):
        c0 = h * MLSTM_HEAD_DIM
        q = z_ref[:, Z_Q + c0:Z_Q + c0 + MLSTM_HEAD_DIM].astype(BF16)
        k = z_ref[:, Z_K + c0:Z_K + c0 + MLSTM_HEAD_DIM] * (MLSTM_HEAD_DIM ** -0.5)
        v = z_ref[:, Z_V + c0:Z_V + c0 + MLSTM_HEAD_DIM].astype(BF16)
        o = z_ref[:, Z_O + c0:Z_O + c0 + MLSTM_HEAD_DIM]
        b_col = b[:, h:h + 1]
        dmat = jnp.where(causal, b_col + r_t[h:h + 1, :], -jnp.inf)
        g_col = g[:, h:h + 1]
        m_t = jnp.maximum(g_col, jnp.max(dmat, axis=1, keepdims=True))
        scores = lax.dot_general(q, k.astype(BF16), (((1,), (1,)), ((), ())),
                                 preferred_element_type=F32)
        wts = jnp.exp(dmat - m_t) * scores
        inter = jnp.exp(g_col - m_t)
        cn_h = cn_ref[h]
        q_cn = _dot(q, cn_h.astype(BF16))
        num = inter * q_cn[:, 0:MLSTM_HEAD_DIM] + _dot(wts.astype(BF16), v)
        den = inter * q_cn[:, MLSTM_HEAD_DIM:MLSTM_HEAD_DIM + 1] + jnp.sum(wts, axis=1, keepdims=True)
        hid = num / jnp.maximum(jnp.abs(den), jnp.exp(-m_t))
        hid = _rms(hid, mnorm_ref[:, c0:c0 + MLSTM_HEAD_DIM])
        y_ref[:, POOL_WIDTH + c0:POOL_WIDTH + c0 + MLSTM_HEAD_DIM] = (jax.nn.sigmoid(o) * hid).astype(BF16)
        m_new = m_t[CHUNK - 1:CHUNK, :]
        bl = b_last[:, h:h + 1]
        decay = jnp.exp(bl + m_prev[:, h:h + 1] - m_new)
        w_s = jnp.exp(bl - b_col + ig[:, h:h + 1] - m_new)
        kw = (k * w_s).astype(BF16)
        v_ext = jnp.concatenate([v, ones_col], axis=1)
        cn_ref[h] = decay * cn_h + lax.dot_general(kw, v_ext, (((0,), (0,)), ((), ())),
                                                   preferred_element_type=F32)
        m_new_row = jnp.where(lane[0:1, :] == h, m_new, m_new_row)
    m_ref[...] = m_new_row


def _mixer_prompt(z, consts, batch, seq):
    nc = seq // CHUNK
    hd = MLSTM_HEAD_DIM
    z3 = z.reshape(batch, seq, Z_WIDTH)
    ns = PROMPT_SEQ_PER_STEP
    const_specs = [pl.BlockSpec(a.shape, lambda b, c, nd=a.ndim: (0,) * nd) for a in consts]
    return pl.pallas_call(
        _mixer_prompt_kernel,
        grid=(batch // ns, nc),
        in_specs=[pl.BlockSpec((ns, CHUNK, Z_WIDTH), lambda b, c: (b, c, 0))] + const_specs,
        out_specs=[pl.BlockSpec((ns, CHUNK, D_MODEL), lambda b, c: (b, c, 0)),
                   pl.BlockSpec((ns, MLSTM_HEADS, hd, 2 * hd), lambda b, c: (b, 0, 0, 0)),
                   pl.BlockSpec((ns, 1, LANES), lambda b, c: (b, 0, 0))],
        out_shape=[jax.ShapeDtypeStruct((batch, seq, D_MODEL), BF16),
                   jax.ShapeDtypeStruct((batch, MLSTM_HEADS, hd, 2 * hd), F32),
                   jax.ShapeDtypeStruct((batch, 1, LANES), F32)],
        scratch_shapes=[pltpu.VMEM((ns, 16 + CHUNK, POOL_WIDTH), F32)],
        compiler_params=_params("parallel", "arbitrary"),
        name="mixer_prompt",
    )(z3, *consts)


def _mixer_sample_kernel(z_ref, sp_ref, c_ref, n_ref, m_ref,
                         poolw_ref, pscale_ref, bi_ref, bf_ref, mnorm_ref, gnorm_ref,
                         gw0_ref, gb0_ref, gmean_ref,
                         y_ref, cn_ref, nn_ref, mn_ref, gv_ref, tk_ref):
    nb = SAMPLE_BLOCK
    hd = MLSTM_HEAD_DIM
    lane = lax.broadcasted_iota(jnp.int32, (nb, LANES), 1)
    seq_id = lax.broadcasted_iota(jnp.int32, (nb, LANES), 0)

    pooled = []
    for tile in range(2):
        col0 = tile * LANES
        u_tile = z_ref[:, Z_POOL + col0:Z_POOL + col0 + LANES]
        w_lo, w_hi = POOL_WINDOWS[2 * tile], POOL_WINDOWS[2 * tile + 1]
        acc = u_tile
        sums = {}
        for shift in range(1, w_hi):
            acc = acc + sp_ref[POOL_STATE - shift, :, col0:col0 + LANES]
            if shift + 1 in (w_lo, w_hi):
                sums[shift + 1] = acc
        pooled.append(jnp.where(lane < POOL_GROUP_DIM, sums[w_lo] / float(w_lo), sums[w_hi] / float(w_hi)) - u_tile)
    pooled = jnp.concatenate(pooled, axis=1).astype(BF16)
    y_ref[:, 0:POOL_WIDTH] = (_dot(pooled, poolw_ref[...]) * pscale_ref[...]).astype(BF16)

    vn = _group_rms(z_ref[:, Z_GV:Z_GV + GMLP_WIDTH], gmean_ref[...], gnorm_ref[...])
    gv_ref[...] = vn
    y_g = z_ref[:, Z_GU:Z_GU + GMLP_WIDTH] * (gw0_ref[...] * vn + gb0_ref[...])
    y_ref[:, 768:768 + GMLP_WIDTH] = y_g.astype(BF16)

    ig = z_ref[:, Z_IG:Z_IG + LANES] + bi_ref[...]
    lf = _log_sigmoid(z_ref[:, Z_FG:Z_FG + LANES] + bf_ref[...])
    m_prev = m_ref[...]
    g = lf + m_prev
    m_t = jnp.maximum(g, ig)
    inter = jnp.exp(g - m_t)
    e_ig = jnp.exp(ig - m_t)
    floor = jnp.exp(-m_t)
    mn_ref[...] = m_t
    tk_ref[...] = jnp.zeros((LANES, LANES), F32)
    for h in range(MLSTM_HEADS):
        tk_ref[nb * h:nb * (h + 1), :] = z_ref[:, Z_K + h * hd:Z_K + (h + 1) * hd] * (hd ** -0.5)
    k_t = jnp.transpose(tk_ref[...])
    for h in range(MLSTM_HEADS):
        c0 = h * hd
        q_h = z_ref[:, Z_Q + c0:Z_Q + c0 + hd]
        k_h = tk_ref[nb * h:nb * (h + 1), :]
        v_h = z_ref[:, Z_V + c0:Z_V + c0 + hd]
        o_h = z_ref[:, Z_O + c0:Z_O + c0 + hd]
        n_h = n_ref[:, c0:c0 + hd]
        inter_b = jnp.broadcast_to(inter[:, h:h + 1], (nb, hd))
        e_b = jnp.broadcast_to(e_ig[:, h:h + 1], (nb, hd))
        floor_b = jnp.broadcast_to(floor[:, h:h + 1], (nb, hd))
        v_w = e_b * v_h
        q_b = q_h.astype(BF16)
        q_c = jnp.zeros((nb, hd), F32)
        for s in range(nb):
            c_sh = c_ref[s, h]
            q_c = jnp.where(seq_id == s, _dot(q_b, c_sh.astype(BF16)), q_c)
            col = nb * h + s
            cn_ref[s, h] = inter_b[s:s + 1, :] * c_sh + k_t[:, col:col + 1] * v_w[s:s + 1, :]
        wts = e_b * jnp.sum(q_h * k_h, axis=1, keepdims=True)
        num = inter_b * q_c + wts * v_h
        den = inter_b * jnp.sum(q_h * n_h, axis=1, keepdims=True) + wts
        hid = num / jnp.maximum(jnp.abs(den), floor_b)
        hid = _rms(hid, mnorm_ref[:, c0:c0 + hd])
        y_ref[:, POOL_WIDTH + c0:POOL_WIDTH + c0 + hd] = (jax.nn.sigmoid(o_h) * hid).astype(BF16)
        nn_ref[:, c0:c0 + hd] = inter_b * n_h + e_b * k_h


def _mixer_sample(z, sp_t, c_state, n_state, m_pad, consts):
    nseq = z.shape[0]
    nb = SAMPLE_BLOCK
    hd = MLSTM_HEAD_DIM
    const_specs = [pl.BlockSpec(a.shape, lambda j, nd=a.ndim: (0,) * nd) for a in consts]
    return pl.pallas_call(
        _mixer_sample_kernel,
        grid=(nseq // nb,),
        in_specs=[pl.BlockSpec((nb, Z_WIDTH), lambda j: (j, 0)),
                  pl.BlockSpec((POOL_STATE, nb, POOL_WIDTH), lambda j: (0, j, 0)),
                  pl.BlockSpec((nb, MLSTM_HEADS, hd, hd), lambda j: (j, 0, 0, 0)),
                  pl.BlockSpec((nb, MLSTM_WIDTH), lambda j: (j, 0)),
                  pl.BlockSpec((nb, LANES), lambda j: (j, 0))] + const_specs,
        out_specs=[pl.BlockSpec((nb, D_MODEL), lambda j: (j, 0)),
                   pl.BlockSpec((nb, MLSTM_HEADS, hd, hd), lambda j: (j, 0, 0, 0)),
                   pl.BlockSpec((nb, MLSTM_WIDTH), lambda j: (j, 0)),
                   pl.BlockSpec((nb, LANES), lambda j: (j, 0)),
                   pl.BlockSpec((nb, GMLP_WIDTH), lambda j: (j, 0))],
        out_shape=[jax.ShapeDtypeStruct((nseq, D_MODEL), BF16),
                   jax.ShapeDtypeStruct((nseq, MLSTM_HEADS, hd, hd), F32),
                   jax.ShapeDtypeStruct((nseq, MLSTM_WIDTH), F32),
                   jax.ShapeDtypeStruct((nseq, LANES), F32),
                   jax.ShapeDtypeStruct((nseq, GMLP_WIDTH), F32)],
        scratch_shapes=[pltpu.VMEM((LANES, LANES), F32)],
        compiler_params=_params("parallel"),
        name="mixer_sample",
    )(z, sp_t, c_state, n_state, m_pad, *consts)


def _proj_norm_res_kernel(y_ref, x_ref, w_ref, nw_ref, o_ref):
    o_ref[...] = x_ref[...] + _rms(_dot(y_ref[...], w_ref[...]), nw_ref[...])


def _proj_norm_res(y, x, w, nw, tm):
    t = x.shape[0]
    return pl.pallas_call(
        _proj_norm_res_kernel,
        grid=(t // tm,),
        in_specs=[pl.BlockSpec((tm, D_MODEL), lambda i: (i, 0)),
                  pl.BlockSpec((tm, D_MODEL), lambda i: (i, 0)),
                  pl.BlockSpec((D_MODEL, D_MODEL), lambda i: (0, 0)),
                  pl.BlockSpec((1, D_MODEL), lambda i: (0, 0))],
        out_specs=pl.BlockSpec((tm, D_MODEL), lambda i: (i, 0)),
        out_shape=jax.ShapeDtypeStruct((t, D_MODEL), F32),
        compiler_params=_params("parallel"),
        name="out_proj",
    )(y, x, w, nw)


def _ffn_dense_kernel(x_ref, npre_ref, npost_ref, wg_ref, wu_ref, wd_ref, o_ref, h_ref, acc_ref):
    j = pl.program_id(1)

    @pl.when(j == 0)
    def _():
        h_ref[...] = _rms(x_ref[...], npre_ref[...]).astype(BF16)
        acc_ref[...] = jnp.zeros(acc_ref.shape, F32)

    h = h_ref[...]
    gate = _dot(h, wg_ref[...].astype(BF16))
    up = _dot(h, wu_ref[...].astype(BF16))
    act = (gate * jax.nn.sigmoid(gate) * up).astype(BF16)
    acc_ref[...] += _dot(act, wd_ref[...].astype(BF16))

    @pl.when(j == pl.num_programs(1) - 1)
    def _():
        o_ref[...] = x_ref[...] + _rms(acc_ref[...], npost_ref[...])


def _ffn_dense(x, npre, npost, wg, wu, wd, tm):
    t = x.shape[0]
    return pl.pallas_call(
        _ffn_dense_kernel,
        grid=(t // tm, D_FF // TF_DENSE),
        in_specs=[pl.BlockSpec((tm, D_MODEL), lambda i, j: (i, 0)),
                  pl.BlockSpec((1, D_MODEL), lambda i, j: (0, 0)),
                  pl.BlockSpec((1, D_MODEL), lambda i, j: (0, 0)),
                  pl.BlockSpec((D_MODEL, TF_DENSE), lambda i, j: (0, j)),
                  pl.BlockSpec((D_MODEL, TF_DENSE), lambda i, j: (0, j)),
                  pl.BlockSpec((TF_DENSE, D_MODEL), lambda i, j: (j, 0))],
        out_specs=pl.BlockSpec((tm, D_MODEL), lambda i, j: (i, 0)),
        out_shape=jax.ShapeDtypeStruct((t, D_MODEL), F32),
        scratch_shapes=[pltpu.VMEM((tm, D_MODEL), BF16), pltpu.VMEM((tm, D_MODEL), F32)],
        compiler_params=_params("parallel", "arbitrary"),
        name="ffn_dense",
    )(x, npre, npost, wg, wu, wd)


def _router_gates(h, rw_ref, rb_ref):
    shape = (h.shape[0], LANES)
    lane = lax.broadcasted_iota(jnp.int32, shape, 1)
    lane_f = lane.astype(F32)
    logits = jnp.where(lane < N_EXPERTS, _dot(h, rw_ref[...]) + rb_ref[...], -jnp.inf)
    e = jnp.exp(logits - jnp.max(logits, axis=-1, keepdims=True))
    probs = e / jnp.sum(e, axis=-1, keepdims=True)
    p1 = jnp.max(probs, axis=-1, keepdims=True)
    i1 = jnp.min(jnp.where(probs == p1, lane_f, float(LANES)), axis=-1, keepdims=True)
    rest = jnp.where(lane_f == i1, -1.0, probs)
    p2 = jnp.max(rest, axis=-1, keepdims=True)
    i2 = jnp.min(jnp.where(rest == p2, lane_f, float(LANES)), axis=-1, keepdims=True)
    total = p1 + p2
    return jnp.where(lane_f == i1, p1 / total, 0.0) + jnp.where(lane_f == i2, p2 / total, 0.0)


def _ffn_moe_kernel(x_ref, npre_ref, npost_ref, rw_ref, rb_ref, wg_ref, wu_ref, wd_ref,
                    o_ref, h_ref, acc_ref, gates_ref):
    e = pl.program_id(1)

    @pl.when(e == 0)
    def _():
        h_ref[...] = _rms(x_ref[...], npre_ref[...]).astype(BF16)
        acc_ref[...] = jnp.zeros(acc_ref.shape, F32)
        gates_ref[...] = _router_gates(h_ref[...], rw_ref, rb_ref)

    h = h_ref[...]
    lane = lax.broadcasted_iota(jnp.int32, gates_ref.shape, 1)
    gate_col = jnp.sum(jnp.where(lane == e, gates_ref[...], 0.0), axis=-1, keepdims=True)
    y = None
    for f0, fw in FF_EXPERT_CHUNKS:
        gate = _dot(h, wg_ref[:, f0:f0 + fw])
        up = _dot(h, wu_ref[:, f0:f0 + fw])
        act = (gate * jax.nn.sigmoid(gate) * up).astype(BF16)
        part = _dot(act, wd_ref[f0:f0 + fw, :])
        y = part if y is None else y + part
    acc_ref[...] += gate_col * y

    @pl.when(e == pl.num_programs(1) - 1)
    def _():
        o_ref[...] = x_ref[...] + _rms(acc_ref[...], npost_ref[...])


def _ffn_moe(x, npre, npost, rw, rb, wg, wu, wd, tm):
    t = x.shape[0]
    return pl.pallas_call(
        _ffn_moe_kernel,
        grid=(t // tm, N_EXPERTS),
        in_specs=[pl.BlockSpec((tm, D_MODEL), lambda i, e: (i, 0)),
                  pl.BlockSpec((1, D_MODEL), lambda i, e: (0, 0)),
                  pl.BlockSpec((1, D_MODEL), lambda i, e: (0, 0)),
                  pl.BlockSpec((D_MODEL, LANES), lambda i, e: (0, 0)),
                  pl.BlockSpec((1, LANES), lambda i, e: (0, 0)),
                  pl.BlockSpec((None, D_MODEL, D_FF_EXPERT), lambda i, e: (e, 0, 0)),
                  pl.BlockSpec((None, D_MODEL, D_FF_EXPERT), lambda i, e: (e, 0, 0)),
                  pl.BlockSpec((None, D_FF_EXPERT, D_MODEL), lambda i, e: (e, 0, 0))],
        out_specs=pl.BlockSpec((tm, D_MODEL), lambda i, e: (i, 0)),
        out_shape=jax.ShapeDtypeStruct((t, D_MODEL), F32),
        scratch_shapes=[pltpu.VMEM((tm, D_MODEL), BF16), pltpu.VMEM((tm, D_MODEL), F32),
                        pltpu.VMEM((tm, LANES), F32)],
        compiler_params=_params("parallel", "arbitrary"),
        name="ffn_moe",
    )(x, npre, npost, rw, rb, wg, wu, wd)


def _ple_kernel(x_ref, p_ref, nw_ref, wg_ref, wp_ref, o_ref):
    x = x_ref[...]
    gate = jax.nn.sigmoid(_dot(_rms(x, nw_ref[...]).astype(BF16), wg_ref[...]))
    o_ref[...] = x + gate * _dot(p_ref[...].astype(BF16), wp_ref[...])


def _ple(x, p, nw, wg, wp, tm):
    t = x.shape[0]
    return pl.pallas_call(
        _ple_kernel,
        grid=(t // tm,),
        in_specs=[pl.BlockSpec((tm, D_MODEL), lambda i: (i, 0)),
                  pl.BlockSpec((tm, PLE_DIM), lambda i: (i, 0)),
                  pl.BlockSpec((1, D_MODEL), lambda i: (0, 0)),
                  pl.BlockSpec((D_MODEL, D_MODEL), lambda i: (0, 0)),
                  pl.BlockSpec((PLE_DIM, D_MODEL), lambda i: (0, 0))],
        out_specs=pl.BlockSpec((tm, D_MODEL), lambda i: (i, 0)),
        out_shape=jax.ShapeDtypeStruct((t, D_MODEL), F32),
        compiler_params=_params("parallel"),
        name="ple",
    )(x, p, nw, wg, wp)


def _pad_lanes(a, width=LANES):
    return jnp.pad(a, [(0, 0)] * (a.ndim - 1) + [(0, width - a.shape[-1])])


def _permute_w_in(w):
    main = w[:, 0:2304]
    ig = w[:, 2304:2308]
    fg = w[:, 2308:2312]
    gu_gv = w[:, 2312:2824]
    return jnp.concatenate([main, gu_gv, _pad_lanes(ig), _pad_lanes(fg)], axis=1).astype(BF16)


def _block_diag(blocks):
    g, d, _ = blocks.shape
    out = jnp.zeros((g * d, g * d), blocks.dtype)
    for i in range(g):
        out = out.at[i * d:(i + 1) * d, i * d:(i + 1) * d].set(blocks[i])
    return out


def _row(a):
    return a.reshape(1, -1).astype(F32)


def kernel(x_prompt, x_sample, state_pool, state_mlstm_C, state_mlstm_n, state_mlstm_m, p_prompt, p_sample,
           norm_mix_pre, norm_mix_post, norm_ffn_pre, norm_ffn_post, norm_ple, w_in, pool_w, pool_scale,
           mlstm_b_i, mlstm_b_f, mlstm_norm_w, gmlp_norm_w, gmlp_ws, gmlp_bs, w_out,
           ffn_w_gate, ffn_w_up, ffn_w_down, moe_router_w, moe_router_b, moe_w_gate, moe_w_up, moe_w_down,
           ple_w_gate, ple_w_proj):
    batch, seq, _ = x_prompt.shape
    nseq = x_sample.shape[0]
    xp = x_prompt.reshape(batch * seq, D_MODEL)
    xs = x_sample.reshape(nseq, D_MODEL)
    gmean = _block_diag(jnp.full((GMLP_GROUPS, GMLP_GROUP_DIM, GMLP_GROUP_DIM), 1.0 / GMLP_GROUP_DIM, BF16))

    pools_p, cs_p, ns_p, ms_p = [], [], [], []
    pools_s, cs_s, ns_s, ms_s, gvs_s = [], [], [], [], []
    for i in range(DEPTH):
        w_in_p = _permute_w_in(w_in[i])
        w_out_b = w_out[i].astype(BF16)
        poolw = _block_diag(pool_w[i]).astype(BF16)
        shared = [poolw, _row(pool_scale[i]), _pad_lanes(_row(mlstm_b_i[i])), _pad_lanes(_row(mlstm_b_f[i])),
                  _row(mlstm_norm_w[i]), _row(gmlp_norm_w[i])]
        gbs_full = jnp.repeat(gmlp_bs[i].T, GMLP_GROUP_DIM, axis=1)
        consts_p = shared + [gmlp_ws[i], gbs_full, gmean]
        gw0 = jnp.repeat(gmlp_ws[i][:, 0, 0], GMLP_GROUP_DIM).reshape(1, GMLP_WIDTH)
        consts_s = shared + [gw0, gbs_full[0:1, :], gmean]
        ple_g = ple_w_gate[i].astype(BF16)
        ple_p = ple_w_proj[i].astype(BF16)
        j = i // 2
        if i % 2 == 1:
            rw = _pad_lanes(moe_router_w[j]).astype(BF16)
            rb = _pad_lanes(_row(moe_router_b[j]))
            moe_g, moe_u, moe_d = (moe_w_gate[j].astype(BF16), moe_w_up[j].astype(BF16),
                                   moe_w_down[j].astype(BF16))

        z = _norm_matmul(xp, _row(norm_mix_pre[i]), w_in_p, TM_PROMPT)
        y, cn_new, m_new = _mixer_prompt(z, consts_p, batch, seq)
        pools_p.append(z.reshape(batch, seq, Z_WIDTH)[:, seq - POOL_STATE:, 0:POOL_WIDTH])
        cs_p.append(cn_new[..., 0:MLSTM_HEAD_DIM])
        ns_p.append(cn_new[..., MLSTM_HEAD_DIM])
        ms_p.append(m_new[:, 0, 0:MLSTM_HEADS])
        xp = _proj_norm_res(y.reshape(batch * seq, D_MODEL), xp, w_out_b, _row(norm_mix_post[i]), TM_PROMPT)
        if i % 2 == 0:
            xp = _ffn_dense(xp, _row(norm_ffn_pre[i]), _row(norm_ffn_post[i]),
                            ffn_w_gate[j], ffn_w_up[j], ffn_w_down[j], TM_FFN)
        else:
            xp = _ffn_moe(xp, _row(norm_ffn_pre[i]), _row(norm_ffn_post[i]), rw, rb, moe_g, moe_u, moe_d, TM_MOE)
        xp = _ple(xp, p_prompt[i].reshape(batch * seq, PLE_DIM), _row(norm_ple[i]), ple_g, ple_p, TM_PROMPT)

        z = _norm_matmul(xs, _row(norm_mix_pre[i]), w_in_p, nseq)
        sp_t = jnp.transpose(state_pool[i], (1, 0, 2))
        y, c_new, n_new, m_new, gv = _mixer_sample(z, sp_t, state_mlstm_C[i],
                                                   state_mlstm_n[i].reshape(nseq, MLSTM_WIDTH),
                                                   _pad_lanes(state_mlstm_m[i]), consts_s)
        pools_s.append(jnp.concatenate([state_pool[i][:, 1:], z[:, None, 0:POOL_WIDTH]], axis=1))
        cs_s.append(c_new)
        ns_s.append(n_new.reshape(nseq, MLSTM_HEADS, MLSTM_HEAD_DIM))
        ms_s.append(m_new[:, 0:MLSTM_HEADS])
        gvs_s.append(gv[:, None, :])
        xs = _proj_norm_res(y, xs, w_out_b, _row(norm_mix_post[i]), nseq)
        if i % 2 == 0:
            xs = _ffn_dense(xs, _row(norm_ffn_pre[i]), _row(norm_ffn_post[i]),
                            ffn_w_gate[j], ffn_w_up[j], ffn_w_down[j], nseq)
        else:
            xs = _ffn_moe(xs, _row(norm_ffn_pre[i]), _row(norm_ffn_post[i]), rw, rb, moe_g, moe_u, moe_d, nseq)
        xs = _ple(xs, p_sample[i].reshape(nseq, PLE_DIM), _row(norm_ple[i]), ple_g, ple_p, nseq)

    return (xp.reshape(batch, seq, D_MODEL), xs.reshape(nseq, 1, D_MODEL),
            jnp.stack(pools_p), jnp.stack(cs_p), jnp.stack(ns_p), jnp.stack(ms_p),
            jnp.stack(pools_s), jnp.stack(cs_s), jnp.stack(ns_s), jnp.stack(ms_s), jnp.stack(gvs_s))
```

```python
import functools

import jax
import jax.numpy as jnp
from jax import lax
from jax.experimental import pallas as pl
from jax.experimental.pallas import tpu as pltpu

F32 = jnp.float32
BF16 = jnp.bfloat16

D_MODEL = 1024
DEPTH = 2
POOL_WIDTH = 256
POOL_WINDOWS = (2, 4, 8, 16)
POOL_GROUP_DIM = 64
POOL_STATE = 15
MLSTM_WIDTH = 512
MLSTM_HEADS = 4
MLSTM_HEAD_DIM = 128
CHUNK = 128
GMLP_WIDTH = 256
GMLP_GROUPS = 4
GMLP_GROUP_DIM = 64
D_FF = 2816
N_EXPERTS = 8
D_FF_EXPERT = 1408
PLE_DIM = 256
RMS_EPS = 1e-6
PAST_LEN = 16384

LANES = 128
SUBLANES = 8
VMEM_LIMIT = 48 * 1024 * 1024

Z_POOL = 0
Z_Q = 256
Z_K = 768
Z_V = 1280
Z_O = 1792
Z_GU = 2304
Z_GV = 2560
Z_IG = 2816
Z_FG = 2944
Z_WIDTH = 3072
Z_CHUNK = 512

TM_PROMPT = 512
TM_FFN = 1024
TF_DENSE = 256
TM_MOE = 512
FF_EXPERT_CHUNKS = ((0, 512), (512, 512), (1024, 384))
SAMPLE_BLOCK = 8
PROMPT_SEQ_PER_STEP = 2


def _params(*semantics):
    return pltpu.CompilerParams(dimension_semantics=semantics, vmem_limit_bytes=VMEM_LIMIT)


def _rms(x, w):
    return x * lax.rsqrt(jnp.mean(x * x, axis=-1, keepdims=True) + RMS_EPS) * w


def _log_sigmoid(x):
    return jnp.minimum(x, 0.0) - jnp.log1p(jnp.exp(-jnp.abs(x)))


def _dot(a, b):
    return jnp.dot(a, b, preferred_element_type=F32)


def _split3(x):
    hi = x.astype(BF16)
    rest = x - hi.astype(F32)
    mid = rest.astype(BF16)
    lo = (rest - mid.astype(F32)).astype(BF16)
    return hi, mid, lo


def _norm_matmul_kernel(x_ref, nw_ref, w_ref, o_ref, h_ref):
    h_ref[...] = _rms(x_ref[...], nw_ref[...]).astype(BF16)
    for n0 in range(0, Z_WIDTH, Z_CHUNK):
        o_ref[:, n0:n0 + Z_CHUNK] = _dot(h_ref[...], w_ref[:, n0:n0 + Z_CHUNK])


def _norm_matmul(x, nw, w, tm):
    t = x.shape[0]
    return pl.pallas_call(
        _norm_matmul_kernel,
        grid=(t // tm,),
        in_specs=[pl.BlockSpec((tm, D_MODEL), lambda i: (i, 0)),
                  pl.BlockSpec((1, D_MODEL), lambda i: (0, 0)),
                  pl.BlockSpec((D_MODEL, Z_WIDTH), lambda i: (0, 0))],
        out_specs=pl.BlockSpec((tm, Z_WIDTH), lambda i: (i, 0)),
        out_shape=jax.ShapeDtypeStruct((t, Z_WIDTH), F32),
        scratch_shapes=[pltpu.VMEM((tm, D_MODEL), BF16)],
        compiler_params=_params("parallel"),
        name="norm_in_proj",
    )(x, nw, w)


def _group_rms(v, gmean, w):
    hi, mid, lo = _split3(v * v)
    ms = _dot(hi, gmean) + _dot(mid, gmean) + _dot(lo, gmean)
    return v * lax.rsqrt(ms + RMS_EPS) * w


def _pool_tile(ext_ref, u_tile, col0, w_lo, w_hi, pos):
    acc = u_tile
    sums = {}
    for shift in range(1, w_hi):
        acc = acc + ext_ref[pl.ds(16 - shift, CHUNK), col0:col0 + LANES]
        if shift + 1 in (w_lo, w_hi):
            sums[shift + 1] = acc
    cnt_lo = jnp.minimum(w_lo, pos + 1).astype(F32)
    cnt_hi = jnp.minimum(w_hi, pos + 1).astype(F32)
    lane = lax.broadcasted_iota(jnp.int32, (CHUNK, LANES), 1)
    return jnp.where(lane < POOL_GROUP_DIM, sums[w_lo] / cnt_lo, sums[w_hi] / cnt_hi) - u_tile


def _mixer_prompt_kernel(z_ref, *refs):
    consts = refs[:9]
    y_ref, cn_ref, m_ref, ext_ref = refs[9:]

    @pl.when(pl.program_id(1) == 0)
    def _():
        ext_ref[:, 0:16, :] = jnp.zeros((PROMPT_SEQ_PER_STEP, 16, POOL_WIDTH), F32)
        cn_ref[...] = jnp.zeros(cn_ref.shape, F32)
        m_ref[...] = jnp.zeros(m_ref.shape, F32)

    for i in range(PROMPT_SEQ_PER_STEP):
        _mixer_prompt_body(z_ref.at[i], *consts, y_ref.at[i], cn_ref.at[i], m_ref.at[i], ext_ref.at[i])


def _mixer_prompt_body(z_ref, poolw_ref, pscale_ref, bi_ref, bf_ref, mnorm_ref, gnorm_ref,
                       gws_ref, gbs_ref, gmean_ref,
                       y_ref, cn_ref, m_ref, ext_ref):
    chunk = pl.program_id(1)
    row = lax.broadcasted_iota(jnp.int32, (CHUNK, CHUNK), 0)
    col = lax.broadcasted_iota(jnp.int32, (CHUNK, CHUNK), 1)
    causal = col <= row
    lane = col

    ext_ref[16:16 + CHUNK, :] = z_ref[:, Z_POOL:Z_POOL + POOL_WIDTH]
    pos = chunk * CHUNK + lax.broadcasted_iota(jnp.int32, (CHUNK, 1), 0)
    pooled = []
    for tile in range(2):
        col0 = tile * LANES
        u_tile = z_ref[:, Z_POOL + col0:Z_POOL + col0 + LANES]
        pooled.append(_pool_tile(ext_ref, u_tile, col0, POOL_WINDOWS[2 * tile],
                                 POOL_WINDOWS[2 * tile + 1], pos))
    pooled = jnp.concatenate(pooled, axis=1).astype(BF16)
    y_pool = _dot(pooled, poolw_ref[...]) * pscale_ref[...]
    y_ref[:, 0:POOL_WIDTH] = y_pool.astype(BF16)
    ext_ref[0:16, :] = ext_ref[CHUNK:CHUNK + 16, :]

    vn = _group_rms(z_ref[:, Z_GV:Z_GV + GMLP_WIDTH], gmean_ref[...], gnorm_ref[...]).astype(BF16)
    for tile in range(2):
        col0 = tile * LANES
        vt = vn[:, col0:col0 + LANES]
        w_a = jnp.where(causal, gws_ref[2 * tile], 0.0).astype(BF16)
        w_b = jnp.where(causal, gws_ref[2 * tile + 1], 0.0).astype(BF16)
        mixed = jnp.where(lane < GMLP_GROUP_DIM, _dot(w_a, vt), _dot(w_b, vt))
        gu = z_ref[:, Z_GU + col0:Z_GU + col0 + LANES]
        y_g = gu * (mixed + gbs_ref[:, col0:col0 + LANES])
        y_ref[:, 768 + col0:768 + col0 + LANES] = y_g.astype(BF16)

    ig = z_ref[:, Z_IG:Z_IG + LANES] + bi_ref[...]
    lf = _log_sigmoid(z_ref[:, Z_FG:Z_FG + LANES] + bf_ref[...])
    tri = jnp.where(causal, 1.0, 0.0).astype(BF16)
    lf_hi, lf_mid, lf_lo = _split3(lf)
    b = _dot(tri, lf_hi) + _dot(tri, lf_mid) + _dot(tri, lf_lo)
    m_prev = m_ref[...]
    g = b + m_prev
    r_t = jnp.transpose(ig - b)
    b_last = b[CHUNK - 1:CHUNK, :]
    ones_col = jnp.where(lane == 0, 1.0, 0.0).astype(BF16)
    m_new_row = m_prev
    for h in range(MLSTM_HEADS):
        c0 = h * MLSTM_HEAD_DIM
        q = z_ref[:, Z_Q + c0:Z_Q + c0 + MLSTM_HEAD_DIM].astype(BF16)
        k = z_ref[:, Z_K + c0:Z_K + c0 + MLSTM_HEAD_DIM] * (MLSTM_HEAD_DIM ** -0.5)
        v = z_ref[:, Z_V + c0:Z_V + c0 + MLSTM_HEAD_DIM].astype(BF16)
        o = z_ref[:, Z_O + c0:Z_O + c0 + MLSTM_HEAD_DIM]
        b_col = b[:, h:h + 1]
        dmat = jnp.where(causal, b_col + r_t[h:h + 1, :], -jnp.inf)
        g_col = g[:, h:h + 1]
        m_t = jnp.maximum(g_col, jnp.max(dmat, axis=1, keepdims=True))
        scores = lax.dot_general(q, k.astype(BF16), (((1,), (1,)), ((), ())),
                                 preferred_element_type=F32)
        wts = jnp.exp(dmat - m_t) * scores
        inter = jnp.exp(g_col - m_t)
        cn_h = cn_ref[h]
        q_cn = _dot(q, cn_h.astype(BF16))
        num = inter * q_cn[:, 0:MLSTM_HEAD_DIM] + _dot(wts.astype(BF16), v)
        den = inter * q_cn[:, MLSTM_HEAD_DIM:MLSTM_HEAD_DIM + 1] + jnp.sum(wts, axis=1, keepdims=True)
        hid = num / jnp.maximum(jnp.abs(den), jnp.exp(-m_t))
        hid = _rms(hid, mnorm_ref[:, c0:c0 + MLSTM_HEAD_DIM])
        y_ref[:, POOL_WIDTH + c0:POOL_WIDTH + c0 + MLSTM_HEAD_DIM] = (jax.nn.sigmoid(o) * hid).astype(BF16)
        m_new = m_t[CHUNK - 1:CHUNK, :]
        bl = b_last[:, h:h + 1]
        decay = jnp.exp(bl + m_prev[:, h:h + 1] - m_new)
        w_s = jnp.exp(bl - b_col + ig[:, h:h + 1] - m_new)
        kw = (k * w_s).astype(BF16)
        v_ext = jnp.concatenate([v, ones_col], axis=1)
        cn_ref[h] = decay * cn_h + lax.dot_general(kw, v_ext, (((0,), (0,)), ((), ())),
                                                   preferred_element_type=F32)
        m_new_row = jnp.where(lane[0:1, :] == h, m_new, m_new_row)
    m_ref[...] = m_new_row


def _mixer_prompt(z, consts, batch, seq):
    nc = seq // CHUNK
    hd = MLSTM_HEAD_DIM
    z3 = z.reshape(batch, seq, Z_WIDTH)
    ns = PROMPT_SEQ_PER_STEP
    const_specs = [pl.BlockSpec(a.shape, lambda b, c, nd=a.ndim: (0,) * nd) for a in consts]
    return pl.pallas_call(
        _mixer_prompt_kernel,
        grid=(batch // ns, nc),
        in_specs=[pl.BlockSpec((ns, CHUNK, Z_WIDTH), lambda b, c: (b, c, 0))] + const_specs,
        out_specs=[pl.BlockSpec((ns, CHUNK, D_MODEL), lambda b, c: (b, c, 0)),
                   pl.BlockSpec((ns, MLSTM_HEADS, hd, 2 * hd), lambda b, c: (b, 0, 0, 0)),
                   pl.BlockSpec((ns, 1, LANES), lambda b, c: (b, 0, 0))],
        out_shape=[jax.ShapeDtypeStruct((batch, seq, D_MODEL), BF16),
                   jax.ShapeDtypeStruct((batch, MLSTM_HEADS, hd, 2 * hd), F32),
                   jax.ShapeDtypeStruct((batch, 1, LANES), F32)],
        scratch_shapes=[pltpu.VMEM((ns, 16 + CHUNK, POOL_WIDTH), F32)],
        compiler_params=_params("parallel", "arbitrary"),
        name="mixer_prompt",
    )(z3, *consts)


def _mixer_sample_kernel(z_ref, sp_ref, c_ref, n_ref, m_ref,
                         poolw_ref, pscale_ref, bi_ref, bf_ref, mnorm_ref, gnorm_ref,
                         gw0_ref, gb0_ref, gmean_ref,
                         y_ref, cn_ref, nn_ref, mn_ref, gv_ref, tk_ref):
    nb = SAMPLE_BLOCK
    hd = MLSTM_HEAD_DIM
    lane = lax.broadcasted_iota(jnp.int32, (nb, LANES), 1)
    seq_id = lax.broadcasted_iota(jnp.int32, (nb, LANES), 0)

    pooled = []
    for tile in range(2):
        col0 = tile * LANES
        u_tile = z_ref[:, Z_POOL + col0:Z_POOL + col0 + LANES]
        w_lo, w_hi = POOL_WINDOWS[2 * tile], POOL_WINDOWS[2 * tile + 1]
        acc = u_tile
        sums = {}
        for shift in range(1, w_hi):
            acc = acc + sp_ref[POOL_STATE - shift, :, col0:col0 + LANES]
            if shift + 1 in (w_lo, w_hi):
                sums[shift + 1] = acc
        pooled.append(jnp.where(lane < POOL_GROUP_DIM, sums[w_lo] / float(w_lo), sums[w_hi] / float(w_hi)) - u_tile)
    pooled = jnp.concatenate(pooled, axis=1).astype(BF16)
    y_ref[:, 0:POOL_WIDTH] = (_dot(pooled, poolw_ref[...]) * pscale_ref[...]).astype(BF16)

    vn = _group_rms(z_ref[:, Z_GV:Z_GV + GMLP_WIDTH], gmean_ref[...], gnorm_ref[...])
    gv_ref[...] = vn
    y_g = z_ref[:, Z_GU:Z_GU + GMLP_WIDTH] * (gw0_ref[...] * vn + gb0_ref[...])
    y_ref[:, 768:768 + GMLP_WIDTH] = y_g.astype(BF16)

    ig = z_ref[:, Z_IG:Z_IG + LANES] + bi_ref[...]
    lf = _log_sigmoid(z_ref[:, Z_FG:Z_FG + LANES] + bf_ref[...])
    m_prev = m_ref[...]
    g = lf + m_prev
    m_t = jnp.maximum(g, ig)
    inter = jnp.exp(g - m_t)
    e_ig = jnp.exp(ig - m_t)
    floor = jnp.exp(-m_t)
    mn_ref[...] = m_t
    tk_ref[...] = jnp.zeros((LANES, LANES), F32)
    for h in range(MLSTM_HEADS):
        tk_ref[nb * h:nb * (h + 1), :] = z_ref[:, Z_K + h * hd:Z_K + (h + 1) * hd] * (hd ** -0.5)
    k_t = jnp.transpose(tk_ref[...])
    for h in range(MLSTM_HEADS):
        c0 = h * hd
        q_h = z_ref[:, Z_Q + c0:Z_Q + c0 + hd]
        k_h = tk_ref[nb * h:nb * (h + 1), :]
        v_h = z_ref[:, Z_V + c0:Z_V + c0 + hd]
        o_h = z_ref[:, Z_O + c0:Z_O + c0 + hd]
        n_h = n_ref[:, c0:c0 + hd]
        inter_b = jnp.broadcast_to(inter[:, h:h + 1], (nb, hd))
        e_b = jnp.broadcast_to(e_ig[:, h:h + 1], (nb, hd))
        floor_b = jnp.broadcast_to(floor[:, h:h + 1], (nb, hd))
        v_w = e_b * v_h
        q_b = q_h.astype(BF16)
        q_c = jnp.zeros((nb, hd), F32)
        for s in range(nb):
            c_sh = c_ref[s, h]
            q_c = jnp.where(seq_id == s, _dot(q_b, c_sh.astype(BF16)), q_c)
            col = nb * h + s
            cn_ref[s, h] = inter_b[s:s + 1, :] * c_sh + k_t[:, col:col + 1] * v_w[s:s + 1, :]
        wts = e_b * jnp.sum(q_h * k_h, axis=1, keepdims=True)
        num = inter_b * q_c + wts * v_h
        den = inter_b * jnp.sum(q_h * n_h, axis=1, keepdims=True) + wts
        hid = num / jnp.maximum(jnp.abs(den), floor_b)
        hid = _rms(hid, mnorm_ref[:, c0:c0 + hd])
        y_ref[:, POOL_WIDTH + c0:POOL_WIDTH + c0 + hd] = (jax.nn.sigmoid(o_h) * hid).astype(BF16)
        nn_ref[:, c0:c0 + hd] = inter_b * n_h + e_b * k_h


def _mixer_sample(z, sp_t, c_state, n_state, m_pad, consts):
    nseq = z.shape[0]
    nb = SAMPLE_BLOCK
    hd = MLSTM_HEAD_DIM
    const_specs = [pl.BlockSpec(a.shape, lambda j, nd=a.ndim: (0,) * nd) for a in consts]
    return pl.pallas_call(
        _mixer_sample_kernel,
        grid=(nseq // nb,),
        in_specs=[pl.BlockSpec((nb, Z_WIDTH), lambda j: (j, 0)),
                  pl.BlockSpec((POOL_STATE, nb, POOL_WIDTH), lambda j: (0, j, 0)),
                  pl.BlockSpec((nb, MLSTM_HEADS, hd, hd), lambda j: (j, 0, 0, 0)),
                  pl.BlockSpec((nb, MLSTM_WIDTH), lambda j: (j, 0)),
                  pl.BlockSpec((nb, LANES), lambda j: (j, 0))] + const_specs,
        out_specs=[pl.BlockSpec((nb, D_MODEL), lambda j: (j, 0)),
                   pl.BlockSpec((nb, MLSTM_HEADS, hd, hd), lambda j: (j, 0, 0, 0)),
                   pl.BlockSpec((nb, MLSTM_WIDTH), lambda j: (j, 0)),
                   pl.BlockSpec((nb, LANES), lambda j: (j, 0)),
                   pl.BlockSpec((nb, GMLP_WIDTH), lambda j: (j, 0))],
        out_shape=[jax.ShapeDtypeStruct((nseq, D_MODEL), BF16),
                   jax.ShapeDtypeStruct((nseq, MLSTM_HEADS, hd, hd), F32),
                   jax.ShapeDtypeStruct((nseq, MLSTM_WIDTH), F32),
                   jax.ShapeDtypeStruct((nseq, LANES), F32),
                   jax.ShapeDtypeStruct((nseq, GMLP_WIDTH), F32)],
        scratch_shapes=[pltpu.VMEM((LANES, LANES), F32)],
        compiler_params=_params("parallel"),
        name="mixer_sample",
    )(z, sp_t, c_state, n_state, m_pad, *consts)


def _proj_norm_res_kernel(y_ref, x_ref, w_ref, nw_ref, o_ref):
    o_ref[...] = x_ref[...] + _rms(_dot(y_ref[...], w_ref[...]), nw_ref[...])


def _proj_norm_res(y, x, w, nw, tm):
    t = x.shape[0]
    return pl.pallas_call(
        _proj_norm_res_kernel,
        grid=(t // tm,),
        in_specs=[pl.BlockSpec((tm, D_MODEL), lambda i: (i, 0)),
                  pl.BlockSpec((tm, D_MODEL), lambda i: (i, 0)),
                  pl.BlockSpec((D_MODEL, D_MODEL), lambda i: (0, 0)),
                  pl.BlockSpec((1, D_MODEL), lambda i: (0, 0))],
        out_specs=pl.BlockSpec((tm, D_MODEL), lambda i: (i, 0)),
        out_shape=jax.ShapeDtypeStruct((t, D_MODEL), F32),
        compiler_params=_params("parallel"),
        name="out_proj",
    )(y, x, w, nw)


def _ffn_dense_kernel(x_ref, npre_ref, npost_ref, wg_ref, wu_ref, wd_ref, o_ref, h_ref, acc_ref):
    j = pl.program_id(1)

    @pl.when(j == 0)
    def _():
        h_ref[...] = _rms(x_ref[...], npre_ref[...]).astype(BF16)
        acc_ref[...] = jnp.zeros(acc_ref.shape, F32)

    h = h_ref[...]
    gate = _dot(h, wg_ref[...].astype(BF16))
    up = _dot(h, wu_ref[...].astype(BF16))
    act = (gate * jax.nn.sigmoid(gate) * up).astype(BF16)
    acc_ref[...] += _dot(act, wd_ref[...].astype(BF16))

    @pl.when(j == pl.num_programs(1) - 1)
    def _():
        o_ref[...] = x_ref[...] + _rms(acc_ref[...], npost_ref[...])


def _ffn_dense(x, npre, npost, wg, wu, wd, tm):
    t = x.shape[0]
    return pl.pallas_call(
        _ffn_dense_kernel,
        grid=(t // tm, D_FF // TF_DENSE),
        in_specs=[pl.BlockSpec((tm, D_MODEL), lambda i, j: (i, 0)),
                  pl.BlockSpec((1, D_MODEL), lambda i, j: (0, 0)),
                  pl.BlockSpec((1, D_MODEL), lambda i, j: (0, 0)),
                  pl.BlockSpec((D_MODEL, TF_DENSE), lambda i, j: (0, j)),
                  pl.BlockSpec((D_MODEL, TF_DENSE), lambda i, j: (0, j)),
                  pl.BlockSpec((TF_DENSE, D_MODEL), lambda i, j: (j, 0))],
        out_specs=pl.BlockSpec((tm, D_MODEL), lambda i, j: (i, 0)),
        out_shape=jax.ShapeDtypeStruct((t, D_MODEL), F32),
        scratch_shapes=[pltpu.VMEM((tm, D_MODEL), BF16), pltpu.VMEM((tm, D_MODEL), F32)],
        compiler_params=_params("parallel", "arbitrary"),
        name="ffn_dense",
    )(x, npre, npost, wg, wu, wd)


def _router_gates(h, rw_ref, rb_ref):
    shape = (h.shape[0], LANES)
    lane = lax.broadcasted_iota(jnp.int32, shape, 1)
    lane_f = lane.astype(F32)
    logits = jnp.where(lane < N_EXPERTS, _dot(h, rw_ref[...]) + rb_ref[...], -jnp.inf)
    e = jnp.exp(logits - jnp.max(logits, axis=-1, keepdims=True))
    probs = e / jnp.sum(e, axis=-1, keepdims=True)
    p1 = jnp.max(probs, axis=-1, keepdims=True)
    i1 = jnp.min(jnp.where(probs == p1, lane_f, float(LANES)), axis=-1, keepdims=True)
    rest = jnp.where(lane_f == i1, -1.0, probs)
    p2 = jnp.max(rest, axis=-1, keepdims=True)
    i2 = jnp.min(jnp.where(rest == p2, lane_f, float(LANES)), axis=-1, keepdims=True)
    total = p1 + p2
    return jnp.where(lane_f == i1, p1 / total, 0.0) + jnp.where(lane_f == i2, p2 / total, 0.0)


def _ffn_moe_kernel(x_ref, npre_ref, npost_ref, rw_ref, rb_ref, wg_ref, wu_ref, wd_ref,
                    o_ref, h_ref, acc_ref, gates_ref):
    e = pl.program_id(1)

    @pl.when(e == 0)
    def _():
        h_ref[...] = _rms(x_ref[...], npre_ref[...]).astype(BF16)
        acc_ref[...] = jnp.zeros(acc_ref.shape, F32)
        gates_ref[...] = _router_gates(h_ref[...], rw_ref, rb_ref)

    h = h_ref[...]
    lane = lax.broadcasted_iota(jnp.int32, gates_ref.shape, 1)
    gate_col = jnp.sum(jnp.where(lane == e, gates_ref[...], 0.0), axis=-1, keepdims=True)
    y = None
    for f0, fw in FF_EXPERT_CHUNKS:
        gate = _dot(h, wg_ref[:, f0:f0 + fw])
        up = _dot(h, wu_ref[:, f0:f0 + fw])
        act = (gate * jax.nn.sigmoid(gate) * up).astype(BF16)
        part = _dot(act, wd_ref[f0:f0 + fw, :])
        y = part if y is None else y + part
    acc_ref[...] += gate_col * y

    @pl.when(e == pl.num_programs(1) - 1)
    def _():
        o_ref[...] = x_ref[...] + _rms(acc_ref[...], npost_ref[...])


def _ffn_moe(x, npre, npost, rw, rb, wg, wu, wd, tm):
    t = x.shape[0]
    return pl.pallas_call(
        _ffn_moe_kernel,
        grid=(t // tm, N_EXPERTS),
        in_specs=[pl.BlockSpec((tm, D_MODEL), lambda i, e: (i, 0)),
                  pl.BlockSpec((1, D_MODEL), lambda i, e: (0, 0)),
                  pl.BlockSpec((1, D_MODEL), lambda i, e: (0, 0)),
                  pl.BlockSpec((D_MODEL, LANES), lambda i, e: (0, 0)),
                  pl.BlockSpec((1, LANES), lambda i, e: (0, 0)),
                  pl.BlockSpec((None, D_MODEL, D_FF_EXPERT), lambda i, e: (e, 0, 0)),
                  pl.BlockSpec((None, D_MODEL, D_FF_EXPERT), lambda i, e: (e, 0, 0)),
                  pl.BlockSpec((None, D_FF_EXPERT, D_MODEL), lambda i, e: (e, 0, 0))],
        out_specs=pl.BlockSpec((tm, D_MODEL), lambda i, e: (i, 0)),
        out_shape=jax.ShapeDtypeStruct((t, D_MODEL), F32),
        scratch_shapes=[pltpu.VMEM((tm, D_MODEL), BF16), pltpu.VMEM((tm, D_MODEL), F32),
                        pltpu.VMEM((tm, LANES), F32)],
        compiler_params=_params("parallel", "arbitrary"),
        name="ffn_moe",
    )(x, npre, npost, rw, rb, wg, wu, wd)


ROUTE_TILE = 256
ROW_ALIGN = 16
ROUTE_SEG = ROUTE_TILE
ROUTE_PACK = 2 * ROUTE_TILE + N_EXPERTS * ROW_ALIGN
ROUTE_W = D_MODEL + 3 * LANES
ROUTE_BLOCK = 512


def _route_region(n_tokens):
    rows = n_tokens + (n_tokens // ROUTE_TILE) * (ROW_ALIGN - 1) + ROUTE_SEG + ROUTE_BLOCK
    return -(-rows // ROUTE_BLOCK) * ROUTE_BLOCK


def _lane_scalar(row, lane, e):
    return jnp.sum(jnp.where(lane == e, row, 0.0)).astype(jnp.int32)


def _route_kernel(x_ref, npre_ref, rw_ref, rb_ref, slot_ref, stats_ref, srt_hbm,
                  stage_ref, runv_ref, run_ref, sem, *, region):
    i = pl.program_id(0)
    last = pl.num_programs(0) - 1
    cur = i % 2

    @pl.when(i == 0)
    def _():
        runv_ref[...] = jnp.zeros(runv_ref.shape, F32)
        stage_ref[:, ROUTE_PACK:, :] = jnp.zeros((2, ROUTE_SEG, ROUTE_W), BF16)
        for e in range(N_EXPERTS):
            run_ref[e] = 0

    h = _rms(x_ref[...], npre_ref[...]).astype(BF16)
    gates = _router_gates(h, rw_ref, rb_ref)
    sel = gates > 0.0
    ones = jnp.where(sel, 1.0, 0.0)
    trow = lax.broadcasted_iota(jnp.int32, (ROUTE_TILE, ROUTE_TILE), 0)
    tcol = lax.broadcasted_iota(jnp.int32, (ROUTE_TILE, ROUTE_TILE), 1)
    before = jnp.where(tcol < trow, 1.0, 0.0).astype(BF16)
    rank = _dot(before, ones.astype(BF16))
    cnt = jnp.sum(ones, axis=0, keepdims=True)
    cnt_pad = jnp.floor((cnt + (ROW_ALIGN - 1)) * (1.0 / ROW_ALIGN)) * ROW_ALIGN
    lrow = lax.broadcasted_iota(jnp.int32, (LANES, LANES), 0)
    lcol = lax.broadcasted_iota(jnp.int32, (LANES, LANES), 1)
    lower = jnp.where(lrow < lcol, 1.0, 0.0).astype(BF16)
    off = _dot(jnp.broadcast_to(cnt_pad, (SUBLANES, LANES)).astype(BF16), lower)[0:1, :]
    slot_ref[...] = jnp.where(sel, rank, -1.0)
    stats_ref[...] = jnp.zeros(stats_ref.shape, F32)
    stats_ref[0:1, :] = runv_ref[...]
    stats_ref[1:2, :] = cnt
    runv_ref[...] = runv_ref[...] + cnt_pad

    pos = jnp.where(sel, rank + off, -1.0)
    pos_t = jnp.concatenate([jnp.transpose(pos[0:LANES, :]), jnp.transpose(pos[LANES:2 * LANES, :])], axis=1)
    pos_a = jnp.max(pos_t, axis=0, keepdims=True)
    pos_b = jnp.max(jnp.where(pos_t == pos_a, -1.0, pos_t), axis=0, keepdims=True)
    prow = lax.broadcasted_iota(jnp.int32, (ROUTE_PACK, ROUTE_TILE), 0).astype(F32)
    perm = jnp.where((prow == pos_a) | (prow == pos_b), 1.0, 0.0).astype(BF16)
    g_hi, g_mid, g_lo = _split3(gates)
    rows = _dot(perm, jnp.concatenate([h, g_hi, g_mid, g_lo], axis=1))
    stage_ref[cur, 0:ROUTE_PACK, :] = rows.astype(BF16)

    lane = lax.broadcasted_iota(jnp.int32, (1, LANES), 1)

    def segment_copy(e, src_row, dst_row, slot):
        return pltpu.make_async_copy(
            stage_ref.at[slot, pl.ds(pl.multiple_of(src_row, ROW_ALIGN), ROUTE_SEG), :],
            srt_hbm.at[pl.ds(pl.multiple_of(dst_row, ROW_ALIGN), ROUTE_SEG), :],
            sem.at[e])

    @pl.when(i > 0)
    def _():
        for e in range(N_EXPERTS):
            segment_copy(e, 0, 0, 1 - cur).wait()

    for e in range(N_EXPERTS):
        segment_copy(e, _lane_scalar(off, lane, e), e * region + run_ref[e], cur).start()
        run_ref[e] = run_ref[e] + _lane_scalar(cnt_pad, lane, e)

    @pl.when(i == last)
    def _():
        for e in range(N_EXPERTS):
            segment_copy(e, 0, 0, cur).wait()
        stage_ref[1 - cur, 0:ROUTE_SEG, :] = jnp.zeros((ROUTE_SEG, ROUTE_W), BF16)
        for part in range(ROUTE_BLOCK // ROUTE_SEG):
            for e in range(N_EXPERTS):
                segment_copy(e, 0, e * region + run_ref[e] + part * ROUTE_SEG, 1 - cur).start()
            for e in range(N_EXPERTS):
                segment_copy(e, 0, 0, 1 - cur).wait()


def _route(x, npre, rw, rb):
    t = x.shape[0]
    nt = t // ROUTE_TILE
    region = _route_region(t)
    return pl.pallas_call(
        functools.partial(_route_kernel, region=region),
        grid=(nt,),
        in_specs=[pl.BlockSpec((ROUTE_TILE, D_MODEL), lambda i: (i, 0)),
                  pl.BlockSpec((1, D_MODEL), lambda i: (0, 0)),
                  pl.BlockSpec((D_MODEL, LANES), lambda i: (0, 0)),
                  pl.BlockSpec((1, LANES), lambda i: (0, 0))],
        out_specs=[pl.BlockSpec((ROUTE_TILE, LANES), lambda i: (i, 0)),
                   pl.BlockSpec((None, SUBLANES, LANES), lambda i: (i, 0, 0)),
                   pl.BlockSpec(memory_space=pl.ANY)],
        out_shape=[jax.ShapeDtypeStruct((t, LANES), F32),
                   jax.ShapeDtypeStruct((nt, SUBLANES, LANES), F32),
                   jax.ShapeDtypeStruct((N_EXPERTS * region, ROUTE_W), BF16)],
        scratch_shapes=[pltpu.VMEM((2, ROUTE_PACK + ROUTE_SEG, ROUTE_W), BF16),
                        pltpu.VMEM((1, LANES), F32),
                        pltpu.SMEM((N_EXPERTS,), jnp.int32),
                        pltpu.SemaphoreType.DMA((N_EXPERTS,))],
        compiler_params=_params("arbitrary"),
        name="moe_route",
    )(x, npre, rw, rb)


def _experts_kernel(blk_row_ref, blk_e_ref, n_used_ref, srt_ref, wg_ref, wu_ref, wd_ref, yhi_ref, ylo_ref):
    k = pl.program_id(0)

    @pl.when(k < n_used_ref[0])
    def _():
        h = srt_ref[:, 0:D_MODEL]
        gate3 = (srt_ref[:, D_MODEL:D_MODEL + LANES].astype(F32)
                 + srt_ref[:, D_MODEL + LANES:D_MODEL + 2 * LANES].astype(F32)
                 + srt_ref[:, D_MODEL + 2 * LANES:D_MODEL + 3 * LANES].astype(F32))
        lane = lax.broadcasted_iota(jnp.int32, gate3.shape, 1)
        gate_col = jnp.sum(jnp.where(lane == blk_e_ref[k], gate3, 0.0), axis=-1, keepdims=True)
        y = None
        for f0, fw in FF_EXPERT_CHUNKS:
            gate = _dot(h, wg_ref[:, f0:f0 + fw])
            up = _dot(h, wu_ref[:, f0:f0 + fw])
            act = (gate * jax.nn.sigmoid(gate) * up).astype(BF16)
            part = _dot(act, wd_ref[f0:f0 + fw, :])
            y = part if y is None else y + part
        y = gate_col * y
        hi = y.astype(BF16)
        yhi_ref[...] = hi
        ylo_ref[...] = (y - hi.astype(F32)).astype(BF16)


def _experts(srt, blk_row, blk_e, n_used, wg, wu, wd, n_blocks):
    rows = srt.shape[0]
    grid_spec = pltpu.PrefetchScalarGridSpec(
        num_scalar_prefetch=3,
        grid=(n_blocks,),
        in_specs=[pl.BlockSpec((ROUTE_BLOCK, ROUTE_W), lambda k, br, be, nu: (br[k], 0)),
                  pl.BlockSpec((None, D_MODEL, D_FF_EXPERT), lambda k, br, be, nu: (be[k], 0, 0)),
                  pl.BlockSpec((None, D_MODEL, D_FF_EXPERT), lambda k, br, be, nu: (be[k], 0, 0)),
                  pl.BlockSpec((None, D_FF_EXPERT, D_MODEL), lambda k, br, be, nu: (be[k], 0, 0))],
        out_specs=[pl.BlockSpec((ROUTE_BLOCK, D_MODEL), lambda k, br, be, nu: (br[k], 0)),
                   pl.BlockSpec((ROUTE_BLOCK, D_MODEL), lambda k, br, be, nu: (br[k], 0))])
    return pl.pallas_call(
        _experts_kernel,
        grid_spec=grid_spec,
        out_shape=[jax.ShapeDtypeStruct((rows, D_MODEL), BF16), jax.ShapeDtypeStruct((rows, D_MODEL), BF16)],
        compiler_params=_params("arbitrary"),
        name="moe_experts",
    )(blk_row, blk_e, n_used, srt, wg, wu, wd)


def _combine_kernel(src_row_ref, valid_ref, x_ref, slot_ref, shift_ref, p_ref, npost_ref, nple_ref,
                    wg_ref, wp_ref, yhi_hbm, ylo_hbm, o_ref, seg_hi_ref, seg_lo_ref, sem):
    i = pl.program_id(0)
    nt = pl.num_programs(0)
    cur = i % 2

    def segment_copies(tile, slot, e):
        src = pl.ds(pl.multiple_of(src_row_ref[tile * N_EXPERTS + e], ROW_ALIGN), ROUTE_SEG)
        dst = pl.ds(e * ROUTE_SEG, ROUTE_SEG)
        return (pltpu.make_async_copy(yhi_hbm.at[src, :], seg_hi_ref.at[slot, dst, :], sem.at[slot, 0, e]),
                pltpu.make_async_copy(ylo_hbm.at[src, :], seg_lo_ref.at[slot, dst, :], sem.at[slot, 1, e]))

    def fetch(tile, slot):
        for e in range(N_EXPERTS):
            @pl.when(valid_ref[tile * N_EXPERTS + e] > 0)
            def _():
                for c in segment_copies(tile, slot, e):
                    c.start()

    @pl.when(i == 0)
    def _():
        seg_hi_ref[...] = jnp.zeros(seg_hi_ref.shape, BF16)
        seg_lo_ref[...] = jnp.zeros(seg_lo_ref.shape, BF16)
        fetch(0, 0)

    @pl.when(i + 1 < nt)
    def _():
        fetch(i + 1, 1 - cur)

    for e in range(N_EXPERTS):
        @pl.when(valid_ref[i * N_EXPERTS + e] > 0)
        def _():
            for c in segment_copies(i, cur, e):
                c.wait()

    slot = slot_ref[...]
    where = jnp.where(slot >= 0.0, slot + shift_ref[...], -1.0)
    seg_lane = lax.broadcasted_iota(jnp.int32, (ROUTE_TILE, ROUTE_SEG), 1).astype(F32)
    perm = jnp.concatenate([jnp.where(where[:, e:e + 1] == seg_lane, 1.0, 0.0).astype(BF16)
                            for e in range(N_EXPERTS)], axis=1)
    y = _dot(perm, seg_hi_ref[cur]) + _dot(perm, seg_lo_ref[cur])
    x = x_ref[...] + _rms(y, npost_ref[...])
    gate = jax.nn.sigmoid(_dot(_rms(x, nple_ref[...]).astype(BF16), wg_ref[...]))
    o_ref[...] = x + gate * _dot(p_ref[...].astype(BF16), wp_ref[...])


def _combine(src_row, valid, x, slot, shift, p, npost, nple, wg, wp, yhi, ylo):
    t = x.shape[0]
    grid_spec = pltpu.PrefetchScalarGridSpec(
        num_scalar_prefetch=2,
        grid=(t // ROUTE_TILE,),
        in_specs=[pl.BlockSpec((ROUTE_TILE, D_MODEL), lambda i, sr, va: (i, 0)),
                  pl.BlockSpec((ROUTE_TILE, LANES), lambda i, sr, va: (i, 0)),
                  pl.BlockSpec((None, 1, LANES), lambda i, sr, va: (i, 0, 0)),
                  pl.BlockSpec((ROUTE_TILE, PLE_DIM), lambda i, sr, va: (i, 0)),
                  pl.BlockSpec((1, D_MODEL), lambda i, sr, va: (0, 0)),
                  pl.BlockSpec((1, D_MODEL), lambda i, sr, va: (0, 0)),
                  pl.BlockSpec((D_MODEL, D_MODEL), lambda i, sr, va: (0, 0)),
                  pl.BlockSpec((PLE_DIM, D_MODEL), lambda i, sr, va: (0, 0)),
                  pl.BlockSpec(memory_space=pl.ANY),
                  pl.BlockSpec(memory_space=pl.ANY)],
        out_specs=pl.BlockSpec((ROUTE_TILE, D_MODEL), lambda i, sr, va: (i, 0)),
        scratch_shapes=[pltpu.VMEM((2, N_EXPERTS * ROUTE_SEG, D_MODEL), BF16),
                        pltpu.VMEM((2, N_EXPERTS * ROUTE_SEG, D_MODEL), BF16),
                        pltpu.SemaphoreType.DMA((2, 2, N_EXPERTS))])
    return pl.pallas_call(
        _combine_kernel,
        grid_spec=grid_spec,
        out_shape=jax.ShapeDtypeStruct((t, D_MODEL), F32),
        compiler_params=_params("arbitrary"),
        name="moe_combine_ple",
    )(src_row, valid, x, slot, shift, p, npost, nple, wg, wp, yhi, ylo)


def _moe_routed_ple(x, p, npre, npost, nple, rw, rb, wg, wu, wd, ple_g, ple_p):
    t = x.shape[0]
    nt = t // ROUTE_TILE
    region = _route_region(t)
    slot, stats, srt = _route(x, npre, rw, rb)
    base = stats[:, 0, 0:N_EXPERTS].astype(jnp.int32)
    cnt = stats[:, 1, 0:N_EXPERTS].astype(jnp.int32)
    cnt_pad = (cnt + (ROW_ALIGN - 1)) // ROW_ALIGN * ROW_ALIGN
    total = base[-1] + cnt_pad[-1]
    nblk = (total + (ROUTE_BLOCK - 1)) // ROUTE_BLOCK
    cum = jnp.cumsum(nblk)
    n_used = cum[-1]
    max_rows = 2 * t + nt * N_EXPERTS * (ROW_ALIGN - 1)
    n_blocks = max_rows // ROUTE_BLOCK + N_EXPERTS
    kk = jnp.minimum(jnp.arange(n_blocks, dtype=jnp.int32), n_used - 1)
    blk_e = jnp.sum(kk[:, None] >= cum[None, :], axis=1).astype(jnp.int32)
    blk_row = blk_e * (region // ROUTE_BLOCK) + kk - (cum - nblk)[blk_e]
    yhi, ylo = _experts(srt, blk_row.astype(jnp.int32), blk_e, n_used.reshape(1).astype(jnp.int32),
                        wg, wu, wd, n_blocks)
    start = jnp.maximum(jnp.minimum(base, nblk[None, :] * ROUTE_BLOCK - ROUTE_SEG), 0)
    src_row = (jnp.arange(N_EXPERTS, dtype=jnp.int32)[None, :] * region + start).reshape(-1)
    valid = (cnt > 0).astype(jnp.int32).reshape(-1)
    shift = _pad_lanes((base - start).astype(F32)).reshape(nt, 1, LANES)
    return _combine(src_row.astype(jnp.int32), valid, x, slot, shift, p, npost, nple, ple_g, ple_p, yhi, ylo)


def _ple_kernel(x_ref, p_ref, nw_ref, wg_ref, wp_ref, o_ref):
    x = x_ref[...]
    gate = jax.nn.sigmoid(_dot(_rms(x, nw_ref[...]).astype(BF16), wg_ref[...]))
    o_ref[...] = x + gate * _dot(p_ref[...].astype(BF16), wp_ref[...])


def _ple(x, p, nw, wg, wp, tm):
    t = x.shape[0]
    return pl.pallas_call(
        _ple_kernel,
        grid=(t // tm,),
        in_specs=[pl.BlockSpec((tm, D_MODEL), lambda i: (i, 0)),
                  pl.BlockSpec((tm, PLE_DIM), lambda i: (i, 0)),
                  pl.BlockSpec((1, D_MODEL), lambda i: (0, 0)),
                  pl.BlockSpec((D_MODEL, D_MODEL), lambda i: (0, 0)),
                  pl.BlockSpec((PLE_DIM, D_MODEL), lambda i: (0, 0))],
        out_specs=pl.BlockSpec((tm, D_MODEL), lambda i: (i, 0)),
        out_shape=jax.ShapeDtypeStruct((t, D_MODEL), F32),
        compiler_params=_params("parallel"),
        name="ple",
    )(x, p, nw, wg, wp)


def _pad_lanes(a, width=LANES):
    return jnp.pad(a, [(0, 0)] * (a.ndim - 1) + [(0, width - a.shape[-1])])


def _permute_w_in(w):
    main = w[:, 0:2304]
    ig = w[:, 2304:2308]
    fg = w[:, 2308:2312]
    gu_gv = w[:, 2312:2824]
    return jnp.concatenate([main, gu_gv, _pad_lanes(ig), _pad_lanes(fg)], axis=1).astype(BF16)


def _block_diag(blocks):
    g, d, _ = blocks.shape
    out = jnp.zeros((g * d, g * d), blocks.dtype)
    for i in range(g):
        out = out.at[i * d:(i + 1) * d, i * d:(i + 1) * d].set(blocks[i])
    return out


def _row(a):
    return a.reshape(1, -1).astype(F32)


def kernel(x_prompt, x_sample, state_pool, state_mlstm_C, state_mlstm_n, state_mlstm_m, p_prompt, p_sample,
           norm_mix_pre, norm_mix_post, norm_ffn_pre, norm_ffn_post, norm_ple, w_in, pool_w, pool_scale,
           mlstm_b_i, mlstm_b_f, mlstm_norm_w, gmlp_norm_w, gmlp_ws, gmlp_bs, w_out,
           ffn_w_gate, ffn_w_up, ffn_w_down, moe_router_w, moe_router_b, moe_w_gate, moe_w_up, moe_w_down,
           ple_w_gate, ple_w_proj):
    batch, seq, _ = x_prompt.shape
    nseq = x_sample.shape[0]
    xp = x_prompt.reshape(batch * seq, D_MODEL)
    xs = x_sample.reshape(nseq, D_MODEL)
    gmean = _block_diag(jnp.full((GMLP_GROUPS, GMLP_GROUP_DIM, GMLP_GROUP_DIM), 1.0 / GMLP_GROUP_DIM, BF16))

    pools_p, cs_p, ns_p, ms_p = [], [], [], []
    pools_s, cs_s, ns_s, ms_s, gvs_s = [], [], [], [], []
    for i in range(DEPTH):
        w_in_p = _permute_w_in(w_in[i])
        w_out_b = w_out[i].astype(BF16)
        poolw = _block_diag(pool_w[i]).astype(BF16)
        shared = [poolw, _row(pool_scale[i]), _pad_lanes(_row(mlstm_b_i[i])), _pad_lanes(_row(mlstm_b_f[i])),
                  _row(mlstm_norm_w[i]), _row(gmlp_norm_w[i])]
        gbs_full = jnp.repeat(gmlp_bs[i].T, GMLP_GROUP_DIM, axis=1)
        consts_p = shared + [gmlp_ws[i], gbs_full, gmean]
        gw0 = jnp.repeat(gmlp_ws[i][:, 0, 0], GMLP_GROUP_DIM).reshape(1, GMLP_WIDTH)
        consts_s = shared + [gw0, gbs_full[0:1, :], gmean]
        ple_g = ple_w_gate[i].astype(BF16)
        ple_p = ple_w_proj[i].astype(BF16)
        j = i // 2
        if i % 2 == 1:
            rw = _pad_lanes(moe_router_w[j]).astype(BF16)
            rb = _pad_lanes(_row(moe_router_b[j]))
            moe_g, moe_u, moe_d = (moe_w_gate[j].astype(BF16), moe_w_up[j].astype(BF16),
                                   moe_w_down[j].astype(BF16))

        z = _norm_matmul(xp, _row(norm_mix_pre[i]), w_in_p, TM_PROMPT)
        y, cn_new, m_new = _mixer_prompt(z, consts_p, batch, seq)
        pools_p.append(z.reshape(batch, seq, Z_WIDTH)[:, seq - POOL_STATE:, 0:POOL_WIDTH])
        cs_p.append(cn_new[..., 0:MLSTM_HEAD_DIM])
        ns_p.append(cn_new[..., MLSTM_HEAD_DIM])
        ms_p.append(m_new[:, 0, 0:MLSTM_HEADS])
        xp = _proj_norm_res(y.reshape(batch * seq, D_MODEL), xp, w_out_b, _row(norm_mix_post[i]), TM_PROMPT)
        if i % 2 == 0:
            xp = _ffn_dense(xp, _row(norm_ffn_pre[i]), _row(norm_ffn_post[i]),
                            ffn_w_gate[j], ffn_w_up[j], ffn_w_down[j], TM_FFN)
            xp = _ple(xp, p_prompt[i].reshape(batch * seq, PLE_DIM), _row(norm_ple[i]), ple_g, ple_p, TM_PROMPT)
        else:
            xp = _moe_routed_ple(xp, p_prompt[i].reshape(batch * seq, PLE_DIM), _row(norm_ffn_pre[i]),
                                 _row(norm_ffn_post[i]), _row(norm_ple[i]), rw, rb, moe_g, moe_u, moe_d,
                                 ple_g, ple_p)

        z = _norm_matmul(xs, _row(norm_mix_pre[i]), w_in_p, nseq)
        sp_t = jnp.transpose(state_pool[i], (1, 0, 2))
        y, c_new, n_new, m_new, gv = _mixer_sample(z, sp_t, state_mlstm_C[i],
                                                   state_mlstm_n[i].reshape(nseq, MLSTM_WIDTH),
                                                   _pad_lanes(state_mlstm_m[i]), consts_s)
        pools_s.append(jnp.concatenate([state_pool[i][:, 1:], z[:, None, 0:POOL_WIDTH]], axis=1))
        cs_s.append(c_new)
        ns_s.append(n_new.reshape(nseq, MLSTM_HEADS, MLSTM_HEAD_DIM))
        ms_s.append(m_new[:, 0:MLSTM_HEADS])
        gvs_s.append(gv[:, None, :])
        xs = _proj_norm_res(y, xs, w_out_b, _row(norm_mix_post[i]), nseq)
        if i % 2 == 0:
            xs = _ffn_dense(xs, _row(norm_ffn_pre[i]), _row(norm_ffn_post[i]),
                            ffn_w_gate[j], ffn_w_up[j], ffn_w_down[j], nseq)
        else:
            xs = _ffn_moe(xs, _row(norm_ffn_pre[i]), _row(norm_ffn_post[i]), rw, rb, moe_g, moe_u, moe_d, nseq)
        xs = _ple(xs, p_sample[i].reshape(nseq, PLE_DIM), _row(norm_ple[i]), ple_g, ple_p, nseq)

    return (xp.reshape(batch, seq, D_MODEL), xs.reshape(nseq, 1, D_MODEL),
            jnp.stack(pools_p), jnp.stack(cs_p), jnp.stack(ns_p), jnp.stack(ms_p),
            jnp.stack(pools_s), jnp.stack(cs_s), jnp.stack(ns_s), jnp.stack(ms_s), jnp.stack(gvs_s))
```

```python
import functools

import jax
import jax.numpy as jnp
from jax import lax
from jax.experimental import pallas as pl
from jax.experimental.pallas import tpu as pltpu

F32 = jnp.float32
BF16 = jnp.bfloat16

D_MODEL = 1024
DEPTH = 2
POOL_WIDTH = 256
POOL_WINDOWS = (2, 4, 8, 16)
POOL_GROUP_DIM = 64
POOL_STATE = 15
MLSTM_WIDTH = 512
MLSTM_HEADS = 4
MLSTM_HEAD_DIM = 128
CHUNK = 128
GMLP_WIDTH = 256
GMLP_GROUPS = 4
GMLP_GROUP_DIM = 64
D_FF = 2816
N_EXPERTS = 8
D_FF_EXPERT = 1408
PLE_DIM = 256
RMS_EPS = 1e-6
PAST_LEN = 16384

LANES = 128
SUBLANES = 8
VMEM_LIMIT = 48 * 1024 * 1024
VMEM_LIMIT_EXPERT_WEIGHTS = 58 * 1024 * 1024

Z_POOL = 0
Z_Q = 256
Z_K = 768
Z_V = 1280
Z_O = 1792
Z_GU = 2304
Z_GV = 2560
Z_IG = 2816
Z_FG = 2944
Z_WIDTH = 3072
Z_CHUNK = 512

TM_PROMPT = 512
TM_FFN = 1024
TF_DENSE = 256
TM_MOE = 512
FF_EXPERT_CHUNKS = ((0, 512), (512, 512), (1024, 384))
SAMPLE_BLOCK = 8
PROMPT_SEQ_PER_STEP = 2


def _params(*semantics, vmem_limit=VMEM_LIMIT):
    return pltpu.CompilerParams(dimension_semantics=semantics, vmem_limit_bytes=vmem_limit)


def _rms(x, w):
    return x * lax.rsqrt(jnp.mean(x * x, axis=-1, keepdims=True) + RMS_EPS) * w


def _log_sigmoid(x):
    return jnp.minimum(x, 0.0) - jnp.log1p(jnp.exp(-jnp.abs(x)))


def _dot(a, b):
    return jnp.dot(a, b, preferred_element_type=F32)


def _split3(x):
    hi = x.astype(BF16)
    rest = x - hi.astype(F32)
    mid = rest.astype(BF16)
    lo = (rest - mid.astype(F32)).astype(BF16)
    return hi, mid, lo


def _norm_matmul_kernel(x_ref, nw_ref, w_ref, o_ref, h_ref):
    h_ref[...] = _rms(x_ref[...], nw_ref[...]).astype(BF16)
    for n0 in range(0, Z_WIDTH, Z_CHUNK):
        o_ref[:, n0:n0 + Z_CHUNK] = _dot(h_ref[...], w_ref[:, n0:n0 + Z_CHUNK])


def _norm_matmul(x, nw, w, tm):
    t = x.shape[0]
    return pl.pallas_call(
        _norm_matmul_kernel,
        grid=(t // tm,),
        in_specs=[pl.BlockSpec((tm, D_MODEL), lambda i: (i, 0)),
                  pl.BlockSpec((1, D_MODEL), lambda i: (0, 0)),
                  pl.BlockSpec((D_MODEL, Z_WIDTH), lambda i: (0, 0))],
        out_specs=pl.BlockSpec((tm, Z_WIDTH), lambda i: (i, 0)),
        out_shape=jax.ShapeDtypeStruct((t, Z_WIDTH), F32),
        scratch_shapes=[pltpu.VMEM((tm, D_MODEL), BF16)],
        compiler_params=_params("parallel"),
        name="norm_in_proj",
    )(x, nw, w)


def _group_rms(v, gmean, w):
    hi, mid, lo = _split3(v * v)
    ms = _dot(hi, gmean) + _dot(mid, gmean) + _dot(lo, gmean)
    return v * lax.rsqrt(ms + RMS_EPS) * w


def _pool_tile(ext_ref, u_tile, col0, w_lo, w_hi, pos):
    acc = u_tile
    sums = {}
    for shift in range(1, w_hi):
        acc = acc + ext_ref[pl.ds(16 - shift, CHUNK), col0:col0 + LANES]
        if shift + 1 in (w_lo, w_hi):
            sums[shift + 1] = acc
    cnt_lo = jnp.minimum(w_lo, pos + 1).astype(F32)
    cnt_hi = jnp.minimum(w_hi, pos + 1).astype(F32)
    lane = lax.broadcasted_iota(jnp.int32, (CHUNK, LANES), 1)
    return jnp.where(lane < POOL_GROUP_DIM, sums[w_lo] / cnt_lo, sums[w_hi] / cnt_hi) - u_tile


def _mixer_prompt_kernel(z_ref, *refs):
    consts = refs[:9]
    y_ref, cn_ref, m_ref, ext_ref = refs[9:]

    @pl.when(pl.program_id(1) == 0)
    def _():
        ext_ref[:, 0:16, :] = jnp.zeros((PROMPT_SEQ_PER_STEP, 16, POOL_WIDTH), F32)
        cn_ref[...] = jnp.zeros(cn_ref.shape, F32)
        m_ref[...] = jnp.zeros(m_ref.shape, F32)

    for i in range(PROMPT_SEQ_PER_STEP):
        _mixer_prompt_body(z_ref.at[i], *consts, y_ref.at[i], cn_ref.at[i], m_ref.at[i], ext_ref.at[i])


def _mixer_prompt_body(z_ref, poolw_ref, pscale_ref, bi_ref, bf_ref, mnorm_ref, gnorm_ref,
                       gws_ref, gbs_ref, gmean_ref,
                       y_ref, cn_ref, m_ref, ext_ref):
    chunk = pl.program_id(1)
    row = lax.broadcasted_iota(jnp.int32, (CHUNK, CHUNK), 0)
    col = lax.broadcasted_iota(jnp.int32, (CHUNK, CHUNK), 1)
    causal = col <= row
    lane = col

    ext_ref[16:16 + CHUNK, :] = z_ref[:, Z_POOL:Z_POOL + POOL_WIDTH]
    pos = chunk * CHUNK + lax.broadcasted_iota(jnp.int32, (CHUNK, 1), 0)
    pooled = []
    for tile in range(2):
        col0 = tile * LANES
        u_tile = z_ref[:, Z_POOL + col0:Z_POOL + col0 + LANES]
        pooled.append(_pool_tile(ext_ref, u_tile, col0, POOL_WINDOWS[2 * tile],
                                 POOL_WINDOWS[2 * tile + 1], pos))
    pooled = jnp.concatenate(pooled, axis=1).astype(BF16)
    y_pool = _dot(pooled, poolw_ref[...]) * pscale_ref[...]
    y_ref[:, 0:POOL_WIDTH] = y_pool.astype(BF16)
    ext_ref[0:16, :] = ext_ref[CHUNK:CHUNK + 16, :]

    vn = _group_rms(z_ref[:, Z_GV:Z_GV + GMLP_WIDTH], gmean_ref[...], gnorm_ref[...]).astype(BF16)
    for tile in range(2):
        col0 = tile * LANES
        vt = vn[:, col0:col0 + LANES]
        w_a = jnp.where(causal, gws_ref[2 * tile], 0.0).astype(BF16)
        w_b = jnp.where(causal, gws_ref[2 * tile + 1], 0.0).astype(BF16)
        mixed = jnp.where(lane < GMLP_GROUP_DIM, _dot(w_a, vt), _dot(w_b, vt))
        gu = z_ref[:, Z_GU + col0:Z_GU + col0 + LANES]
        y_g = gu * (mixed + gbs_ref[:, col0:col0 + LANES])
        y_ref[:, 768 + col0:768 + col0 + LANES] = y_g.astype(BF16)

    ig = z_ref[:, Z_IG:Z_IG + LANES] + bi_ref[...]
    lf = _log_sigmoid(z_ref[:, Z_FG:Z_FG + LANES] + bf_ref[...])
    tri = jnp.where(causal, 1.0, 0.0).astype(BF16)
    lf_hi, lf_mid, lf_lo = _split3(lf)
    b = _dot(tri, lf_hi) + _dot(tri, lf_mid) + _dot(tri, lf_lo)
    m_prev = m_ref[...]
    g = b + m_prev
    r_t = jnp.transpose(ig - b)
    b_last = b[CHUNK - 1:CHUNK, :]
    ones_col = jnp.where(lane == 0, 1.0, 0.0).astype(BF16)
    m_new_row = m_prev
    for h in range(MLSTM_HEADS):
        c0 = h * MLSTM_HEAD_DIM
        q = z_ref[:, Z_Q + c0:Z_Q + c0 + MLSTM_HEAD_DIM].astype(BF16)
        k = z_ref[:, Z_K + c0:Z_K + c0 + MLSTM_HEAD_DIM] * (MLSTM_HEAD_DIM ** -0.5)
        v = z_ref[:, Z_V + c0:Z_V + c0 + MLSTM_HEAD_DIM].astype(BF16)
        o = z_ref[:, Z_O + c0:Z_O + c0 + MLSTM_HEAD_DIM]
        b_col = b[:, h:h + 1]
        dmat = jnp.where(causal, b_col + r_t[h:h + 1, :], -jnp.inf)
        g_col = g[:, h:h + 1]
        m_t = jnp.maximum(g_col, jnp.max(dmat, axis=1, keepdims=True))
        scores = lax.dot_general(q, k.astype(BF16), (((1,), (1,)), ((), ())),
                                 preferred_element_type=F32)
        wts = jnp.exp(dmat - m_t) * scores
        inter = jnp.exp(g_col - m_t)
        cn_h = cn_ref[h]
        q_cn = _dot(q, cn_h.astype(BF16))
        num = inter * q_cn[:, 0:MLSTM_HEAD_DIM] + _dot(wts.astype(BF16), v)
        den = inter * q_cn[:, MLSTM_HEAD_DIM:MLSTM_HEAD_DIM + 1] + jnp.sum(wts, axis=1, keepdims=True)
        hid = num / jnp.maximum(jnp.abs(den), jnp.exp(-m_t))
        hid = _rms(hid, mnorm_ref[:, c0:c0 + MLSTM_HEAD_DIM])
        y_ref[:, POOL_WIDTH + c0:POOL_WIDTH + c0 + MLSTM_HEAD_DIM] = (jax.nn.sigmoid(o) * hid).astype(BF16)
        m_new = m_t[CHUNK - 1:CHUNK, :]
        bl = b_last[:, h:h + 1]
        decay = jnp.exp(bl + m_prev[:, h:h + 1] - m_new)
        w_s = jnp.exp(bl - b_col + ig[:, h:h + 1] - m_new)
        kw = (k * w_s).astype(BF16)
        v_ext = jnp.concatenate([v, ones_col], axis=1)
        cn_ref[h] = decay * cn_h + lax.dot_general(kw, v_ext, (((0,), (0,)), ((), ())),
                                                   preferred_element_type=F32)
        m_new_row = jnp.where(lane[0:1, :] == h, m_new, m_new_row)
    m_ref[...] = m_new_row


def _mixer_prompt(z, consts, batch, seq):
    nc = seq // CHUNK
    hd = MLSTM_HEAD_DIM
    z3 = z.reshape(batch, seq, Z_WIDTH)
    ns = PROMPT_SEQ_PER_STEP
    const_specs = [pl.BlockSpec(a.shape, lambda b, c, nd=a.ndim: (0,) * nd) for a in consts]
    return pl.pallas_call(
        _mixer_prompt_kernel,
        grid=(batch // ns, nc),
        in_specs=[pl.BlockSpec((ns, CHUNK, Z_WIDTH), lambda b, c: (b, c, 0))] + const_specs,
        out_specs=[pl.BlockSpec((ns, CHUNK, D_MODEL), lambda b, c: (b, c, 0)),
                   pl.BlockSpec((ns, MLSTM_HEADS, hd, 2 * hd), lambda b, c: (b, 0, 0, 0)),
                   pl.BlockSpec((ns, 1, LANES), lambda b, c: (b, 0, 0))],
        out_shape=[jax.ShapeDtypeStruct((batch, seq, D_MODEL), BF16),
                   jax.ShapeDtypeStruct((batch, MLSTM_HEADS, hd, 2 * hd), F32),
                   jax.ShapeDtypeStruct((batch, 1, LANES), F32)],
        scratch_shapes=[pltpu.VMEM((ns, 16 + CHUNK, POOL_WIDTH), F32)],
        compiler_params=_params("parallel", "arbitrary"),
        name="mixer_prompt",
    )(z3, *consts)


def _mixer_sample_kernel(z_ref, sp_ref, c_ref, n_ref, m_ref,
                         poolw_ref, pscale_ref, bi_ref, bf_ref, mnorm_ref, gnorm_ref,
                         gw0_ref, gb0_ref, gmean_ref,
                         y_ref, cn_ref, nn_ref, mn_ref, gv_ref, tk_ref):
    nb = SAMPLE_BLOCK
    hd = MLSTM_HEAD_DIM
    lane = lax.broadcasted_iota(jnp.int32, (nb, LANES), 1)
    seq_id = lax.broadcasted_iota(jnp.int32, (nb, LANES), 0)

    pooled = []
    for tile in range(2):
        col0 = tile * LANES
        u_tile = z_ref[:, Z_POOL + col0:Z_POOL + col0 + LANES]
        w_lo, w_hi = POOL_WINDOWS[2 * tile], POOL_WINDOWS[2 * tile + 1]
        acc = u_tile
        sums = {}
        for shift in range(1, w_hi):
            acc = acc + sp_ref[POOL_STATE - shift, :, col0:col0 + LANES]
            if shift + 1 in (w_lo, w_hi):
                sums[shift + 1] = acc
        pooled.append(jnp.where(lane < POOL_GROUP_DIM, sums[w_lo] / float(w_lo), sums[w_hi] / float(w_hi)) - u_tile)
    pooled = jnp.concatenate(pooled, axis=1).astype(BF16)
    y_ref[:, 0:POOL_WIDTH] = (_dot(pooled, poolw_ref[...]) * pscale_ref[...]).astype(BF16)

    vn = _group_rms(z_ref[:, Z_GV:Z_GV + GMLP_WIDTH], gmean_ref[...], gnorm_ref[...])
    gv_ref[...] = vn
    y_g = z_ref[:, Z_GU:Z_GU + GMLP_WIDTH] * (gw0_ref[...] * vn + gb0_ref[...])
    y_ref[:, 768:768 + GMLP_WIDTH] = y_g.astype(BF16)

    ig = z_ref[:, Z_IG:Z_IG + LANES] + bi_ref[...]
    lf = _log_sigmoid(z_ref[:, Z_FG:Z_FG + LANES] + bf_ref[...])
    m_prev = m_ref[...]
    g = lf + m_prev
    m_t = jnp.maximum(g, ig)
    inter = jnp.exp(g - m_t)
    e_ig = jnp.exp(ig - m_t)
    floor = jnp.exp(-m_t)
    mn_ref[...] = m_t
    tk_ref[...] = jnp.zeros((LANES, LANES), F32)
    for h in range(MLSTM_HEADS):
        tk_ref[nb * h:nb * (h + 1), :] = z_ref[:, Z_K + h * hd:Z_K + (h + 1) * hd] * (hd ** -0.5)
    k_t = jnp.transpose(tk_ref[...])
    for h in range(MLSTM_HEADS):
        c0 = h * hd
        q_h = z_ref[:, Z_Q + c0:Z_Q + c0 + hd]
        k_h = tk_ref[nb * h:nb * (h + 1), :]
        v_h = z_ref[:, Z_V + c0:Z_V + c0 + hd]
        o_h = z_ref[:, Z_O + c0:Z_O + c0 + hd]
        n_h = n_ref[:, c0:c0 + hd]
        inter_b = jnp.broadcast_to(inter[:, h:h + 1], (nb, hd))
        e_b = jnp.broadcast_to(e_ig[:, h:h + 1], (nb, hd))
        floor_b = jnp.broadcast_to(floor[:, h:h + 1], (nb, hd))
        v_w = e_b * v_h
        q_b = q_h.astype(BF16)
        q_c = jnp.zeros((nb, hd), F32)
        for s in range(nb):
            c_sh = c_ref[s, h]
            q_c = jnp.where(seq_id == s, _dot(q_b, c_sh.astype(BF16)), q_c)
            col = nb * h + s
            cn_ref[s, h] = inter_b[s:s + 1, :] * c_sh + k_t[:, col:col + 1] * v_w[s:s + 1, :]
        wts = e_b * jnp.sum(q_h * k_h, axis=1, keepdims=True)
        num = inter_b * q_c + wts * v_h
        den = inter_b * jnp.sum(q_h * n_h, axis=1, keepdims=True) + wts
        hid = num / jnp.maximum(jnp.abs(den), floor_b)
        hid = _rms(hid, mnorm_ref[:, c0:c0 + hd])
        y_ref[:, POOL_WIDTH + c0:POOL_WIDTH + c0 + hd] = (jax.nn.sigmoid(o_h) * hid).astype(BF16)
        nn_ref[:, c0:c0 + hd] = inter_b * n_h + e_b * k_h


def _mixer_sample(z, sp_t, c_state, n_state, m_pad, consts):
    nseq = z.shape[0]
    nb = SAMPLE_BLOCK
    hd = MLSTM_HEAD_DIM
    const_specs = [pl.BlockSpec(a.shape, lambda j, nd=a.ndim: (0,) * nd) for a in consts]
    return pl.pallas_call(
        _mixer_sample_kernel,
        grid=(nseq // nb,),
        in_specs=[pl.BlockSpec((nb, Z_WIDTH), lambda j: (j, 0)),
                  pl.BlockSpec((POOL_STATE, nb, POOL_WIDTH), lambda j: (0, j, 0)),
                  pl.BlockSpec((nb, MLSTM_HEADS, hd, hd), lambda j: (j, 0, 0, 0)),
                  pl.BlockSpec((nb, MLSTM_WIDTH), lambda j: (j, 0)),
                  pl.BlockSpec((nb, LANES), lambda j: (j, 0))] + const_specs,
        out_specs=[pl.BlockSpec((nb, D_MODEL), lambda j: (j, 0)),
                   pl.BlockSpec((nb, MLSTM_HEADS, hd, hd), lambda j: (j, 0, 0, 0)),
                   pl.BlockSpec((nb, MLSTM_WIDTH), lambda j: (j, 0)),
                   pl.BlockSpec((nb, LANES), lambda j: (j, 0)),
                   pl.BlockSpec((nb, GMLP_WIDTH), lambda j: (j, 0))],
        out_shape=[jax.ShapeDtypeStruct((nseq, D_MODEL), BF16),
                   jax.ShapeDtypeStruct((nseq, MLSTM_HEADS, hd, hd), F32),
                   jax.ShapeDtypeStruct((nseq, MLSTM_WIDTH), F32),
                   jax.ShapeDtypeStruct((nseq, LANES), F32),
                   jax.ShapeDtypeStruct((nseq, GMLP_WIDTH), F32)],
        scratch_shapes=[pltpu.VMEM((LANES, LANES), F32)],
        compiler_params=_params("parallel"),
        name="mixer_sample",
    )(z, sp_t, c_state, n_state, m_pad, *consts)


def _proj_norm_res_kernel(y_ref, x_ref, w_ref, nw_ref, o_ref):
    o_ref[...] = x_ref[...] + _rms(_dot(y_ref[...], w_ref[...]), nw_ref[...])


def _proj_norm_res(y, x, w, nw, tm):
    t = x.shape[0]
    return pl.pallas_call(
        _proj_norm_res_kernel,
        grid=(t // tm,),
        in_specs=[pl.BlockSpec((tm, D_MODEL), lambda i: (i, 0)),
                  pl.BlockSpec((tm, D_MODEL), lambda i: (i, 0)),
                  pl.BlockSpec((D_MODEL, D_MODEL), lambda i: (0, 0)),
                  pl.BlockSpec((1, D_MODEL), lambda i: (0, 0))],
        out_specs=pl.BlockSpec((tm, D_MODEL), lambda i: (i, 0)),
        out_shape=jax.ShapeDtypeStruct((t, D_MODEL), F32),
        compiler_params=_params("parallel"),
        name="out_proj",
    )(y, x, w, nw)


def _dense_layer_kernel(y_ref, x_ref, p_ref, wout_ref, nmix_ref, npre_ref, npost_ref, nple_ref,
                        wg_ref, wu_ref, wd_ref, pg_ref, pp_ref, o_ref, x1_ref, h_ref, acc_ref):
    j = pl.program_id(1)

    @pl.when(j == 0)
    def _():
        x1 = x_ref[...] + _rms(_dot(y_ref[...], wout_ref[...]), nmix_ref[...])
        x1_ref[...] = x1
        h_ref[...] = _rms(x1, npre_ref[...]).astype(BF16)
        acc_ref[...] = jnp.zeros(acc_ref.shape, F32)

    h = h_ref[...]
    y = None
    for f0, fw in FF_EXPERT_CHUNKS:
        gate = _dot(h, wg_ref[:, f0:f0 + fw])
        up = _dot(h, wu_ref[:, f0:f0 + fw])
        act = (gate * jax.nn.sigmoid(gate) * up).astype(BF16)
        part = _dot(act, wd_ref[f0:f0 + fw, :])
        y = part if y is None else y + part
    acc_ref[...] += y

    @pl.when(j == pl.num_programs(1) - 1)
    def _():
        x2 = x1_ref[...] + _rms(acc_ref[...], npost_ref[...])
        gate = jax.nn.sigmoid(_dot(_rms(x2, nple_ref[...]).astype(BF16), pg_ref[...]))
        o_ref[...] = x2 + gate * _dot(p_ref[...].astype(BF16), pp_ref[...])


def _dense_layer(y, x, p, w_out, nmix, npre, npost, nple, wg, wu, wd, ple_g, ple_p, tm):
    t = x.shape[0]
    half = D_FF // 2
    assert half == D_FF_EXPERT
    row_spec = lambda: pl.BlockSpec((1, D_MODEL), lambda i, j: (0, 0))
    return pl.pallas_call(
        _dense_layer_kernel,
        grid=(t // tm, 2),
        in_specs=[pl.BlockSpec((tm, D_MODEL), lambda i, j: (i, 0)),
                  pl.BlockSpec((tm, D_MODEL), lambda i, j: (i, 0)),
                  pl.BlockSpec((tm, PLE_DIM), lambda i, j: (i, 0)),
                  pl.BlockSpec((D_MODEL, D_MODEL), lambda i, j: (0, 0)),
                  row_spec(), row_spec(), row_spec(), row_spec(),
                  pl.BlockSpec((D_MODEL, half), lambda i, j: (0, j)),
                  pl.BlockSpec((D_MODEL, half), lambda i, j: (0, j)),
                  pl.BlockSpec((half, D_MODEL), lambda i, j: (j, 0)),
                  pl.BlockSpec((D_MODEL, D_MODEL), lambda i, j: (0, 0)),
                  pl.BlockSpec((PLE_DIM, D_MODEL), lambda i, j: (0, 0))],
        out_specs=pl.BlockSpec((tm, D_MODEL), lambda i, j: (i, 0)),
        out_shape=jax.ShapeDtypeStruct((t, D_MODEL), F32),
        scratch_shapes=[pltpu.VMEM((tm, D_MODEL), F32), pltpu.VMEM((tm, D_MODEL), BF16),
                        pltpu.VMEM((tm, D_MODEL), F32)],
        compiler_params=_params("parallel", "arbitrary", vmem_limit=VMEM_LIMIT_EXPERT_WEIGHTS),
        name="dense_layer",
    )(y, x, p, w_out, nmix, npre, npost, nple, wg, wu, wd, ple_g, ple_p)


def _router_gates(h, rw_ref, rb_ref):
    shape = (h.shape[0], LANES)
    lane = lax.broadcasted_iota(jnp.int32, shape, 1)
    lane_f = lane.astype(F32)
    logits = jnp.where(lane < N_EXPERTS, _dot(h, rw_ref[...]) + rb_ref[...], -jnp.inf)
    e = jnp.exp(logits - jnp.max(logits, axis=-1, keepdims=True))
    probs = e / jnp.sum(e, axis=-1, keepdims=True)
    p1 = jnp.max(probs, axis=-1, keepdims=True)
    i1 = jnp.min(jnp.where(probs == p1, lane_f, float(LANES)), axis=-1, keepdims=True)
    rest = jnp.where(lane_f == i1, -1.0, probs)
    p2 = jnp.max(rest, axis=-1, keepdims=True)
    i2 = jnp.min(jnp.where(rest == p2, lane_f, float(LANES)), axis=-1, keepdims=True)
    total = p1 + p2
    return jnp.where(lane_f == i1, p1 / total, 0.0) + jnp.where(lane_f == i2, p2 / total, 0.0)


def _ffn_moe_kernel(x_ref, npre_ref, npost_ref, rw_ref, rb_ref, wg_ref, wu_ref, wd_ref,
                    o_ref, h_ref, acc_ref, gates_ref):
    e = pl.program_id(1)

    @pl.when(e == 0)
    def _():
        h_ref[...] = _rms(x_ref[...], npre_ref[...]).astype(BF16)
        acc_ref[...] = jnp.zeros(acc_ref.shape, F32)
        gates_ref[...] = _router_gates(h_ref[...], rw_ref, rb_ref)

    h = h_ref[...]
    lane = lax.broadcasted_iota(jnp.int32, gates_ref.shape, 1)
    gate_col = jnp.sum(jnp.where(lane == e, gates_ref[...], 0.0), axis=-1, keepdims=True)
    y = None
    for f0, fw in FF_EXPERT_CHUNKS:
        gate = _dot(h, wg_ref[:, f0:f0 + fw].astype(BF16))
        up = _dot(h, wu_ref[:, f0:f0 + fw].astype(BF16))
        act = (gate * jax.nn.sigmoid(gate) * up).astype(BF16)
        part = _dot(act, wd_ref[f0:f0 + fw, :].astype(BF16))
        y = part if y is None else y + part
    acc_ref[...] += gate_col * y

    @pl.when(e == pl.num_programs(1) - 1)
    def _():
        o_ref[...] = x_ref[...] + _rms(acc_ref[...], npost_ref[...])


def _ffn_moe(x, npre, npost, rw, rb, wg, wu, wd, tm):
    t = x.shape[0]
    return pl.pallas_call(
        _ffn_moe_kernel,
        grid=(t // tm, N_EXPERTS),
        in_specs=[pl.BlockSpec((tm, D_MODEL), lambda i, e: (i, 0)),
                  pl.BlockSpec((1, D_MODEL), lambda i, e: (0, 0)),
                  pl.BlockSpec((1, D_MODEL), lambda i, e: (0, 0)),
                  pl.BlockSpec((D_MODEL, LANES), lambda i, e: (0, 0)),
                  pl.BlockSpec((1, LANES), lambda i, e: (0, 0)),
                  pl.BlockSpec((None, D_MODEL, D_FF_EXPERT), lambda i, e: (e, 0, 0)),
                  pl.BlockSpec((None, D_MODEL, D_FF_EXPERT), lambda i, e: (e, 0, 0)),
                  pl.BlockSpec((None, D_FF_EXPERT, D_MODEL), lambda i, e: (e, 0, 0))],
        out_specs=pl.BlockSpec((tm, D_MODEL), lambda i, e: (i, 0)),
        out_shape=jax.ShapeDtypeStruct((t, D_MODEL), F32),
        scratch_shapes=[pltpu.VMEM((tm, D_MODEL), BF16), pltpu.VMEM((tm, D_MODEL), F32),
                        pltpu.VMEM((tm, LANES), F32)],
        compiler_params=_params("parallel", "arbitrary", vmem_limit=VMEM_LIMIT_EXPERT_WEIGHTS),
        name="ffn_moe",
    )(x, npre, npost, rw, rb, wg, wu, wd)


ROUTE_TILE = 256
ROW_ALIGN = 16
ROUTE_SEG = ROUTE_TILE
ROUTE_PACK = 2 * ROUTE_TILE + N_EXPERTS * ROW_ALIGN
ROUTE_W = D_MODEL + 3 * LANES
ROUTE_BLOCK = 512


def _route_region(n_tokens):
    rows = n_tokens + (n_tokens // ROUTE_TILE) * (ROW_ALIGN - 1) + ROUTE_SEG + ROUTE_BLOCK
    return -(-rows // ROUTE_BLOCK) * ROUTE_BLOCK


def _lane_scalar(row, lane, e):
    return jnp.sum(jnp.where(lane == e, row, 0.0)).astype(jnp.int32)


def _route_kernel(y_ref, x_ref, wout_ref, nmix_ref, npre_ref, rw_ref, rb_ref,
                  x1_ref, slot_ref, stats_ref, srt_hbm,
                  stage_ref, runv_ref, run_ref, sem, *, region):
    i = pl.program_id(0)
    last = pl.num_programs(0) - 1
    cur = i % 2

    @pl.when(i == 0)
    def _():
        runv_ref[...] = jnp.zeros(runv_ref.shape, F32)
        stage_ref[:, ROUTE_PACK:, :] = jnp.zeros((2, ROUTE_SEG, ROUTE_W), BF16)
        for e in range(N_EXPERTS):
            run_ref[e] = 0

    x1 = x_ref[...] + _rms(_dot(y_ref[...], wout_ref[...]), nmix_ref[...])
    x1_ref[...] = x1
    h = _rms(x1, npre_ref[...]).astype(BF16)
    gates = _router_gates(h, rw_ref, rb_ref)
    sel = gates > 0.0
    ones = jnp.where(sel, 1.0, 0.0)
    trow = lax.broadcasted_iota(jnp.int32, (ROUTE_TILE, ROUTE_TILE), 0)
    tcol = lax.broadcasted_iota(jnp.int32, (ROUTE_TILE, ROUTE_TILE), 1)
    before = jnp.where(tcol < trow, 1.0, 0.0).astype(BF16)
    rank = _dot(before, ones.astype(BF16))
    cnt = jnp.sum(ones, axis=0, keepdims=True)
    cnt_pad = jnp.floor((cnt + (ROW_ALIGN - 1)) * (1.0 / ROW_ALIGN)) * ROW_ALIGN
    lrow = lax.broadcasted_iota(jnp.int32, (LANES, LANES), 0)
    lcol = lax.broadcasted_iota(jnp.int32, (LANES, LANES), 1)
    lower = jnp.where(lrow < lcol, 1.0, 0.0).astype(BF16)
    off = _dot(jnp.broadcast_to(cnt_pad, (SUBLANES, LANES)).astype(BF16), lower)[0:1, :]
    slot_ref[...] = jnp.where(sel, rank, -1.0)
    stats_ref[...] = jnp.zeros(stats_ref.shape, F32)
    stats_ref[0:1, :] = runv_ref[...]
    stats_ref[1:2, :] = cnt
    runv_ref[...] = runv_ref[...] + cnt_pad

    pos = jnp.where(sel, rank + off, -1.0)
    pos_t = jnp.concatenate([jnp.transpose(pos[0:LANES, :]), jnp.transpose(pos[LANES:2 * LANES, :])], axis=1)
    pos_a = jnp.max(pos_t, axis=0, keepdims=True)
    pos_b = jnp.max(jnp.where(pos_t == pos_a, -1.0, pos_t), axis=0, keepdims=True)
    prow = lax.broadcasted_iota(jnp.int32, (ROUTE_PACK, ROUTE_TILE), 0).astype(F32)
    perm = jnp.where((prow == pos_a) | (prow == pos_b), 1.0, 0.0).astype(BF16)
    g_hi, g_mid, g_lo = _split3(gates)
    rows = _dot(perm, jnp.concatenate([h, g_hi, g_mid, g_lo], axis=1))
    stage_ref[cur, 0:ROUTE_PACK, :] = rows.astype(BF16)

    lane = lax.broadcasted_iota(jnp.int32, (1, LANES), 1)

    def segment_copy(e, src_row, dst_row, slot):
        return pltpu.make_async_copy(
            stage_ref.at[slot, pl.ds(pl.multiple_of(src_row, ROW_ALIGN), ROUTE_SEG), :],
            srt_hbm.at[pl.ds(pl.multiple_of(dst_row, ROW_ALIGN), ROUTE_SEG), :],
            sem.at[e])

    @pl.when(i > 0)
    def _():
        for e in range(N_EXPERTS):
            segment_copy(e, 0, 0, 1 - cur).wait()

    for e in range(N_EXPERTS):
        segment_copy(e, _lane_scalar(off, lane, e), e * region + run_ref[e], cur).start()
        run_ref[e] = run_ref[e] + _lane_scalar(cnt_pad, lane, e)

    @pl.when(i == last)
    def _():
        for e in range(N_EXPERTS):
            segment_copy(e, 0, 0, cur).wait()
        stage_ref[1 - cur, 0:ROUTE_SEG, :] = jnp.zeros((ROUTE_SEG, ROUTE_W), BF16)
        for part in range(ROUTE_BLOCK // ROUTE_SEG):
            for e in range(N_EXPERTS):
                segment_copy(e, 0, e * region + run_ref[e] + part * ROUTE_SEG, 1 - cur).start()
            for e in range(N_EXPERTS):
                segment_copy(e, 0, 0, 1 - cur).wait()


def _route(y, x, w_out, nmix, npre, rw, rb):
    t = x.shape[0]
    nt = t // ROUTE_TILE
    region = _route_region(t)
    return pl.pallas_call(
        functools.partial(_route_kernel, region=region),
        grid=(nt,),
        in_specs=[pl.BlockSpec((ROUTE_TILE, D_MODEL), lambda i: (i, 0)),
                  pl.BlockSpec((ROUTE_TILE, D_MODEL), lambda i: (i, 0)),
                  pl.BlockSpec((D_MODEL, D_MODEL), lambda i: (0, 0)),
                  pl.BlockSpec((1, D_MODEL), lambda i: (0, 0)),
                  pl.BlockSpec((1, D_MODEL), lambda i: (0, 0)),
                  pl.BlockSpec((D_MODEL, LANES), lambda i: (0, 0)),
                  pl.BlockSpec((1, LANES), lambda i: (0, 0))],
        out_specs=[pl.BlockSpec((ROUTE_TILE, D_MODEL), lambda i: (i, 0)),
                   pl.BlockSpec((ROUTE_TILE, LANES), lambda i: (i, 0)),
                   pl.BlockSpec((None, SUBLANES, LANES), lambda i: (i, 0, 0)),
                   pl.BlockSpec(memory_space=pl.ANY)],
        out_shape=[jax.ShapeDtypeStruct((t, D_MODEL), F32),
                   jax.ShapeDtypeStruct((t, LANES), F32),
                   jax.ShapeDtypeStruct((nt, SUBLANES, LANES), F32),
                   jax.ShapeDtypeStruct((N_EXPERTS * region, ROUTE_W), BF16)],
        scratch_shapes=[pltpu.VMEM((2, ROUTE_PACK + ROUTE_SEG, ROUTE_W), BF16),
                        pltpu.VMEM((1, LANES), F32),
                        pltpu.SMEM((N_EXPERTS,), jnp.int32),
                        pltpu.SemaphoreType.DMA((N_EXPERTS,))],
        compiler_params=_params("arbitrary"),
        name="moe_route",
    )(y, x, w_out, nmix, npre, rw, rb)


def _experts_kernel(blk_row_ref, blk_e_ref, n_used_ref, srt_ref, wg_ref, wu_ref, wd_ref, yhi_ref, ylo_ref):
    k = pl.program_id(0)

    @pl.when(k < n_used_ref[0])
    def _():
        h = srt_ref[:, 0:D_MODEL]
        gate3 = (srt_ref[:, D_MODEL:D_MODEL + LANES].astype(F32)
                 + srt_ref[:, D_MODEL + LANES:D_MODEL + 2 * LANES].astype(F32)
                 + srt_ref[:, D_MODEL + 2 * LANES:D_MODEL + 3 * LANES].astype(F32))
        lane = lax.broadcasted_iota(jnp.int32, gate3.shape, 1)
        gate_col = jnp.sum(jnp.where(lane == blk_e_ref[k], gate3, 0.0), axis=-1, keepdims=True)
        y = None
        for f0, fw in FF_EXPERT_CHUNKS:
            gate = _dot(h, wg_ref[:, f0:f0 + fw].astype(BF16))
            up = _dot(h, wu_ref[:, f0:f0 + fw].astype(BF16))
            act = (gate * jax.nn.sigmoid(gate) * up).astype(BF16)
            part = _dot(act, wd_ref[f0:f0 + fw, :].astype(BF16))
            y = part if y is None else y + part
        y = gate_col * y
        hi = y.astype(BF16)
        yhi_ref[...] = hi
        ylo_ref[...] = (y - hi.astype(F32)).astype(BF16)


def _experts(srt, blk_row, blk_e, n_used, wg, wu, wd, n_blocks):
    rows = srt.shape[0]
    grid_spec = pltpu.PrefetchScalarGridSpec(
        num_scalar_prefetch=3,
        grid=(n_blocks,),
        in_specs=[pl.BlockSpec((ROUTE_BLOCK, ROUTE_W), lambda k, br, be, nu: (br[k], 0)),
                  pl.BlockSpec((None, D_MODEL, D_FF_EXPERT), lambda k, br, be, nu: (be[k], 0, 0)),
                  pl.BlockSpec((None, D_MODEL, D_FF_EXPERT), lambda k, br, be, nu: (be[k], 0, 0)),
                  pl.BlockSpec((None, D_FF_EXPERT, D_MODEL), lambda k, br, be, nu: (be[k], 0, 0))],
        out_specs=[pl.BlockSpec((ROUTE_BLOCK, D_MODEL), lambda k, br, be, nu: (br[k], 0)),
                   pl.BlockSpec((ROUTE_BLOCK, D_MODEL), lambda k, br, be, nu: (br[k], 0))])
    return pl.pallas_call(
        _experts_kernel,
        grid_spec=grid_spec,
        out_shape=[jax.ShapeDtypeStruct((rows, D_MODEL), BF16), jax.ShapeDtypeStruct((rows, D_MODEL), BF16)],
        compiler_params=_params("arbitrary", vmem_limit=VMEM_LIMIT_EXPERT_WEIGHTS),
        name="moe_experts",
    )(blk_row, blk_e, n_used, srt, wg, wu, wd)


def _combine_kernel(src_row_ref, valid_ref, x_ref, slot_ref, shift_ref, p_ref, npost_ref, nple_ref,
                    wg_ref, wp_ref, yhi_hbm, ylo_hbm, o_ref, seg_hi_ref, seg_lo_ref, sem):
    i = pl.program_id(0)
    nt = pl.num_programs(0)
    cur = i % 2

    def segment_copies(tile, slot, e):
        src = pl.ds(pl.multiple_of(src_row_ref[tile * N_EXPERTS + e], ROW_ALIGN), ROUTE_SEG)
        dst = pl.ds(e * ROUTE_SEG, ROUTE_SEG)
        return (pltpu.make_async_copy(yhi_hbm.at[src, :], seg_hi_ref.at[slot, dst, :], sem.at[slot, 0, e]),
                pltpu.make_async_copy(ylo_hbm.at[src, :], seg_lo_ref.at[slot, dst, :], sem.at[slot, 1, e]))

    def fetch(tile, slot):
        for e in range(N_EXPERTS):
            @pl.when(valid_ref[tile * N_EXPERTS + e] > 0)
            def _():
                for c in segment_copies(tile, slot, e):
                    c.start()

    @pl.when(i == 0)
    def _():
        seg_hi_ref[...] = jnp.zeros(seg_hi_ref.shape, BF16)
        seg_lo_ref[...] = jnp.zeros(seg_lo_ref.shape, BF16)
        fetch(0, 0)

    @pl.when(i + 1 < nt)
    def _():
        fetch(i + 1, 1 - cur)

    for e in range(N_EXPERTS):
        @pl.when(valid_ref[i * N_EXPERTS + e] > 0)
        def _():
            for c in segment_copies(i, cur, e):
                c.wait()

    slot = slot_ref[...]
    where = jnp.where(slot >= 0.0, slot + shift_ref[...], -1.0)
    seg_lane = lax.broadcasted_iota(jnp.int32, (ROUTE_TILE, ROUTE_SEG), 1).astype(F32)
    perm = jnp.concatenate([jnp.where(where[:, e:e + 1] == seg_lane, 1.0, 0.0).astype(BF16)
                            for e in range(N_EXPERTS)], axis=1)
    y = _dot(perm, seg_hi_ref[cur]) + _dot(perm, seg_lo_ref[cur])
    x = x_ref[...] + _rms(y, npost_ref[...])
    gate = jax.nn.sigmoid(_dot(_rms(x, nple_ref[...]).astype(BF16), wg_ref[...]))
    o_ref[...] = x + gate * _dot(p_ref[...].astype(BF16), wp_ref[...])


def _combine(src_row, valid, x, slot, shift, p, npost, nple, wg, wp, yhi, ylo):
    t = x.shape[0]
    grid_spec = pltpu.PrefetchScalarGridSpec(
        num_scalar_prefetch=2,
        grid=(t // ROUTE_TILE,),
        in_specs=[pl.BlockSpec((ROUTE_TILE, D_MODEL), lambda i, sr, va: (i, 0)),
                  pl.BlockSpec((ROUTE_TILE, LANES), lambda i, sr, va: (i, 0)),
                  pl.BlockSpec((None, 1, LANES), lambda i, sr, va: (i, 0, 0)),
                  pl.BlockSpec((ROUTE_TILE, PLE_DIM), lambda i, sr, va: (i, 0)),
                  pl.BlockSpec((1, D_MODEL), lambda i, sr, va: (0, 0)),
                  pl.BlockSpec((1, D_MODEL), lambda i, sr, va: (0, 0)),
                  pl.BlockSpec((D_MODEL, D_MODEL), lambda i, sr, va: (0, 0)),
                  pl.BlockSpec((PLE_DIM, D_MODEL), lambda i, sr, va: (0, 0)),
                  pl.BlockSpec(memory_space=pl.ANY),
                  pl.BlockSpec(memory_space=pl.ANY)],
        out_specs=pl.BlockSpec((ROUTE_TILE, D_MODEL), lambda i, sr, va: (i, 0)),
        scratch_shapes=[pltpu.VMEM((2, N_EXPERTS * ROUTE_SEG, D_MODEL), BF16),
                        pltpu.VMEM((2, N_EXPERTS * ROUTE_SEG, D_MODEL), BF16),
                        pltpu.SemaphoreType.DMA((2, 2, N_EXPERTS))])
    return pl.pallas_call(
        _combine_kernel,
        grid_spec=grid_spec,
        out_shape=jax.ShapeDtypeStruct((t, D_MODEL), F32),
        compiler_params=_params("arbitrary"),
        name="moe_combine_ple",
    )(src_row, valid, x, slot, shift, p, npost, nple, wg, wp, yhi, ylo)


def _moe_layer_routed(y, x, p, w_out, nmix, npre, npost, nple, rw, rb, wg, wu, wd, ple_g, ple_p):
    t = x.shape[0]
    nt = t // ROUTE_TILE
    region = _route_region(t)
    x, slot, stats, srt = _route(y, x, w_out, nmix, npre, rw, rb)
    base = stats[:, 0, 0:N_EXPERTS].astype(jnp.int32)
    cnt = stats[:, 1, 0:N_EXPERTS].astype(jnp.int32)
    cnt_pad = (cnt + (ROW_ALIGN - 1)) // ROW_ALIGN * ROW_ALIGN
    total = base[-1] + cnt_pad[-1]
    nblk = (total + (ROUTE_BLOCK - 1)) // ROUTE_BLOCK
    cum = jnp.cumsum(nblk)
    n_used = cum[-1]
    max_rows = 2 * t + nt * N_EXPERTS * (ROW_ALIGN - 1)
    n_blocks = max_rows // ROUTE_BLOCK + N_EXPERTS
    kk = jnp.minimum(jnp.arange(n_blocks, dtype=jnp.int32), n_used - 1)
    blk_e = jnp.sum(kk[:, None] >= cum[None, :], axis=1).astype(jnp.int32)
    blk_row = blk_e * (region // ROUTE_BLOCK) + kk - (cum - nblk)[blk_e]
    yhi, ylo = _experts(srt, blk_row.astype(jnp.int32), blk_e, n_used.reshape(1).astype(jnp.int32),
                        wg, wu, wd, n_blocks)
    start = jnp.maximum(jnp.minimum(base, nblk[None, :] * ROUTE_BLOCK - ROUTE_SEG), 0)
    src_row = (jnp.arange(N_EXPERTS, dtype=jnp.int32)[None, :] * region + start).reshape(-1)
    valid = (cnt > 0).astype(jnp.int32).reshape(-1)
    shift = _pad_lanes((base - start).astype(F32)).reshape(nt, 1, LANES)
    return _combine(src_row.astype(jnp.int32), valid, x, slot, shift, p, npost, nple, ple_g, ple_p, yhi, ylo)


def _ple_kernel(x_ref, p_ref, nw_ref, wg_ref, wp_ref, o_ref):
    x = x_ref[...]
    gate = jax.nn.sigmoid(_dot(_rms(x, nw_ref[...]).astype(BF16), wg_ref[...]))
    o_ref[...] = x + gate * _dot(p_ref[...].astype(BF16), wp_ref[...])


def _ple(x, p, nw, wg, wp, tm):
    t = x.shape[0]
    return pl.pallas_call(
        _ple_kernel,
        grid=(t // tm,),
        in_specs=[pl.BlockSpec((tm, D_MODEL), lambda i: (i, 0)),
                  pl.BlockSpec((tm, PLE_DIM), lambda i: (i, 0)),
                  pl.BlockSpec((1, D_MODEL), lambda i: (0, 0)),
                  pl.BlockSpec((D_MODEL, D_MODEL), lambda i: (0, 0)),
                  pl.BlockSpec((PLE_DIM, D_MODEL), lambda i: (0, 0))],
        out_specs=pl.BlockSpec((tm, D_MODEL), lambda i: (i, 0)),
        out_shape=jax.ShapeDtypeStruct((t, D_MODEL), F32),
        compiler_params=_params("parallel"),
        name="ple",
    )(x, p, nw, wg, wp)


def _pad_lanes(a, width=LANES):
    return jnp.pad(a, [(0, 0)] * (a.ndim - 1) + [(0, width - a.shape[-1])])


def _permute_w_in(w):
    main = w[:, 0:2304]
    ig = w[:, 2304:2308]
    fg = w[:, 2308:2312]
    gu_gv = w[:, 2312:2824]
    return jnp.concatenate([main, gu_gv, _pad_lanes(ig), _pad_lanes(fg)], axis=1).astype(BF16)


def _block_diag(blocks):
    g, d, _ = blocks.shape
    out = jnp.zeros((g * d, g * d), blocks.dtype)
    for i in range(g):
        out = out.at[i * d:(i + 1) * d, i * d:(i + 1) * d].set(blocks[i])
    return out


def _row(a):
    return a.reshape(1, -1).astype(F32)


def kernel(x_prompt, x_sample, state_pool, state_mlstm_C, state_mlstm_n, state_mlstm_m, p_prompt, p_sample,
           norm_mix_pre, norm_mix_post, norm_ffn_pre, norm_ffn_post, norm_ple, w_in, pool_w, pool_scale,
           mlstm_b_i, mlstm_b_f, mlstm_norm_w, gmlp_norm_w, gmlp_ws, gmlp_bs, w_out,
           ffn_w_gate, ffn_w_up, ffn_w_down, moe_router_w, moe_router_b, moe_w_gate, moe_w_up, moe_w_down,
           ple_w_gate, ple_w_proj):
    batch, seq, _ = x_prompt.shape
    nseq = x_sample.shape[0]
    xp = x_prompt.reshape(batch * seq, D_MODEL)
    xs = x_sample.reshape(nseq, D_MODEL)
    gmean = _block_diag(jnp.full((GMLP_GROUPS, GMLP_GROUP_DIM, GMLP_GROUP_DIM), 1.0 / GMLP_GROUP_DIM, BF16))

    pools_p, cs_p, ns_p, ms_p = [], [], [], []
    pools_s, cs_s, ns_s, ms_s, gvs_s = [], [], [], [], []
    for i in range(DEPTH):
        w_in_p = _permute_w_in(w_in[i])
        w_out_b = w_out[i].astype(BF16)
        poolw = _block_diag(pool_w[i]).astype(BF16)
        shared = [poolw, _row(pool_scale[i]), _pad_lanes(_row(mlstm_b_i[i])), _pad_lanes(_row(mlstm_b_f[i])),
                  _row(mlstm_norm_w[i]), _row(gmlp_norm_w[i])]
        gbs_full = jnp.repeat(gmlp_bs[i].T, GMLP_GROUP_DIM, axis=1)
        consts_p = shared + [gmlp_ws[i], gbs_full, gmean]
        gw0 = jnp.repeat(gmlp_ws[i][:, 0, 0], GMLP_GROUP_DIM).reshape(1, GMLP_WIDTH)
        consts_s = shared + [gw0, gbs_full[0:1, :], gmean]
        ple_g = ple_w_gate[i].astype(BF16)
        ple_p = ple_w_proj[i].astype(BF16)
        j = i // 2
        if i % 2 == 0:
            ffn_g, ffn_u, ffn_d = (ffn_w_gate[j].astype(BF16), ffn_w_up[j].astype(BF16),
                                   ffn_w_down[j].astype(BF16))
        else:
            rw = _pad_lanes(moe_router_w[j]).astype(BF16)
            rb = _pad_lanes(_row(moe_router_b[j]))
            moe_g, moe_u, moe_d = moe_w_gate[j], moe_w_up[j], moe_w_down[j]

        z = _norm_matmul(xp, _row(norm_mix_pre[i]), w_in_p, TM_PROMPT)
        y, cn_new, m_new = _mixer_prompt(z, consts_p, batch, seq)
        pools_p.append(z.reshape(batch, seq, Z_WIDTH)[:, seq - POOL_STATE:, 0:POOL_WIDTH])
        cs_p.append(cn_new[..., 0:MLSTM_HEAD_DIM])
        ns_p.append(cn_new[..., MLSTM_HEAD_DIM])
        ms_p.append(m_new[:, 0, 0:MLSTM_HEADS])
        y = y.reshape(batch * seq, D_MODEL)
        pp = p_prompt[i].reshape(batch * seq, PLE_DIM)
        norms = (_row(norm_mix_post[i]), _row(norm_ffn_pre[i]), _row(norm_ffn_post[i]), _row(norm_ple[i]))
        if i % 2 == 0:
            xp = _dense_layer(y, xp, pp, w_out_b, *norms, ffn_g, ffn_u, ffn_d, ple_g, ple_p, TM_PROMPT)
        else:
            xp = _moe_layer_routed(y, xp, pp, w_out_b, *norms, rw, rb, moe_g, moe_u, moe_d, ple_g, ple_p)

        z = _norm_matmul(xs, _row(norm_mix_pre[i]), w_in_p, nseq)
        sp_t = jnp.transpose(state_pool[i], (1, 0, 2))
        y, c_new, n_new, m_new, gv = _mixer_sample(z, sp_t, state_mlstm_C[i],
                                                   state_mlstm_n[i].reshape(nseq, MLSTM_WIDTH),
                                                   _pad_lanes(state_mlstm_m[i]), consts_s)
        pools_s.append(jnp.concatenate([state_pool[i][:, 1:], z[:, None, 0:POOL_WIDTH]], axis=1))
        cs_s.append(c_new)
        ns_s.append(n_new.reshape(nseq, MLSTM_HEADS, MLSTM_HEAD_DIM))
        ms_s.append(m_new[:, 0:MLSTM_HEADS])
        gvs_s.append(gv[:, None, :])
        ps = p_sample[i].reshape(nseq, PLE_DIM)
        if i % 2 == 0:
            xs = _dense_layer(y, xs, ps, w_out_b, *norms, ffn_g, ffn_u, ffn_d, ple_g, ple_p, nseq)
        else:
            xs = _proj_norm_res(y, xs, w_out_b, norms[0], nseq)
            xs = _ffn_moe(xs, norms[1], norms[2], rw, rb, moe_g, moe_u, moe_d, nseq)
            xs = _ple(xs, ps, norms[3], ple_g, ple_p, nseq)

    return (xp.reshape(batch, seq, D_MODEL), xs.reshape(nseq, 1, D_MODEL),
            jnp.stack(pools_p), jnp.stack(cs_p), jnp.stack(ns_p), jnp.stack(ms_p),
            jnp.stack(pools_s), jnp.stack(cs_s), jnp.stack(ns_s), jnp.stack(ms_s), jnp.stack(gvs_s))
```

```python
import functools

import jax
import jax.numpy as jnp
from jax import lax
from jax.experimental import pallas as pl
from jax.experimental.pallas import tpu as pltpu

F32 = jnp.float32
BF16 = jnp.bfloat16

D_MODEL = 1024
DEPTH = 2
POOL_WIDTH = 256
POOL_WINDOWS = (2, 4, 8, 16)
POOL_GROUP_DIM = 64
POOL_STATE = 15
MLSTM_WIDTH = 512
MLSTM_HEADS = 4
MLSTM_HEAD_DIM = 128
CHUNK = 128
GMLP_WIDTH = 256
GMLP_GROUPS = 4
GMLP_GROUP_DIM = 64
D_FF = 2816
N_EXPERTS = 8
D_FF_EXPERT = 1408
PLE_DIM = 256
RMS_EPS = 1e-6
PAST_LEN = 16384

LANES = 128
SUBLANES = 8
VMEM_LIMIT = 48 * 1024 * 1024
VMEM_LIMIT_EXPERT_WEIGHTS = 58 * 1024 * 1024

Z_POOL = 0
Z_Q = 256
Z_K = 768
Z_V = 1280
Z_O = 1792
Z_GU = 2304
Z_GV = 2560
Z_IG = 2816
Z_FG = 2944
Z_WIDTH = 3072
Z_CHUNK = 512

TM_PROMPT = 512
FF_EXPERT_CHUNKS = ((0, 512), (512, 512), (1024, 384))
SAMPLE_BLOCK = 8
PROMPT_SEQ_PER_STEP = 2


def _params(*semantics, vmem_limit=VMEM_LIMIT):
    return pltpu.CompilerParams(dimension_semantics=semantics, vmem_limit_bytes=vmem_limit)


def _rms(x, w):
    return x * lax.rsqrt(jnp.mean(x * x, axis=-1, keepdims=True) + RMS_EPS) * w


def _log_sigmoid(x):
    return jnp.minimum(x, 0.0) - jnp.log1p(jnp.exp(-jnp.abs(x)))


def _dot(a, b):
    return jnp.dot(a, b, preferred_element_type=F32)


def _split3(x):
    hi = x.astype(BF16)
    rest = x - hi.astype(F32)
    mid = rest.astype(BF16)
    lo = (rest - mid.astype(F32)).astype(BF16)
    return hi, mid, lo


def _norm_matmul_kernel(x_ref, nw_ref, w_ref, o_ref, h_ref):
    h_ref[...] = _rms(x_ref[...], nw_ref[...]).astype(BF16)
    for n0 in range(0, Z_WIDTH, Z_CHUNK):
        o_ref[:, n0:n0 + Z_CHUNK] = _dot(h_ref[...], w_ref[:, n0:n0 + Z_CHUNK])


def _norm_matmul(x, nw, w, tm):
    t = x.shape[0]
    return pl.pallas_call(
        _norm_matmul_kernel,
        grid=(t // tm,),
        in_specs=[pl.BlockSpec((tm, D_MODEL), lambda i: (i, 0)),
                  pl.BlockSpec((1, D_MODEL), lambda i: (0, 0)),
                  pl.BlockSpec((D_MODEL, Z_WIDTH), lambda i: (0, 0))],
        out_specs=pl.BlockSpec((tm, Z_WIDTH), lambda i: (i, 0)),
        out_shape=jax.ShapeDtypeStruct((t, Z_WIDTH), F32),
        scratch_shapes=[pltpu.VMEM((tm, D_MODEL), BF16)],
        compiler_params=_params("parallel"),
        name="norm_in_proj",
    )(x, nw, w)


def _group_rms(v, gmean, w):
    hi, mid, lo = _split3(v * v)
    ms = _dot(hi, gmean) + _dot(mid, gmean) + _dot(lo, gmean)
    return v * lax.rsqrt(ms + RMS_EPS) * w


def _pool_tile(ext_ref, u_tile, col0, w_lo, w_hi, pos):
    acc = u_tile
    sums = {}
    for shift in range(1, w_hi):
        acc = acc + ext_ref[pl.ds(16 - shift, CHUNK), col0:col0 + LANES]
        if shift + 1 in (w_lo, w_hi):
            sums[shift + 1] = acc
    cnt_lo = jnp.minimum(w_lo, pos + 1).astype(F32)
    cnt_hi = jnp.minimum(w_hi, pos + 1).astype(F32)
    lane = lax.broadcasted_iota(jnp.int32, (CHUNK, LANES), 1)
    return jnp.where(lane < POOL_GROUP_DIM, sums[w_lo] / cnt_lo, sums[w_hi] / cnt_hi) - u_tile


def _mixer_prompt_kernel(z_ref, *refs):
    consts = refs[:9]
    y_ref, cn_ref, m_ref, ext_ref = refs[9:]

    @pl.when(pl.program_id(1) == 0)
    def _():
        ext_ref[:, 0:16, :] = jnp.zeros((PROMPT_SEQ_PER_STEP, 16, POOL_WIDTH), F32)
        cn_ref[...] = jnp.zeros(cn_ref.shape, F32)
        m_ref[...] = jnp.zeros(m_ref.shape, F32)

    for i in range(PROMPT_SEQ_PER_STEP):
        _mixer_prompt_body(z_ref.at[i], *consts, y_ref.at[i], cn_ref.at[i], m_ref.at[i], ext_ref.at[i])


def _mixer_prompt_body(z_ref, poolw_ref, pscale_ref, bi_ref, bf_ref, mnorm_ref, gnorm_ref,
                       gws_ref, gbs_ref, gmean_ref,
                       y_ref, cn_ref, m_ref, ext_ref):
    chunk = pl.program_id(1)
    row = lax.broadcasted_iota(jnp.int32, (CHUNK, CHUNK), 0)
    col = lax.broadcasted_iota(jnp.int32, (CHUNK, CHUNK), 1)
    causal = col <= row
    lane = col

    ext_ref[16:16 + CHUNK, :] = z_ref[:, Z_POOL:Z_POOL + POOL_WIDTH]
    pos = chunk * CHUNK + lax.broadcasted_iota(jnp.int32, (CHUNK, 1), 0)
    pooled = []
    for tile in range(2):
        col0 = tile * LANES
        u_tile = z_ref[:, Z_POOL + col0:Z_POOL + col0 + LANES]
        pooled.append(_pool_tile(ext_ref, u_tile, col0, POOL_WINDOWS[2 * tile],
                                 POOL_WINDOWS[2 * tile + 1], pos))
    pooled = jnp.concatenate(pooled, axis=1).astype(BF16)
    y_pool = _dot(pooled, poolw_ref[...]) * pscale_ref[...]
    y_ref[:, 0:POOL_WIDTH] = y_pool.astype(BF16)
    ext_ref[0:16, :] = ext_ref[CHUNK:CHUNK + 16, :]

    vn = _group_rms(z_ref[:, Z_GV:Z_GV + GMLP_WIDTH], gmean_ref[...], gnorm_ref[...]).astype(BF16)
    for tile in range(2):
        col0 = tile * LANES
        vt = vn[:, col0:col0 + LANES]
        w_a = jnp.where(causal, gws_ref[2 * tile], 0.0).astype(BF16)
        w_b = jnp.where(causal, gws_ref[2 * tile + 1], 0.0).astype(BF16)
        mixed = jnp.where(lane < GMLP_GROUP_DIM, _dot(w_a, vt), _dot(w_b, vt))
        gu = z_ref[:, Z_GU + col0:Z_GU + col0 + LANES]
        y_g = gu * (mixed + gbs_ref[:, col0:col0 + LANES])
        y_ref[:, 768 + col0:768 + col0 + LANES] = y_g.astype(BF16)

    ig = z_ref[:, Z_IG:Z_IG + LANES] + bi_ref[...]
    lf = _log_sigmoid(z_ref[:, Z_FG:Z_FG + LANES] + bf_ref[...])
    tri = jnp.where(causal, 1.0, 0.0).astype(BF16)
    lf_hi, lf_mid, lf_lo = _split3(lf)
    b = _dot(tri, lf_hi) + _dot(tri, lf_mid) + _dot(tri, lf_lo)
    m_prev = m_ref[...]
    g = b + m_prev
    r_t = jnp.transpose(ig - b)
    b_last = b[CHUNK - 1:CHUNK, :]
    ones_col = jnp.where(lane == 0, 1.0, 0.0).astype(BF16)
    m_new_row = m_prev
    for h in range(MLSTM_HEADS):
        c0 = h * MLSTM_HEAD_DIM
        q = z_ref[:, Z_Q + c0:Z_Q + c0 + MLSTM_HEAD_DIM].astype(BF16)
        k = z_ref[:, Z_K + c0:Z_K + c0 + MLSTM_HEAD_DIM] * (MLSTM_HEAD_DIM ** -0.5)
        v = z_ref[:, Z_V + c0:Z_V + c0 + MLSTM_HEAD_DIM].astype(BF16)
        o = z_ref[:, Z_O + c0:Z_O + c0 + MLSTM_HEAD_DIM]
        b_col = b[:, h:h + 1]
        dmat = jnp.where(causal, b_col + r_t[h:h + 1, :], -jnp.inf)
        g_col = g[:, h:h + 1]
        m_t = jnp.maximum(g_col, jnp.max(dmat, axis=1, keepdims=True))
        scores = lax.dot_general(q, k.astype(BF16), (((1,), (1,)), ((), ())),
                                 preferred_element_type=F32)
        wts = jnp.exp(dmat - m_t) * scores
        inter = jnp.exp(g_col - m_t)
        cn_h = cn_ref[h]
        q_cn = _dot(q, cn_h.astype(BF16))
        num = inter * q_cn[:, 0:MLSTM_HEAD_DIM] + _dot(wts.astype(BF16), v)
        den = inter * q_cn[:, MLSTM_HEAD_DIM:MLSTM_HEAD_DIM + 1] + jnp.sum(wts, axis=1, keepdims=True)
        hid = num / jnp.maximum(jnp.abs(den), jnp.exp(-m_t))
        hid = _rms(hid, mnorm_ref[:, c0:c0 + MLSTM_HEAD_DIM])
        y_ref[:, POOL_WIDTH + c0:POOL_WIDTH + c0 + MLSTM_HEAD_DIM] = (jax.nn.sigmoid(o) * hid).astype(BF16)
        m_new = m_t[CHUNK - 1:CHUNK, :]
        bl = b_last[:, h:h + 1]
        decay = jnp.exp(bl + m_prev[:, h:h + 1] - m_new)
        w_s = jnp.exp(bl - b_col + ig[:, h:h + 1] - m_new)
        kw = (k * w_s).astype(BF16)
        v_ext = jnp.concatenate([v, ones_col], axis=1)
        cn_ref[h] = decay * cn_h + lax.dot_general(kw, v_ext, (((0,), (0,)), ((), ())),
                                                   preferred_element_type=F32)
        m_new_row = jnp.where(lane[0:1, :] == h, m_new, m_new_row)
    m_ref[...] = m_new_row


def _mixer_prompt(z, consts, batch, seq):
    nc = seq // CHUNK
    hd = MLSTM_HEAD_DIM
    z3 = z.reshape(batch, seq, Z_WIDTH)
    ns = PROMPT_SEQ_PER_STEP
    const_specs = [pl.BlockSpec(a.shape, lambda b, c, nd=a.ndim: (0,) * nd) for a in consts]
    return pl.pallas_call(
        _mixer_prompt_kernel,
        grid=(batch // ns, nc),
        in_specs=[pl.BlockSpec((ns, CHUNK, Z_WIDTH), lambda b, c: (b, c, 0))] + const_specs,
        out_specs=[pl.BlockSpec((ns, CHUNK, D_MODEL), lambda b, c: (b, c, 0)),
                   pl.BlockSpec((ns, MLSTM_HEADS, hd, 2 * hd), lambda b, c: (b, 0, 0, 0)),
                   pl.BlockSpec((ns, 1, LANES), lambda b, c: (b, 0, 0))],
        out_shape=[jax.ShapeDtypeStruct((batch, seq, D_MODEL), BF16),
                   jax.ShapeDtypeStruct((batch, MLSTM_HEADS, hd, 2 * hd), F32),
                   jax.ShapeDtypeStruct((batch, 1, LANES), F32)],
        scratch_shapes=[pltpu.VMEM((ns, 16 + CHUNK, POOL_WIDTH), F32)],
        compiler_params=_params("parallel", "arbitrary"),
        name="mixer_prompt",
    )(z3, *consts)


def _mixer_sample_kernel(z_ref, sp_ref, c_ref, n_ref, m_ref, c_other_layers_ref,
                         poolw_ref, pscale_ref, bi_ref, bf_ref, mnorm_ref, gnorm_ref,
                         gw0_ref, gb0_ref, gmean_ref,
                         y_ref, cn_ref, nn_ref, mn_ref, gv_ref, tk_ref):
    del c_other_layers_ref
    nb = SAMPLE_BLOCK
    hd = MLSTM_HEAD_DIM
    lane = lax.broadcasted_iota(jnp.int32, (nb, LANES), 1)
    seq_id = lax.broadcasted_iota(jnp.int32, (nb, LANES), 0)

    pooled = []
    for tile in range(2):
        col0 = tile * LANES
        u_tile = z_ref[:, Z_POOL + col0:Z_POOL + col0 + LANES]
        w_lo, w_hi = POOL_WINDOWS[2 * tile], POOL_WINDOWS[2 * tile + 1]
        acc = u_tile
        sums = {}
        for shift in range(1, w_hi):
            acc = acc + sp_ref[POOL_STATE - shift, :, col0:col0 + LANES]
            if shift + 1 in (w_lo, w_hi):
                sums[shift + 1] = acc
        pooled.append(jnp.where(lane < POOL_GROUP_DIM, sums[w_lo] / float(w_lo), sums[w_hi] / float(w_hi)) - u_tile)
    pooled = jnp.concatenate(pooled, axis=1).astype(BF16)
    y_ref[:, 0:POOL_WIDTH] = (_dot(pooled, poolw_ref[...]) * pscale_ref[...]).astype(BF16)

    vn = _group_rms(z_ref[:, Z_GV:Z_GV + GMLP_WIDTH], gmean_ref[...], gnorm_ref[...])
    gv_ref[...] = vn
    y_g = z_ref[:, Z_GU:Z_GU + GMLP_WIDTH] * (gw0_ref[...] * vn + gb0_ref[...])
    y_ref[:, 768:768 + GMLP_WIDTH] = y_g.astype(BF16)

    ig = z_ref[:, Z_IG:Z_IG + LANES] + bi_ref[...]
    lf = _log_sigmoid(z_ref[:, Z_FG:Z_FG + LANES] + bf_ref[...])
    m_prev = m_ref[...]
    g = lf + m_prev
    m_t = jnp.maximum(g, ig)
    inter = jnp.exp(g - m_t)
    e_ig = jnp.exp(ig - m_t)
    floor = jnp.exp(-m_t)
    mn_ref[...] = m_t
    tk_ref[...] = jnp.zeros((LANES, LANES), F32)
    for h in range(MLSTM_HEADS):
        tk_ref[nb * h:nb * (h + 1), :] = z_ref[:, Z_K + h * hd:Z_K + (h + 1) * hd] * (hd ** -0.5)
    k_t = jnp.transpose(tk_ref[...])
    for h in range(MLSTM_HEADS):
        c0 = h * hd
        q_h = z_ref[:, Z_Q + c0:Z_Q + c0 + hd]
        k_h = tk_ref[nb * h:nb * (h + 1), :]
        v_h = z_ref[:, Z_V + c0:Z_V + c0 + hd]
        o_h = z_ref[:, Z_O + c0:Z_O + c0 + hd]
        n_h = n_ref[:, c0:c0 + hd]
        inter_b = jnp.broadcast_to(inter[:, h:h + 1], (nb, hd))
        e_b = jnp.broadcast_to(e_ig[:, h:h + 1], (nb, hd))
        floor_b = jnp.broadcast_to(floor[:, h:h + 1], (nb, hd))
        v_w = e_b * v_h
        q_b = q_h.astype(BF16)
        q_c = jnp.zeros((nb, hd), F32)
        for s in range(nb):
            c_sh = c_ref[s, h]
            q_c = jnp.where(seq_id == s, _dot(q_b, c_sh.astype(BF16)), q_c)
            col = nb * h + s
            cn_ref[s, h] = inter_b[s:s + 1, :] * c_sh + k_t[:, col:col + 1] * v_w[s:s + 1, :]
        wts = e_b * jnp.sum(q_h * k_h, axis=1, keepdims=True)
        num = inter_b * q_c + wts * v_h
        den = inter_b * jnp.sum(q_h * n_h, axis=1, keepdims=True) + wts
        hid = num / jnp.maximum(jnp.abs(den), floor_b)
        hid = _rms(hid, mnorm_ref[:, c0:c0 + hd])
        y_ref[:, POOL_WIDTH + c0:POOL_WIDTH + c0 + hd] = (jax.nn.sigmoid(o_h) * hid).astype(BF16)
        nn_ref[:, c0:c0 + hd] = inter_b * n_h + e_b * k_h


def _mixer_sample(z, sp_t, c_all, layer, c_new_all, n_state, m_pad, consts):
    nseq = z.shape[0]
    nb = SAMPLE_BLOCK
    hd = MLSTM_HEAD_DIM
    const_specs = [pl.BlockSpec(a.shape, lambda j, nd=a.ndim: (0,) * nd) for a in consts]
    c_spec = pl.BlockSpec((None, nb, MLSTM_HEADS, hd, hd), lambda j: (layer, j, 0, 0, 0))
    aliases = {} if c_new_all is None else {5: 1}
    return pl.pallas_call(
        _mixer_sample_kernel,
        grid=(nseq // nb,),
        in_specs=[pl.BlockSpec((nb, Z_WIDTH), lambda j: (j, 0)),
                  pl.BlockSpec((POOL_STATE, nb, POOL_WIDTH), lambda j: (0, j, 0)),
                  c_spec,
                  pl.BlockSpec((nb, MLSTM_WIDTH), lambda j: (j, 0)),
                  pl.BlockSpec((nb, LANES), lambda j: (j, 0)),
                  pl.BlockSpec(memory_space=pl.ANY)] + const_specs,
        out_specs=[pl.BlockSpec((nb, D_MODEL), lambda j: (j, 0)),
                   c_spec,
                   pl.BlockSpec((nb, MLSTM_WIDTH), lambda j: (j, 0)),
                   pl.BlockSpec((nb, LANES), lambda j: (j, 0)),
                   pl.BlockSpec((nb, GMLP_WIDTH), lambda j: (j, 0))],
        out_shape=[jax.ShapeDtypeStruct((nseq, D_MODEL), BF16),
                   jax.ShapeDtypeStruct(c_all.shape, F32),
                   jax.ShapeDtypeStruct((nseq, MLSTM_WIDTH), F32),
                   jax.ShapeDtypeStruct((nseq, LANES), F32),
                   jax.ShapeDtypeStruct((nseq, GMLP_WIDTH), F32)],
        scratch_shapes=[pltpu.VMEM((LANES, LANES), F32)],
        input_output_aliases=aliases,
        compiler_params=_params("parallel"),
        name="mixer_sample",
    )(z, sp_t, c_all, n_state, m_pad, c_all if c_new_all is None else c_new_all, *consts)


def _proj_norm_res_kernel(y_ref, x_ref, w_ref, nw_ref, o_ref):
    o_ref[...] = x_ref[...] + _rms(_dot(y_ref[...], w_ref[...]), nw_ref[...])


def _proj_norm_res(y, x, w, nw, tm):
    t = x.shape[0]
    return pl.pallas_call(
        _proj_norm_res_kernel,
        grid=(t // tm,),
        in_specs=[pl.BlockSpec((tm, D_MODEL), lambda i: (i, 0)),
                  pl.BlockSpec((tm, D_MODEL), lambda i: (i, 0)),
                  pl.BlockSpec((D_MODEL, D_MODEL), lambda i: (0, 0)),
                  pl.BlockSpec((1, D_MODEL), lambda i: (0, 0))],
        out_specs=pl.BlockSpec((tm, D_MODEL), lambda i: (i, 0)),
        out_shape=jax.ShapeDtypeStruct((t, D_MODEL), F32),
        compiler_params=_params("parallel"),
        name="out_proj",
    )(y, x, w, nw)


def _dense_layer_kernel(y_ref, x_ref, p_ref, wout_ref, nmix_ref, npre_ref, npost_ref, nple_ref,
                        wg_ref, wu_ref, wd_ref, pg_ref, pp_ref, o_ref, x1_ref, h_ref, acc_ref):
    j = pl.program_id(1)

    @pl.when(j == 0)
    def _():
        x1 = x_ref[...] + _rms(_dot(y_ref[...], wout_ref[...]), nmix_ref[...])
        x1_ref[...] = x1
        h_ref[...] = _rms(x1, npre_ref[...]).astype(BF16)
        acc_ref[...] = jnp.zeros(acc_ref.shape, F32)

    h = h_ref[...]
    y = None
    for f0, fw in FF_EXPERT_CHUNKS:
        gate = _dot(h, wg_ref[:, f0:f0 + fw])
        up = _dot(h, wu_ref[:, f0:f0 + fw])
        act = (gate * jax.nn.sigmoid(gate) * up).astype(BF16)
        part = _dot(act, wd_ref[f0:f0 + fw, :])
        y = part if y is None else y + part
    acc_ref[...] += y

    @pl.when(j == pl.num_programs(1) - 1)
    def _():
        x2 = x1_ref[...] + _rms(acc_ref[...], npost_ref[...])
        gate = jax.nn.sigmoid(_dot(_rms(x2, nple_ref[...]).astype(BF16), pg_ref[...]))
        o_ref[...] = x2 + gate * _dot(p_ref[...].astype(BF16), pp_ref[...])


def _dense_layer(y, x, p, w_out, nmix, npre, npost, nple, wg, wu, wd, ple_g, ple_p, tm):
    t = x.shape[0]
    half = D_FF // 2
    assert half == D_FF_EXPERT
    row_spec = lambda: pl.BlockSpec((1, D_MODEL), lambda i, j: (0, 0))
    return pl.pallas_call(
        _dense_layer_kernel,
        grid=(t // tm, 2),
        in_specs=[pl.BlockSpec((tm, D_MODEL), lambda i, j: (i, 0)),
                  pl.BlockSpec((tm, D_MODEL), lambda i, j: (i, 0)),
                  pl.BlockSpec((tm, PLE_DIM), lambda i, j: (i, 0)),
                  pl.BlockSpec((D_MODEL, D_MODEL), lambda i, j: (0, 0)),
                  row_spec(), row_spec(), row_spec(), row_spec(),
                  pl.BlockSpec((D_MODEL, half), lambda i, j: (0, j)),
                  pl.BlockSpec((D_MODEL, half), lambda i, j: (0, j)),
                  pl.BlockSpec((half, D_MODEL), lambda i, j: (j, 0)),
                  pl.BlockSpec((D_MODEL, D_MODEL), lambda i, j: (0, 0)),
                  pl.BlockSpec((PLE_DIM, D_MODEL), lambda i, j: (0, 0))],
        out_specs=pl.BlockSpec((tm, D_MODEL), lambda i, j: (i, 0)),
        out_shape=jax.ShapeDtypeStruct((t, D_MODEL), F32),
        scratch_shapes=[pltpu.VMEM((tm, D_MODEL), F32), pltpu.VMEM((tm, D_MODEL), BF16),
                        pltpu.VMEM((tm, D_MODEL), F32)],
        compiler_params=_params("parallel", "arbitrary", vmem_limit=VMEM_LIMIT_EXPERT_WEIGHTS),
        name="dense_layer",
    )(y, x, p, w_out, nmix, npre, npost, nple, wg, wu, wd, ple_g, ple_p)


def _router_gates(h, rw_ref, rb_ref):
    shape = (h.shape[0], LANES)
    lane = lax.broadcasted_iota(jnp.int32, shape, 1)
    lane_f = lane.astype(F32)
    logits = jnp.where(lane < N_EXPERTS, _dot(h, rw_ref[...]) + rb_ref[...], -jnp.inf)
    e = jnp.exp(logits - jnp.max(logits, axis=-1, keepdims=True))
    probs = e / jnp.sum(e, axis=-1, keepdims=True)
    p1 = jnp.max(probs, axis=-1, keepdims=True)
    i1 = jnp.min(jnp.where(probs == p1, lane_f, float(LANES)), axis=-1, keepdims=True)
    rest = jnp.where(lane_f == i1, -1.0, probs)
    p2 = jnp.max(rest, axis=-1, keepdims=True)
    i2 = jnp.min(jnp.where(rest == p2, lane_f, float(LANES)), axis=-1, keepdims=True)
    total = p1 + p2
    return jnp.where(lane_f == i1, p1 / total, 0.0) + jnp.where(lane_f == i2, p2 / total, 0.0)


def _ffn_moe_kernel(x_ref, npre_ref, npost_ref, rw_ref, rb_ref, wg_ref, wu_ref, wd_ref,
                    o_ref, h_ref, acc_ref, gates_ref):
    e = pl.program_id(1)

    @pl.when(e == 0)
    def _():
        h_ref[...] = _rms(x_ref[...], npre_ref[...]).astype(BF16)
        acc_ref[...] = jnp.zeros(acc_ref.shape, F32)
        gates_ref[...] = _router_gates(h_ref[...], rw_ref, rb_ref)

    h = h_ref[...]
    lane = lax.broadcasted_iota(jnp.int32, gates_ref.shape, 1)
    gate_col = jnp.sum(jnp.where(lane == e, gates_ref[...], 0.0), axis=-1, keepdims=True)
    y = None
    for f0, fw in FF_EXPERT_CHUNKS:
        gate = _dot(h, wg_ref[:, f0:f0 + fw].astype(BF16))
        up = _dot(h, wu_ref[:, f0:f0 + fw].astype(BF16))
        act = (gate * jax.nn.sigmoid(gate) * up).astype(BF16)
        part = _dot(act, wd_ref[f0:f0 + fw, :].astype(BF16))
        y = part if y is None else y + part
    acc_ref[...] += gate_col * y

    @pl.when(e == pl.num_programs(1) - 1)
    def _():
        o_ref[...] = x_ref[...] + _rms(acc_ref[...], npost_ref[...])


def _ffn_moe(x, npre, npost, rw, rb, wg, wu, wd, tm):
    t = x.shape[0]
    return pl.pallas_call(
        _ffn_moe_kernel,
        grid=(t // tm, N_EXPERTS),
        in_specs=[pl.BlockSpec((tm, D_MODEL), lambda i, e: (i, 0)),
                  pl.BlockSpec((1, D_MODEL), lambda i, e: (0, 0)),
                  pl.BlockSpec((1, D_MODEL), lambda i, e: (0, 0)),
                  pl.BlockSpec((D_MODEL, LANES), lambda i, e: (0, 0)),
                  pl.BlockSpec((1, LANES), lambda i, e: (0, 0)),
                  pl.BlockSpec((None, D_MODEL, D_FF_EXPERT), lambda i, e: (e, 0, 0)),
                  pl.BlockSpec((None, D_MODEL, D_FF_EXPERT), lambda i, e: (e, 0, 0)),
                  pl.BlockSpec((None, D_FF_EXPERT, D_MODEL), lambda i, e: (e, 0, 0))],
        out_specs=pl.BlockSpec((tm, D_MODEL), lambda i, e: (i, 0)),
        out_shape=jax.ShapeDtypeStruct((t, D_MODEL), F32),
        scratch_shapes=[pltpu.VMEM((tm, D_MODEL), BF16), pltpu.VMEM((tm, D_MODEL), F32),
                        pltpu.VMEM((tm, LANES), F32)],
        compiler_params=_params("parallel", "arbitrary", vmem_limit=VMEM_LIMIT_EXPERT_WEIGHTS),
        name="ffn_moe",
    )(x, npre, npost, rw, rb, wg, wu, wd)


ROUTE_TILE = 256
ROW_ALIGN = 16
ROUTE_SEG = ROUTE_TILE
ROUTE_PACK = 2 * ROUTE_TILE + N_EXPERTS * ROW_ALIGN
ROUTE_W = D_MODEL + 3 * LANES
ROUTE_BLOCK = 512
ROUTE_SEG_SHORT = 128
ROUTE_SHORT_MAX = ROUTE_SEG_SHORT


def _route_region(n_tokens):
    rows = n_tokens + (n_tokens // ROUTE_TILE) * (ROW_ALIGN - 1) + ROUTE_SEG + ROUTE_BLOCK
    return -(-rows // ROUTE_BLOCK) * ROUTE_BLOCK


def _lane_scalar(row, lane, e):
    return jnp.sum(jnp.where(lane == e, row, 0.0)).astype(jnp.int32)


def _route_kernel(y_ref, x_ref, wout_ref, nmix_ref, npre_ref, rw_ref, rb_ref,
                  x1_ref, slot_ref, stats_ref, srt_hbm,
                  stage_ref, runv_ref, run_ref, sem, *, region):
    i = pl.program_id(0)
    last = pl.num_programs(0) - 1
    cur = i % 2

    @pl.when(i == 0)
    def _():
        runv_ref[...] = jnp.zeros(runv_ref.shape, F32)
        stage_ref[:, ROUTE_PACK:, :] = jnp.zeros((2, ROUTE_SEG, ROUTE_W), BF16)
        for e in range(N_EXPERTS):
            run_ref[e] = 0

    x1 = x_ref[...] + _rms(_dot(y_ref[...], wout_ref[...]), nmix_ref[...])
    x1_ref[...] = x1
    h = _rms(x1, npre_ref[...]).astype(BF16)
    gates = _router_gates(h, rw_ref, rb_ref)
    sel = gates > 0.0
    ones = jnp.where(sel, 1.0, 0.0)
    trow = lax.broadcasted_iota(jnp.int32, (ROUTE_TILE, ROUTE_TILE), 0)
    tcol = lax.broadcasted_iota(jnp.int32, (ROUTE_TILE, ROUTE_TILE), 1)
    before = jnp.where(tcol < trow, 1.0, 0.0).astype(BF16)
    rank = _dot(before, ones.astype(BF16))
    cnt = jnp.sum(ones, axis=0, keepdims=True)
    cnt_pad = jnp.floor((cnt + (ROW_ALIGN - 1)) * (1.0 / ROW_ALIGN)) * ROW_ALIGN
    lrow = lax.broadcasted_iota(jnp.int32, (LANES, LANES), 0)
    lcol = lax.broadcasted_iota(jnp.int32, (LANES, LANES), 1)
    lower = jnp.where(lrow < lcol, 1.0, 0.0).astype(BF16)
    off = _dot(jnp.broadcast_to(cnt_pad, (SUBLANES, LANES)).astype(BF16), lower)[0:1, :]
    slot_ref[...] = jnp.where(sel, rank, -1.0)
    stats_ref[...] = jnp.zeros(stats_ref.shape, F32)
    stats_ref[0:1, :] = runv_ref[...]
    stats_ref[1:2, :] = cnt
    runv_ref[...] = runv_ref[...] + cnt_pad

    pos = jnp.where(sel, rank + off, -1.0)
    pos_t = jnp.concatenate([jnp.transpose(pos[0:LANES, :]), jnp.transpose(pos[LANES:2 * LANES, :])], axis=1)
    pos_a = jnp.max(pos_t, axis=0, keepdims=True)
    pos_b = jnp.max(jnp.where(pos_t == pos_a, -1.0, pos_t), axis=0, keepdims=True)
    prow = lax.broadcasted_iota(jnp.int32, (ROUTE_PACK, ROUTE_TILE), 0).astype(F32)
    perm = jnp.where((prow == pos_a) | (prow == pos_b), 1.0, 0.0).astype(BF16)
    g_hi, g_mid, g_lo = _split3(gates)
    rows = _dot(perm, jnp.concatenate([h, g_hi, g_mid, g_lo], axis=1))
    stage_ref[cur, 0:ROUTE_PACK, :] = rows.astype(BF16)

    lane = lax.broadcasted_iota(jnp.int32, (1, LANES), 1)

    def segment_copy(e, src_row, dst_row, slot):
        return pltpu.make_async_copy(
            stage_ref.at[slot, pl.ds(pl.multiple_of(src_row, ROW_ALIGN), ROUTE_SEG), :],
            srt_hbm.at[pl.ds(pl.multiple_of(dst_row, ROW_ALIGN), ROUTE_SEG), :],
            sem.at[e])

    @pl.when(i > 0)
    def _():
        for e in range(N_EXPERTS):
            segment_copy(e, 0, 0, 1 - cur).wait()

    for e in range(N_EXPERTS):
        segment_copy(e, _lane_scalar(off, lane, e), e * region + run_ref[e], cur).start()
        run_ref[e] = run_ref[e] + _lane_scalar(cnt_pad, lane, e)

    @pl.when(i == last)
    def _():
        for e in range(N_EXPERTS):
            segment_copy(e, 0, 0, cur).wait()
        stage_ref[1 - cur, 0:ROUTE_SEG, :] = jnp.zeros((ROUTE_SEG, ROUTE_W), BF16)
        for part in range(ROUTE_BLOCK // ROUTE_SEG):
            for e in range(N_EXPERTS):
                segment_copy(e, 0, e * region + run_ref[e] + part * ROUTE_SEG, 1 - cur).start()
            for e in range(N_EXPERTS):
                segment_copy(e, 0, 0, 1 - cur).wait()


def _route(y, x, w_out, nmix, npre, rw, rb):
    t = x.shape[0]
    nt = t // ROUTE_TILE
    region = _route_region(t)
    return pl.pallas_call(
        functools.partial(_route_kernel, region=region),
        grid=(nt,),
        in_specs=[pl.BlockSpec((ROUTE_TILE, D_MODEL), lambda i: (i, 0)),
                  pl.BlockSpec((ROUTE_TILE, D_MODEL), lambda i: (i, 0)),
                  pl.BlockSpec((D_MODEL, D_MODEL), lambda i: (0, 0)),
                  pl.BlockSpec((1, D_MODEL), lambda i: (0, 0)),
                  pl.BlockSpec((1, D_MODEL), lambda i: (0, 0)),
                  pl.BlockSpec((D_MODEL, LANES), lambda i: (0, 0)),
                  pl.BlockSpec((1, LANES), lambda i: (0, 0))],
        out_specs=[pl.BlockSpec((ROUTE_TILE, D_MODEL), lambda i: (i, 0)),
                   pl.BlockSpec((ROUTE_TILE, LANES), lambda i: (i, 0)),
                   pl.BlockSpec((None, SUBLANES, LANES), lambda i: (i, 0, 0)),
                   pl.BlockSpec(memory_space=pl.ANY)],
        out_shape=[jax.ShapeDtypeStruct((t, D_MODEL), F32),
                   jax.ShapeDtypeStruct((t, LANES), F32),
                   jax.ShapeDtypeStruct((nt, SUBLANES, LANES), F32),
                   jax.ShapeDtypeStruct((N_EXPERTS * region, ROUTE_W), BF16)],
        scratch_shapes=[pltpu.VMEM((2, ROUTE_PACK + ROUTE_SEG, ROUTE_W), BF16),
                        pltpu.VMEM((1, LANES), F32),
                        pltpu.SMEM((N_EXPERTS,), jnp.int32),
                        pltpu.SemaphoreType.DMA((N_EXPERTS,))],
        compiler_params=_params("arbitrary"),
        name="moe_route",
    )(y, x, w_out, nmix, npre, rw, rb)


def _experts_kernel(blk_row_ref, blk_e_ref, n_used_ref, srt_ref, wg_ref, wu_ref, wd_ref, yhi_ref, ylo_ref):
    k = pl.program_id(0)

    @pl.when(k < n_used_ref[0])
    def _():
        h = srt_ref[:, 0:D_MODEL]
        gate3 = (srt_ref[:, D_MODEL:D_MODEL + LANES].astype(F32)
                 + srt_ref[:, D_MODEL + LANES:D_MODEL + 2 * LANES].astype(F32)
                 + srt_ref[:, D_MODEL + 2 * LANES:D_MODEL + 3 * LANES].astype(F32))
        lane = lax.broadcasted_iota(jnp.int32, gate3.shape, 1)
        gate_col = jnp.sum(jnp.where(lane == blk_e_ref[k], gate3, 0.0), axis=-1, keepdims=True)
        y = None
        for f0, fw in FF_EXPERT_CHUNKS:
            gate = _dot(h, wg_ref[:, f0:f0 + fw].astype(BF16))
            up = _dot(h, wu_ref[:, f0:f0 + fw].astype(BF16))
            act = (gate * jax.nn.sigmoid(gate) * up).astype(BF16)
            part = _dot(act, wd_ref[f0:f0 + fw, :].astype(BF16))
            y = part if y is None else y + part
        y = gate_col * y
        hi = y.astype(BF16)
        yhi_ref[...] = hi
        ylo_ref[...] = (y - hi.astype(F32)).astype(BF16)


def _experts(srt, blk_row, blk_e, n_used, wg, wu, wd, n_blocks):
    rows = srt.shape[0]
    grid_spec = pltpu.PrefetchScalarGridSpec(
        num_scalar_prefetch=3,
        grid=(n_blocks,),
        in_specs=[pl.BlockSpec((ROUTE_BLOCK, ROUTE_W), lambda k, br, be, nu: (br[k], 0)),
                  pl.BlockSpec((None, D_MODEL, D_FF_EXPERT), lambda k, br, be, nu: (be[k], 0, 0)),
                  pl.BlockSpec((None, D_MODEL, D_FF_EXPERT), lambda k, br, be, nu: (be[k], 0, 0)),
                  pl.BlockSpec((None, D_FF_EXPERT, D_MODEL), lambda k, br, be, nu: (be[k], 0, 0))],
        out_specs=[pl.BlockSpec((ROUTE_BLOCK, D_MODEL), lambda k, br, be, nu: (br[k], 0)),
                   pl.BlockSpec((ROUTE_BLOCK, D_MODEL), lambda k, br, be, nu: (br[k], 0))])
    return pl.pallas_call(
        _experts_kernel,
        grid_spec=grid_spec,
        out_shape=[jax.ShapeDtypeStruct((rows, D_MODEL), BF16), jax.ShapeDtypeStruct((rows, D_MODEL), BF16)],
        compiler_params=_params("arbitrary", vmem_limit=VMEM_LIMIT_EXPERT_WEIGHTS),
        name="moe_experts",
    )(blk_row, blk_e, n_used, srt, wg, wu, wd)


def _combine_kernel(src_row_ref, valid_ref, short_ref, x_ref, slot_ref, shift_ref, p_ref, npost_ref, nple_ref,
                    wg_ref, wp_ref, yhi_hbm, ylo_hbm, o_ref, seg_hi_ref, seg_lo_ref, y_ref, sem):
    i = pl.program_id(0)
    nt = pl.num_programs(0)
    cur = i % 2

    def segment_copies(tile, slot, e, rows):
        src = pl.ds(pl.multiple_of(src_row_ref[tile * N_EXPERTS + e], ROW_ALIGN), rows)
        dst = pl.ds(e * rows, rows)
        return (pltpu.make_async_copy(yhi_hbm.at[src, :], seg_hi_ref.at[slot, dst, :], sem.at[slot, 0, e]),
                pltpu.make_async_copy(ylo_hbm.at[src, :], seg_lo_ref.at[slot, dst, :], sem.at[slot, 1, e]))

    def for_each_segment(tile, slot, action):
        for rows, is_short in ((ROUTE_SEG_SHORT, 1), (ROUTE_SEG, 0)):
            @pl.when(short_ref[tile] == is_short)
            def _():
                for e in range(N_EXPERTS):
                    @pl.when(valid_ref[tile * N_EXPERTS + e] > 0)
                    def _():
                        for c in segment_copies(tile, slot, e, rows):
                            action(c)

    @pl.when(i == 0)
    def _():
        seg_hi_ref[...] = jnp.zeros(seg_hi_ref.shape, BF16)
        seg_lo_ref[...] = jnp.zeros(seg_lo_ref.shape, BF16)
        for_each_segment(0, 0, lambda c: c.start())

    @pl.when(i + 1 < nt)
    def _():
        for_each_segment(i + 1, 1 - cur, lambda c: c.start())

    for_each_segment(i, cur, lambda c: c.wait())

    slot = slot_ref[...]
    where = jnp.where(slot >= 0.0, slot + shift_ref[...], -1.0)

    def gather(rows):
        seg_lane = lax.broadcasted_iota(jnp.int32, (ROUTE_TILE, rows), 1).astype(F32)
        perm = jnp.concatenate([jnp.where(where[:, e:e + 1] == seg_lane, 1.0, 0.0).astype(BF16)
                                for e in range(N_EXPERTS)], axis=1)
        k = N_EXPERTS * rows
        y_ref[...] = _dot(perm, seg_hi_ref[cur, 0:k, :]) + _dot(perm, seg_lo_ref[cur, 0:k, :])

    @pl.when(short_ref[i] == 1)
    def _():
        gather(ROUTE_SEG_SHORT)

    @pl.when(short_ref[i] == 0)
    def _():
        gather(ROUTE_SEG)

    x = x_ref[...] + _rms(y_ref[...], npost_ref[...])
    gate = jax.nn.sigmoid(_dot(_rms(x, nple_ref[...]).astype(BF16), wg_ref[...]))
    o_ref[...] = x + gate * _dot(p_ref[...].astype(BF16), wp_ref[...])


def _combine(src_row, valid, short, x, slot, shift, p, npost, nple, wg, wp, yhi, ylo):
    t = x.shape[0]
    grid_spec = pltpu.PrefetchScalarGridSpec(
        num_scalar_prefetch=3,
        grid=(t // ROUTE_TILE,),
        in_specs=[pl.BlockSpec((ROUTE_TILE, D_MODEL), lambda i, *_: (i, 0)),
                  pl.BlockSpec((ROUTE_TILE, LANES), lambda i, *_: (i, 0)),
                  pl.BlockSpec((None, 1, LANES), lambda i, *_: (i, 0, 0)),
                  pl.BlockSpec((ROUTE_TILE, PLE_DIM), lambda i, *_: (i, 0)),
                  pl.BlockSpec((1, D_MODEL), lambda i, *_: (0, 0)),
                  pl.BlockSpec((1, D_MODEL), lambda i, *_: (0, 0)),
                  pl.BlockSpec((D_MODEL, D_MODEL), lambda i, *_: (0, 0)),
                  pl.BlockSpec((PLE_DIM, D_MODEL), lambda i, *_: (0, 0)),
                  pl.BlockSpec(memory_space=pl.ANY),
                  pl.BlockSpec(memory_space=pl.ANY)],
        out_specs=pl.BlockSpec((ROUTE_TILE, D_MODEL), lambda i, *_: (i, 0)),
        scratch_shapes=[pltpu.VMEM((2, N_EXPERTS * ROUTE_SEG, D_MODEL), BF16),
                        pltpu.VMEM((2, N_EXPERTS * ROUTE_SEG, D_MODEL), BF16),
                        pltpu.VMEM((ROUTE_TILE, D_MODEL), F32),
                        pltpu.SemaphoreType.DMA((2, 2, N_EXPERTS))])
    return pl.pallas_call(
        _combine_kernel,
        grid_spec=grid_spec,
        out_shape=jax.ShapeDtypeStruct((t, D_MODEL), F32),
        compiler_params=_params("arbitrary"),
        name="moe_combine_ple",
    )(src_row, valid, short, x, slot, shift, p, npost, nple, wg, wp, yhi, ylo)


def _moe_layer_routed(y, x, p, w_out, nmix, npre, npost, nple, rw, rb, wg, wu, wd, ple_g, ple_p):
    t = x.shape[0]
    nt = t // ROUTE_TILE
    region = _route_region(t)
    x, slot, stats, srt = _route(y, x, w_out, nmix, npre, rw, rb)
    base = stats[:, 0, 0:N_EXPERTS].astype(jnp.int32)
    cnt = stats[:, 1, 0:N_EXPERTS].astype(jnp.int32)
    cnt_pad = (cnt + (ROW_ALIGN - 1)) // ROW_ALIGN * ROW_ALIGN
    total = base[-1] + cnt_pad[-1]
    nblk = (total + (ROUTE_BLOCK - 1)) // ROUTE_BLOCK
    cum = jnp.cumsum(nblk)
    n_used = cum[-1]
    max_rows = 2 * t + nt * N_EXPERTS * (ROW_ALIGN - 1)
    n_blocks = max_rows // ROUTE_BLOCK + N_EXPERTS
    kk = jnp.minimum(jnp.arange(n_blocks, dtype=jnp.int32), n_used - 1)
    blk_e = jnp.sum(kk[:, None] >= cum[None, :], axis=1).astype(jnp.int32)
    blk_row = blk_e * (region // ROUTE_BLOCK) + kk - (cum - nblk)[blk_e]
    yhi, ylo = _experts(srt, blk_row.astype(jnp.int32), blk_e, n_used.reshape(1).astype(jnp.int32),
                        wg, wu, wd, n_blocks)
    short = jnp.all(cnt_pad <= ROUTE_SHORT_MAX, axis=1)
    seg_rows = jnp.where(short, ROUTE_SEG_SHORT, ROUTE_SEG)[:, None]
    start = jnp.maximum(jnp.minimum(base, nblk[None, :] * ROUTE_BLOCK - seg_rows), 0)
    src_row = (jnp.arange(N_EXPERTS, dtype=jnp.int32)[None, :] * region + start).reshape(-1)
    valid = (cnt > 0).astype(jnp.int32).reshape(-1)
    shift = _pad_lanes((base - start).astype(F32)).reshape(nt, 1, LANES)
    return _combine(src_row.astype(jnp.int32), valid, short.astype(jnp.int32), x, slot, shift, p, npost, nple,
                    ple_g, ple_p, yhi, ylo)


def _ple_kernel(x_ref, p_ref, nw_ref, wg_ref, wp_ref, o_ref):
    x = x_ref[...]
    gate = jax.nn.sigmoid(_dot(_rms(x, nw_ref[...]).astype(BF16), wg_ref[...]))
    o_ref[...] = x + gate * _dot(p_ref[...].astype(BF16), wp_ref[...])


def _ple(x, p, nw, wg, wp, tm):
    t = x.shape[0]
    return pl.pallas_call(
        _ple_kernel,
        grid=(t // tm,),
        in_specs=[pl.BlockSpec((tm, D_MODEL), lambda i: (i, 0)),
                  pl.BlockSpec((tm, PLE_DIM), lambda i: (i, 0)),
                  pl.BlockSpec((1, D_MODEL), lambda i: (0, 0)),
                  pl.BlockSpec((D_MODEL, D_MODEL), lambda i: (0, 0)),
                  pl.BlockSpec((PLE_DIM, D_MODEL), lambda i: (0, 0))],
        out_specs=pl.BlockSpec((tm, D_MODEL), lambda i: (i, 0)),
        out_shape=jax.ShapeDtypeStruct((t, D_MODEL), F32),
        compiler_params=_params("parallel"),
        name="ple",
    )(x, p, nw, wg, wp)


def _pad_lanes(a, width=LANES):
    return jnp.pad(a, [(0, 0)] * (a.ndim - 1) + [(0, width - a.shape[-1])])


def _permute_w_in(w):
    main = w[:, 0:2304]
    ig = w[:, 2304:2308]
    fg = w[:, 2308:2312]
    gu_gv = w[:, 2312:2824]
    return jnp.concatenate([main, gu_gv, _pad_lanes(ig), _pad_lanes(fg)], axis=1).astype(BF16)


def _block_diag(blocks):
    g, d, _ = blocks.shape
    out = jnp.zeros((g * d, g * d), blocks.dtype)
    for i in range(g):
        out = out.at[i * d:(i + 1) * d, i * d:(i + 1) * d].set(blocks[i])
    return out


def _row(a):
    return a.reshape(1, -1).astype(F32)


def kernel(x_prompt, x_sample, state_pool, state_mlstm_C, state_mlstm_n, state_mlstm_m, p_prompt, p_sample,
           norm_mix_pre, norm_mix_post, norm_ffn_pre, norm_ffn_post, norm_ple, w_in, pool_w, pool_scale,
           mlstm_b_i, mlstm_b_f, mlstm_norm_w, gmlp_norm_w, gmlp_ws, gmlp_bs, w_out,
           ffn_w_gate, ffn_w_up, ffn_w_down, moe_router_w, moe_router_b, moe_w_gate, moe_w_up, moe_w_down,
           ple_w_gate, ple_w_proj):
    batch, seq, _ = x_prompt.shape
    nseq = x_sample.shape[0]
    xp = x_prompt.reshape(batch * seq, D_MODEL)
    xs = x_sample.reshape(nseq, D_MODEL)
    gmean = _block_diag(jnp.full((GMLP_GROUPS, GMLP_GROUP_DIM, GMLP_GROUP_DIM), 1.0 / GMLP_GROUP_DIM, BF16))

    pools_p, cs_p, ns_p, ms_p = [], [], [], []
    pools_s, ns_s, ms_s, gvs_s = [], [], [], []
    c_new_s = None
    for i in range(DEPTH):
        w_in_p = _permute_w_in(w_in[i])
        w_out_b = w_out[i].astype(BF16)
        poolw = _block_diag(pool_w[i]).astype(BF16)
        shared = [poolw, _row(pool_scale[i]), _pad_lanes(_row(mlstm_b_i[i])), _pad_lanes(_row(mlstm_b_f[i])),
                  _row(mlstm_norm_w[i]), _row(gmlp_norm_w[i])]
        gbs_full = jnp.repeat(gmlp_bs[i].T, GMLP_GROUP_DIM, axis=1)
        consts_p = shared + [gmlp_ws[i], gbs_full, gmean]
        gw0 = jnp.repeat(gmlp_ws[i][:, 0, 0], GMLP_GROUP_DIM).reshape(1, GMLP_WIDTH)
        consts_s = shared + [gw0, gbs_full[0:1, :], gmean]
        ple_g = ple_w_gate[i].astype(BF16)
        ple_p = ple_w_proj[i].astype(BF16)
        j = i // 2
        if i % 2 == 0:
            ffn_g, ffn_u, ffn_d = (ffn_w_gate[j].astype(BF16), ffn_w_up[j].astype(BF16),
                                   ffn_w_down[j].astype(BF16))
        else:
            rw = _pad_lanes(moe_router_w[j]).astype(BF16)
            rb = _pad_lanes(_row(moe_router_b[j]))
            moe_g, moe_u, moe_d = moe_w_gate[j], moe_w_up[j], moe_w_down[j]

        z = _norm_matmul(xp, _row(norm_mix_pre[i]), w_in_p, TM_PROMPT)
        y, cn_new, m_new = _mixer_prompt(z, consts_p, batch, seq)
        pools_p.append(z.reshape(batch, seq, Z_WIDTH)[:, seq - POOL_STATE:, 0:POOL_WIDTH])
        cs_p.append(cn_new[..., 0:MLSTM_HEAD_DIM])
        ns_p.append(cn_new[..., MLSTM_HEAD_DIM])
        ms_p.append(m_new[:, 0, 0:MLSTM_HEADS])
        y = y.reshape(batch * seq, D_MODEL)
        pp = p_prompt[i].reshape(batch * seq, PLE_DIM)
        norms = (_row(norm_mix_post[i]), _row(norm_ffn_pre[i]), _row(norm_ffn_post[i]), _row(norm_ple[i]))
        if i % 2 == 0:
            xp = _dense_layer(y, xp, pp, w_out_b, *norms, ffn_g, ffn_u, ffn_d, ple_g, ple_p, TM_PROMPT)
        else:
            xp = _moe_layer_routed(y, xp, pp, w_out_b, *norms, rw, rb, moe_g, moe_u, moe_d, ple_g, ple_p)

        z = _norm_matmul(xs, _row(norm_mix_pre[i]), w_in_p, nseq)
        sp_t = jnp.transpose(state_pool[i], (1, 0, 2))
        y, c_new_s, n_new, m_new, gv = _mixer_sample(z, sp_t, state_mlstm_C, i, c_new_s,
                                                     state_mlstm_n[i].reshape(nseq, MLSTM_WIDTH),
                                                     _pad_lanes(state_mlstm_m[i]), consts_s)
        pools_s.append(jnp.concatenate([state_pool[i][:, 1:], z[:, None, 0:POOL_WIDTH]], axis=1))
        ns_s.append(n_new.reshape(nseq, MLSTM_HEADS, MLSTM_HEAD_DIM))
        ms_s.append(m_new[:, 0:MLSTM_HEADS])
        gvs_s.append(gv[:, None, :])
        ps = p_sample[i].reshape(nseq, PLE_DIM)
        if i % 2 == 0:
            xs = _dense_layer(y, xs, ps, w_out_b, *norms, ffn_g, ffn_u, ffn_d, ple_g, ple_p, nseq)
        else:
            xs = _proj_norm_res(y, xs, w_out_b, norms[0], nseq)
            xs = _ffn_moe(xs, norms[1], norms[2], rw, rb, moe_g, moe_u, moe_d, nseq)
            xs = _ple(xs, ps, norms[3], ple_g, ple_p, nseq)

    return (xp.reshape(batch, seq, D_MODEL), xs.reshape(nseq, 1, D_MODEL),
            jnp.stack(pools_p), jnp.stack(cs_p), jnp.stack(ns_p), jnp.stack(ms_p),
            jnp.stack(pools_s), c_new_s, jnp.stack(ns_s), jnp.stack(ms_s), jnp.stack(gvs_s))
```

```python
import functools

import jax
import jax.numpy as jnp
import numpy as np
from jax import lax
from jax.experimental import pallas as pl
from jax.experimental.pallas import tpu as pltpu

F32 = jnp.float32
BF16 = jnp.bfloat16

D_MODEL = 1024
DEPTH = 2
POOL_WIDTH = 256
POOL_WINDOWS = (2, 4, 8, 16)
POOL_GROUP_DIM = 64
POOL_STATE = 15
POOL_PREV_ROWS = 16
MLSTM_WIDTH = 512
MLSTM_HEADS = 4
MLSTM_HEAD_DIM = 128
CHUNK = 128
GMLP_WIDTH = 256
GMLP_GROUPS = 4
GMLP_GROUP_DIM = 64
D_FF = 2816
N_EXPERTS = 8
D_FF_EXPERT = 1408
PLE_DIM = 256
RMS_EPS = 1e-6
PAST_LEN = 16384

LANES = 128
SUBLANES = 8
VMEM_LIMIT = 48 * 1024 * 1024
VMEM_LIMIT_EXPERT_WEIGHTS = 58 * 1024 * 1024

Z_POOL = 0
Z_Q = 256
Z_K = 768
Z_V = 1280
Z_O = 1792
Z_GU = 2304
Z_GV = 2560
Z_IG = 2816
Z_FG = 2944
Z_WIDTH = 3072
Z_CHUNK = 512

TM_PROMPT = 512
FF_EXPERT_CHUNKS = ((0, 512), (512, 512), (1024, 384))
SAMPLE_BLOCK = 8
PROMPT_SEQ_PER_STEP = 2


def _params(*semantics, vmem_limit=VMEM_LIMIT):
    return pltpu.CompilerParams(dimension_semantics=semantics, vmem_limit_bytes=vmem_limit)


def _rms(x, w):
    return x * lax.rsqrt(jnp.mean(x * x, axis=-1, keepdims=True) + RMS_EPS) * w


def _log_sigmoid(x):
    return jnp.minimum(x, 0.0) - jnp.log1p(jnp.exp(-jnp.abs(x)))


def _dot(a, b):
    return jnp.dot(a, b, preferred_element_type=F32)


def _split3(x):
    hi = x.astype(BF16)
    rest = x - hi.astype(F32)
    mid = rest.astype(BF16)
    lo = (rest - mid.astype(F32)).astype(BF16)
    return hi, mid, lo


def _norm_matmul_kernel(x_ref, nw_ref, w_ref, o_ref, h_ref):
    h_ref[...] = _rms(x_ref[...], nw_ref[...]).astype(BF16)
    for n0 in range(0, Z_WIDTH, Z_CHUNK):
        o_ref[:, n0:n0 + Z_CHUNK] = _dot(h_ref[...], w_ref[:, n0:n0 + Z_CHUNK])


def _norm_matmul(x, nw, w, tm):
    t = x.shape[0]
    return pl.pallas_call(
        _norm_matmul_kernel,
        grid=(t // tm,),
        in_specs=[pl.BlockSpec((tm, D_MODEL), lambda i: (i, 0)),
                  pl.BlockSpec((1, D_MODEL), lambda i: (0, 0)),
                  pl.BlockSpec((D_MODEL, Z_WIDTH), lambda i: (0, 0))],
        out_specs=pl.BlockSpec((tm, Z_WIDTH), lambda i: (i, 0)),
        out_shape=jax.ShapeDtypeStruct((t, Z_WIDTH), F32),
        scratch_shapes=[pltpu.VMEM((tm, D_MODEL), BF16)],
        compiler_params=_params("parallel"),
        name="norm_in_proj",
    )(x, nw, w)


def _group_rms(v, gmean, w):
    hi, mid, lo = _split3(v * v)
    ms = _dot(hi, gmean) + _dot(mid, gmean) + _dot(lo, gmean)
    return v * lax.rsqrt(ms + RMS_EPS) * w


def _pool_tile(band_ref, ext_pieces, u_tile, tile, col0, w_lo, w_hi, pos):
    sums = sum(_dot(band_ref[tile], piece[:, col0:col0 + LANES]) for piece in ext_pieces)
    cnt_lo = jnp.minimum(w_lo, pos + 1).astype(F32)
    cnt_hi = jnp.minimum(w_hi, pos + 1).astype(F32)
    lane = lax.broadcasted_iota(jnp.int32, (CHUNK, LANES), 1)
    return jnp.where(lane < POOL_GROUP_DIM, sums[0:CHUNK] / cnt_lo, sums[CHUNK:2 * CHUNK] / cnt_hi) - u_tile


def _pool_band_matrices():
    t = np.arange(CHUNK)[:, None] + POOL_PREV_ROWS
    r = np.arange(POOL_PREV_ROWS + CHUNK)[None, :]
    bands = [np.concatenate([(r <= t) & (r > t - w) for w in POOL_WINDOWS[2 * tile:2 * tile + 2]], axis=0)
             for tile in range(2)]
    return jnp.asarray(np.stack(bands), BF16)


def _mixer_prompt_kernel(z_ref, *refs):
    consts = refs[:10]
    y_ref, cn_ref, m_ref, ext_ref, gwm_ref = refs[10:]
    gws_ref = consts[6]

    @pl.when(pl.program_id(1) == 0)
    def _():
        ext_ref[:, 0:POOL_PREV_ROWS, :] = jnp.zeros((PROMPT_SEQ_PER_STEP, POOL_PREV_ROWS, POOL_WIDTH), F32)
        cn_ref[...] = jnp.zeros(cn_ref.shape, F32)
        m_ref[...] = jnp.zeros(m_ref.shape, F32)
        causal = (lax.broadcasted_iota(jnp.int32, (CHUNK, CHUNK), 1)
                  <= lax.broadcasted_iota(jnp.int32, (CHUNK, CHUNK), 0))
        for grp in range(GMLP_GROUPS):
            gwm_ref[grp] = jnp.where(causal, gws_ref[grp], 0.0).astype(BF16)

    for i in range(PROMPT_SEQ_PER_STEP):
        _mixer_prompt_body(z_ref.at[i], *consts, y_ref.at[i], cn_ref.at[i], m_ref.at[i], ext_ref.at[i], gwm_ref)


def _mixer_prompt_body(z_ref, poolw_ref, pscale_ref, bi_ref, bf_ref, mnorm_ref, gnorm_ref,
                       gws_ref, gbs_ref, gmean_ref, band_ref,
                       y_ref, cn_ref, m_ref, ext_ref, gwm_ref):
    del gws_ref
    chunk = pl.program_id(1)
    row = lax.broadcasted_iota(jnp.int32, (CHUNK, CHUNK), 0)
    col = lax.broadcasted_iota(jnp.int32, (CHUNK, CHUNK), 1)
    causal = col <= row
    lane = col

    ext_ref[POOL_PREV_ROWS:POOL_PREV_ROWS + CHUNK, :] = z_ref[:, Z_POOL:Z_POOL + POOL_WIDTH]
    ext_pieces = _split3(ext_ref[...])
    pos = chunk * CHUNK + lax.broadcasted_iota(jnp.int32, (CHUNK, 1), 0)
    pooled = []
    for tile in range(2):
        col0 = tile * LANES
        u_tile = z_ref[:, Z_POOL + col0:Z_POOL + col0 + LANES]
        pooled.append(_pool_tile(band_ref, ext_pieces, u_tile, tile, col0, POOL_WINDOWS[2 * tile],
                                 POOL_WINDOWS[2 * tile + 1], pos))
    pooled = jnp.concatenate(pooled, axis=1).astype(BF16)
    y_pool = _dot(pooled, poolw_ref[...]) * pscale_ref[...]
    y_ref[:, 0:POOL_WIDTH] = y_pool.astype(BF16)
    ext_ref[0:POOL_PREV_ROWS, :] = ext_ref[CHUNK:CHUNK + POOL_PREV_ROWS, :]

    vn = _group_rms(z_ref[:, Z_GV:Z_GV + GMLP_WIDTH], gmean_ref[...], gnorm_ref[...]).astype(BF16)
    for tile in range(2):
        col0 = tile * LANES
        vt = vn[:, col0:col0 + LANES]
        mixed = jnp.where(lane < GMLP_GROUP_DIM, _dot(gwm_ref[2 * tile], vt), _dot(gwm_ref[2 * tile + 1], vt))
        gu = z_ref[:, Z_GU + col0:Z_GU + col0 + LANES]
        y_g = gu * (mixed + gbs_ref[:, col0:col0 + LANES])
        y_ref[:, 768 + col0:768 + col0 + LANES] = y_g.astype(BF16)

    ig = z_ref[:, Z_IG:Z_IG + LANES] + bi_ref[...]
    lf = _log_sigmoid(z_ref[:, Z_FG:Z_FG + LANES] + bf_ref[...])
    tri = jnp.where(causal, 1.0, 0.0).astype(BF16)
    lf_hi, lf_mid, lf_lo = _split3(lf)
    b = _dot(tri, lf_hi) + _dot(tri, lf_mid) + _dot(tri, lf_lo)
    m_prev = m_ref[...]
    g = b + m_prev
    r_t = jnp.transpose(ig - b)
    b_last = b[CHUNK - 1:CHUNK, :]
    ones_col = jnp.where(lane == 0, 1.0, 0.0).astype(BF16)
    m_new_row = m_prev
    for h in range(MLSTM_HEADS):
        c0 = h * MLSTM_HEAD_DIM
        q = z_ref[:, Z_Q + c0:Z_Q + c0 + MLSTM_HEAD_DIM].astype(BF16)
        k = z_ref[:, Z_K + c0:Z_K + c0 + MLSTM_HEAD_DIM] * (MLSTM_HEAD_DIM ** -0.5)
        v = z_ref[:, Z_V + c0:Z_V + c0 + MLSTM_HEAD_DIM].astype(BF16)
        o = z_ref[:, Z_O + c0:Z_O + c0 + MLSTM_HEAD_DIM]
        b_col = b[:, h:h + 1]
        dmat = jnp.where(causal, b_col + r_t[h:h + 1, :], -jnp.inf)
        g_col = g[:, h:h + 1]
        m_t = jnp.maximum(g_col, jnp.max(dmat, axis=1, keepdims=True))
        scores = lax.dot_general(q, k.astype(BF16), (((1,), (1,)), ((), ())),
                                 preferred_element_type=F32)
        wts = jnp.exp(dmat - m_t) * scores
        inter = jnp.exp(g_col - m_t)
        cn_h = cn_ref[h]
        q_cn = _dot(q, cn_h.astype(BF16))
        num = inter * q_cn[:, 0:MLSTM_HEAD_DIM] + _dot(wts.astype(BF16), v)
        den = inter * q_cn[:, MLSTM_HEAD_DIM:MLSTM_HEAD_DIM + 1] + jnp.sum(wts, axis=1, keepdims=True)
        hid = num / jnp.maximum(jnp.abs(den), jnp.exp(-m_t))
        hid = _rms(hid, mnorm_ref[:, c0:c0 + MLSTM_HEAD_DIM])
        y_ref[:, POOL_WIDTH + c0:POOL_WIDTH + c0 + MLSTM_HEAD_DIM] = (jax.nn.sigmoid(o) * hid).astype(BF16)
        m_new = m_t[CHUNK - 1:CHUNK, :]
        bl = b_last[:, h:h + 1]
        decay = jnp.exp(bl + m_prev[:, h:h + 1] - m_new)
        w_s = jnp.exp(bl - b_col + ig[:, h:h + 1] - m_new)
        kw = (k * w_s).astype(BF16)
        v_ext = jnp.concatenate([v, ones_col], axis=1)
        cn_ref[h] = decay * cn_h + lax.dot_general(kw, v_ext, (((0,), (0,)), ((), ())),
                                                   preferred_element_type=F32)
        m_new_row = jnp.where(lane[0:1, :] == h, m_new, m_new_row)
    m_ref[...] = m_new_row


def _mixer_prompt(z, consts, batch, seq):
    nc = seq // CHUNK
    hd = MLSTM_HEAD_DIM
    z3 = z.reshape(batch, seq, Z_WIDTH)
    ns = PROMPT_SEQ_PER_STEP
    const_specs = [pl.BlockSpec(a.shape, lambda b, c, nd=a.ndim: (0,) * nd) for a in consts]
    return pl.pallas_call(
        _mixer_prompt_kernel,
        grid=(batch // ns, nc),
        in_specs=[pl.BlockSpec((ns, CHUNK, Z_WIDTH), lambda b, c: (b, c, 0))] + const_specs,
        out_specs=[pl.BlockSpec((ns, CHUNK, D_MODEL), lambda b, c: (b, c, 0)),
                   pl.BlockSpec((ns, MLSTM_HEADS, hd, 2 * hd), lambda b, c: (b, 0, 0, 0)),
                   pl.BlockSpec((ns, 1, LANES), lambda b, c: (b, 0, 0))],
        out_shape=[jax.ShapeDtypeStruct((batch, seq, D_MODEL), BF16),
                   jax.ShapeDtypeStruct((batch, MLSTM_HEADS, hd, 2 * hd), F32),
                   jax.ShapeDtypeStruct((batch, 1, LANES), F32)],
        scratch_shapes=[pltpu.VMEM((ns, POOL_PREV_ROWS + CHUNK, POOL_WIDTH), F32),
                        pltpu.VMEM((GMLP_GROUPS, CHUNK, CHUNK), BF16)],
        compiler_params=_params("parallel", "arbitrary"),
        name="mixer_prompt",
    )(z3, *consts)


def _mixer_sample_kernel(z_ref, sp_ref, c_ref, n_ref, m_ref, c_other_layers_ref,
                         poolw_ref, pscale_ref, bi_ref, bf_ref, mnorm_ref, gnorm_ref,
                         gw0_ref, gb0_ref, gmean_ref,
                         y_ref, cn_ref, nn_ref, mn_ref, gv_ref, tk_ref):
    del c_other_layers_ref
    nb = SAMPLE_BLOCK
    hd = MLSTM_HEAD_DIM
    lane = lax.broadcasted_iota(jnp.int32, (nb, LANES), 1)
    seq_id = lax.broadcasted_iota(jnp.int32, (nb, LANES), 0)

    pooled = []
    for tile in range(2):
        col0 = tile * LANES
        u_tile = z_ref[:, Z_POOL + col0:Z_POOL + col0 + LANES]
        w_lo, w_hi = POOL_WINDOWS[2 * tile], POOL_WINDOWS[2 * tile + 1]
        acc = u_tile
        sums = {}
        for shift in range(1, w_hi):
            acc = acc + sp_ref[POOL_STATE - shift, :, col0:col0 + LANES]
            if shift + 1 in (w_lo, w_hi):
                sums[shift + 1] = acc
        pooled.append(jnp.where(lane < POOL_GROUP_DIM, sums[w_lo] / float(w_lo), sums[w_hi] / float(w_hi)) - u_tile)
    pooled = jnp.concatenate(pooled, axis=1).astype(BF16)
    y_ref[:, 0:POOL_WIDTH] = (_dot(pooled, poolw_ref[...]) * pscale_ref[...]).astype(BF16)

    vn = _group_rms(z_ref[:, Z_GV:Z_GV + GMLP_WIDTH], gmean_ref[...], gnorm_ref[...])
    gv_ref[...] = vn
    y_g = z_ref[:, Z_GU:Z_GU + GMLP_WIDTH] * (gw0_ref[...] * vn + gb0_ref[...])
    y_ref[:, 768:768 + GMLP_WIDTH] = y_g.astype(BF16)

    ig = z_ref[:, Z_IG:Z_IG + LANES] + bi_ref[...]
    lf = _log_sigmoid(z_ref[:, Z_FG:Z_FG + LANES] + bf_ref[...])
    m_prev = m_ref[...]
    g = lf + m_prev
    m_t = jnp.maximum(g, ig)
    inter = jnp.exp(g - m_t)
    e_ig = jnp.exp(ig - m_t)
    floor = jnp.exp(-m_t)
    mn_ref[...] = m_t
    tk_ref[...] = jnp.zeros((LANES, LANES), F32)
    for h in range(MLSTM_HEADS):
        tk_ref[nb * h:nb * (h + 1), :] = z_ref[:, Z_K + h * hd:Z_K + (h + 1) * hd] * (hd ** -0.5)
    k_t = jnp.transpose(tk_ref[...])
    for h in range(MLSTM_HEADS):
        c0 = h * hd
        q_h = z_ref[:, Z_Q + c0:Z_Q + c0 + hd]
        k_h = tk_ref[nb * h:nb * (h + 1), :]
        v_h = z_ref[:, Z_V + c0:Z_V + c0 + hd]
        o_h = z_ref[:, Z_O + c0:Z_O + c0 + hd]
        n_h = n_ref[:, c0:c0 + hd]
        inter_b = jnp.broadcast_to(inter[:, h:h + 1], (nb, hd))
        e_b = jnp.broadcast_to(e_ig[:, h:h + 1], (nb, hd))
        floor_b = jnp.broadcast_to(floor[:, h:h + 1], (nb, hd))
        v_w = e_b * v_h
        q_b = q_h.astype(BF16)
        q_c = jnp.zeros((nb, hd), F32)
        for s in range(nb):
            c_sh = c_ref[s, h]
            q_c = jnp.where(seq_id == s, _dot(q_b, c_sh.astype(BF16)), q_c)
            col = nb * h + s
            cn_ref[s, h] = inter_b[s:s + 1, :] * c_sh + k_t[:, col:col + 1] * v_w[s:s + 1, :]
        wts = e_b * jnp.sum(q_h * k_h, axis=1, keepdims=True)
        num = inter_b * q_c + wts * v_h
        den = inter_b * jnp.sum(q_h * n_h, axis=1, keepdims=True) + wts
        hid = num / jnp.maximum(jnp.abs(den), floor_b)
        hid = _rms(hid, mnorm_ref[:, c0:c0 + hd])
        y_ref[:, POOL_WIDTH + c0:POOL_WIDTH + c0 + hd] = (jax.nn.sigmoid(o_h) * hid).astype(BF16)
        nn_ref[:, c0:c0 + hd] = inter_b * n_h + e_b * k_h


def _mixer_sample(z, sp_t, c_all, layer, c_new_all, n_state, m_pad, consts):
    nseq = z.shape[0]
    nb = SAMPLE_BLOCK
    hd = MLSTM_HEAD_DIM
    const_specs = [pl.BlockSpec(a.shape, lambda j, nd=a.ndim: (0,) * nd) for a in consts]
    c_spec = pl.BlockSpec((None, nb, MLSTM_HEADS, hd, hd), lambda j: (layer, j, 0, 0, 0))
    aliases = {} if c_new_all is None else {5: 1}
    return pl.pallas_call(
        _mixer_sample_kernel,
        grid=(nseq // nb,),
        in_specs=[pl.BlockSpec((nb, Z_WIDTH), lambda j: (j, 0)),
                  pl.BlockSpec((POOL_STATE, nb, POOL_WIDTH), lambda j: (0, j, 0)),
                  c_spec,
                  pl.BlockSpec((nb, MLSTM_WIDTH), lambda j: (j, 0)),
                  pl.BlockSpec((nb, LANES), lambda j: (j, 0)),
                  pl.BlockSpec(memory_space=pl.ANY)] + const_specs,
        out_specs=[pl.BlockSpec((nb, D_MODEL), lambda j: (j, 0)),
                   c_spec,
                   pl.BlockSpec((nb, MLSTM_WIDTH), lambda j: (j, 0)),
                   pl.BlockSpec((nb, LANES), lambda j: (j, 0)),
                   pl.BlockSpec((nb, GMLP_WIDTH), lambda j: (j, 0))],
        out_shape=[jax.ShapeDtypeStruct((nseq, D_MODEL), BF16),
                   jax.ShapeDtypeStruct(c_all.shape, F32),
                   jax.ShapeDtypeStruct((nseq, MLSTM_WIDTH), F32),
                   jax.ShapeDtypeStruct((nseq, LANES), F32),
                   jax.ShapeDtypeStruct((nseq, GMLP_WIDTH), F32)],
        scratch_shapes=[pltpu.VMEM((LANES, LANES), F32)],
        input_output_aliases=aliases,
        compiler_params=_params("parallel"),
        name="mixer_sample",
    )(z, sp_t, c_all, n_state, m_pad, c_all if c_new_all is None else c_new_all, *consts)


def _proj_norm_res_kernel(y_ref, x_ref, w_ref, nw_ref, o_ref):
    o_ref[...] = x_ref[...] + _rms(_dot(y_ref[...], w_ref[...]), nw_ref[...])


def _proj_norm_res(y, x, w, nw, tm):
    t = x.shape[0]
    return pl.pallas_call(
        _proj_norm_res_kernel,
        grid=(t // tm,),
        in_specs=[pl.BlockSpec((tm, D_MODEL), lambda i: (i, 0)),
                  pl.BlockSpec((tm, D_MODEL), lambda i: (i, 0)),
                  pl.BlockSpec((D_MODEL, D_MODEL), lambda i: (0, 0)),
                  pl.BlockSpec((1, D_MODEL), lambda i: (0, 0))],
        out_specs=pl.BlockSpec((tm, D_MODEL), lambda i: (i, 0)),
        out_shape=jax.ShapeDtypeStruct((t, D_MODEL), F32),
        compiler_params=_params("parallel"),
        name="out_proj",
    )(y, x, w, nw)


def _dense_layer_kernel(y_ref, x_ref, p_ref, wout_ref, nmix_ref, npre_ref, npost_ref, nple_ref,
                        wg_ref, wu_ref, wd_ref, pg_ref, pp_ref, o_ref, x1_ref, h_ref, acc_ref):
    j = pl.program_id(1)

    @pl.when(j == 0)
    def _():
        x1 = x_ref[...] + _rms(_dot(y_ref[...], wout_ref[...]), nmix_ref[...])
        x1_ref[...] = x1
        h_ref[...] = _rms(x1, npre_ref[...]).astype(BF16)
        acc_ref[...] = jnp.zeros(acc_ref.shape, F32)

    h = h_ref[...]
    y = None
    for f0, fw in FF_EXPERT_CHUNKS:
        gate = _dot(h, wg_ref[:, f0:f0 + fw])
        up = _dot(h, wu_ref[:, f0:f0 + fw])
        act = (gate * jax.nn.sigmoid(gate) * up).astype(BF16)
        part = _dot(act, wd_ref[f0:f0 + fw, :])
        y = part if y is None else y + part
    acc_ref[...] += y

    @pl.when(j == pl.num_programs(1) - 1)
    def _():
        x2 = x1_ref[...] + _rms(acc_ref[...], npost_ref[...])
        gate = jax.nn.sigmoid(_dot(_rms(x2, nple_ref[...]).astype(BF16), pg_ref[...]))
        o_ref[...] = x2 + gate * _dot(p_ref[...].astype(BF16), pp_ref[...])


def _dense_layer(y, x, p, w_out, nmix, npre, npost, nple, wg, wu, wd, ple_g, ple_p, tm):
    t = x.shape[0]
    half = D_FF // 2
    assert half == D_FF_EXPERT
    row_spec = lambda: pl.BlockSpec((1, D_MODEL), lambda i, j: (0, 0))
    return pl.pallas_call(
        _dense_layer_kernel,
        grid=(t // tm, 2),
        in_specs=[pl.BlockSpec((tm, D_MODEL), lambda i, j: (i, 0)),
                  pl.BlockSpec((tm, D_MODEL), lambda i, j: (i, 0)),
                  pl.BlockSpec((tm, PLE_DIM), lambda i, j: (i, 0)),
                  pl.BlockSpec((D_MODEL, D_MODEL), lambda i, j: (0, 0)),
                  row_spec(), row_spec(), row_spec(), row_spec(),
                  pl.BlockSpec((D_MODEL, half), lambda i, j: (0, j)),
                  pl.BlockSpec((D_MODEL, half), lambda i, j: (0, j)),
                  pl.BlockSpec((half, D_MODEL), lambda i, j: (j, 0)),
                  pl.BlockSpec((D_MODEL, D_MODEL), lambda i, j: (0, 0)),
                  pl.BlockSpec((PLE_DIM, D_MODEL), lambda i, j: (0, 0))],
        out_specs=pl.BlockSpec((tm, D_MODEL), lambda i, j: (i, 0)),
        out_shape=jax.ShapeDtypeStruct((t, D_MODEL), F32),
        scratch_shapes=[pltpu.VMEM((tm, D_MODEL), F32), pltpu.VMEM((tm, D_MODEL), BF16),
                        pltpu.VMEM((tm, D_MODEL), F32)],
        compiler_params=_params("parallel", "arbitrary", vmem_limit=VMEM_LIMIT_EXPERT_WEIGHTS),
        name="dense_layer",
    )(y, x, p, w_out, nmix, npre, npost, nple, wg, wu, wd, ple_g, ple_p)


def _router_gates(h, rw_ref, rb_ref):
    shape = (h.shape[0], LANES)
    lane = lax.broadcasted_iota(jnp.int32, shape, 1)
    lane_f = lane.astype(F32)
    logits = jnp.where(lane < N_EXPERTS, _dot(h, rw_ref[...]) + rb_ref[...], -jnp.inf)
    e = jnp.exp(logits - jnp.max(logits, axis=-1, keepdims=True))
    probs = e / jnp.sum(e, axis=-1, keepdims=True)
    p1 = jnp.max(probs, axis=-1, keepdims=True)
    i1 = jnp.min(jnp.where(probs == p1, lane_f, float(LANES)), axis=-1, keepdims=True)
    rest = jnp.where(lane_f == i1, -1.0, probs)
    p2 = jnp.max(rest, axis=-1, keepdims=True)
    i2 = jnp.min(jnp.where(rest == p2, lane_f, float(LANES)), axis=-1, keepdims=True)
    total = p1 + p2
    return jnp.where(lane_f == i1, p1 / total, 0.0) + jnp.where(lane_f == i2, p2 / total, 0.0)


def _ffn_moe_kernel(x_ref, npre_ref, npost_ref, rw_ref, rb_ref, wg_ref, wu_ref, wd_ref,
                    o_ref, h_ref, acc_ref, gates_ref):
    e = pl.program_id(1)

    @pl.when(e == 0)
    def _():
        h_ref[...] = _rms(x_ref[...], npre_ref[...]).astype(BF16)
        acc_ref[...] = jnp.zeros(acc_ref.shape, F32)
        gates_ref[...] = _router_gates(h_ref[...], rw_ref, rb_ref)

    h = h_ref[...]
    lane = lax.broadcasted_iota(jnp.int32, gates_ref.shape, 1)
    gate_col = jnp.sum(jnp.where(lane == e, gates_ref[...], 0.0), axis=-1, keepdims=True)
    y = None
    for f0, fw in FF_EXPERT_CHUNKS:
        gate = _dot(h, wg_ref[:, f0:f0 + fw].astype(BF16))
        up = _dot(h, wu_ref[:, f0:f0 + fw].astype(BF16))
        act = (gate * jax.nn.sigmoid(gate) * up).astype(BF16)
        part = _dot(act, wd_ref[f0:f0 + fw, :].astype(BF16))
        y = part if y is None else y + part
    acc_ref[...] += gate_col * y

    @pl.when(e == pl.num_programs(1) - 1)
    def _():
        o_ref[...] = x_ref[...] + _rms(acc_ref[...], npost_ref[...])


def _ffn_moe(x, npre, npost, rw, rb, wg, wu, wd, tm):
    t = x.shape[0]
    return pl.pallas_call(
        _ffn_moe_kernel,
        grid=(t // tm, N_EXPERTS),
        in_specs=[pl.BlockSpec((tm, D_MODEL), lambda i, e: (i, 0)),
                  pl.BlockSpec((1, D_MODEL), lambda i, e: (0, 0)),
                  pl.BlockSpec((1, D_MODEL), lambda i, e: (0, 0)),
                  pl.BlockSpec((D_MODEL, LANES), lambda i, e: (0, 0)),
                  pl.BlockSpec((1, LANES), lambda i, e: (0, 0)),
                  pl.BlockSpec((None, D_MODEL, D_FF_EXPERT), lambda i, e: (e, 0, 0)),
                  pl.BlockSpec((None, D_MODEL, D_FF_EXPERT), lambda i, e: (e, 0, 0)),
                  pl.BlockSpec((None, D_FF_EXPERT, D_MODEL), lambda i, e: (e, 0, 0))],
        out_specs=pl.BlockSpec((tm, D_MODEL), lambda i, e: (i, 0)),
        out_shape=jax.ShapeDtypeStruct((t, D_MODEL), F32),
        scratch_shapes=[pltpu.VMEM((tm, D_MODEL), BF16), pltpu.VMEM((tm, D_MODEL), F32),
                        pltpu.VMEM((tm, LANES), F32)],
        compiler_params=_params("parallel", "arbitrary", vmem_limit=VMEM_LIMIT_EXPERT_WEIGHTS),
        name="ffn_moe",
    )(x, npre, npost, rw, rb, wg, wu, wd)


ROUTE_TILE = 256
ROW_ALIGN = 16
ROUTE_SEG = ROUTE_TILE
ROUTE_PACK = 2 * ROUTE_TILE + N_EXPERTS * ROW_ALIGN
ROUTE_W = D_MODEL + 3 * LANES
ROUTE_BLOCK = 512
ROUTE_SUBTILES = 2
ROUTE_SEG_SHORT = 128
ROUTE_SHORT_MAX = ROUTE_SEG_SHORT


def _route_region(n_tokens):
    rows = n_tokens + (n_tokens // ROUTE_TILE) * (ROW_ALIGN - 1) + ROUTE_SEG + ROUTE_BLOCK
    return -(-rows // ROUTE_BLOCK) * ROUTE_BLOCK


def _lane_scalar(row, lane, e):
    return jnp.sum(jnp.where(lane == e, row, 0.0)).astype(jnp.int32)


def _route_tile(y_ref, x_ref, wout_ref, nmix_ref, npre_ref, rw_ref, rb_ref,
                x1_ref, slot_ref, stats_ref, stage_ref, runv_ref):
    x1 = x_ref[...] + _rms(_dot(y_ref[...], wout_ref[...]), nmix_ref[...])
    x1_ref[...] = x1
    h = _rms(x1, npre_ref[...]).astype(BF16)
    gates = _router_gates(h, rw_ref, rb_ref)
    sel = gates > 0.0
    ones = jnp.where(sel, 1.0, 0.0)
    trow = lax.broadcasted_iota(jnp.int32, (ROUTE_TILE, ROUTE_TILE), 0)
    tcol = lax.broadcasted_iota(jnp.int32, (ROUTE_TILE, ROUTE_TILE), 1)
    before = jnp.where(tcol < trow, 1.0, 0.0).astype(BF16)
    rank = _dot(before, ones.astype(BF16))
    cnt = jnp.sum(ones, axis=0, keepdims=True)
    cnt_pad = jnp.floor((cnt + (ROW_ALIGN - 1)) * (1.0 / ROW_ALIGN)) * ROW_ALIGN
    lrow = lax.broadcasted_iota(jnp.int32, (LANES, LANES), 0)
    lcol = lax.broadcasted_iota(jnp.int32, (LANES, LANES), 1)
    lower = jnp.where(lrow < lcol, 1.0, 0.0).astype(BF16)
    off = _dot(jnp.broadcast_to(cnt_pad, (SUBLANES, LANES)).astype(BF16), lower)[0:1, :]
    slot_ref[...] = jnp.where(sel, rank, -1.0)
    stats_ref[...] = jnp.zeros(stats_ref.shape, F32)
    stats_ref[0:1, :] = runv_ref[...]
    stats_ref[1:2, :] = cnt
    runv_ref[...] = runv_ref[...] + cnt_pad

    pos = jnp.where(sel, rank + off, -1.0)
    pos_t = jnp.concatenate([jnp.transpose(pos[0:LANES, :]), jnp.transpose(pos[LANES:2 * LANES, :])], axis=1)
    pos_a = jnp.max(pos_t, axis=0, keepdims=True)
    pos_b = jnp.max(jnp.where(pos_t == pos_a, -1.0, pos_t), axis=0, keepdims=True)
    prow = lax.broadcasted_iota(jnp.int32, (ROUTE_PACK, ROUTE_TILE), 0).astype(F32)
    perm = jnp.where((prow == pos_a) | (prow == pos_b), 1.0, 0.0).astype(BF16)
    g_hi, g_mid, g_lo = _split3(gates)
    rows = _dot(perm, jnp.concatenate([h, g_hi, g_mid, g_lo], axis=1))
    stage_ref[0:ROUTE_PACK, :] = rows.astype(BF16)
    return off, cnt_pad


def _route_kernel(y_ref, x_ref, wout_ref, nmix_ref, npre_ref, rw_ref, rb_ref,
                  x1_ref, slot_ref, stats_ref, srt_hbm,
                  stage_ref, runv_ref, run_ref, sem, *, region):
    i = pl.program_id(0)
    last_slot = ROUTE_SUBTILES - 1

    @pl.when(i == 0)
    def _():
        runv_ref[...] = jnp.zeros(runv_ref.shape, F32)
        stage_ref[:, ROUTE_PACK:, :] = jnp.zeros((ROUTE_SUBTILES, ROUTE_SEG, ROUTE_W), BF16)
        for e in range(N_EXPERTS):
            run_ref[e] = 0

    lane = lax.broadcasted_iota(jnp.int32, (1, LANES), 1)

    def segment_copy(e, src_row, dst_row, slot):
        return pltpu.make_async_copy(
            stage_ref.at[slot, pl.ds(pl.multiple_of(src_row, ROW_ALIGN), ROUTE_SEG), :],
            srt_hbm.at[pl.ds(pl.multiple_of(dst_row, ROW_ALIGN), ROUTE_SEG), :],
            sem.at[e])

    def wait_segments(slot):
        for e in range(N_EXPERTS):
            segment_copy(e, 0, 0, slot).wait()

    for sub in range(ROUTE_SUBTILES):
        tokens = pl.ds(sub * ROUTE_TILE, ROUTE_TILE)
        off, cnt_pad = _route_tile(y_ref.at[tokens], x_ref.at[tokens], wout_ref, nmix_ref, npre_ref, rw_ref, rb_ref,
                                   x1_ref.at[tokens], slot_ref.at[tokens], stats_ref.at[sub],
                                   stage_ref.at[sub], runv_ref)
        if sub == 0:
            @pl.when(i > 0)
            def _():
                wait_segments(last_slot)
        else:
            wait_segments(sub - 1)
        for e in range(N_EXPERTS):
            segment_copy(e, _lane_scalar(off, lane, e), e * region + run_ref[e], sub).start()
            run_ref[e] = run_ref[e] + _lane_scalar(cnt_pad, lane, e)

    @pl.when(i == pl.num_programs(0) - 1)
    def _():
        wait_segments(last_slot)
        stage_ref[0, 0:ROUTE_SEG, :] = jnp.zeros((ROUTE_SEG, ROUTE_W), BF16)
        for part in range(ROUTE_BLOCK // ROUTE_SEG):
            for e in range(N_EXPERTS):
                segment_copy(e, 0, e * region + run_ref[e] + part * ROUTE_SEG, 0).start()
            wait_segments(0)


def _route(y, x, w_out, nmix, npre, rw, rb):
    t = x.shape[0]
    nt = t // ROUTE_TILE
    step = ROUTE_SUBTILES * ROUTE_TILE
    region = _route_region(t)
    return pl.pallas_call(
        functools.partial(_route_kernel, region=region),
        grid=(t // step,),
        in_specs=[pl.BlockSpec((step, D_MODEL), lambda i: (i, 0)),
                  pl.BlockSpec((step, D_MODEL), lambda i: (i, 0)),
                  pl.BlockSpec((D_MODEL, D_MODEL), lambda i: (0, 0)),
                  pl.BlockSpec((1, D_MODEL), lambda i: (0, 0)),
                  pl.BlockSpec((1, D_MODEL), lambda i: (0, 0)),
                  pl.BlockSpec((D_MODEL, LANES), lambda i: (0, 0)),
                  pl.BlockSpec((1, LANES), lambda i: (0, 0))],
        out_specs=[pl.BlockSpec((step, D_MODEL), lambda i: (i, 0)),
                   pl.BlockSpec((step, LANES), lambda i: (i, 0)),
                   pl.BlockSpec((ROUTE_SUBTILES, SUBLANES, LANES), lambda i: (i, 0, 0)),
                   pl.BlockSpec(memory_space=pl.ANY)],
        out_shape=[jax.ShapeDtypeStruct((t, D_MODEL), F32),
                   jax.ShapeDtypeStruct((t, LANES), F32),
                   jax.ShapeDtypeStruct((nt, SUBLANES, LANES), F32),
                   jax.ShapeDtypeStruct((N_EXPERTS * region, ROUTE_W), BF16)],
        scratch_shapes=[pltpu.VMEM((ROUTE_SUBTILES, ROUTE_PACK + ROUTE_SEG, ROUTE_W), BF16),
                        pltpu.VMEM((1, LANES), F32),
                        pltpu.SMEM((N_EXPERTS,), jnp.int32),
                        pltpu.SemaphoreType.DMA((N_EXPERTS,))],
        compiler_params=_params("arbitrary"),
        name="moe_route",
    )(y, x, w_out, nmix, npre, rw, rb)


def _experts_kernel(blk_row_ref, blk_e_ref, n_used_ref, srt_ref, wg_ref, wu_ref, wd_ref, yhi_ref, ylo_ref):
    k = pl.program_id(0)

    @pl.when(k < n_used_ref[0])
    def _():
        h = srt_ref[:, 0:D_MODEL]
        gate3 = (srt_ref[:, D_MODEL:D_MODEL + LANES].astype(F32)
                 + srt_ref[:, D_MODEL + LANES:D_MODEL + 2 * LANES].astype(F32)
                 + srt_ref[:, D_MODEL + 2 * LANES:D_MODEL + 3 * LANES].astype(F32))
        lane = lax.broadcasted_iota(jnp.int32, gate3.shape, 1)
        gate_col = jnp.sum(jnp.where(lane == blk_e_ref[k], gate3, 0.0), axis=-1, keepdims=True)
        y = None
        for f0, fw in FF_EXPERT_CHUNKS:
            gate = _dot(h, wg_ref[:, f0:f0 + fw].astype(BF16))
            up = _dot(h, wu_ref[:, f0:f0 + fw].astype(BF16))
            act = (gate * jax.nn.sigmoid(gate) * up).astype(BF16)
            part = _dot(act, wd_ref[f0:f0 + fw, :].astype(BF16))
            y = part if y is None else y + part
        y = gate_col * y
        hi = y.astype(BF16)
        yhi_ref[...] = hi
        ylo_ref[...] = (y - hi.astype(F32)).astype(BF16)


def _experts(srt, blk_row, blk_e, n_used, wg, wu, wd, n_blocks):
    rows = srt.shape[0]
    grid_spec = pltpu.PrefetchScalarGridSpec(
        num_scalar_prefetch=3,
        grid=(n_blocks,),
        in_specs=[pl.BlockSpec((ROUTE_BLOCK, ROUTE_W), lambda k, br, be, nu: (br[k], 0)),
                  pl.BlockSpec((None, D_MODEL, D_FF_EXPERT), lambda k, br, be, nu: (be[k], 0, 0)),
                  pl.BlockSpec((None, D_MODEL, D_FF_EXPERT), lambda k, br, be, nu: (be[k], 0, 0)),
                  pl.BlockSpec((None, D_FF_EXPERT, D_MODEL), lambda k, br, be, nu: (be[k], 0, 0))],
        out_specs=[pl.BlockSpec((ROUTE_BLOCK, D_MODEL), lambda k, br, be, nu: (br[k], 0)),
                   pl.BlockSpec((ROUTE_BLOCK, D_MODEL), lambda k, br, be, nu: (br[k], 0))])
    return pl.pallas_call(
        _experts_kernel,
        grid_spec=grid_spec,
        out_shape=[jax.ShapeDtypeStruct((rows, D_MODEL), BF16), jax.ShapeDtypeStruct((rows, D_MODEL), BF16)],
        compiler_params=_params("arbitrary", vmem_limit=VMEM_LIMIT_EXPERT_WEIGHTS),
        name="moe_experts",
    )(blk_row, blk_e, n_used, srt, wg, wu, wd)


def _combine_kernel(src_row_ref, valid_ref, short_ref, x_ref, slot_ref, shift_ref, p_ref, npost_ref, nple_ref,
                    wg_ref, wp_ref, yhi_hbm, ylo_hbm, o_ref, seg_hi_ref, seg_lo_ref, y_ref, sem):
    i = pl.program_id(0)
    nt = pl.num_programs(0)
    cur = i % 2

    def segment_copies(tile, slot, e, rows):
        src = pl.ds(pl.multiple_of(src_row_ref[tile * N_EXPERTS + e], ROW_ALIGN), rows)
        dst = pl.ds(e * rows, rows)
        return (pltpu.make_async_copy(yhi_hbm.at[src, :], seg_hi_ref.at[slot, dst, :], sem.at[slot, 0, e]),
                pltpu.make_async_copy(ylo_hbm.at[src, :], seg_lo_ref.at[slot, dst, :], sem.at[slot, 1, e]))

    def for_each_segment(tile, slot, action):
        for rows, is_short in ((ROUTE_SEG_SHORT, 1), (ROUTE_SEG, 0)):
            @pl.when(short_ref[tile] == is_short)
            def _():
                for e in range(N_EXPERTS):
                    @pl.when(valid_ref[tile * N_EXPERTS + e] > 0)
                    def _():
                        for c in segment_copies(tile, slot, e, rows):
                            action(c)

    @pl.when(i == 0)
    def _():
        seg_hi_ref[...] = jnp.zeros(seg_hi_ref.shape, BF16)
        seg_lo_ref[...] = jnp.zeros(seg_lo_ref.shape, BF16)
        for_each_segment(0, 0, lambda c: c.start())

    @pl.when(i + 1 < nt)
    def _():
        for_each_segment(i + 1, 1 - cur, lambda c: c.start())

    for_each_segment(i, cur, lambda c: c.wait())

    slot = slot_ref[...]
    where = jnp.where(slot >= 0.0, slot + shift_ref[...], -1.0)

    def gather(rows):
        seg_lane = lax.broadcasted_iota(jnp.int32, (ROUTE_TILE, rows), 1).astype(F32)
        perm = jnp.concatenate([jnp.where(where[:, e:e + 1] == seg_lane, 1.0, 0.0).astype(BF16)
                                for e in range(N_EXPERTS)], axis=1)
        k = N_EXPERTS * rows
        y_ref[...] = _dot(perm, seg_hi_ref[cur, 0:k, :]) + _dot(perm, seg_lo_ref[cur, 0:k, :])

    @pl.when(short_ref[i] == 1)
    def _():
        gather(ROUTE_SEG_SHORT)

    @pl.when(short_ref[i] == 0)
    def _():
        gather(ROUTE_SEG)

    x = x_ref[...] + _rms(y_ref[...], npost_ref[...])
    gate = jax.nn.sigmoid(_dot(_rms(x, nple_ref[...]).astype(BF16), wg_ref[...]))
    o_ref[...] = x + gate * _dot(p_ref[...].astype(BF16), wp_ref[...])


def _combine(src_row, valid, short, x, slot, shift, p, npost, nple, wg, wp, yhi, ylo):
    t = x.shape[0]
    grid_spec = pltpu.PrefetchScalarGridSpec(
        num_scalar_prefetch=3,
        grid=(t // ROUTE_TILE,),
        in_specs=[pl.BlockSpec((ROUTE_TILE, D_MODEL), lambda i, *_: (i, 0)),
                  pl.BlockSpec((ROUTE_TILE, LANES), lambda i, *_: (i, 0)),
                  pl.BlockSpec((None, 1, LANES), lambda i, *_: (i, 0, 0)),
                  pl.BlockSpec((ROUTE_TILE, PLE_DIM), lambda i, *_: (i, 0)),
                  pl.BlockSpec((1, D_MODEL), lambda i, *_: (0, 0)),
                  pl.BlockSpec((1, D_MODEL), lambda i, *_: (0, 0)),
                  pl.BlockSpec((D_MODEL, D_MODEL), lambda i, *_: (0, 0)),
                  pl.BlockSpec((PLE_DIM, D_MODEL), lambda i, *_: (0, 0)),
                  pl.BlockSpec(memory_space=pl.ANY),
                  pl.BlockSpec(memory_space=pl.ANY)],
        out_specs=pl.BlockSpec((ROUTE_TILE, D_MODEL), lambda i, *_: (i, 0)),
        scratch_shapes=[pltpu.VMEM((2, N_EXPERTS * ROUTE_SEG, D_MODEL), BF16),
                        pltpu.VMEM((2, N_EXPERTS * ROUTE_SEG, D_MODEL), BF16),
                        pltpu.VMEM((ROUTE_TILE, D_MODEL), F32),
                        pltpu.SemaphoreType.DMA((2, 2, N_EXPERTS))])
    return pl.pallas_call(
        _combine_kernel,
        grid_spec=grid_spec,
        out_shape=jax.ShapeDtypeStruct((t, D_MODEL), F32),
        compiler_params=_params("arbitrary"),
        name="moe_combine_ple",
    )(src_row, valid, short, x, slot, shift, p, npost, nple, wg, wp, yhi, ylo)


def _moe_layer_routed(y, x, p, w_out, nmix, npre, npost, nple, rw, rb, wg, wu, wd, ple_g, ple_p):
    t = x.shape[0]
    nt = t // ROUTE_TILE
    region = _route_region(t)
    x, slot, stats, srt = _route(y, x, w_out, nmix, npre, rw, rb)
    base = stats[:, 0, 0:N_EXPERTS].astype(jnp.int32)
    cnt = stats[:, 1, 0:N_EXPERTS].astype(jnp.int32)
    cnt_pad = (cnt + (ROW_ALIGN - 1)) // ROW_ALIGN * ROW_ALIGN
    total = base[-1] + cnt_pad[-1]
    nblk = (total + (ROUTE_BLOCK - 1)) // ROUTE_BLOCK
    cum = jnp.cumsum(nblk)
    n_used = cum[-1]
    max_rows = 2 * t + nt * N_EXPERTS * (ROW_ALIGN - 1)
    n_blocks = max_rows // ROUTE_BLOCK + N_EXPERTS
    kk = jnp.minimum(jnp.arange(n_blocks, dtype=jnp.int32), n_used - 1)
    blk_e = jnp.sum(kk[:, None] >= cum[None, :], axis=1).astype(jnp.int32)
    blk_row = blk_e * (region // ROUTE_BLOCK) + kk - (cum - nblk)[blk_e]
    yhi, ylo = _experts(srt, blk_row.astype(jnp.int32), blk_e, n_used.reshape(1).astype(jnp.int32),
                        wg, wu, wd, n_blocks)
    short = jnp.all(cnt_pad <= ROUTE_SHORT_MAX, axis=1)
    seg_rows = jnp.where(short, ROUTE_SEG_SHORT, ROUTE_SEG)[:, None]
    start = jnp.maximum(jnp.minimum(base, nblk[None, :] * ROUTE_BLOCK - seg_rows), 0)
    src_row = (jnp.arange(N_EXPERTS, dtype=jnp.int32)[None, :] * region + start).reshape(-1)
    valid = (cnt > 0).astype(jnp.int32).reshape(-1)
    shift = _pad_lanes((base - start).astype(F32)).reshape(nt, 1, LANES)
    return _combine(src_row.astype(jnp.int32), valid, short.astype(jnp.int32), x, slot, shift, p, npost, nple,
                    ple_g, ple_p, yhi, ylo)


def _ple_kernel(x_ref, p_ref, nw_ref, wg_ref, wp_ref, o_ref):
    x = x_ref[...]
    gate = jax.nn.sigmoid(_dot(_rms(x, nw_ref[...]).astype(BF16), wg_ref[...]))
    o_ref[...] = x + gate * _dot(p_ref[...].astype(BF16), wp_ref[...])


def _ple(x, p, nw, wg, wp, tm):
    t = x.shape[0]
    return pl.pallas_call(
        _ple_kernel,
        grid=(t // tm,),
        in_specs=[pl.BlockSpec((tm, D_MODEL), lambda i: (i, 0)),
                  pl.BlockSpec((tm, PLE_DIM), lambda i: (i, 0)),
                  pl.BlockSpec((1, D_MODEL), lambda i: (0, 0)),
                  pl.BlockSpec((D_MODEL, D_MODEL), lambda i: (0, 0)),
                  pl.BlockSpec((PLE_DIM, D_MODEL), lambda i: (0, 0))],
        out_specs=pl.BlockSpec((tm, D_MODEL), lambda i: (i, 0)),
        out_shape=jax.ShapeDtypeStruct((t, D_MODEL), F32),
        compiler_params=_params("parallel"),
        name="ple",
    )(x, p, nw, wg, wp)


def _pad_lanes(a, width=LANES):
    return jnp.pad(a, [(0, 0)] * (a.ndim - 1) + [(0, width - a.shape[-1])])


def _permute_w_in(w):
    main = w[:, 0:2304]
    ig = w[:, 2304:2308]
    fg = w[:, 2308:2312]
    gu_gv = w[:, 2312:2824]
    return jnp.concatenate([main, gu_gv, _pad_lanes(ig), _pad_lanes(fg)], axis=1).astype(BF16)


def _block_diag(blocks):
    g, d, _ = blocks.shape
    out = jnp.zeros((g * d, g * d), blocks.dtype)
    for i in range(g):
        out = out.at[i * d:(i + 1) * d, i * d:(i + 1) * d].set(blocks[i])
    return out


def _row(a):
    return a.reshape(1, -1).astype(F32)


def kernel(x_prompt, x_sample, state_pool, state_mlstm_C, state_mlstm_n, state_mlstm_m, p_prompt, p_sample,
           norm_mix_pre, norm_mix_post, norm_ffn_pre, norm_ffn_post, norm_ple, w_in, pool_w, pool_scale,
           mlstm_b_i, mlstm_b_f, mlstm_norm_w, gmlp_norm_w, gmlp_ws, gmlp_bs, w_out,
           ffn_w_gate, ffn_w_up, ffn_w_down, moe_router_w, moe_router_b, moe_w_gate, moe_w_up, moe_w_down,
           ple_w_gate, ple_w_proj):
    batch, seq, _ = x_prompt.shape
    nseq = x_sample.shape[0]
    xp = x_prompt.reshape(batch * seq, D_MODEL)
    xs = x_sample.reshape(nseq, D_MODEL)
    gmean = _block_diag(jnp.full((GMLP_GROUPS, GMLP_GROUP_DIM, GMLP_GROUP_DIM), 1.0 / GMLP_GROUP_DIM, BF16))
    pool_bands = _pool_band_matrices()

    pools_p, cs_p, ns_p, ms_p = [], [], [], []
    pools_s, ns_s, ms_s, gvs_s = [], [], [], []
    c_new_s = None
    for i in range(DEPTH):
        w_in_p = _permute_w_in(w_in[i])
        w_out_b = w_out[i].astype(BF16)
        poolw = _block_diag(pool_w[i]).astype(BF16)
        shared = [poolw, _row(pool_scale[i]), _pad_lanes(_row(mlstm_b_i[i])), _pad_lanes(_row(mlstm_b_f[i])),
                  _row(mlstm_norm_w[i]), _row(gmlp_norm_w[i])]
        gbs_full = jnp.repeat(gmlp_bs[i].T, GMLP_GROUP_DIM, axis=1)
        consts_p = shared + [gmlp_ws[i], gbs_full, gmean, pool_bands]
        gw0 = jnp.repeat(gmlp_ws[i][:, 0, 0], GMLP_GROUP_DIM).reshape(1, GMLP_WIDTH)
        consts_s = shared + [gw0, gbs_full[0:1, :], gmean]
        ple_g = ple_w_gate[i].astype(BF16)
        ple_p = ple_w_proj[i].astype(BF16)
        j = i // 2
        if i % 2 == 0:
            ffn_g, ffn_u, ffn_d = (ffn_w_gate[j].astype(BF16), ffn_w_up[j].astype(BF16),
                                   ffn_w_down[j].astype(BF16))
        else:
            rw = _pad_lanes(moe_router_w[j]).astype(BF16)
            rb = _pad_lanes(_row(moe_router_b[j]))
            moe_g, moe_u, moe_d = moe_w_gate[j], moe_w_up[j], moe_w_down[j]

        z = _norm_matmul(xp, _row(norm_mix_pre[i]), w_in_p, TM_PROMPT)
        y, cn_new, m_new = _mixer_prompt(z, consts_p, batch, seq)
        pools_p.append(z.reshape(batch, seq, Z_WIDTH)[:, seq - POOL_STATE:, 0:POOL_WIDTH])
        cs_p.append(cn_new[..., 0:MLSTM_HEAD_DIM])
        ns_p.append(cn_new[..., MLSTM_HEAD_DIM])
        ms_p.append(m_new[:, 0, 0:MLSTM_HEADS])
        y = y.reshape(batch * seq, D_MODEL)
        pp = p_prompt[i].reshape(batch * seq, PLE_DIM)
        norms = (_row(norm_mix_post[i]), _row(norm_ffn_pre[i]), _row(norm_ffn_post[i]), _row(norm_ple[i]))
        if i % 2 == 0:
            xp = _dense_layer(y, xp, pp, w_out_b, *norms, ffn_g, ffn_u, ffn_d, ple_g, ple_p, TM_PROMPT)
        else:
            xp = _moe_layer_routed(y, xp, pp, w_out_b, *norms, rw, rb, moe_g, moe_u, moe_d, ple_g, ple_p)

        z = _norm_matmul(xs, _row(norm_mix_pre[i]), w_in_p, nseq)
        sp_t = jnp.transpose(state_pool[i], (1, 0, 2))
        y, c_new_s, n_new, m_new, gv = _mixer_sample(z, sp_t, state_mlstm_C, i, c_new_s,
                                                     state_mlstm_n[i].reshape(nseq, MLSTM_WIDTH),
                                                     _pad_lanes(state_mlstm_m[i]), consts_s)
        pools_s.append(jnp.concatenate([state_pool[i][:, 1:], z[:, None, 0:POOL_WIDTH]], axis=1))
        ns_s.append(n_new.reshape(nseq, MLSTM_HEADS, MLSTM_HEAD_DIM))
        ms_s.append(m_new[:, 0:MLSTM_HEADS])
        gvs_s.append(gv[:, None, :])
        ps = p_sample[i].reshape(nseq, PLE_DIM)
        if i % 2 == 0:
            xs = _dense_layer(y, xs, ps, w_out_b, *norms, ffn_g, ffn_u, ffn_d, ple_g, ple_p, nseq)
        else:
            xs = _proj_norm_res(y, xs, w_out_b, norms[0], nseq)
            xs = _ffn_moe(xs, norms[1], norms[2], rw, rb, moe_g, moe_u, moe_d, nseq)
            xs = _ple(xs, ps, norms[3], ple_g, ple_p, nseq)

    return (xp.reshape(batch, seq, D_MODEL), xs.reshape(nseq, 1, D_MODEL),
            jnp.stack(pools_p), jnp.stack(cs_p), jnp.stack(ns_p), jnp.stack(ms_p),
            jnp.stack(pools_s), c_new_s, jnp.stack(ns_s), jnp.stack(ms_s), jnp.stack(gvs_s))
```

```python
import functools

import jax
import jax.numpy as jnp
from jax import lax
from jax.experimental import pallas as pl
from jax.experimental.pallas import tpu as pltpu

F32 = jnp.float32
BF16 = jnp.bfloat16

D_MODEL = 1024
DEPTH = 2
POOL_WIDTH = 256
POOL_WINDOWS = (2, 4, 8, 16)
POOL_GROUP_DIM = 64
POOL_STATE = 15
MLSTM_WIDTH = 512
MLSTM_HEADS = 4
MLSTM_HEAD_DIM = 128
CHUNK = 128
GMLP_WIDTH = 256
GMLP_GROUPS = 4
GMLP_GROUP_DIM = 64
D_FF = 2816
N_EXPERTS = 8
D_FF_EXPERT = 1408
PLE_DIM = 256
RMS_EPS = 1e-6
PAST_LEN = 16384

LANES = 128
SUBLANES = 8
VMEM_LIMIT = 48 * 1024 * 1024
VMEM_LIMIT_EXPERT_WEIGHTS = 58 * 1024 * 1024

Z_POOL = 0
Z_Q = 256
Z_K = 768
Z_V = 1280
Z_O = 1792
Z_GU = 2304
Z_GV = 2560
Z_IG = 2816
Z_FG = 2944
Z_WIDTH = 3072
Z_CHUNK = 512

TM_PROMPT = 512
FF_CHUNK = 512
FF_EXPERT_CHUNKS = ((0, 512), (512, 512), (1024, 384))
SAMPLE_BLOCK = 8
PROMPT_SEQ_PER_STEP = 2


def _params(*semantics, vmem_limit=VMEM_LIMIT):
    return pltpu.CompilerParams(dimension_semantics=semantics, vmem_limit_bytes=vmem_limit)


def _rms(x, w):
    return x * lax.rsqrt(jnp.mean(x * x, axis=-1, keepdims=True) + RMS_EPS) * w


def _log_sigmoid(x):
    return jnp.minimum(x, 0.0) - jnp.log1p(jnp.exp(-jnp.abs(x)))


def _dot(a, b):
    return jnp.dot(a, b, preferred_element_type=F32)


def _split3(x):
    hi = x.astype(BF16)
    rest = x - hi.astype(F32)
    mid = rest.astype(BF16)
    lo = (rest - mid.astype(F32)).astype(BF16)
    return hi, mid, lo


def _resident(shape):
    return pl.BlockSpec(shape, lambda i: (0,) * len(shape), pipeline_mode=pl.Buffered(1))


def _norm_matmul_kernel(x_ref, nw_ref, w_ref, o_ref, h_ref):
    h_ref[...] = _rms(x_ref[...], nw_ref[...]).astype(BF16)
    for n0 in range(0, Z_WIDTH, Z_CHUNK):
        o_ref[:, n0:n0 + Z_CHUNK] = _dot(h_ref[...], w_ref[:, n0:n0 + Z_CHUNK])


def _norm_matmul(x, nw, w, tm):
    t = x.shape[0]
    return pl.pallas_call(
        _norm_matmul_kernel,
        grid=(t // tm,),
        in_specs=[pl.BlockSpec((tm, D_MODEL), lambda i: (i, 0)),
                  pl.BlockSpec((1, D_MODEL), lambda i: (0, 0)),
                  pl.BlockSpec((D_MODEL, Z_WIDTH), lambda i: (0, 0))],
        out_specs=pl.BlockSpec((tm, Z_WIDTH), lambda i: (i, 0)),
        out_shape=jax.ShapeDtypeStruct((t, Z_WIDTH), F32),
        scratch_shapes=[pltpu.VMEM((tm, D_MODEL), BF16)],
        compiler_params=_params("parallel"),
        name="norm_in_proj",
    )(x, nw, w)


def _group_rms(v, gmean, w):
    hi, mid, lo = _split3(v * v)
    ms = _dot(hi, gmean) + _dot(mid, gmean) + _dot(lo, gmean)
    return v * lax.rsqrt(ms + RMS_EPS) * w


def _pool_tile(ext_ref, u_tile, col0, w_lo, w_hi, pos):
    acc = u_tile
    sums = {}
    for shift in range(1, w_hi):
        acc = acc + ext_ref[pl.ds(16 - shift, CHUNK), col0:col0 + LANES]
        if shift + 1 in (w_lo, w_hi):
            sums[shift + 1] = acc
    cnt_lo = jnp.minimum(w_lo, pos + 1).astype(F32)
    cnt_hi = jnp.minimum(w_hi, pos + 1).astype(F32)
    lane = lax.broadcasted_iota(jnp.int32, (CHUNK, LANES), 1)
    return jnp.where(lane < POOL_GROUP_DIM, sums[w_lo] / cnt_lo, sums[w_hi] / cnt_hi) - u_tile


def _mixer_prompt_kernel(z_ref, *refs):
    consts = refs[:9]
    y_ref, cn_ref, m_ref, ext_ref = refs[9:]

    @pl.when(pl.program_id(1) == 0)
    def _():
        ext_ref[:, 0:16, :] = jnp.zeros((PROMPT_SEQ_PER_STEP, 16, POOL_WIDTH), F32)
        cn_ref[...] = jnp.zeros(cn_ref.shape, F32)
        m_ref[...] = jnp.zeros(m_ref.shape, F32)

    for i in range(PROMPT_SEQ_PER_STEP):
        _mixer_prompt_body(z_ref.at[i], *consts, y_ref.at[i], cn_ref.at[i], m_ref.at[i], ext_ref.at[i])


def _mixer_prompt_body(z_ref, poolw_ref, pscale_ref, bi_ref, bf_ref, mnorm_ref, gnorm_ref,
                       gws_ref, gbs_ref, gmean_ref,
                       y_ref, cn_ref, m_ref, ext_ref):
    chunk = pl.program_id(1)
    row = lax.broadcasted_iota(jnp.int32, (CHUNK, CHUNK), 0)
    col = lax.broadcasted_iota(jnp.int32, (CHUNK, CHUNK), 1)
    causal = col <= row
    lane = col

    ext_ref[16:16 + CHUNK, :] = z_ref[:, Z_POOL:Z_POOL + POOL_WIDTH]
    pos = chunk * CHUNK + lax.broadcasted_iota(jnp.int32, (CHUNK, 1), 0)
    pooled = []
    for tile in range(2):
        col0 = tile * LANES
        u_tile = z_ref[:, Z_POOL + col0:Z_POOL + col0 + LANES]
        pooled.append(_pool_tile(ext_ref, u_tile, col0, POOL_WINDOWS[2 * tile],
                                 POOL_WINDOWS[2 * tile + 1], pos))
    pooled = jnp.concatenate(pooled, axis=1).astype(BF16)
    y_pool = _dot(pooled, poolw_ref[...]) * pscale_ref[...]
    y_ref[:, 0:POOL_WIDTH] = y_pool.astype(BF16)
    ext_ref[0:16, :] = ext_ref[CHUNK:CHUNK + 16, :]

    vn = _group_rms(z_ref[:, Z_GV:Z_GV + GMLP_WIDTH], gmean_ref[...], gnorm_ref[...]).astype(BF16)
    for tile in range(2):
        col0 = tile * LANES
        vt = vn[:, col0:col0 + LANES]
        w_a = jnp.where(causal, gws_ref[2 * tile], 0.0).astype(BF16)
        w_b = jnp.where(causal, gws_ref[2 * tile + 1], 0.0).astype(BF16)
        mixed = jnp.where(lane < GMLP_GROUP_DIM, _dot(w_a, vt), _dot(w_b, vt))
        gu = z_ref[:, Z_GU + col0:Z_GU + col0 + LANES]
        y_g = gu * (mixed + gbs_ref[:, col0:col0 + LANES])
        y_ref[:, 768 + col0:768 + col0 + LANES] = y_g.astype(BF16)

    ig = z_ref[:, Z_IG:Z_IG + LANES] + bi_ref[...]
    lf = _log_sigmoid(z_ref[:, Z_FG:Z_FG + LANES] + bf_ref[...])
    tri = jnp.where(causal, 1.0, 0.0).astype(BF16)
    lf_hi, lf_mid, lf_lo = _split3(lf)
    b = _dot(tri, lf_hi) + _dot(tri, lf_mid) + _dot(tri, lf_lo)
    m_prev = m_ref[...]
    g = b + m_prev
    r_t = jnp.transpose(ig - b)
    b_last = b[CHUNK - 1:CHUNK, :]
    ones_col = jnp.where(lane == 0, 1.0, 0.0).astype(BF16)
    m_new_row = m_prev
    for h in range(MLSTM_HEADS):
        c0 = h * MLSTM_HEAD_DIM
        q = z_ref[:, Z_Q + c0:Z_Q + c0 + MLSTM_HEAD_DIM].astype(BF16)
        k = z_ref[:, Z_K + c0:Z_K + c0 + MLSTM_HEAD_DIM] * (MLSTM_HEAD_DIM ** -0.5)
        v = z_ref[:, Z_V + c0:Z_V + c0 + MLSTM_HEAD_DIM].astype(BF16)
        o = z_ref[:, Z_O + c0:Z_O + c0 + MLSTM_HEAD_DIM]
        b_col = b[:, h:h + 1]
        dmat = jnp.where(causal, b_col + r_t[h:h + 1, :], -jnp.inf)
        g_col = g[:, h:h + 1]
        m_t = jnp.maximum(g_col, jnp.max(dmat, axis=1, keepdims=True))
        scores = lax.dot_general(q, k.astype(BF16), (((1,), (1,)), ((), ())),
                                 preferred_element_type=F32)
        wts = jnp.exp(dmat - m_t) * scores
        inter = jnp.exp(g_col - m_t)
        cn_h = cn_ref[h]
        q_cn = _dot(q, cn_h.astype(BF16))
        num = inter * q_cn[:, 0:MLSTM_HEAD_DIM] + _dot(wts.astype(BF16), v)
        den = inter * q_cn[:, MLSTM_HEAD_DIM:MLSTM_HEAD_DIM + 1] + jnp.sum(wts, axis=1, keepdims=True)
        hid = num / jnp.maximum(jnp.abs(den), jnp.exp(-m_t))
        hid = _rms(hid, mnorm_ref[:, c0:c0 + MLSTM_HEAD_DIM])
        y_ref[:, POOL_WIDTH + c0:POOL_WIDTH + c0 + MLSTM_HEAD_DIM] = (jax.nn.sigmoid(o) * hid).astype(BF16)
        m_new = m_t[CHUNK - 1:CHUNK, :]
        bl = b_last[:, h:h + 1]
        decay = jnp.exp(bl + m_prev[:, h:h + 1] - m_new)
        w_s = jnp.exp(bl - b_col + ig[:, h:h + 1] - m_new)
        kw = (k * w_s).astype(BF16)
        v_ext = jnp.concatenate([v, ones_col], axis=1)
        cn_ref[h] = decay * cn_h + lax.dot_general(kw, v_ext, (((0,), (0,)), ((), ())),
                                                   preferred_element_type=F32)
        m_new_row = jnp.where(lane[0:1, :] == h, m_new, m_new_row)
    m_ref[...] = m_new_row


def _mixer_prompt(z, consts, batch, seq):
    nc = seq // CHUNK
    hd = MLSTM_HEAD_DIM
    z3 = z.reshape(batch, seq, Z_WIDTH)
    ns = PROMPT_SEQ_PER_STEP
    const_specs = [pl.BlockSpec(a.shape, lambda b, c, nd=a.ndim: (0,) * nd) for a in consts]
    return pl.pallas_call(
        _mixer_prompt_kernel,
        grid=(batch // ns, nc),
        in_specs=[pl.BlockSpec((ns, CHUNK, Z_WIDTH), lambda b, c: (b, c, 0))] + const_specs,
        out_specs=[pl.BlockSpec((ns, CHUNK, D_MODEL), lambda b, c: (b, c, 0)),
                   pl.BlockSpec((ns, MLSTM_HEADS, hd, 2 * hd), lambda b, c: (b, 0, 0, 0)),
                   pl.BlockSpec((ns, 1, LANES), lambda b, c: (b, 0, 0))],
        out_shape=[jax.ShapeDtypeStruct((batch, seq, D_MODEL), BF16),
                   jax.ShapeDtypeStruct((batch, MLSTM_HEADS, hd, 2 * hd), F32),
                   jax.ShapeDtypeStruct((batch, 1, LANES), F32)],
        scratch_shapes=[pltpu.VMEM((ns, 16 + CHUNK, POOL_WIDTH), F32)],
        compiler_params=_params("parallel", "arbitrary"),
        name="mixer_prompt",
    )(z3, *consts)


def _mixer_sample_kernel(z_ref, sp_ref, c_ref, n_ref, m_ref, c_other_layers_ref,
                         poolw_ref, pscale_ref, bi_ref, bf_ref, mnorm_ref, gnorm_ref,
                         gw0_ref, gb0_ref, gmean_ref,
                         y_ref, cn_ref, nn_ref, mn_ref, gv_ref, tk_ref):
    del c_other_layers_ref
    nb = SAMPLE_BLOCK
    hd = MLSTM_HEAD_DIM
    lane = lax.broadcasted_iota(jnp.int32, (nb, LANES), 1)
    seq_id = lax.broadcasted_iota(jnp.int32, (nb, LANES), 0)

    pooled = []
    for tile in range(2):
        col0 = tile * LANES
        u_tile = z_ref[:, Z_POOL + col0:Z_POOL + col0 + LANES]
        w_lo, w_hi = POOL_WINDOWS[2 * tile], POOL_WINDOWS[2 * tile + 1]
        acc = u_tile
        sums = {}
        for shift in range(1, w_hi):
            acc = acc + sp_ref[POOL_STATE - shift, :, col0:col0 + LANES]
            if shift + 1 in (w_lo, w_hi):
                sums[shift + 1] = acc
        pooled.append(jnp.where(lane < POOL_GROUP_DIM, sums[w_lo] / float(w_lo), sums[w_hi] / float(w_hi)) - u_tile)
    pooled = jnp.concatenate(pooled, axis=1).astype(BF16)
    y_ref[:, 0:POOL_WIDTH] = (_dot(pooled, poolw_ref[...]) * pscale_ref[...]).astype(BF16)

    vn = _group_rms(z_ref[:, Z_GV:Z_GV + GMLP_WIDTH], gmean_ref[...], gnorm_ref[...])
    gv_ref[...] = vn
    y_g = z_ref[:, Z_GU:Z_GU + GMLP_WIDTH] * (gw0_ref[...] * vn + gb0_ref[...])
    y_ref[:, 768:768 + GMLP_WIDTH] = y_g.astype(BF16)

    ig = z_ref[:, Z_IG:Z_IG + LANES] + bi_ref[...]
    lf = _log_sigmoid(z_ref[:, Z_FG:Z_FG + LANES] + bf_ref[...])
    m_prev = m_ref[...]
    g = lf + m_prev
    m_t = jnp.maximum(g, ig)
    inter = jnp.exp(g - m_t)
    e_ig = jnp.exp(ig - m_t)
    floor = jnp.exp(-m_t)
    mn_ref[...] = m_t
    tk_ref[...] = jnp.zeros((LANES, LANES), F32)
    for h in range(MLSTM_HEADS):
        tk_ref[nb * h:nb * (h + 1), :] = z_ref[:, Z_K + h * hd:Z_K + (h + 1) * hd] * (hd ** -0.5)
    k_t = jnp.transpose(tk_ref[...])
    for h in range(MLSTM_HEADS):
        c0 = h * hd
        q_h = z_ref[:, Z_Q + c0:Z_Q + c0 + hd]
        k_h = tk_ref[nb * h:nb * (h + 1), :]
        v_h = z_ref[:, Z_V + c0:Z_V + c0 + hd]
        o_h = z_ref[:, Z_O + c0:Z_O + c0 + hd]
        n_h = n_ref[:, c0:c0 + hd]
        inter_b = jnp.broadcast_to(inter[:, h:h + 1], (nb, hd))
        e_b = jnp.broadcast_to(e_ig[:, h:h + 1], (nb, hd))
        floor_b = jnp.broadcast_to(floor[:, h:h + 1], (nb, hd))
        v_w = e_b * v_h
        q_b = q_h.astype(BF16)
        q_c = jnp.zeros((nb, hd), F32)
        for s in range(nb):
            c_sh = c_ref[s, h]
            q_c = jnp.where(seq_id == s, _dot(q_b, c_sh.astype(BF16)), q_c)
            col = nb * h + s
            cn_ref[s, h] = inter_b[s:s + 1, :] * c_sh + k_t[:, col:col + 1] * v_w[s:s + 1, :]
        wts = e_b * jnp.sum(q_h * k_h, axis=1, keepdims=True)
        num = inter_b * q_c + wts * v_h
        den = inter_b * jnp.sum(q_h * n_h, axis=1, keepdims=True) + wts
        hid = num / jnp.maximum(jnp.abs(den), floor_b)
        hid = _rms(hid, mnorm_ref[:, c0:c0 + hd])
        y_ref[:, POOL_WIDTH + c0:POOL_WIDTH + c0 + hd] = (jax.nn.sigmoid(o_h) * hid).astype(BF16)
        nn_ref[:, c0:c0 + hd] = inter_b * n_h + e_b * k_h


def _mixer_sample(z, sp_t, c_all, layer, c_new_all, n_state, m_pad, consts):
    nseq = z.shape[0]
    nb = SAMPLE_BLOCK
    hd = MLSTM_HEAD_DIM
    const_specs = [pl.BlockSpec(a.shape, lambda j, nd=a.ndim: (0,) * nd) for a in consts]
    c_spec = pl.BlockSpec((None, nb, MLSTM_HEADS, hd, hd), lambda j: (layer, j, 0, 0, 0))
    aliases = {} if c_new_all is None else {5: 1}
    return pl.pallas_call(
        _mixer_sample_kernel,
        grid=(nseq // nb,),
        in_specs=[pl.BlockSpec((nb, Z_WIDTH), lambda j: (j, 0)),
                  pl.BlockSpec((POOL_STATE, nb, POOL_WIDTH), lambda j: (0, j, 0)),
                  c_spec,
                  pl.BlockSpec((nb, MLSTM_WIDTH), lambda j: (j, 0)),
                  pl.BlockSpec((nb, LANES), lambda j: (j, 0)),
                  pl.BlockSpec(memory_space=pl.ANY)] + const_specs,
        out_specs=[pl.BlockSpec((nb, D_MODEL), lambda j: (j, 0)),
                   c_spec,
                   pl.BlockSpec((nb, MLSTM_WIDTH), lambda j: (j, 0)),
                   pl.BlockSpec((nb, LANES), lambda j: (j, 0)),
                   pl.BlockSpec((nb, GMLP_WIDTH), lambda j: (j, 0))],
        out_shape=[jax.ShapeDtypeStruct((nseq, D_MODEL), BF16),
                   jax.ShapeDtypeStruct(c_all.shape, F32),
                   jax.ShapeDtypeStruct((nseq, MLSTM_WIDTH), F32),
                   jax.ShapeDtypeStruct((nseq, LANES), F32),
                   jax.ShapeDtypeStruct((nseq, GMLP_WIDTH), F32)],
        scratch_shapes=[pltpu.VMEM((LANES, LANES), F32)],
        input_output_aliases=aliases,
        compiler_params=_params("parallel"),
        name="mixer_sample",
    )(z, sp_t, c_all, n_state, m_pad, c_all if c_new_all is None else c_new_all, *consts)


def _proj_norm_res_kernel(y_ref, x_ref, w_ref, nw_ref, o_ref):
    o_ref[...] = x_ref[...] + _rms(_dot(y_ref[...], w_ref[...]), nw_ref[...])


def _proj_norm_res(y, x, w, nw, tm):
    t = x.shape[0]
    return pl.pallas_call(
        _proj_norm_res_kernel,
        grid=(t // tm,),
        in_specs=[pl.BlockSpec((tm, D_MODEL), lambda i: (i, 0)),
                  pl.BlockSpec((tm, D_MODEL), lambda i: (i, 0)),
                  pl.BlockSpec((D_MODEL, D_MODEL), lambda i: (0, 0)),
                  pl.BlockSpec((1, D_MODEL), lambda i: (0, 0))],
        out_specs=pl.BlockSpec((tm, D_MODEL), lambda i: (i, 0)),
        out_shape=jax.ShapeDtypeStruct((t, D_MODEL), F32),
        compiler_params=_params("parallel"),
        name="out_proj",
    )(y, x, w, nw)


def _dense_layer_kernel(y_ref, x_ref, p_ref, wout_ref, nmix_ref, npre_ref, npost_ref, nple_ref,
                        wg_ref, wu_ref, wd_ref, pg_ref, pp_ref, o_ref):
    x1 = x_ref[...] + _rms(_dot(y_ref[...], wout_ref[...]), nmix_ref[...])
    h = _rms(x1, npre_ref[...]).astype(BF16)
    y = None
    for f0 in range(0, D_FF, FF_CHUNK):
        fw = min(FF_CHUNK, D_FF - f0)
        gate = _dot(h, wg_ref[:, f0:f0 + fw])
        up = _dot(h, wu_ref[:, f0:f0 + fw])
        act = (gate * jax.nn.sigmoid(gate) * up).astype(BF16)
        part = _dot(act, wd_ref[f0:f0 + fw, :])
        y = part if y is None else y + part
    x2 = x1 + _rms(y, npost_ref[...])
    gate = jax.nn.sigmoid(_dot(_rms(x2, nple_ref[...]).astype(BF16), pg_ref[...]))
    o_ref[...] = x2 + gate * _dot(p_ref[...].astype(BF16), pp_ref[...])


def _dense_layer(y, x, p, w_out, nmix, npre, npost, nple, wg, wu, wd, ple_g, ple_p, tm):
    t = x.shape[0]
    return pl.pallas_call(
        _dense_layer_kernel,
        grid=(t // tm,),
        in_specs=[pl.BlockSpec((tm, D_MODEL), lambda i: (i, 0)),
                  pl.BlockSpec((tm, D_MODEL), lambda i: (i, 0)),
                  pl.BlockSpec((tm, PLE_DIM), lambda i: (i, 0)),
                  _resident((D_MODEL, D_MODEL)),
                  _resident((1, D_MODEL)), _resident((1, D_MODEL)), _resident((1, D_MODEL)), _resident((1, D_MODEL)),
                  _resident((D_MODEL, D_FF)), _resident((D_MODEL, D_FF)), _resident((D_FF, D_MODEL)),
                  _resident((D_MODEL, D_MODEL)), _resident((PLE_DIM, D_MODEL))],
        out_specs=pl.BlockSpec((tm, D_MODEL), lambda i: (i, 0)),
        out_shape=jax.ShapeDtypeStruct((t, D_MODEL), F32),
        compiler_params=_params("parallel"),
        name="dense_layer",
    )(y, x, p, w_out, nmix, npre, npost, nple, wg, wu, wd, ple_g, ple_p)


def _router_gates(h, rw_ref, rb_ref):
    shape = (h.shape[0], LANES)
    lane = lax.broadcasted_iota(jnp.int32, shape, 1)
    lane_f = lane.astype(F32)
    logits = jnp.where(lane < N_EXPERTS, _dot(h, rw_ref[...]) + rb_ref[...], -jnp.inf)
    e = jnp.exp(logits - jnp.max(logits, axis=-1, keepdims=True))
    probs = e / jnp.sum(e, axis=-1, keepdims=True)
    p1 = jnp.max(probs, axis=-1, keepdims=True)
    i1 = jnp.min(jnp.where(probs == p1, lane_f, float(LANES)), axis=-1, keepdims=True)
    rest = jnp.where(lane_f == i1, -1.0, probs)
    p2 = jnp.max(rest, axis=-1, keepdims=True)
    i2 = jnp.min(jnp.where(rest == p2, lane_f, float(LANES)), axis=-1, keepdims=True)
    total = p1 + p2
    return jnp.where(lane_f == i1, p1 / total, 0.0) + jnp.where(lane_f == i2, p2 / total, 0.0)


def _ffn_moe_kernel(x_ref, npre_ref, npost_ref, rw_ref, rb_ref, wg_ref, wu_ref, wd_ref,
                    o_ref, h_ref, acc_ref, gates_ref):
    e = pl.program_id(1)

    @pl.when(e == 0)
    def _():
        h_ref[...] = _rms(x_ref[...], npre_ref[...]).astype(BF16)
        acc_ref[...] = jnp.zeros(acc_ref.shape, F32)
        gates_ref[...] = _router_gates(h_ref[...], rw_ref, rb_ref)

    h = h_ref[...]
    lane = lax.broadcasted_iota(jnp.int32, gates_ref.shape, 1)
    gate_col = jnp.sum(jnp.where(lane == e, gates_ref[...], 0.0), axis=-1, keepdims=True)
    y = None
    for f0, fw in FF_EXPERT_CHUNKS:
        gate = _dot(h, wg_ref[:, f0:f0 + fw].astype(BF16))
        up = _dot(h, wu_ref[:, f0:f0 + fw].astype(BF16))
        act = (gate * jax.nn.sigmoid(gate) * up).astype(BF16)
        part = _dot(act, wd_ref[f0:f0 + fw, :].astype(BF16))
        y = part if y is None else y + part
    acc_ref[...] += gate_col * y

    @pl.when(e == pl.num_programs(1) - 1)
    def _():
        o_ref[...] = x_ref[...] + _rms(acc_ref[...], npost_ref[...])


def _ffn_moe(x, npre, npost, rw, rb, wg, wu, wd, tm):
    t = x.shape[0]
    return pl.pallas_call(
        _ffn_moe_kernel,
        grid=(t // tm, N_EXPERTS),
        in_specs=[pl.BlockSpec((tm, D_MODEL), lambda i, e: (i, 0)),
                  pl.BlockSpec((1, D_MODEL), lambda i, e: (0, 0)),
                  pl.BlockSpec((1, D_MODEL), lambda i, e: (0, 0)),
                  pl.BlockSpec((D_MODEL, LANES), lambda i, e: (0, 0)),
                  pl.BlockSpec((1, LANES), lambda i, e: (0, 0)),
                  pl.BlockSpec((None, D_MODEL, D_FF_EXPERT), lambda i, e: (e, 0, 0)),
                  pl.BlockSpec((None, D_MODEL, D_FF_EXPERT), lambda i, e: (e, 0, 0)),
                  pl.BlockSpec((None, D_FF_EXPERT, D_MODEL), lambda i, e: (e, 0, 0))],
        out_specs=pl.BlockSpec((tm, D_MODEL), lambda i, e: (i, 0)),
        out_shape=jax.ShapeDtypeStruct((t, D_MODEL), F32),
        scratch_shapes=[pltpu.VMEM((tm, D_MODEL), BF16), pltpu.VMEM((tm, D_MODEL), F32),
                        pltpu.VMEM((tm, LANES), F32)],
        compiler_params=_params("parallel", "arbitrary", vmem_limit=VMEM_LIMIT_EXPERT_WEIGHTS),
        name="ffn_moe",
    )(x, npre, npost, rw, rb, wg, wu, wd)


ROUTE_TILE = 256
ROW_ALIGN = 16
ROUTE_SEG = ROUTE_TILE
ROUTE_PACK = 2 * ROUTE_TILE + N_EXPERTS * ROW_ALIGN
ROUTE_W = D_MODEL + 3 * LANES
ROUTE_BLOCK = 512
ROUTE_SEG_SHORT = 128
ROUTE_SHORT_MAX = ROUTE_SEG_SHORT


def _route_region(n_tokens):
    rows = n_tokens + (n_tokens // ROUTE_TILE) * (ROW_ALIGN - 1) + ROUTE_SEG + ROUTE_BLOCK
    return -(-rows // ROUTE_BLOCK) * ROUTE_BLOCK


def _lane_scalar(row, lane, e):
    return jnp.sum(jnp.where(lane == e, row, 0.0)).astype(jnp.int32)


def _route_kernel(y_ref, x_ref, wout_ref, nmix_ref, npre_ref, rw_ref, rb_ref,
                  x1_ref, slot_ref, stats_ref, srt_hbm,
                  stage_ref, runv_ref, run_ref, short_ref, sem, *, region):
    i = pl.program_id(0)
    last = pl.num_programs(0) - 1
    cur = i % 2

    @pl.when(i == 0)
    def _():
        runv_ref[...] = jnp.zeros(runv_ref.shape, F32)
        stage_ref[:, ROUTE_PACK:, :] = jnp.zeros((2, ROUTE_SEG, ROUTE_W), BF16)
        for e in range(N_EXPERTS):
            run_ref[e] = 0

    x1 = x_ref[...] + _rms(_dot(y_ref[...], wout_ref[...]), nmix_ref[...])
    x1_ref[...] = x1
    h = _rms(x1, npre_ref[...]).astype(BF16)
    gates = _router_gates(h, rw_ref, rb_ref)
    sel = gates > 0.0
    ones = jnp.where(sel, 1.0, 0.0)
    trow = lax.broadcasted_iota(jnp.int32, (ROUTE_TILE, ROUTE_TILE), 0)
    tcol = lax.broadcasted_iota(jnp.int32, (ROUTE_TILE, ROUTE_TILE), 1)
    before = jnp.where(tcol < trow, 1.0, 0.0).astype(BF16)
    rank = _dot(before, ones.astype(BF16))
    cnt = jnp.sum(ones, axis=0, keepdims=True)
    cnt_pad = jnp.floor((cnt + (ROW_ALIGN - 1)) * (1.0 / ROW_ALIGN)) * ROW_ALIGN
    lrow = lax.broadcasted_iota(jnp.int32, (LANES, LANES), 0)
    lcol = lax.broadcasted_iota(jnp.int32, (LANES, LANES), 1)
    lower = jnp.where(lrow < lcol, 1.0, 0.0).astype(BF16)
    off = _dot(jnp.broadcast_to(cnt_pad, (SUBLANES, LANES)).astype(BF16), lower)[0:1, :]
    slot_ref[...] = jnp.where(sel, rank, -1.0)
    stats_ref[...] = jnp.zeros(stats_ref.shape, F32)
    stats_ref[0:1, :] = runv_ref[...]
    stats_ref[1:2, :] = cnt
    runv_ref[...] = runv_ref[...] + cnt_pad

    pos = jnp.where(sel, rank + off, -1.0)
    pos_t = jnp.concatenate([jnp.transpose(pos[0:LANES, :]), jnp.transpose(pos[LANES:2 * LANES, :])], axis=1)
    pos_a = jnp.max(pos_t, axis=0, keepdims=True)
    pos_b = jnp.max(jnp.where(pos_t == pos_a, -1.0, pos_t), axis=0, keepdims=True)
    prow = lax.broadcasted_iota(jnp.int32, (ROUTE_PACK, ROUTE_TILE), 0).astype(F32)
    perm = jnp.where((prow == pos_a) | (prow == pos_b), 1.0, 0.0).astype(BF16)
    g_hi, g_mid, g_lo = _split3(gates)
    rows = _dot(perm, jnp.concatenate([h, g_hi, g_mid, g_lo], axis=1))
    stage_ref[cur, 0:ROUTE_PACK, :] = rows.astype(BF16)

    lane = lax.broadcasted_iota(jnp.int32, (1, LANES), 1)
    short = (jnp.max(cnt_pad) <= ROUTE_SHORT_MAX).astype(jnp.int32)

    def segment_copy(e, src_row, dst_row, slot, rows=ROUTE_SEG):
        return pltpu.make_async_copy(
            stage_ref.at[slot, pl.ds(pl.multiple_of(src_row, ROW_ALIGN), rows), :],
            srt_hbm.at[pl.ds(pl.multiple_of(dst_row, ROW_ALIGN), rows), :],
            sem.at[e])

    def for_each_segment(is_short, action):
        for rows, flag in ((ROUTE_SEG_SHORT, 1), (ROUTE_SEG, 0)):
            @pl.when(is_short == flag)
            def _():
                for e in range(N_EXPERTS):
                    action(e, rows)

    @pl.when(i > 0)
    def _():
        for_each_segment(short_ref[0], lambda e, rows: segment_copy(e, 0, 0, 1 - cur, rows).wait())

    src_rows = [_lane_scalar(off, lane, e) for e in range(N_EXPERTS)]
    dst_rows = [e * region + run_ref[e] for e in range(N_EXPERTS)]
    for_each_segment(short, lambda e, rows: segment_copy(e, src_rows[e], dst_rows[e], cur, rows).start())
    for e in range(N_EXPERTS):
        run_ref[e] = run_ref[e] + _lane_scalar(cnt_pad, lane, e)
    short_ref[0] = short

    @pl.when(i == last)
    def _():
        for_each_segment(short, lambda e, rows: segment_copy(e, 0, 0, cur, rows).wait())
        stage_ref[1 - cur, 0:ROUTE_SEG, :] = jnp.zeros((ROUTE_SEG, ROUTE_W), BF16)
        for part in range(ROUTE_BLOCK // ROUTE_SEG):
            for e in range(N_EXPERTS):
                segment_copy(e, 0, e * region + run_ref[e] + part * ROUTE_SEG, 1 - cur).start()
            for e in range(N_EXPERTS):
                segment_copy(e, 0, 0, 1 - cur).wait()


def _route(y, x, w_out, nmix, npre, rw, rb):
    t = x.shape[0]
    nt = t // ROUTE_TILE
    region = _route_region(t)
    return pl.pallas_call(
        functools.partial(_route_kernel, region=region),
        grid=(nt,),
        in_specs=[pl.BlockSpec((ROUTE_TILE, D_MODEL), lambda i: (i, 0)),
                  pl.BlockSpec((ROUTE_TILE, D_MODEL), lambda i: (i, 0)),
                  pl.BlockSpec((D_MODEL, D_MODEL), lambda i: (0, 0)),
                  pl.BlockSpec((1, D_MODEL), lambda i: (0, 0)),
                  pl.BlockSpec((1, D_MODEL), lambda i: (0, 0)),
                  pl.BlockSpec((D_MODEL, LANES), lambda i: (0, 0)),
                  pl.BlockSpec((1, LANES), lambda i: (0, 0))],
        out_specs=[pl.BlockSpec((ROUTE_TILE, D_MODEL), lambda i: (i, 0)),
                   pl.BlockSpec((ROUTE_TILE, LANES), lambda i: (i, 0)),
                   pl.BlockSpec((None, SUBLANES, LANES), lambda i: (i, 0, 0)),
                   pl.BlockSpec(memory_space=pl.ANY)],
        out_shape=[jax.ShapeDtypeStruct((t, D_MODEL), F32),
                   jax.ShapeDtypeStruct((t, LANES), F32),
                   jax.ShapeDtypeStruct((nt, SUBLANES, LANES), F32),
                   jax.ShapeDtypeStruct((N_EXPERTS * region, ROUTE_W), BF16)],
        scratch_shapes=[pltpu.VMEM((2, ROUTE_PACK + ROUTE_SEG, ROUTE_W), BF16),
                        pltpu.VMEM((1, LANES), F32),
                        pltpu.SMEM((N_EXPERTS,), jnp.int32),
                        pltpu.SMEM((1,), jnp.int32),
                        pltpu.SemaphoreType.DMA((N_EXPERTS,))],
        compiler_params=_params("arbitrary"),
        name="moe_route",
    )(y, x, w_out, nmix, npre, rw, rb)


def _experts_kernel(blk_row_ref, blk_e_ref, n_used_ref, srt_ref, wg_ref, wu_ref, wd_ref, yhi_ref, ylo_ref):
    k = pl.program_id(0)

    @pl.when(k < n_used_ref[0])
    def _():
        h = srt_ref[:, 0:D_MODEL]
        gate3 = (srt_ref[:, D_MODEL:D_MODEL + LANES].astype(F32)
                 + srt_ref[:, D_MODEL + LANES:D_MODEL + 2 * LANES].astype(F32)
                 + srt_ref[:, D_MODEL + 2 * LANES:D_MODEL + 3 * LANES].astype(F32))
        lane = lax.broadcasted_iota(jnp.int32, gate3.shape, 1)
        gate_col = jnp.sum(jnp.where(lane == blk_e_ref[k], gate3, 0.0), axis=-1, keepdims=True)
        y = None
        for f0, fw in FF_EXPERT_CHUNKS:
            gate = _dot(h, wg_ref[:, f0:f0 + fw].astype(BF16))
            up = _dot(h, wu_ref[:, f0:f0 + fw].astype(BF16))
            act = (gate * jax.nn.sigmoid(gate) * up).astype(BF16)
            part = _dot(act, wd_ref[f0:f0 + fw, :].astype(BF16))
            y = part if y is None else y + part
        y = gate_col * y
        hi = y.astype(BF16)
        yhi_ref[...] = hi
        ylo_ref[...] = (y - hi.astype(F32)).astype(BF16)


def _experts(srt, blk_row, blk_e, n_used, wg, wu, wd, n_blocks):
    rows = srt.shape[0]
    grid_spec = pltpu.PrefetchScalarGridSpec(
        num_scalar_prefetch=3,
        grid=(n_blocks,),
        in_specs=[pl.BlockSpec((ROUTE_BLOCK, ROUTE_W), lambda k, br, be, nu: (br[k], 0)),
                  pl.BlockSpec((None, D_MODEL, D_FF_EXPERT), lambda k, br, be, nu: (be[k], 0, 0)),
                  pl.BlockSpec((None, D_MODEL, D_FF_EXPERT), lambda k, br, be, nu: (be[k], 0, 0)),
                  pl.BlockSpec((None, D_FF_EXPERT, D_MODEL), lambda k, br, be, nu: (be[k], 0, 0))],
        out_specs=[pl.BlockSpec((ROUTE_BLOCK, D_MODEL), lambda k, br, be, nu: (br[k], 0)),
                   pl.BlockSpec((ROUTE_BLOCK, D_MODEL), lambda k, br, be, nu: (br[k], 0))])
    return pl.pallas_call(
        _experts_kernel,
        grid_spec=grid_spec,
        out_shape=[jax.ShapeDtypeStruct((rows, D_MODEL), BF16), jax.ShapeDtypeStruct((rows, D_MODEL), BF16)],
        compiler_params=_params("arbitrary", vmem_limit=VMEM_LIMIT_EXPERT_WEIGHTS),
        name="moe_experts",
    )(blk_row, blk_e, n_used, srt, wg, wu, wd)


def _combine_kernel(src_row_ref, valid_ref, short_ref, x_ref, slot_ref, shift_ref, p_ref, npost_ref, nple_ref,
                    wg_ref, wp_ref, yhi_hbm, ylo_hbm, o_ref, seg_hi_ref, seg_lo_ref, y_ref, sem):
    i = pl.program_id(0)
    nt = pl.num_programs(0)
    cur = i % 2

    def segment_copies(tile, slot, e, rows):
        src = pl.ds(pl.multiple_of(src_row_ref[tile * N_EXPERTS + e], ROW_ALIGN), rows)
        dst = pl.ds(e * rows, rows)
        return (pltpu.make_async_copy(yhi_hbm.at[src, :], seg_hi_ref.at[slot, dst, :], sem.at[slot, 0, e]),
                pltpu.make_async_copy(ylo_hbm.at[src, :], seg_lo_ref.at[slot, dst, :], sem.at[slot, 1, e]))

    def for_each_segment(tile, slot, action):
        for rows, is_short in ((ROUTE_SEG_SHORT, 1), (ROUTE_SEG, 0)):
            @pl.when(short_ref[tile] == is_short)
            def _():
                for e in range(N_EXPERTS):
                    @pl.when(valid_ref[tile * N_EXPERTS + e] > 0)
                    def _():
                        for c in segment_copies(tile, slot, e, rows):
                            action(c)

    @pl.when(i == 0)
    def _():
        seg_hi_ref[...] = jnp.zeros(seg_hi_ref.shape, BF16)
        seg_lo_ref[...] = jnp.zeros(seg_lo_ref.shape, BF16)
        for_each_segment(0, 0, lambda c: c.start())

    @pl.when(i + 1 < nt)
    def _():
        for_each_segment(i + 1, 1 - cur, lambda c: c.start())

    for_each_segment(i, cur, lambda c: c.wait())

    slot = slot_ref[...]
    where = jnp.where(slot >= 0.0, slot + shift_ref[...], -1.0)

    def gather(rows):
        seg_lane = lax.broadcasted_iota(jnp.int32, (ROUTE_TILE, rows), 1).astype(F32)
        perm = jnp.concatenate([jnp.where(where[:, e:e + 1] == seg_lane, 1.0, 0.0).astype(BF16)
                                for e in range(N_EXPERTS)], axis=1)
        k = N_EXPERTS * rows
        y_ref[...] = _dot(perm, seg_hi_ref[cur, 0:k, :]) + _dot(perm, seg_lo_ref[cur, 0:k, :])

    @pl.when(short_ref[i] == 1)
    def _():
        gather(ROUTE_SEG_SHORT)

    @pl.when(short_ref[i] == 0)
    def _():
        gather(ROUTE_SEG)

    x = x_ref[...] + _rms(y_ref[...], npost_ref[...])
    gate = jax.nn.sigmoid(_dot(_rms(x, nple_ref[...]).astype(BF16), wg_ref[...]))
    o_ref[...] = x + gate * _dot(p_ref[...].astype(BF16), wp_ref[...])


def _combine(src_row, valid, short, x, slot, shift, p, npost, nple, wg, wp, yhi, ylo):
    t = x.shape[0]
    grid_spec = pltpu.PrefetchScalarGridSpec(
        num_scalar_prefetch=3,
        grid=(t // ROUTE_TILE,),
        in_specs=[pl.BlockSpec((ROUTE_TILE, D_MODEL), lambda i, *_: (i, 0)),
                  pl.BlockSpec((ROUTE_TILE, LANES), lambda i, *_: (i, 0)),
                  pl.BlockSpec((None, 1, LANES), lambda i, *_: (i, 0, 0)),
                  pl.BlockSpec((ROUTE_TILE, PLE_DIM), lambda i, *_: (i, 0)),
                  pl.BlockSpec((1, D_MODEL), lambda i, *_: (0, 0)),
                  pl.BlockSpec((1, D_MODEL), lambda i, *_: (0, 0)),
                  pl.BlockSpec((D_MODEL, D_MODEL), lambda i, *_: (0, 0)),
                  pl.BlockSpec((PLE_DIM, D_MODEL), lambda i, *_: (0, 0)),
                  pl.BlockSpec(memory_space=pl.ANY),
                  pl.BlockSpec(memory_space=pl.ANY)],
        out_specs=pl.BlockSpec((ROUTE_TILE, D_MODEL), lambda i, *_: (i, 0)),
        scratch_shapes=[pltpu.VMEM((2, N_EXPERTS * ROUTE_SEG, D_MODEL), BF16),
                        pltpu.VMEM((2, N_EXPERTS * ROUTE_SEG, D_MODEL), BF16),
                        pltpu.VMEM((ROUTE_TILE, D_MODEL), F32),
                        pltpu.SemaphoreType.DMA((2, 2, N_EXPERTS))])
    return pl.pallas_call(
        _combine_kernel,
        grid_spec=grid_spec,
        out_shape=jax.ShapeDtypeStruct((t, D_MODEL), F32),
        compiler_params=_params("arbitrary"),
        name="moe_combine_ple",
    )(src_row, valid, short, x, slot, shift, p, npost, nple, wg, wp, yhi, ylo)


def _moe_layer_routed(y, x, p, w_out, nmix, npre, npost, nple, rw, rb, wg, wu, wd, ple_g, ple_p):
    t = x.shape[0]
    nt = t // ROUTE_TILE
    region = _route_region(t)
    x, slot, stats, srt = _route(y, x, w_out, nmix, npre, rw, rb)
    base = stats[:, 0, 0:N_EXPERTS].astype(jnp.int32)
    cnt = stats[:, 1, 0:N_EXPERTS].astype(jnp.int32)
    cnt_pad = (cnt + (ROW_ALIGN - 1)) // ROW_ALIGN * ROW_ALIGN
    total = base[-1] + cnt_pad[-1]
    nblk = (total + (ROUTE_BLOCK - 1)) // ROUTE_BLOCK
    cum = jnp.cumsum(nblk)
    n_used = cum[-1]
    max_rows = 2 * t + nt * N_EXPERTS * (ROW_ALIGN - 1)
    n_blocks = max_rows // ROUTE_BLOCK + N_EXPERTS
    kk = jnp.minimum(jnp.arange(n_blocks, dtype=jnp.int32), n_used - 1)
    blk_e = jnp.sum(kk[:, None] >= cum[None, :], axis=1).astype(jnp.int32)
    blk_row = blk_e * (region // ROUTE_BLOCK) + kk - (cum - nblk)[blk_e]
    yhi, ylo = _experts(srt, blk_row.astype(jnp.int32), blk_e, n_used.reshape(1).astype(jnp.int32),
                        wg, wu, wd, n_blocks)
    short = jnp.all(cnt_pad <= ROUTE_SHORT_MAX, axis=1)
    seg_rows = jnp.where(short, ROUTE_SEG_SHORT, ROUTE_SEG)[:, None]
    start = jnp.maximum(jnp.minimum(base, nblk[None, :] * ROUTE_BLOCK - seg_rows), 0)
    src_row = (jnp.arange(N_EXPERTS, dtype=jnp.int32)[None, :] * region + start).reshape(-1)
    valid = (cnt > 0).astype(jnp.int32).reshape(-1)
    shift = _pad_lanes((base - start).astype(F32)).reshape(nt, 1, LANES)
    return _combine(src_row.astype(jnp.int32), valid, short.astype(jnp.int32), x, slot, shift, p, npost, nple,
                    ple_g, ple_p, yhi, ylo)


def _ple_kernel(x_ref, p_ref, nw_ref, wg_ref, wp_ref, o_ref):
    x = x_ref[...]
    gate = jax.nn.sigmoid(_dot(_rms(x, nw_ref[...]).astype(BF16), wg_ref[...]))
    o_ref[...] = x + gate * _dot(p_ref[...].astype(BF16), wp_ref[...])


def _ple(x, p, nw, wg, wp, tm):
    t = x.shape[0]
    return pl.pallas_call(
        _ple_kernel,
        grid=(t // tm,),
        in_specs=[pl.BlockSpec((tm, D_MODEL), lambda i: (i, 0)),
                  pl.BlockSpec((tm, PLE_DIM), lambda i: (i, 0)),
                  pl.BlockSpec((1, D_MODEL), lambda i: (0, 0)),
                  pl.BlockSpec((D_MODEL, D_MODEL), lambda i: (0, 0)),
                  pl.BlockSpec((PLE_DIM, D_MODEL), lambda i: (0, 0))],
        out_specs=pl.BlockSpec((tm, D_MODEL), lambda i: (i, 0)),
        out_shape=jax.ShapeDtypeStruct((t, D_MODEL), F32),
        compiler_params=_params("parallel"),
        name="ple",
    )(x, p, nw, wg, wp)


def _pad_lanes(a, width=LANES):
    return jnp.pad(a, [(0, 0)] * (a.ndim - 1) + [(0, width - a.shape[-1])])


def _permute_w_in(w):
    main = w[:, 0:2304]
    ig = w[:, 2304:2308]
    fg = w[:, 2308:2312]
    gu_gv = w[:, 2312:2824]
    return jnp.concatenate([main, gu_gv, _pad_lanes(ig), _pad_lanes(fg)], axis=1).astype(BF16)


def _block_diag(blocks):
    g, d, _ = blocks.shape
    out = jnp.zeros((g * d, g * d), blocks.dtype)
    for i in range(g):
        out = out.at[i * d:(i + 1) * d, i * d:(i + 1) * d].set(blocks[i])
    return out


def _row(a):
    return a.reshape(1, -1).astype(F32)


def kernel(x_prompt, x_sample, state_pool, state_mlstm_C, state_mlstm_n, state_mlstm_m, p_prompt, p_sample,
           norm_mix_pre, norm_mix_post, norm_ffn_pre, norm_ffn_post, norm_ple, w_in, pool_w, pool_scale,
           mlstm_b_i, mlstm_b_f, mlstm_norm_w, gmlp_norm_w, gmlp_ws, gmlp_bs, w_out,
           ffn_w_gate, ffn_w_up, ffn_w_down, moe_router_w, moe_router_b, moe_w_gate, moe_w_up, moe_w_down,
           ple_w_gate, ple_w_proj):
    batch, seq, _ = x_prompt.shape
    nseq = x_sample.shape[0]
    xp = x_prompt.reshape(batch * seq, D_MODEL)
    xs = x_sample.reshape(nseq, D_MODEL)
    gmean = _block_diag(jnp.full((GMLP_GROUPS, GMLP_GROUP_DIM, GMLP_GROUP_DIM), 1.0 / GMLP_GROUP_DIM, BF16))

    pools_p, cs_p, ns_p, ms_p = [], [], [], []
    pools_s, ns_s, ms_s, gvs_s = [], [], [], []
    c_new_s = None
    for i in range(DEPTH):
        w_in_p = _permute_w_in(w_in[i])
        w_out_b = w_out[i].astype(BF16)
        poolw = _block_diag(pool_w[i]).astype(BF16)
        shared = [poolw, _row(pool_scale[i]), _pad_lanes(_row(mlstm_b_i[i])), _pad_lanes(_row(mlstm_b_f[i])),
                  _row(mlstm_norm_w[i]), _row(gmlp_norm_w[i])]
        gbs_full = jnp.repeat(gmlp_bs[i].T, GMLP_GROUP_DIM, axis=1)
        consts_p = shared + [gmlp_ws[i], gbs_full, gmean]
        gw0 = jnp.repeat(gmlp_ws[i][:, 0, 0], GMLP_GROUP_DIM).reshape(1, GMLP_WIDTH)
        consts_s = shared + [gw0, gbs_full[0:1, :], gmean]
        ple_g = ple_w_gate[i].astype(BF16)
        ple_p = ple_w_proj[i].astype(BF16)
        j = i // 2
        if i % 2 == 0:
            ffn_g, ffn_u, ffn_d = (ffn_w_gate[j].astype(BF16), ffn_w_up[j].astype(BF16),
                                   ffn_w_down[j].astype(BF16))
        else:
            rw = _pad_lanes(moe_router_w[j]).astype(BF16)
            rb = _pad_lanes(_row(moe_router_b[j]))
            moe_g, moe_u, moe_d = moe_w_gate[j], moe_w_up[j], moe_w_down[j]

        z = _norm_matmul(xp, _row(norm_mix_pre[i]), w_in_p, TM_PROMPT)
        y, cn_new, m_new = _mixer_prompt(z, consts_p, batch, seq)
        pools_p.append(z.reshape(batch, seq, Z_WIDTH)[:, seq - POOL_STATE:, 0:POOL_WIDTH])
        cs_p.append(cn_new[..., 0:MLSTM_HEAD_DIM])
        ns_p.append(cn_new[..., MLSTM_HEAD_DIM])
        ms_p.append(m_new[:, 0, 0:MLSTM_HEADS])
        y = y.reshape(batch * seq, D_MODEL)
        pp = p_prompt[i].reshape(batch * seq, PLE_DIM)
        norms = (_row(norm_mix_post[i]), _row(norm_ffn_pre[i]), _row(norm_ffn_post[i]), _row(norm_ple[i]))
        if i % 2 == 0:
            xp = _dense_layer(y, xp, pp, w_out_b, *norms, ffn_g, ffn_u, ffn_d, ple_g, ple_p, TM_PROMPT)
        else:
            xp = _moe_layer_routed(y, xp, pp, w_out_b, *norms, rw, rb, moe_g, moe_u, moe_d, ple_g, ple_p)

        z = _norm_matmul(xs, _row(norm_mix_pre[i]), w_in_p, nseq)
        sp_t = jnp.transpose(state_pool[i], (1, 0, 2))
        y, c_new_s, n_new, m_new, gv = _mixer_sample(z, sp_t, state_mlstm_C, i, c_new_s,
                                                     state_mlstm_n[i].reshape(nseq, MLSTM_WIDTH),
                                                     _pad_lanes(state_mlstm_m[i]), consts_s)
        pools_s.append(jnp.concatenate([state_pool[i][:, 1:], z[:, None, 0:POOL_WIDTH]], axis=1))
        ns_s.append(n_new.reshape(nseq, MLSTM_HEADS, MLSTM_HEAD_DIM))
        ms_s.append(m_new[:, 0:MLSTM_HEADS])
        gvs_s.append(gv[:, None, :])
        ps = p_sample[i].reshape(nseq, PLE_DIM)
        if i % 2 == 0:
            xs = _dense_layer(y, xs, ps, w_out_b, *norms, ffn_g, ffn_u, ffn_d, ple_g, ple_p, nseq)
        else:
            xs = _proj_norm_res(y, xs, w_out_b, norms[0], nseq)
            xs = _ffn_moe(xs, norms[1], norms[2], rw, rb, moe_g, moe_u, moe_d, nseq)
            xs = _ple(xs, ps, norms[3], ple_g, ple_p, nseq)

    return (xp.reshape(batch, seq, D_MODEL), xs.reshape(nseq, 1, D_MODEL),
            jnp.stack(pools_p), jnp.stack(cs_p), jnp.stack(ns_p), jnp.stack(ms_p),
            jnp.stack(pools_s), c_new_s, jnp.stack(ns_s), jnp.stack(ms_s), jnp.stack(gvs_s))
```

```python
import functools

import jax
import jax.numpy as jnp
from jax import lax
from jax.experimental import pallas as pl
from jax.experimental.pallas import tpu as pltpu

F32 = jnp.float32
BF16 = jnp.bfloat16

D_MODEL = 1024
DEPTH = 2
POOL_WIDTH = 256
POOL_WINDOWS = (2, 4, 8, 16)
POOL_GROUP_DIM = 64
POOL_STATE = 15
MLSTM_WIDTH = 512
MLSTM_HEADS = 4
MLSTM_HEAD_DIM = 128
CHUNK = 128
GMLP_WIDTH = 256
GMLP_GROUPS = 4
GMLP_GROUP_DIM = 64
D_FF = 2816
N_EXPERTS = 8
D_FF_EXPERT = 1408
PLE_DIM = 256
RMS_EPS = 1e-6
PAST_LEN = 16384

LANES = 128
SUBLANES = 8
VMEM_LIMIT = 48 * 1024 * 1024
VMEM_LIMIT_EXPERT_WEIGHTS = 58 * 1024 * 1024
VMEM_LIMIT_IN_PROJ = 56 * 1024 * 1024

Z_POOL = 0
Z_Q = 256
Z_K = 768
Z_V = 1280
Z_O = 1792
Z_GU = 2304
Z_GV = 2560
Z_IG = 2816
Z_FG = 2944
Z_WIDTH = 3072
Z_CHUNK = 512
W_IN_IG = 2304
W_IN_GU = 2312
W_IN_WIDTH = 2824

TM_PROMPT = 512
TM_IN_PROJ = 1024
FF_CHUNK = 512
FF_EXPERT_CHUNKS = ((0, 512), (512, 512), (1024, 384))
SAMPLE_BLOCK = 8
PROMPT_SEQ_PER_STEP = 2


def _params(*semantics, vmem_limit=VMEM_LIMIT):
    return pltpu.CompilerParams(dimension_semantics=semantics, vmem_limit_bytes=vmem_limit)


def _rms(x, w):
    return x * lax.rsqrt(jnp.mean(x * x, axis=-1, keepdims=True) + RMS_EPS) * w


def _log_sigmoid(x):
    return jnp.minimum(x, 0.0) - jnp.log1p(jnp.exp(-jnp.abs(x)))


def _dot(a, b):
    return jnp.dot(a, b, preferred_element_type=F32)


def _split3(x):
    hi = x.astype(BF16)
    rest = x - hi.astype(F32)
    mid = rest.astype(BF16)
    lo = (rest - mid.astype(F32)).astype(BF16)
    return hi, mid, lo


def _resident(shape):
    return pl.BlockSpec(shape, lambda i: (0,) * len(shape), pipeline_mode=pl.Buffered(1))


def _norm_matmul_kernel(x_ref, nw_ref, w_ref, o_ref, h_ref, wz_ref):
    @pl.when(pl.program_id(0) == 0)
    def _():
        wz_ref[:, 0:Z_GU] = w_ref[:, 0:Z_GU].astype(BF16)
        wz_ref[:, Z_GU:Z_IG] = w_ref[:, W_IN_GU:W_IN_GU + 2 * GMLP_WIDTH].astype(BF16)
        gates = w_ref[:, W_IN_IG:W_IN_IG + LANES]
        lane = lax.broadcasted_iota(jnp.int32, gates.shape, 1)
        wz_ref[:, Z_IG:Z_FG] = jnp.where(lane < MLSTM_HEADS, gates, 0.0).astype(BF16)
        fg_first = pltpu.roll(gates, LANES - MLSTM_HEADS, 1)
        wz_ref[:, Z_FG:Z_WIDTH] = jnp.where(lane < MLSTM_HEADS, fg_first, 0.0).astype(BF16)

    h_ref[...] = _rms(x_ref[...], nw_ref[...]).astype(BF16)
    for n0 in range(0, Z_WIDTH, Z_CHUNK):
        o_ref[:, n0:n0 + Z_CHUNK] = _dot(h_ref[...], wz_ref[:, n0:n0 + Z_CHUNK])


def _norm_matmul(x, nw, w_all, layer, tm):
    t = x.shape[0]
    return pl.pallas_call(
        _norm_matmul_kernel,
        grid=(t // tm,),
        in_specs=[pl.BlockSpec((tm, D_MODEL), lambda i: (i, 0)),
                  _resident((1, D_MODEL)),
                  pl.BlockSpec((None, D_MODEL, W_IN_WIDTH), lambda i: (layer, 0, 0),
                               pipeline_mode=pl.Buffered(1))],
        out_specs=pl.BlockSpec((tm, Z_WIDTH), lambda i: (i, 0)),
        out_shape=jax.ShapeDtypeStruct((t, Z_WIDTH), F32),
        scratch_shapes=[pltpu.VMEM((tm, D_MODEL), BF16), pltpu.VMEM((D_MODEL, Z_WIDTH), BF16)],
        compiler_params=_params("arbitrary", vmem_limit=VMEM_LIMIT_IN_PROJ),
        name="norm_in_proj",
    )(x, nw, w_all)


def _group_rms(v, gmean, w):
    hi, mid, lo = _split3(v * v)
    ms = _dot(hi, gmean) + _dot(mid, gmean) + _dot(lo, gmean)
    return v * lax.rsqrt(ms + RMS_EPS) * w


def _pool_tile(ext_ref, u_tile, col0, w_lo, w_hi, pos):
    acc = u_tile
    sums = {}
    for shift in range(1, w_hi):
        acc = acc + ext_ref[pl.ds(16 - shift, CHUNK), col0:col0 + LANES]
        if shift + 1 in (w_lo, w_hi):
            sums[shift + 1] = acc
    cnt_lo = jnp.minimum(w_lo, pos + 1).astype(F32)
    cnt_hi = jnp.minimum(w_hi, pos + 1).astype(F32)
    lane = lax.broadcasted_iota(jnp.int32, (CHUNK, LANES), 1)
    return jnp.where(lane < POOL_GROUP_DIM, sums[w_lo] / cnt_lo, sums[w_hi] / cnt_hi) - u_tile


def _mixer_prompt_kernel(z_ref, *refs):
    consts = refs[:9]
    y_ref, cn_ref, m_ref, ext_ref = refs[9:]

    @pl.when(pl.program_id(1) == 0)
    def _():
        ext_ref[:, 0:16, :] = jnp.zeros((PROMPT_SEQ_PER_STEP, 16, POOL_WIDTH), F32)
        cn_ref[...] = jnp.zeros(cn_ref.shape, F32)
        m_ref[...] = jnp.zeros(m_ref.shape, F32)

    for i in range(PROMPT_SEQ_PER_STEP):
        _mixer_prompt_body(z_ref.at[i], *consts, y_ref.at[i], cn_ref.at[i], m_ref.at[i], ext_ref.at[i])


def _mixer_prompt_body(z_ref, poolw_ref, pscale_ref, bi_ref, bf_ref, mnorm_ref, gnorm_ref,
                       gws_ref, gbs_ref, gmean_ref,
                       y_ref, cn_ref, m_ref, ext_ref):
    chunk = pl.program_id(1)
    row = lax.broadcasted_iota(jnp.int32, (CHUNK, CHUNK), 0)
    col = lax.broadcasted_iota(jnp.int32, (CHUNK, CHUNK), 1)
    causal = col <= row
    lane = col

    ext_ref[16:16 + CHUNK, :] = z_ref[:, Z_POOL:Z_POOL + POOL_WIDTH]
    pos = chunk * CHUNK + lax.broadcasted_iota(jnp.int32, (CHUNK, 1), 0)
    pooled = []
    for tile in range(2):
        col0 = tile * LANES
        u_tile = z_ref[:, Z_POOL + col0:Z_POOL + col0 + LANES]
        pooled.append(_pool_tile(ext_ref, u_tile, col0, POOL_WINDOWS[2 * tile],
                                 POOL_WINDOWS[2 * tile + 1], pos))
    pooled = jnp.concatenate(pooled, axis=1).astype(BF16)
    y_pool = _dot(pooled, poolw_ref[...]) * pscale_ref[...]
    y_ref[:, 0:POOL_WIDTH] = y_pool.astype(BF16)
    ext_ref[0:16, :] = ext_ref[CHUNK:CHUNK + 16, :]

    vn = _group_rms(z_ref[:, Z_GV:Z_GV + GMLP_WIDTH], gmean_ref[...], gnorm_ref[...]).astype(BF16)
    for tile in range(2):
        col0 = tile * LANES
        vt = vn[:, col0:col0 + LANES]
        w_a = jnp.where(causal, gws_ref[2 * tile], 0.0).astype(BF16)
        w_b = jnp.where(causal, gws_ref[2 * tile + 1], 0.0).astype(BF16)
        mixed = jnp.where(lane < GMLP_GROUP_DIM, _dot(w_a, vt), _dot(w_b, vt))
        gu = z_ref[:, Z_GU + col0:Z_GU + col0 + LANES]
        y_g = gu * (mixed + gbs_ref[:, col0:col0 + LANES])
        y_ref[:, 768 + col0:768 + col0 + LANES] = y_g.astype(BF16)

    ig = z_ref[:, Z_IG:Z_IG + LANES] + bi_ref[...]
    lf = _log_sigmoid(z_ref[:, Z_FG:Z_FG + LANES] + bf_ref[...])
    tri = jnp.where(causal, 1.0, 0.0).astype(BF16)
    lf_hi, lf_mid, lf_lo = _split3(lf)
    b = _dot(tri, lf_hi) + _dot(tri, lf_mid) + _dot(tri, lf_lo)
    m_prev = m_ref[...]
    g = b + m_prev
    r_t = jnp.transpose(ig - b)
    b_last = b[CHUNK - 1:CHUNK, :]
    ones_col = jnp.where(lane == 0, 1.0, 0.0).astype(BF16)
    m_new_row = m_prev
    for h in range(MLSTM_HEADS):
        c0 = h * MLSTM_HEAD_DIM
        q = z_ref[:, Z_Q + c0:Z_Q + c0 + MLSTM_HEAD_DIM].astype(BF16)
        k = z_ref[:, Z_K + c0:Z_K + c0 + MLSTM_HEAD_DIM] * (MLSTM_HEAD_DIM ** -0.5)
        v = z_ref[:, Z_V + c0:Z_V + c0 + MLSTM_HEAD_DIM].astype(BF16)
        o = z_ref[:, Z_O + c0:Z_O + c0 + MLSTM_HEAD_DIM]
        b_col = b[:, h:h + 1]
        dmat = jnp.where(causal, b_col + r_t[h:h + 1, :], -jnp.inf)
        g_col = g[:, h:h + 1]
        m_t = jnp.maximum(g_col, jnp.max(dmat, axis=1, keepdims=True))
        scores = lax.dot_general(q, k.astype(BF16), (((1,), (1,)), ((), ())),
                                 preferred_element_type=F32)
        wts = jnp.exp(dmat - m_t) * scores
        inter = jnp.exp(g_col - m_t)
        cn_h = cn_ref[h]
        q_cn = _dot(q, cn_h.astype(BF16))
        num = inter * q_cn[:, 0:MLSTM_HEAD_DIM] + _dot(wts.astype(BF16), v)
        den = inter * q_cn[:, MLSTM_HEAD_DIM:MLSTM_HEAD_DIM + 1] + jnp.sum(wts, axis=1, keepdims=True)
        hid = num / jnp.maximum(jnp.abs(den), jnp.exp(-m_t))
        hid = _rms(hid, mnorm_ref[:, c0:c0 + MLSTM_HEAD_DIM])
        y_ref[:, POOL_WIDTH + c0:POOL_WIDTH + c0 + MLSTM_HEAD_DIM] = (jax.nn.sigmoid(o) * hid).astype(BF16)
        m_new = m_t[CHUNK - 1:CHUNK, :]
        bl = b_last[:, h:h + 1]
        decay = jnp.exp(bl + m_prev[:, h:h + 1] - m_new)
        w_s = jnp.exp(bl - b_col + ig[:, h:h + 1] - m_new)
        kw = (k * w_s).astype(BF16)
        v_ext = jnp.concatenate([v, ones_col], axis=1)
        cn_ref[h] = decay * cn_h + lax.dot_general(kw, v_ext, (((0,), (0,)), ((), ())),
                                                   preferred_element_type=F32)
        m_new_row = jnp.where(lane[0:1, :] == h, m_new, m_new_row)
    m_ref[...] = m_new_row


def _mixer_prompt(z, consts, batch, seq):
    nc = seq // CHUNK
    hd = MLSTM_HEAD_DIM
    z3 = z.reshape(batch, seq, Z_WIDTH)
    ns = PROMPT_SEQ_PER_STEP
    const_specs = [pl.BlockSpec(a.shape, lambda b, c, nd=a.ndim: (0,) * nd) for a in consts]
    return pl.pallas_call(
        _mixer_prompt_kernel,
        grid=(batch // ns, nc),
        in_specs=[pl.BlockSpec((ns, CHUNK, Z_WIDTH), lambda b, c: (b, c, 0))] + const_specs,
        out_specs=[pl.BlockSpec((ns, CHUNK, D_MODEL), lambda b, c: (b, c, 0)),
                   pl.BlockSpec((ns, MLSTM_HEADS, hd, 2 * hd), lambda b, c: (b, 0, 0, 0)),
                   pl.BlockSpec((ns, 1, LANES), lambda b, c: (b, 0, 0))],
        out_shape=[jax.ShapeDtypeStruct((batch, seq, D_MODEL), BF16),
                   jax.ShapeDtypeStruct((batch, MLSTM_HEADS, hd, 2 * hd), F32),
                   jax.ShapeDtypeStruct((batch, 1, LANES), F32)],
        scratch_shapes=[pltpu.VMEM((ns, 16 + CHUNK, POOL_WIDTH), F32)],
        compiler_params=_params("parallel", "arbitrary"),
        name="mixer_prompt",
    )(z3, *consts)


def _mixer_sample_kernel(z_ref, sp_ref, c_ref, n_ref, m_ref, c_other_layers_ref,
                         poolw_ref, pscale_ref, bi_ref, bf_ref, mnorm_ref, gnorm_ref,
                         gw0_ref, gb0_ref, gmean_ref,
                         y_ref, cn_ref, nn_ref, mn_ref, gv_ref, tk_ref):
    del c_other_layers_ref
    nb = SAMPLE_BLOCK
    hd = MLSTM_HEAD_DIM
    lane = lax.broadcasted_iota(jnp.int32, (nb, LANES), 1)
    seq_id = lax.broadcasted_iota(jnp.int32, (nb, LANES), 0)

    pooled = []
    for tile in range(2):
        col0 = tile * LANES
        u_tile = z_ref[:, Z_POOL + col0:Z_POOL + col0 + LANES]
        w_lo, w_hi = POOL_WINDOWS[2 * tile], POOL_WINDOWS[2 * tile + 1]
        acc = u_tile
        sums = {}
        for shift in range(1, w_hi):
            acc = acc + sp_ref[POOL_STATE - shift, :, col0:col0 + LANES]
            if shift + 1 in (w_lo, w_hi):
                sums[shift + 1] = acc
        pooled.append(jnp.where(lane < POOL_GROUP_DIM, sums[w_lo] / float(w_lo), sums[w_hi] / float(w_hi)) - u_tile)
    pooled = jnp.concatenate(pooled, axis=1).astype(BF16)
    y_ref[:, 0:POOL_WIDTH] = (_dot(pooled, poolw_ref[...]) * pscale_ref[...]).astype(BF16)

    vn = _group_rms(z_ref[:, Z_GV:Z_GV + GMLP_WIDTH], gmean_ref[...], gnorm_ref[...])
    gv_ref[...] = vn
    y_g = z_ref[:, Z_GU:Z_GU + GMLP_WIDTH] * (gw0_ref[...] * vn + gb0_ref[...])
    y_ref[:, 768:768 + GMLP_WIDTH] = y_g.astype(BF16)

    ig = z_ref[:, Z_IG:Z_IG + LANES] + bi_ref[...]
    lf = _log_sigmoid(z_ref[:, Z_FG:Z_FG + LANES] + bf_ref[...])
    m_prev = m_ref[...]
    g = lf + m_prev
    m_t = jnp.maximum(g, ig)
    inter = jnp.exp(g - m_t)
    e_ig = jnp.exp(ig - m_t)
    floor = jnp.exp(-m_t)
    mn_ref[...] = m_t
    tk_ref[...] = jnp.zeros((LANES, LANES), F32)
    for h in range(MLSTM_HEADS):
        tk_ref[nb * h:nb * (h + 1), :] = z_ref[:, Z_K + h * hd:Z_K + (h + 1) * hd] * (hd ** -0.5)
    k_t = jnp.transpose(tk_ref[...])
    for h in range(MLSTM_HEADS):
        c0 = h * hd
        q_h = z_ref[:, Z_Q + c0:Z_Q + c0 + hd]
        k_h = tk_ref[nb * h:nb * (h + 1), :]
        v_h = z_ref[:, Z_V + c0:Z_V + c0 + hd]
        o_h = z_ref[:, Z_O + c0:Z_O + c0 + hd]
        n_h = n_ref[:, c0:c0 + hd]
        inter_b = jnp.broadcast_to(inter[:, h:h + 1], (nb, hd))
        e_b = jnp.broadcast_to(e_ig[:, h:h + 1], (nb, hd))
        floor_b = jnp.broadcast_to(floor[:, h:h + 1], (nb, hd))
        v_w = e_b * v_h
        q_b = q_h.astype(BF16)
        q_c = jnp.zeros((nb, hd), F32)
        for s in range(nb):
            c_sh = c_ref[s, h]
            q_c = jnp.where(seq_id == s, _dot(q_b, c_sh.astype(BF16)), q_c)
            col = nb * h + s
            cn_ref[s, h] = inter_b[s:s + 1, :] * c_sh + k_t[:, col:col + 1] * v_w[s:s + 1, :]
        wts = e_b * jnp.sum(q_h * k_h, axis=1, keepdims=True)
        num = inter_b * q_c + wts * v_h
        den = inter_b * jnp.sum(q_h * n_h, axis=1, keepdims=True) + wts
        hid = num / jnp.maximum(jnp.abs(den), floor_b)
        hid = _rms(hid, mnorm_ref[:, c0:c0 + hd])
        y_ref[:, POOL_WIDTH + c0:POOL_WIDTH + c0 + hd] = (jax.nn.sigmoid(o_h) * hid).astype(BF16)
        nn_ref[:, c0:c0 + hd] = inter_b * n_h + e_b * k_h


def _mixer_sample(z, sp_t, c_all, layer, c_new_all, n_state, m_pad, consts):
    nseq = z.shape[0]
    nb = SAMPLE_BLOCK
    hd = MLSTM_HEAD_DIM
    const_specs = [pl.BlockSpec(a.shape, lambda j, nd=a.ndim: (0,) * nd) for a in consts]
    c_spec = pl.BlockSpec((None, nb, MLSTM_HEADS, hd, hd), lambda j: (layer, j, 0, 0, 0))
    aliases = {} if c_new_all is None else {5: 1}
    return pl.pallas_call(
        _mixer_sample_kernel,
        grid=(nseq // nb,),
        in_specs=[pl.BlockSpec((nb, Z_WIDTH), lambda j: (j, 0)),
                  pl.BlockSpec((POOL_STATE, nb, POOL_WIDTH), lambda j: (0, j, 0)),
                  c_spec,
                  pl.BlockSpec((nb, MLSTM_WIDTH), lambda j: (j, 0)),
                  pl.BlockSpec((nb, LANES), lambda j: (j, 0)),
                  pl.BlockSpec(memory_space=pl.ANY)] + const_specs,
        out_specs=[pl.BlockSpec((nb, D_MODEL), lambda j: (j, 0)),
                   c_spec,
                   pl.BlockSpec((nb, MLSTM_WIDTH), lambda j: (j, 0)),
                   pl.BlockSpec((nb, LANES), lambda j: (j, 0)),
                   pl.BlockSpec((nb, GMLP_WIDTH), lambda j: (j, 0))],
        out_shape=[jax.ShapeDtypeStruct((nseq, D_MODEL), BF16),
                   jax.ShapeDtypeStruct(c_all.shape, F32),
                   jax.ShapeDtypeStruct((nseq, MLSTM_WIDTH), F32),
                   jax.ShapeDtypeStruct((nseq, LANES), F32),
                   jax.ShapeDtypeStruct((nseq, GMLP_WIDTH), F32)],
        scratch_shapes=[pltpu.VMEM((LANES, LANES), F32)],
        input_output_aliases=aliases,
        compiler_params=_params("parallel"),
        name="mixer_sample",
    )(z, sp_t, c_all, n_state, m_pad, c_all if c_new_all is None else c_new_all, *consts)


def _proj_norm_res_kernel(y_ref, x_ref, w_ref, nw_ref, o_ref):
    o_ref[...] = x_ref[...] + _rms(_dot(y_ref[...], w_ref[...]), nw_ref[...])


def _proj_norm_res(y, x, w, nw, tm):
    t = x.shape[0]
    return pl.pallas_call(
        _proj_norm_res_kernel,
        grid=(t // tm,),
        in_specs=[pl.BlockSpec((tm, D_MODEL), lambda i: (i, 0)),
                  pl.BlockSpec((tm, D_MODEL), lambda i: (i, 0)),
                  pl.BlockSpec((D_MODEL, D_MODEL), lambda i: (0, 0)),
                  pl.BlockSpec((1, D_MODEL), lambda i: (0, 0))],
        out_specs=pl.BlockSpec((tm, D_MODEL), lambda i: (i, 0)),
        out_shape=jax.ShapeDtypeStruct((t, D_MODEL), F32),
        compiler_params=_params("parallel"),
        name="out_proj",
    )(y, x, w, nw)


def _dense_layer_kernel(y_ref, x_ref, p_ref, wout_ref, nmix_ref, npre_ref, npost_ref, nple_ref,
                        wg_ref, wu_ref, wd_ref, pg_ref, pp_ref, o_ref):
    x1 = x_ref[...] + _rms(_dot(y_ref[...], wout_ref[...]), nmix_ref[...])
    h = _rms(x1, npre_ref[...]).astype(BF16)
    y = None
    for f0 in range(0, D_FF, FF_CHUNK):
        fw = min(FF_CHUNK, D_FF - f0)
        gate = _dot(h, wg_ref[:, f0:f0 + fw])
        up = _dot(h, wu_ref[:, f0:f0 + fw])
        act = (gate * jax.nn.sigmoid(gate) * up).astype(BF16)
        part = _dot(act, wd_ref[f0:f0 + fw, :])
        y = part if y is None else y + part
    x2 = x1 + _rms(y, npost_ref[...])
    gate = jax.nn.sigmoid(_dot(_rms(x2, nple_ref[...]).astype(BF16), pg_ref[...]))
    o_ref[...] = x2 + gate * _dot(p_ref[...].astype(BF16), pp_ref[...])


def _dense_layer(y, x, p, w_out, nmix, npre, npost, nple, wg, wu, wd, ple_g, ple_p, tm):
    t = x.shape[0]
    return pl.pallas_call(
        _dense_layer_kernel,
        grid=(t // tm,),
        in_specs=[pl.BlockSpec((tm, D_MODEL), lambda i: (i, 0)),
                  pl.BlockSpec((tm, D_MODEL), lambda i: (i, 0)),
                  pl.BlockSpec((tm, PLE_DIM), lambda i: (i, 0)),
                  _resident((D_MODEL, D_MODEL)),
                  _resident((1, D_MODEL)), _resident((1, D_MODEL)), _resident((1, D_MODEL)), _resident((1, D_MODEL)),
                  _resident((D_MODEL, D_FF)), _resident((D_MODEL, D_FF)), _resident((D_FF, D_MODEL)),
                  _resident((D_MODEL, D_MODEL)), _resident((PLE_DIM, D_MODEL))],
        out_specs=pl.BlockSpec((tm, D_MODEL), lambda i: (i, 0)),
        out_shape=jax.ShapeDtypeStruct((t, D_MODEL), F32),
        compiler_params=_params("parallel"),
        name="dense_layer",
    )(y, x, p, w_out, nmix, npre, npost, nple, wg, wu, wd, ple_g, ple_p)


def _router_gates(h, rw_ref, rb_ref):
    shape = (h.shape[0], LANES)
    lane = lax.broadcasted_iota(jnp.int32, shape, 1)
    lane_f = lane.astype(F32)
    logits = jnp.where(lane < N_EXPERTS, _dot(h, rw_ref[...]) + rb_ref[...], -jnp.inf)
    e = jnp.exp(logits - jnp.max(logits, axis=-1, keepdims=True))
    probs = e / jnp.sum(e, axis=-1, keepdims=True)
    p1 = jnp.max(probs, axis=-1, keepdims=True)
    i1 = jnp.min(jnp.where(probs == p1, lane_f, float(LANES)), axis=-1, keepdims=True)
    rest = jnp.where(lane_f == i1, -1.0, probs)
    p2 = jnp.max(rest, axis=-1, keepdims=True)
    i2 = jnp.min(jnp.where(rest == p2, lane_f, float(LANES)), axis=-1, keepdims=True)
    total = p1 + p2
    return jnp.where(lane_f == i1, p1 / total, 0.0) + jnp.where(lane_f == i2, p2 / total, 0.0)


def _ffn_moe_kernel(x_ref, npre_ref, npost_ref, rw_ref, rb_ref, wg_ref, wu_ref, wd_ref,
                    o_ref, h_ref, acc_ref, gates_ref):
    e = pl.program_id(1)

    @pl.when(e == 0)
    def _():
        h_ref[...] = _rms(x_ref[...], npre_ref[...]).astype(BF16)
        acc_ref[...] = jnp.zeros(acc_ref.shape, F32)
        gates_ref[...] = _router_gates(h_ref[...], rw_ref, rb_ref)

    h = h_ref[...]
    lane = lax.broadcasted_iota(jnp.int32, gates_ref.shape, 1)
    gate_col = jnp.sum(jnp.where(lane == e, gates_ref[...], 0.0), axis=-1, keepdims=True)
    y = None
    for f0, fw in FF_EXPERT_CHUNKS:
        gate = _dot(h, wg_ref[:, f0:f0 + fw].astype(BF16))
        up = _dot(h, wu_ref[:, f0:f0 + fw].astype(BF16))
        act = (gate * jax.nn.sigmoid(gate) * up).astype(BF16)
        part = _dot(act, wd_ref[f0:f0 + fw, :].astype(BF16))
        y = part if y is None else y + part
    acc_ref[...] += gate_col * y

    @pl.when(e == pl.num_programs(1) - 1)
    def _():
        o_ref[...] = x_ref[...] + _rms(acc_ref[...], npost_ref[...])


def _ffn_moe(x, npre, npost, rw, rb, wg, wu, wd, tm):
    t = x.shape[0]
    return pl.pallas_call(
        _ffn_moe_kernel,
        grid=(t // tm, N_EXPERTS),
        in_specs=[pl.BlockSpec((tm, D_MODEL), lambda i, e: (i, 0)),
                  pl.BlockSpec((1, D_MODEL), lambda i, e: (0, 0)),
                  pl.BlockSpec((1, D_MODEL), lambda i, e: (0, 0)),
                  pl.BlockSpec((D_MODEL, LANES), lambda i, e: (0, 0)),
                  pl.BlockSpec((1, LANES), lambda i, e: (0, 0)),
                  pl.BlockSpec((None, D_MODEL, D_FF_EXPERT), lambda i, e: (e, 0, 0)),
                  pl.BlockSpec((None, D_MODEL, D_FF_EXPERT), lambda i, e: (e, 0, 0)),
                  pl.BlockSpec((None, D_FF_EXPERT, D_MODEL), lambda i, e: (e, 0, 0))],
        out_specs=pl.BlockSpec((tm, D_MODEL), lambda i, e: (i, 0)),
        out_shape=jax.ShapeDtypeStruct((t, D_MODEL), F32),
        scratch_shapes=[pltpu.VMEM((tm, D_MODEL), BF16), pltpu.VMEM((tm, D_MODEL), F32),
                        pltpu.VMEM((tm, LANES), F32)],
        compiler_params=_params("parallel", "arbitrary", vmem_limit=VMEM_LIMIT_EXPERT_WEIGHTS),
        name="ffn_moe",
    )(x, npre, npost, rw, rb, wg, wu, wd)


ROUTE_TILE = 256
ROW_ALIGN = 16
ROUTE_SEG = ROUTE_TILE
ROUTE_PACK = 2 * ROUTE_TILE + N_EXPERTS * ROW_ALIGN
ROUTE_W = D_MODEL + 3 * LANES
ROUTE_BLOCK = 512
ROUTE_SEG_SHORT = 96
ROUTE_SHORT_MAX = ROUTE_SEG_SHORT


def _route_region(n_tokens):
    rows = n_tokens + (n_tokens // ROUTE_TILE) * (ROW_ALIGN - 1) + ROUTE_SEG + ROUTE_BLOCK
    return -(-rows // ROUTE_BLOCK) * ROUTE_BLOCK


def _lane_scalar(row, lane, e):
    return jnp.sum(jnp.where(lane == e, row, 0.0)).astype(jnp.int32)


def _route_kernel(y_ref, x_ref, wout_ref, nmix_ref, npre_ref, rw_ref, rb_ref,
                  x1_ref, slot_ref, stats_ref, srt_hbm,
                  stage_ref, runv_ref, run_ref, short_ref, sem, *, region):
    i = pl.program_id(0)
    last = pl.num_programs(0) - 1
    cur = i % 2

    @pl.when(i == 0)
    def _():
        runv_ref[...] = jnp.zeros(runv_ref.shape, F32)
        stage_ref[:, ROUTE_PACK:, :] = jnp.zeros((2, ROUTE_SEG, ROUTE_W), BF16)
        for e in range(N_EXPERTS):
            run_ref[e] = 0

    x1 = x_ref[...] + _rms(_dot(y_ref[...], wout_ref[...]), nmix_ref[...])
    x1_ref[...] = x1
    h = _rms(x1, npre_ref[...]).astype(BF16)
    gates = _router_gates(h, rw_ref, rb_ref)
    sel = gates > 0.0
    ones = jnp.where(sel, 1.0, 0.0)
    trow = lax.broadcasted_iota(jnp.int32, (ROUTE_TILE, ROUTE_TILE), 0)
    tcol = lax.broadcasted_iota(jnp.int32, (ROUTE_TILE, ROUTE_TILE), 1)
    before = jnp.where(tcol < trow, 1.0, 0.0).astype(BF16)
    rank = _dot(before, ones.astype(BF16))
    cnt = jnp.sum(ones, axis=0, keepdims=True)
    cnt_pad = jnp.floor((cnt + (ROW_ALIGN - 1)) * (1.0 / ROW_ALIGN)) * ROW_ALIGN
    lrow = lax.broadcasted_iota(jnp.int32, (LANES, LANES), 0)
    lcol = lax.broadcasted_iota(jnp.int32, (LANES, LANES), 1)
    lower = jnp.where(lrow < lcol, 1.0, 0.0).astype(BF16)
    off = _dot(jnp.broadcast_to(cnt_pad, (SUBLANES, LANES)).astype(BF16), lower)[0:1, :]
    slot_ref[...] = jnp.where(sel, rank, -1.0)
    stats_ref[...] = jnp.zeros(stats_ref.shape, F32)
    stats_ref[0:1, :] = runv_ref[...]
    stats_ref[1:2, :] = cnt
    runv_ref[...] = runv_ref[...] + cnt_pad

    pos = jnp.where(sel, rank + off, -1.0)
    pos_t = jnp.concatenate([jnp.transpose(pos[0:LANES, :]), jnp.transpose(pos[LANES:2 * LANES, :])], axis=1)
    pos_a = jnp.max(pos_t, axis=0, keepdims=True)
    pos_b = jnp.max(jnp.where(pos_t == pos_a, -1.0, pos_t), axis=0, keepdims=True)
    prow = lax.broadcasted_iota(jnp.int32, (ROUTE_PACK, ROUTE_TILE), 0).astype(F32)
    perm = jnp.where((prow == pos_a) | (prow == pos_b), 1.0, 0.0).astype(BF16)
    g_hi, g_mid, g_lo = _split3(gates)
    rows = _dot(perm, jnp.concatenate([h, g_hi, g_mid, g_lo], axis=1))
    stage_ref[cur, 0:ROUTE_PACK, :] = rows.astype(BF16)

    lane = lax.broadcasted_iota(jnp.int32, (1, LANES), 1)
    short = (jnp.max(cnt_pad) <= ROUTE_SHORT_MAX).astype(jnp.int32)

    def segment_copy(e, src_row, dst_row, slot, rows=ROUTE_SEG):
        return pltpu.make_async_copy(
            stage_ref.at[slot, pl.ds(pl.multiple_of(src_row, ROW_ALIGN), rows), :],
            srt_hbm.at[pl.ds(pl.multiple_of(dst_row, ROW_ALIGN), rows), :],
            sem.at[e])

    def for_each_segment(is_short, action):
        for rows, flag in ((ROUTE_SEG_SHORT, 1), (ROUTE_SEG, 0)):
            @pl.when(is_short == flag)
            def _():
                for e in range(N_EXPERTS):
                    action(e, rows)

    @pl.when(i > 0)
    def _():
        for_each_segment(short_ref[0], lambda e, rows: segment_copy(e, 0, 0, 1 - cur, rows).wait())

    src_rows = [_lane_scalar(off, lane, e) for e in range(N_EXPERTS)]
    dst_rows = [e * region + run_ref[e] for e in range(N_EXPERTS)]
    for_each_segment(short, lambda e, rows: segment_copy(e, src_rows[e], dst_rows[e], cur, rows).start())
    for e in range(N_EXPERTS):
        run_ref[e] = run_ref[e] + _lane_scalar(cnt_pad, lane, e)
    short_ref[0] = short

    @pl.when(i == last)
    def _():
        for_each_segment(short, lambda e, rows: segment_copy(e, 0, 0, cur, rows).wait())
        stage_ref[1 - cur, 0:ROUTE_SEG, :] = jnp.zeros((ROUTE_SEG, ROUTE_W), BF16)
        for part in range(ROUTE_BLOCK // ROUTE_SEG):
            for e in range(N_EXPERTS):
                segment_copy(e, 0, e * region + run_ref[e] + part * ROUTE_SEG, 1 - cur).start()
            for e in range(N_EXPERTS):
                segment_copy(e, 0, 0, 1 - cur).wait()


def _route(y, x, w_out, nmix, npre, rw, rb):
    t = x.shape[0]
    nt = t // ROUTE_TILE
    region = _route_region(t)
    return pl.pallas_call(
        functools.partial(_route_kernel, region=region),
        grid=(nt,),
        in_specs=[pl.BlockSpec((ROUTE_TILE, D_MODEL), lambda i: (i, 0)),
                  pl.BlockSpec((ROUTE_TILE, D_MODEL), lambda i: (i, 0)),
                  pl.BlockSpec((D_MODEL, D_MODEL), lambda i: (0, 0)),
                  pl.BlockSpec((1, D_MODEL), lambda i: (0, 0)),
                  pl.BlockSpec((1, D_MODEL), lambda i: (0, 0)),
                  pl.BlockSpec((D_MODEL, LANES), lambda i: (0, 0)),
                  pl.BlockSpec((1, LANES), lambda i: (0, 0))],
        out_specs=[pl.BlockSpec((ROUTE_TILE, D_MODEL), lambda i: (i, 0)),
                   pl.BlockSpec((ROUTE_TILE, LANES), lambda i: (i, 0)),
                   pl.BlockSpec((None, SUBLANES, LANES), lambda i: (i, 0, 0)),
                   pl.BlockSpec(memory_space=pl.ANY)],
        out_shape=[jax.ShapeDtypeStruct((t, D_MODEL), F32),
                   jax.ShapeDtypeStruct((t, LANES), F32),
                   jax.ShapeDtypeStruct((nt, SUBLANES, LANES), F32),
                   jax.ShapeDtypeStruct((N_EXPERTS * region, ROUTE_W), BF16)],
        scratch_shapes=[pltpu.VMEM((2, ROUTE_PACK + ROUTE_SEG, ROUTE_W), BF16),
                        pltpu.VMEM((1, LANES), F32),
                        pltpu.SMEM((N_EXPERTS,), jnp.int32),
                        pltpu.SMEM((1,), jnp.int32),
                        pltpu.SemaphoreType.DMA((N_EXPERTS,))],
        compiler_params=_params("arbitrary"),
        name="moe_route",
    )(y, x, w_out, nmix, npre, rw, rb)


def _experts_kernel(blk_row_ref, blk_e_ref, n_used_ref, srt_ref, wg_ref, wu_ref, wd_ref, yhi_ref, ylo_ref):
    k = pl.program_id(0)

    @pl.when(k < n_used_ref[0])
    def _():
        h = srt_ref[:, 0:D_MODEL]
        gate3 = (srt_ref[:, D_MODEL:D_MODEL + LANES].astype(F32)
                 + srt_ref[:, D_MODEL + LANES:D_MODEL + 2 * LANES].astype(F32)
                 + srt_ref[:, D_MODEL + 2 * LANES:D_MODEL + 3 * LANES].astype(F32))
        lane = lax.broadcasted_iota(jnp.int32, gate3.shape, 1)
        gate_col = jnp.sum(jnp.where(lane == blk_e_ref[k], gate3, 0.0), axis=-1, keepdims=True)
        y = None
        for f0, fw in FF_EXPERT_CHUNKS:
            gate = _dot(h, wg_ref[:, f0:f0 + fw].astype(BF16))
            up = _dot(h, wu_ref[:, f0:f0 + fw].astype(BF16))
            act = (gate * jax.nn.sigmoid(gate) * up).astype(BF16)
            part = _dot(act, wd_ref[f0:f0 + fw, :].astype(BF16))
            y = part if y is None else y + part
        y = gate_col * y
        hi = y.astype(BF16)
        yhi_ref[...] = hi
        ylo_ref[...] = (y - hi.astype(F32)).astype(BF16)


def _experts(srt, blk_row, blk_e, n_used, wg, wu, wd, n_blocks):
    rows = srt.shape[0]
    grid_spec = pltpu.PrefetchScalarGridSpec(
        num_scalar_prefetch=3,
        grid=(n_blocks,),
        in_specs=[pl.BlockSpec((ROUTE_BLOCK, ROUTE_W), lambda k, br, be, nu: (br[k], 0)),
                  pl.BlockSpec((None, D_MODEL, D_FF_EXPERT), lambda k, br, be, nu: (be[k], 0, 0)),
                  pl.BlockSpec((None, D_MODEL, D_FF_EXPERT), lambda k, br, be, nu: (be[k], 0, 0)),
                  pl.BlockSpec((None, D_FF_EXPERT, D_MODEL), lambda k, br, be, nu: (be[k], 0, 0))],
        out_specs=[pl.BlockSpec((ROUTE_BLOCK, D_MODEL), lambda k, br, be, nu: (br[k], 0)),
                   pl.BlockSpec((ROUTE_BLOCK, D_MODEL), lambda k, br, be, nu: (br[k], 0))])
    return pl.pallas_call(
        _experts_kernel,
        grid_spec=grid_spec,
        out_shape=[jax.ShapeDtypeStruct((rows, D_MODEL), BF16), jax.ShapeDtypeStruct((rows, D_MODEL), BF16)],
        compiler_params=_params("arbitrary", vmem_limit=VMEM_LIMIT_EXPERT_WEIGHTS),
        name="moe_experts",
    )(blk_row, blk_e, n_used, srt, wg, wu, wd)


def _combine_kernel(src_row_ref, valid_ref, short_ref, x_ref, slot_ref, shift_ref, p_ref, npost_ref, nple_ref,
                    wg_ref, wp_ref, yhi_hbm, ylo_hbm, o_ref, seg_hi_ref, seg_lo_ref, y_ref, sem):
    i = pl.program_id(0)
    nt = pl.num_programs(0)
    cur = i % 2

    def segment_copies(tile, slot, e, rows):
        src = pl.ds(pl.multiple_of(src_row_ref[tile * N_EXPERTS + e], ROW_ALIGN), rows)
        dst = pl.ds(e * rows, rows)
        return (pltpu.make_async_copy(yhi_hbm.at[src, :], seg_hi_ref.at[slot, dst, :], sem.at[slot, 0, e]),
                pltpu.make_async_copy(ylo_hbm.at[src, :], seg_lo_ref.at[slot, dst, :], sem.at[slot, 1, e]))

    def for_each_segment(tile, slot, action):
        for rows, is_short in ((ROUTE_SEG_SHORT, 1), (ROUTE_SEG, 0)):
            @pl.when(short_ref[tile] == is_short)
            def _():
                for e in range(N_EXPERTS):
                    @pl.when(valid_ref[tile * N_EXPERTS + e] > 0)
                    def _():
                        for c in segment_copies(tile, slot, e, rows):
                            action(c)

    @pl.when(i == 0)
    def _():
        seg_hi_ref[...] = jnp.zeros(seg_hi_ref.shape, BF16)
        seg_lo_ref[...] = jnp.zeros(seg_lo_ref.shape, BF16)
        for_each_segment(0, 0, lambda c: c.start())

    @pl.when(i + 1 < nt)
    def _():
        for_each_segment(i + 1, 1 - cur, lambda c: c.start())

    for_each_segment(i, cur, lambda c: c.wait())

    slot = slot_ref[...]
    where = jnp.where(slot >= 0.0, slot + shift_ref[...], -1.0)

    def gather(rows):
        seg_lane = lax.broadcasted_iota(jnp.int32, (ROUTE_TILE, rows), 1).astype(F32)
        perm = jnp.concatenate([jnp.where(where[:, e:e + 1] == seg_lane, 1.0, 0.0).astype(BF16)
                                for e in range(N_EXPERTS)], axis=1)
        k = N_EXPERTS * rows
        y_ref[...] = _dot(perm, seg_hi_ref[cur, 0:k, :]) + _dot(perm, seg_lo_ref[cur, 0:k, :])

    @pl.when(short_ref[i] == 1)
    def _():
        gather(ROUTE_SEG_SHORT)

    @pl.when(short_ref[i] == 0)
    def _():
        gather(ROUTE_SEG)

    x = x_ref[...] + _rms(y_ref[...], npost_ref[...])
    gate = jax.nn.sigmoid(_dot(_rms(x, nple_ref[...]).astype(BF16), wg_ref[...]))
    o_ref[...] = x + gate * _dot(p_ref[...].astype(BF16), wp_ref[...])


def _combine(src_row, valid, short, x, slot, shift, p, npost, nple, wg, wp, yhi, ylo):
    t = x.shape[0]
    grid_spec = pltpu.PrefetchScalarGridSpec(
        num_scalar_prefetch=3,
        grid=(t // ROUTE_TILE,),
        in_specs=[pl.BlockSpec((ROUTE_TILE, D_MODEL), lambda i, *_: (i, 0)),
                  pl.BlockSpec((ROUTE_TILE, LANES), lambda i, *_: (i, 0)),
                  pl.BlockSpec((None, 1, LANES), lambda i, *_: (i, 0, 0)),
                  pl.BlockSpec((ROUTE_TILE, PLE_DIM), lambda i, *_: (i, 0)),
                  pl.BlockSpec((1, D_MODEL), lambda i, *_: (0, 0)),
                  pl.BlockSpec((1, D_MODEL), lambda i, *_: (0, 0)),
                  pl.BlockSpec((D_MODEL, D_MODEL), lambda i, *_: (0, 0)),
                  pl.BlockSpec((PLE_DIM, D_MODEL), lambda i, *_: (0, 0)),
                  pl.BlockSpec(memory_space=pl.ANY),
                  pl.BlockSpec(memory_space=pl.ANY)],
        out_specs=pl.BlockSpec((ROUTE_TILE, D_MODEL), lambda i, *_: (i, 0)),
        scratch_shapes=[pltpu.VMEM((2, N_EXPERTS * ROUTE_SEG, D_MODEL), BF16),
                        pltpu.VMEM((2, N_EXPERTS * ROUTE_SEG, D_MODEL), BF16),
                        pltpu.VMEM((ROUTE_TILE, D_MODEL), F32),
                        pltpu.SemaphoreType.DMA((2, 2, N_EXPERTS))])
    return pl.pallas_call(
        _combine_kernel,
        grid_spec=grid_spec,
        out_shape=jax.ShapeDtypeStruct((t, D_MODEL), F32),
        compiler_params=_params("arbitrary"),
        name="moe_combine_ple",
    )(src_row, valid, short, x, slot, shift, p, npost, nple, wg, wp, yhi, ylo)


def _moe_layer_routed(y, x, p, w_out, nmix, npre, npost, nple, rw, rb, wg, wu, wd, ple_g, ple_p):
    t = x.shape[0]
    nt = t // ROUTE_TILE
    region = _route_region(t)
    x, slot, stats, srt = _route(y, x, w_out, nmix, npre, rw, rb)
    base = stats[:, 0, 0:N_EXPERTS].astype(jnp.int32)
    cnt = stats[:, 1, 0:N_EXPERTS].astype(jnp.int32)
    cnt_pad = (cnt + (ROW_ALIGN - 1)) // ROW_ALIGN * ROW_ALIGN
    total = base[-1] + cnt_pad[-1]
    nblk = (total + (ROUTE_BLOCK - 1)) // ROUTE_BLOCK
    cum = jnp.cumsum(nblk)
    n_used = cum[-1]
    max_rows = 2 * t + nt * N_EXPERTS * (ROW_ALIGN - 1)
    n_blocks = max_rows // ROUTE_BLOCK + N_EXPERTS
    kk = jnp.minimum(jnp.arange(n_blocks, dtype=jnp.int32), n_used - 1)
    blk_e = jnp.sum(kk[:, None] >= cum[None, :], axis=1).astype(jnp.int32)
    blk_row = blk_e * (region // ROUTE_BLOCK) + kk - (cum - nblk)[blk_e]
    yhi, ylo = _experts(srt, blk_row.astype(jnp.int32), blk_e, n_used.reshape(1).astype(jnp.int32),
                        wg, wu, wd, n_blocks)
    short = jnp.all(cnt_pad <= ROUTE_SHORT_MAX, axis=1)
    seg_rows = jnp.where(short, ROUTE_SEG_SHORT, ROUTE_SEG)[:, None]
    start = jnp.maximum(jnp.minimum(base, nblk[None, :] * ROUTE_BLOCK - seg_rows), 0)
    src_row = (jnp.arange(N_EXPERTS, dtype=jnp.int32)[None, :] * region + start).reshape(-1)
    valid = (cnt > 0).astype(jnp.int32).reshape(-1)
    shift = _pad_lanes((base - start).astype(F32)).reshape(nt, 1, LANES)
    return _combine(src_row.astype(jnp.int32), valid, short.astype(jnp.int32), x, slot, shift, p, npost, nple,
                    ple_g, ple_p, yhi, ylo)


def _ple_kernel(x_ref, p_ref, nw_ref, wg_ref, wp_ref, o_ref):
    x = x_ref[...]
    gate = jax.nn.sigmoid(_dot(_rms(x, nw_ref[...]).astype(BF16), wg_ref[...]))
    o_ref[...] = x + gate * _dot(p_ref[...].astype(BF16), wp_ref[...])


def _ple(x, p, nw, wg, wp, tm):
    t = x.shape[0]
    return pl.pallas_call(
        _ple_kernel,
        grid=(t // tm,),
        in_specs=[pl.BlockSpec((tm, D_MODEL), lambda i: (i, 0)),
                  pl.BlockSpec((tm, PLE_DIM), lambda i: (i, 0)),
                  pl.BlockSpec((1, D_MODEL), lambda i: (0, 0)),
                  pl.BlockSpec((D_MODEL, D_MODEL), lambda i: (0, 0)),
                  pl.BlockSpec((PLE_DIM, D_MODEL), lambda i: (0, 0))],
        out_specs=pl.BlockSpec((tm, D_MODEL), lambda i: (i, 0)),
        out_shape=jax.ShapeDtypeStruct((t, D_MODEL), F32),
        compiler_params=_params("parallel"),
        name="ple",
    )(x, p, nw, wg, wp)


def _pad_lanes(a, width=LANES):
    return jnp.pad(a, [(0, 0)] * (a.ndim - 1) + [(0, width - a.shape[-1])])


def _block_diag(blocks):
    g, d, _ = blocks.shape
    out = jnp.zeros((g * d, g * d), blocks.dtype)
    for i in range(g):
        out = out.at[i * d:(i + 1) * d, i * d:(i + 1) * d].set(blocks[i])
    return out


def _row(a):
    return a.reshape(1, -1).astype(F32)


def kernel(x_prompt, x_sample, state_pool, state_mlstm_C, state_mlstm_n, state_mlstm_m, p_prompt, p_sample,
           norm_mix_pre, norm_mix_post, norm_ffn_pre, norm_ffn_post, norm_ple, w_in, pool_w, pool_scale,
           mlstm_b_i, mlstm_b_f, mlstm_norm_w, gmlp_norm_w, gmlp_ws, gmlp_bs, w_out,
           ffn_w_gate, ffn_w_up, ffn_w_down, moe_router_w, moe_router_b, moe_w_gate, moe_w_up, moe_w_down,
           ple_w_gate, ple_w_proj):
    batch, seq, _ = x_prompt.shape
    nseq = x_sample.shape[0]
    xp = x_prompt.reshape(batch * seq, D_MODEL)
    xs = x_sample.reshape(nseq, D_MODEL)
    gmean = _block_diag(jnp.full((GMLP_GROUPS, GMLP_GROUP_DIM, GMLP_GROUP_DIM), 1.0 / GMLP_GROUP_DIM, BF16))

    pools_p, cs_p, ns_p, ms_p = [], [], [], []
    pools_s, ns_s, ms_s, gvs_s = [], [], [], []
    c_new_s = None
    for i in range(DEPTH):
        w_out_b = w_out[i].astype(BF16)
        poolw = _block_diag(pool_w[i]).astype(BF16)
        shared = [poolw, _row(pool_scale[i]), _pad_lanes(_row(mlstm_b_i[i])), _pad_lanes(_row(mlstm_b_f[i])),
                  _row(mlstm_norm_w[i]), _row(gmlp_norm_w[i])]
        gbs_full = jnp.repeat(gmlp_bs[i].T, GMLP_GROUP_DIM, axis=1)
        consts_p = shared + [gmlp_ws[i], gbs_full, gmean]
        gw0 = jnp.repeat(gmlp_ws[i][:, 0, 0], GMLP_GROUP_DIM).reshape(1, GMLP_WIDTH)
        consts_s = shared + [gw0, gbs_full[0:1, :], gmean]
        ple_g = ple_w_gate[i].astype(BF16)
        ple_p = ple_w_proj[i].astype(BF16)
        j = i // 2
        if i % 2 == 0:
            ffn_g, ffn_u, ffn_d = (ffn_w_gate[j].astype(BF16), ffn_w_up[j].astype(BF16),
                                   ffn_w_down[j].astype(BF16))
        else:
            rw = _pad_lanes(moe_router_w[j]).astype(BF16)
            rb = _pad_lanes(_row(moe_router_b[j]))
            moe_g, moe_u, moe_d = moe_w_gate[j], moe_w_up[j], moe_w_down[j]

        z = _norm_matmul(xp, _row(norm_mix_pre[i]), w_in, i, TM_IN_PROJ)
        y, cn_new, m_new = _mixer_prompt(z, consts_p, batch, seq)
        pools_p.append(z.reshape(batch, seq, Z_WIDTH)[:, seq - POOL_STATE:, 0:POOL_WIDTH])
        cs_p.append(cn_new[..., 0:MLSTM_HEAD_DIM])
        ns_p.append(cn_new[..., MLSTM_HEAD_DIM])
        ms_p.append(m_new[:, 0, 0:MLSTM_HEADS])
        y = y.reshape(batch * seq, D_MODEL)
        pp = p_prompt[i].reshape(batch * seq, PLE_DIM)
        norms = (_row(norm_mix_post[i]), _row(norm_ffn_pre[i]), _row(norm_ffn_post[i]), _row(norm_ple[i]))
        if i % 2 == 0:
            xp = _dense_layer(y, xp, pp, w_out_b, *norms, ffn_g, ffn_u, ffn_d, ple_g, ple_p, TM_PROMPT)
        else:
            xp = _moe_layer_routed(y, xp, pp, w_out_b, *norms, rw, rb, moe_g, moe_u, moe_d, ple_g, ple_p)

        z = _norm_matmul(xs, _row(norm_mix_pre[i]), w_in, i, nseq)
        sp_t = jnp.transpose(state_pool[i], (1, 0, 2))
        y, c_new_s, n_new, m_new, gv = _mixer_sample(z, sp_t, state_mlstm_C, i, c_new_s,
                                                     state_mlstm_n[i].reshape(nseq, MLSTM_WIDTH),
                                                     _pad_lanes(state_mlstm_m[i]), consts_s)
        pools_s.append(jnp.concatenate([state_pool[i][:, 1:], z[:, None, 0:POOL_WIDTH]], axis=1))
        ns_s.append(n_new.reshape(nseq, MLSTM_HEADS, MLSTM_HEAD_DIM))
        ms_s.append(m_new[:, 0:MLSTM_HEADS])
        gvs_s.append(gv[:, None, :])
        ps = p_sample[i].reshape(nseq, PLE_DIM)
        if i % 2 == 0:
            xs = _dense_layer(y, xs, ps, w_out_b, *norms, ffn_g, ffn_u, ffn_d, ple_g, ple_p, nseq)
        else:
            xs = _proj_norm_res(y, xs, w_out_b, norms[0], nseq)
            xs = _ffn_moe(xs, norms[1], norms[2], rw, rb, moe_g, moe_u, moe_d, nseq)
            xs = _ple(xs, ps, norms[3], ple_g, ple_p, nseq)

    return (xp.reshape(batch, seq, D_MODEL), xs.reshape(nseq, 1, D_MODEL),
            jnp.stack(pools_p), jnp.stack(cs_p), jnp.stack(ns_p), jnp.stack(ms_p),
            jnp.stack(pools_s), c_new_s, jnp.stack(ns_s), jnp.stack(ms_s), jnp.stack(gvs_s))
```

```python
import functools

import jax
import jax.numpy as jnp
from jax import lax
from jax.experimental import pallas as pl
from jax.experimental.pallas import tpu as pltpu

F32 = jnp.float32
BF16 = jnp.bfloat16

D_MODEL = 1024
DEPTH = 2
POOL_WIDTH = 256
POOL_WINDOWS = (2, 4, 8, 16)
POOL_GROUP_DIM = 64
POOL_STATE = 15
MLSTM_WIDTH = 512
MLSTM_HEADS = 4
MLSTM_HEAD_DIM = 128
CHUNK = 128
GMLP_WIDTH = 256
GMLP_GROUPS = 4
GMLP_GROUP_DIM = 64
D_FF = 2816
N_EXPERTS = 8
D_FF_EXPERT = 1408
PLE_DIM = 256
RMS_EPS = 1e-6
PAST_LEN = 16384

LANES = 128
SUBLANES = 8
VMEM_LIMIT = 48 * 1024 * 1024
VMEM_LIMIT_EXPERT_WEIGHTS = 58 * 1024 * 1024
VMEM_LIMIT_IN_PROJ = 56 * 1024 * 1024

Z_POOL = 0
Z_Q = 256
Z_K = 768
Z_V = 1280
Z_O = 1792
Z_GU = 2304
Z_GV = 2560
Z_IG = 2816
Z_FG = 2944
Z_WIDTH = 3072
Z_CHUNK = 512
W_IN_IG = 2304
W_IN_GU = 2312
W_IN_WIDTH = 2824

TM_PROMPT = 512
TM_IN_PROJ = 1024
FF_CHUNK = 512
FF_EXPERT_CHUNKS = ((0, 512), (512, 512), (1024, 384))
SAMPLE_BLOCK = 8
PROMPT_SEQ_PER_STEP = 2


def _params(*semantics, vmem_limit=VMEM_LIMIT):
    return pltpu.CompilerParams(dimension_semantics=semantics, vmem_limit_bytes=vmem_limit)


def _rms(x, w):
    return x * lax.rsqrt(jnp.mean(x * x, axis=-1, keepdims=True) + RMS_EPS) * w


def _log_sigmoid(x):
    return jnp.minimum(x, 0.0) - jnp.log1p(jnp.exp(-jnp.abs(x)))


def _dot(a, b):
    return jnp.dot(a, b, preferred_element_type=F32)


def _split3(x):
    hi = x.astype(BF16)
    rest = x - hi.astype(F32)
    mid = rest.astype(BF16)
    lo = (rest - mid.astype(F32)).astype(BF16)
    return hi, mid, lo


def _resident(shape):
    return pl.BlockSpec(shape, lambda i: (0,) * len(shape), pipeline_mode=pl.Buffered(1))


def _norm_matmul_kernel(x_ref, nw_ref, w_ref, o_ref, h_ref, wz_ref):
    @pl.when(pl.program_id(0) == 0)
    def _():
        wz_ref[:, 0:Z_GU] = w_ref[:, 0:Z_GU].astype(BF16)
        wz_ref[:, Z_GU:Z_IG] = w_ref[:, W_IN_GU:W_IN_GU + 2 * GMLP_WIDTH].astype(BF16)
        gates = w_ref[:, W_IN_IG:W_IN_IG + LANES]
        lane = lax.broadcasted_iota(jnp.int32, gates.shape, 1)
        wz_ref[:, Z_IG:Z_FG] = jnp.where(lane < MLSTM_HEADS, gates, 0.0).astype(BF16)
        fg_first = pltpu.roll(gates, LANES - MLSTM_HEADS, 1)
        wz_ref[:, Z_FG:Z_WIDTH] = jnp.where(lane < MLSTM_HEADS, fg_first, 0.0).astype(BF16)

    h_ref[...] = _rms(x_ref[...], nw_ref[...]).astype(BF16)
    for n0 in range(0, Z_WIDTH, Z_CHUNK):
        o_ref[:, n0:n0 + Z_CHUNK] = _dot(h_ref[...], wz_ref[:, n0:n0 + Z_CHUNK])


def _norm_matmul(x, nw, w_all, layer, tm):
    t = x.shape[0]
    return pl.pallas_call(
        _norm_matmul_kernel,
        grid=(t // tm,),
        in_specs=[pl.BlockSpec((tm, D_MODEL), lambda i: (i, 0)),
                  _resident((1, D_MODEL)),
                  pl.BlockSpec((None, D_MODEL, W_IN_WIDTH), lambda i: (layer, 0, 0),
                               pipeline_mode=pl.Buffered(1))],
        out_specs=pl.BlockSpec((tm, Z_WIDTH), lambda i: (i, 0)),
        out_shape=jax.ShapeDtypeStruct((t, Z_WIDTH), F32),
        scratch_shapes=[pltpu.VMEM((tm, D_MODEL), BF16), pltpu.VMEM((D_MODEL, Z_WIDTH), BF16)],
        compiler_params=_params("arbitrary", vmem_limit=VMEM_LIMIT_IN_PROJ),
        name="norm_in_proj",
    )(x, nw, w_all)


def _group_rms(v, gmean, w):
    hi, mid, lo = _split3(v * v)
    ms = _dot(hi, gmean) + _dot(mid, gmean) + _dot(lo, gmean)
    return v * lax.rsqrt(ms + RMS_EPS) * w


def _pool_tile(ext_ref, u_tile, col0, w_lo, w_hi, pos):
    acc = u_tile
    sums = {}
    for shift in range(1, w_hi):
        acc = acc + ext_ref[pl.ds(16 - shift, CHUNK), col0:col0 + LANES]
        if shift + 1 in (w_lo, w_hi):
            sums[shift + 1] = acc
    cnt_lo = jnp.minimum(w_lo, pos + 1).astype(F32)
    cnt_hi = jnp.minimum(w_hi, pos + 1).astype(F32)
    lane = lax.broadcasted_iota(jnp.int32, (CHUNK, LANES), 1)
    return jnp.where(lane < POOL_GROUP_DIM, sums[w_lo] / cnt_lo, sums[w_hi] / cnt_hi) - u_tile


def _mixer_prompt_kernel(z_ref, *refs):
    consts = refs[:9]
    y_ref, cn_ref, m_ref, ext_ref = refs[9:]

    @pl.when(pl.program_id(1) == 0)
    def _():
        ext_ref[:, 0:16, :] = jnp.zeros((PROMPT_SEQ_PER_STEP, 16, POOL_WIDTH), F32)
        cn_ref[...] = jnp.zeros(cn_ref.shape, F32)
        m_ref[...] = jnp.zeros(m_ref.shape, F32)

    for i in range(PROMPT_SEQ_PER_STEP):
        _mixer_prompt_body(z_ref.at[i], *consts, y_ref.at[i], cn_ref.at[i], m_ref.at[i], ext_ref.at[i])


def _mixer_prompt_body(z_ref, poolw_ref, pscale_ref, bi_ref, bf_ref, mnorm_ref, gnorm_ref,
                       gws_ref, gbs_ref, gmean_ref,
                       y_ref, cn_ref, m_ref, ext_ref):
    chunk = pl.program_id(1)
    row = lax.broadcasted_iota(jnp.int32, (CHUNK, CHUNK), 0)
    col = lax.broadcasted_iota(jnp.int32, (CHUNK, CHUNK), 1)
    causal = col <= row
    lane = col

    ext_ref[16:16 + CHUNK, :] = z_ref[:, Z_POOL:Z_POOL + POOL_WIDTH]
    pos = chunk * CHUNK + lax.broadcasted_iota(jnp.int32, (CHUNK, 1), 0)
    pooled = []
    for tile in range(2):
        col0 = tile * LANES
        u_tile = z_ref[:, Z_POOL + col0:Z_POOL + col0 + LANES]
        pooled.append(_pool_tile(ext_ref, u_tile, col0, POOL_WINDOWS[2 * tile],
                                 POOL_WINDOWS[2 * tile + 1], pos))
    pooled = jnp.concatenate(pooled, axis=1).astype(BF16)
    y_pool = _dot(pooled, poolw_ref[...]) * pscale_ref[...]
    y_ref[:, 0:POOL_WIDTH] = y_pool.astype(BF16)
    ext_ref[0:16, :] = ext_ref[CHUNK:CHUNK + 16, :]

    vn = _group_rms(z_ref[:, Z_GV:Z_GV + GMLP_WIDTH], gmean_ref[...], gnorm_ref[...]).astype(BF16)
    for tile in range(2):
        col0 = tile * LANES
        vt = vn[:, col0:col0 + LANES]
        w_a = jnp.where(causal, gws_ref[2 * tile], 0.0).astype(BF16)
        w_b = jnp.where(causal, gws_ref[2 * tile + 1], 0.0).astype(BF16)
        mixed = jnp.where(lane < GMLP_GROUP_DIM, _dot(w_a, vt), _dot(w_b, vt))
        gu = z_ref[:, Z_GU + col0:Z_GU + col0 + LANES]
        y_g = gu * (mixed + gbs_ref[:, col0:col0 + LANES])
        y_ref[:, 768 + col0:768 + col0 + LANES] = y_g.astype(BF16)

    ig = z_ref[:, Z_IG:Z_IG + LANES] + bi_ref[...]
    lf = _log_sigmoid(z_ref[:, Z_FG:Z_FG + LANES] + bf_ref[...])
    tri = jnp.where(causal, 1.0, 0.0).astype(BF16)
    lf_hi, lf_mid, lf_lo = _split3(lf)
    b = _dot(tri, lf_hi) + _dot(tri, lf_mid) + _dot(tri, lf_lo)
    m_prev = m_ref[...]
    g = b + m_prev
    r_t = jnp.transpose(ig - b)
    b_last = b[CHUNK - 1:CHUNK, :]
    ones_col = jnp.where(lane == 0, 1.0, 0.0).astype(BF16)
    m_new_row = m_prev
    for h in range(MLSTM_HEADS):
        c0 = h * MLSTM_HEAD_DIM
        q = z_ref[:, Z_Q + c0:Z_Q + c0 + MLSTM_HEAD_DIM].astype(BF16)
        k = z_ref[:, Z_K + c0:Z_K + c0 + MLSTM_HEAD_DIM] * (MLSTM_HEAD_DIM ** -0.5)
        v = z_ref[:, Z_V + c0:Z_V + c0 + MLSTM_HEAD_DIM].astype(BF16)
        o = z_ref[:, Z_O + c0:Z_O + c0 + MLSTM_HEAD_DIM]
        b_col = b[:, h:h + 1]
        dmat = jnp.where(causal, b_col + r_t[h:h + 1, :], -jnp.inf)
        g_col = g[:, h:h + 1]
        m_t = jnp.maximum(g_col, jnp.max(dmat, axis=1, keepdims=True))
        scores = lax.dot_general(q, k.astype(BF16), (((1,), (1,)), ((), ())),
                                 preferred_element_type=F32)
        wts = jnp.exp(dmat - m_t) * scores
        inter = jnp.exp(g_col - m_t)
        cn_h = cn_ref[h]
        q_cn = _dot(q, cn_h.astype(BF16))
        num = inter * q_cn[:, 0:MLSTM_HEAD_DIM] + _dot(wts.astype(BF16), v)
        den = inter * q_cn[:, MLSTM_HEAD_DIM:MLSTM_HEAD_DIM + 1] + jnp.sum(wts, axis=1, keepdims=True)
        hid = num / jnp.maximum(jnp.abs(den), jnp.exp(-m_t))
        hid = _rms(hid, mnorm_ref[:, c0:c0 + MLSTM_HEAD_DIM])
        y_ref[:, POOL_WIDTH + c0:POOL_WIDTH + c0 + MLSTM_HEAD_DIM] = (jax.nn.sigmoid(o) * hid).astype(BF16)
        m_new = m_t[CHUNK - 1:CHUNK, :]
        bl = b_last[:, h:h + 1]
        decay = jnp.exp(bl + m_prev[:, h:h + 1] - m_new)
        w_s = jnp.exp(bl - b_col + ig[:, h:h + 1] - m_new)
        kw = (k * w_s).astype(BF16)
        v_ext = jnp.concatenate([v, ones_col], axis=1)
        cn_ref[h] = decay * cn_h + lax.dot_general(kw, v_ext, (((0,), (0,)), ((), ())),
                                                   preferred_element_type=F32)
        m_new_row = jnp.where(lane[0:1, :] == h, m_new, m_new_row)
    m_ref[...] = m_new_row


def _mixer_prompt(z, consts, batch, seq):
    nc = seq // CHUNK
    hd = MLSTM_HEAD_DIM
    z3 = z.reshape(batch, seq, Z_WIDTH)
    ns = PROMPT_SEQ_PER_STEP
    const_specs = [pl.BlockSpec(a.shape, lambda b, c, nd=a.ndim: (0,) * nd) for a in consts]
    return pl.pallas_call(
        _mixer_prompt_kernel,
        grid=(batch // ns, nc),
        in_specs=[pl.BlockSpec((ns, CHUNK, Z_WIDTH), lambda b, c: (b, c, 0))] + const_specs,
        out_specs=[pl.BlockSpec((ns, CHUNK, D_MODEL), lambda b, c: (b, c, 0)),
                   pl.BlockSpec((ns, MLSTM_HEADS, hd, 2 * hd), lambda b, c: (b, 0, 0, 0)),
                   pl.BlockSpec((ns, 1, LANES), lambda b, c: (b, 0, 0))],
        out_shape=[jax.ShapeDtypeStruct((batch, seq, D_MODEL), BF16),
                   jax.ShapeDtypeStruct((batch, MLSTM_HEADS, hd, 2 * hd), F32),
                   jax.ShapeDtypeStruct((batch, 1, LANES), F32)],
        scratch_shapes=[pltpu.VMEM((ns, 16 + CHUNK, POOL_WIDTH), F32)],
        compiler_params=_params("parallel", "arbitrary"),
        name="mixer_prompt",
    )(z3, *consts)


def _mixer_sample_kernel(z_ref, sp_ref, c_ref, n_ref, m_ref, c_other_layers_ref,
                         poolw_ref, pscale_ref, bi_ref, bf_ref, mnorm_ref, gnorm_ref,
                         gw0_ref, gb0_ref, gmean_ref,
                         y_ref, cn_ref, nn_ref, mn_ref, gv_ref, tk_ref):
    del c_other_layers_ref
    nb = SAMPLE_BLOCK
    hd = MLSTM_HEAD_DIM
    lane = lax.broadcasted_iota(jnp.int32, (nb, LANES), 1)
    seq_id = lax.broadcasted_iota(jnp.int32, (nb, LANES), 0)

    pooled = []
    for tile in range(2):
        col0 = tile * LANES
        u_tile = z_ref[:, Z_POOL + col0:Z_POOL + col0 + LANES]
        w_lo, w_hi = POOL_WINDOWS[2 * tile], POOL_WINDOWS[2 * tile + 1]
        acc = u_tile
        sums = {}
        for shift in range(1, w_hi):
            acc = acc + sp_ref[POOL_STATE - shift, :, col0:col0 + LANES]
            if shift + 1 in (w_lo, w_hi):
                sums[shift + 1] = acc
        pooled.append(jnp.where(lane < POOL_GROUP_DIM, sums[w_lo] / float(w_lo), sums[w_hi] / float(w_hi)) - u_tile)
    pooled = jnp.concatenate(pooled, axis=1).astype(BF16)
    y_ref[:, 0:POOL_WIDTH] = (_dot(pooled, poolw_ref[...]) * pscale_ref[...]).astype(BF16)

    vn = _group_rms(z_ref[:, Z_GV:Z_GV + GMLP_WIDTH], gmean_ref[...], gnorm_ref[...])
    gv_ref[...] = vn
    y_g = z_ref[:, Z_GU:Z_GU + GMLP_WIDTH] * (gw0_ref[...] * vn + gb0_ref[...])
    y_ref[:, 768:768 + GMLP_WIDTH] = y_g.astype(BF16)

    ig = z_ref[:, Z_IG:Z_IG + LANES] + bi_ref[...]
    lf = _log_sigmoid(z_ref[:, Z_FG:Z_FG + LANES] + bf_ref[...])
    m_prev = m_ref[...]
    g = lf + m_prev
    m_t = jnp.maximum(g, ig)
    inter = jnp.exp(g - m_t)
    e_ig = jnp.exp(ig - m_t)
    floor = jnp.exp(-m_t)
    mn_ref[...] = m_t
    tk_ref[...] = jnp.zeros((LANES, LANES), F32)
    for h in range(MLSTM_HEADS):
        tk_ref[nb * h:nb * (h + 1), :] = z_ref[:, Z_K + h * hd:Z_K + (h + 1) * hd] * (hd ** -0.5)
    k_t = jnp.transpose(tk_ref[...])
    for h in range(MLSTM_HEADS):
        c0 = h * hd
        q_h = z_ref[:, Z_Q + c0:Z_Q + c0 + hd]
        k_h = tk_ref[nb * h:nb * (h + 1), :]
        v_h = z_ref[:, Z_V + c0:Z_V + c0 + hd]
        o_h = z_ref[:, Z_O + c0:Z_O + c0 + hd]
        n_h = n_ref[:, c0:c0 + hd]
        inter_b = jnp.broadcast_to(inter[:, h:h + 1], (nb, hd))
        e_b = jnp.broadcast_to(e_ig[:, h:h + 1], (nb, hd))
        floor_b = jnp.broadcast_to(floor[:, h:h + 1], (nb, hd))
        v_w = e_b * v_h
        q_b = q_h.astype(BF16)
        q_c = jnp.zeros((nb, hd), F32)
        for s in range(nb):
            c_sh = c_ref[s, h]
            q_c = jnp.where(seq_id == s, _dot(q_b, c_sh.astype(BF16)), q_c)
            col = nb * h + s
            cn_ref[s, h] = inter_b[s:s + 1, :] * c_sh + k_t[:, col:col + 1] * v_w[s:s + 1, :]
        wts = e_b * jnp.sum(q_h * k_h, axis=1, keepdims=True)
        num = inter_b * q_c + wts * v_h
        den = inter_b * jnp.sum(q_h * n_h, axis=1, keepdims=True) + wts
        hid = num / jnp.maximum(jnp.abs(den), floor_b)
        hid = _rms(hid, mnorm_ref[:, c0:c0 + hd])
        y_ref[:, POOL_WIDTH + c0:POOL_WIDTH + c0 + hd] = (jax.nn.sigmoid(o_h) * hid).astype(BF16)
        nn_ref[:, c0:c0 + hd] = inter_b * n_h + e_b * k_h


def _mixer_sample(z, sp_t, c_all, layer, c_new_all, n_state, m_pad, consts):
    nseq = z.shape[0]
    nb = SAMPLE_BLOCK
    hd = MLSTM_HEAD_DIM
    const_specs = [pl.BlockSpec(a.shape, lambda j, nd=a.ndim: (0,) * nd) for a in consts]
    c_spec = pl.BlockSpec((None, nb, MLSTM_HEADS, hd, hd), lambda j: (layer, j, 0, 0, 0))
    aliases = {} if c_new_all is None else {5: 1}
    return pl.pallas_call(
        _mixer_sample_kernel,
        grid=(nseq // nb,),
        in_specs=[pl.BlockSpec((nb, Z_WIDTH), lambda j: (j, 0)),
                  pl.BlockSpec((POOL_STATE, nb, POOL_WIDTH), lambda j: (0, j, 0)),
                  c_spec,
                  pl.BlockSpec((nb, MLSTM_WIDTH), lambda j: (j, 0)),
                  pl.BlockSpec((nb, LANES), lambda j: (j, 0)),
                  pl.BlockSpec(memory_space=pl.ANY)] + const_specs,
        out_specs=[pl.BlockSpec((nb, D_MODEL), lambda j: (j, 0)),
                   c_spec,
                   pl.BlockSpec((nb, MLSTM_WIDTH), lambda j: (j, 0)),
                   pl.BlockSpec((nb, LANES), lambda j: (j, 0)),
                   pl.BlockSpec((nb, GMLP_WIDTH), lambda j: (j, 0))],
        out_shape=[jax.ShapeDtypeStruct((nseq, D_MODEL), BF16),
                   jax.ShapeDtypeStruct(c_all.shape, F32),
                   jax.ShapeDtypeStruct((nseq, MLSTM_WIDTH), F32),
                   jax.ShapeDtypeStruct((nseq, LANES), F32),
                   jax.ShapeDtypeStruct((nseq, GMLP_WIDTH), F32)],
        scratch_shapes=[pltpu.VMEM((LANES, LANES), F32)],
        input_output_aliases=aliases,
        compiler_params=_params("parallel"),
        name="mixer_sample",
    )(z, sp_t, c_all, n_state, m_pad, c_all if c_new_all is None else c_new_all, *consts)


def _proj_norm_res_kernel(y_ref, x_ref, w_ref, nw_ref, o_ref):
    o_ref[...] = x_ref[...] + _rms(_dot(y_ref[...], w_ref[...]), nw_ref[...])


def _proj_norm_res(y, x, w, nw, tm):
    t = x.shape[0]
    return pl.pallas_call(
        _proj_norm_res_kernel,
        grid=(t // tm,),
        in_specs=[pl.BlockSpec((tm, D_MODEL), lambda i: (i, 0)),
                  pl.BlockSpec((tm, D_MODEL), lambda i: (i, 0)),
                  pl.BlockSpec((D_MODEL, D_MODEL), lambda i: (0, 0)),
                  pl.BlockSpec((1, D_MODEL), lambda i: (0, 0))],
        out_specs=pl.BlockSpec((tm, D_MODEL), lambda i: (i, 0)),
        out_shape=jax.ShapeDtypeStruct((t, D_MODEL), F32),
        compiler_params=_params("parallel"),
        name="out_proj",
    )(y, x, w, nw)


def _dense_layer_kernel(y_ref, x_ref, p_ref, wout_ref, nmix_ref, npre_ref, npost_ref, nple_ref,
                        wg_ref, wu_ref, wd_ref, pg_ref, pp_ref, o_ref):
    x1 = x_ref[...] + _rms(_dot(y_ref[...], wout_ref[...]), nmix_ref[...])
    h = _rms(x1, npre_ref[...]).astype(BF16)
    y = None
    for f0 in range(0, D_FF, FF_CHUNK):
        fw = min(FF_CHUNK, D_FF - f0)
        gate = _dot(h, wg_ref[:, f0:f0 + fw])
        up = _dot(h, wu_ref[:, f0:f0 + fw])
        act = (gate * jax.nn.sigmoid(gate) * up).astype(BF16)
        part = _dot(act, wd_ref[f0:f0 + fw, :])
        y = part if y is None else y + part
    x2 = x1 + _rms(y, npost_ref[...])
    gate = jax.nn.sigmoid(_dot(_rms(x2, nple_ref[...]).astype(BF16), pg_ref[...]))
    o_ref[...] = x2 + gate * _dot(p_ref[...].astype(BF16), pp_ref[...])


def _dense_layer(y, x, p, w_out, nmix, npre, npost, nple, wg, wu, wd, ple_g, ple_p, tm):
    t = x.shape[0]
    return pl.pallas_call(
        _dense_layer_kernel,
        grid=(t // tm,),
        in_specs=[pl.BlockSpec((tm, D_MODEL), lambda i: (i, 0)),
                  pl.BlockSpec((tm, D_MODEL), lambda i: (i, 0)),
                  pl.BlockSpec((tm, PLE_DIM), lambda i: (i, 0)),
                  _resident((D_MODEL, D_MODEL)),
                  _resident((1, D_MODEL)), _resident((1, D_MODEL)), _resident((1, D_MODEL)), _resident((1, D_MODEL)),
                  _resident((D_MODEL, D_FF)), _resident((D_MODEL, D_FF)), _resident((D_FF, D_MODEL)),
                  _resident((D_MODEL, D_MODEL)), _resident((PLE_DIM, D_MODEL))],
        out_specs=pl.BlockSpec((tm, D_MODEL), lambda i: (i, 0)),
        out_shape=jax.ShapeDtypeStruct((t, D_MODEL), F32),
        compiler_params=_params("parallel"),
        name="dense_layer",
    )(y, x, p, w_out, nmix, npre, npost, nple, wg, wu, wd, ple_g, ple_p)


def _router_gates(h, rw_ref, rb_ref):
    shape = (h.shape[0], LANES)
    lane = lax.broadcasted_iota(jnp.int32, shape, 1)
    lane_f = lane.astype(F32)
    logits = jnp.where(lane < N_EXPERTS, _dot(h, rw_ref[...]) + rb_ref[...], -jnp.inf)
    l1 = jnp.max(logits, axis=-1, keepdims=True)
    i1 = jnp.min(jnp.where(logits == l1, lane_f, float(LANES)), axis=-1, keepdims=True)
    rest = jnp.where(lane_f == i1, -jnp.inf, logits)
    l2 = jnp.max(rest, axis=-1, keepdims=True)
    i2 = jnp.min(jnp.where(rest == l2, lane_f, float(LANES)), axis=-1, keepdims=True)
    e2 = jnp.exp(l2 - l1)
    total = 1.0 + e2
    return jnp.where(lane_f == i1, 1.0 / total, 0.0) + jnp.where(lane_f == i2, e2 / total, 0.0)


def _ffn_moe_kernel(x_ref, npre_ref, npost_ref, rw_ref, rb_ref, wg_ref, wu_ref, wd_ref,
                    o_ref, h_ref, acc_ref, gates_ref):
    e = pl.program_id(1)

    @pl.when(e == 0)
    def _():
        h_ref[...] = _rms(x_ref[...], npre_ref[...]).astype(BF16)
        acc_ref[...] = jnp.zeros(acc_ref.shape, F32)
        gates_ref[...] = _router_gates(h_ref[...], rw_ref, rb_ref)

    h = h_ref[...]
    lane = lax.broadcasted_iota(jnp.int32, gates_ref.shape, 1)
    gate_col = jnp.sum(jnp.where(lane == e, gates_ref[...], 0.0), axis=-1, keepdims=True)
    y = None
    for f0, fw in FF_EXPERT_CHUNKS:
        gate = _dot(h, wg_ref[:, f0:f0 + fw].astype(BF16))
        up = _dot(h, wu_ref[:, f0:f0 + fw].astype(BF16))
        act = (gate * jax.nn.sigmoid(gate) * up).astype(BF16)
        part = _dot(act, wd_ref[f0:f0 + fw, :].astype(BF16))
        y = part if y is None else y + part
    acc_ref[...] += gate_col * y

    @pl.when(e == pl.num_programs(1) - 1)
    def _():
        o_ref[...] = x_ref[...] + _rms(acc_ref[...], npost_ref[...])


def _ffn_moe(x, npre, npost, rw, rb, wg, wu, wd, tm):
    t = x.shape[0]
    return pl.pallas_call(
        _ffn_moe_kernel,
        grid=(t // tm, N_EXPERTS),
        in_specs=[pl.BlockSpec((tm, D_MODEL), lambda i, e: (i, 0)),
                  pl.BlockSpec((1, D_MODEL), lambda i, e: (0, 0)),
                  pl.BlockSpec((1, D_MODEL), lambda i, e: (0, 0)),
                  pl.BlockSpec((D_MODEL, LANES), lambda i, e: (0, 0)),
                  pl.BlockSpec((1, LANES), lambda i, e: (0, 0)),
                  pl.BlockSpec((None, D_MODEL, D_FF_EXPERT), lambda i, e: (e, 0, 0)),
                  pl.BlockSpec((None, D_MODEL, D_FF_EXPERT), lambda i, e: (e, 0, 0)),
                  pl.BlockSpec((None, D_FF_EXPERT, D_MODEL), lambda i, e: (e, 0, 0))],
        out_specs=pl.BlockSpec((tm, D_MODEL), lambda i, e: (i, 0)),
        out_shape=jax.ShapeDtypeStruct((t, D_MODEL), F32),
        scratch_shapes=[pltpu.VMEM((tm, D_MODEL), BF16), pltpu.VMEM((tm, D_MODEL), F32),
                        pltpu.VMEM((tm, LANES), F32)],
        compiler_params=_params("parallel", "arbitrary", vmem_limit=VMEM_LIMIT_EXPERT_WEIGHTS),
        name="ffn_moe",
    )(x, npre, npost, rw, rb, wg, wu, wd)


ROUTE_TILE = 256
ROW_ALIGN = 16
ROUTE_SEG = ROUTE_TILE
ROUTE_PACK = 2 * ROUTE_TILE + N_EXPERTS * ROW_ALIGN
ROUTE_W = D_MODEL + 3 * LANES
ROUTE_BLOCK = 512
ROUTE_SEG_SHORT = 96
ROUTE_SHORT_MAX = ROUTE_SEG_SHORT


def _route_region(n_tokens):
    rows = n_tokens + (n_tokens // ROUTE_TILE) * (ROW_ALIGN - 1) + ROUTE_SEG + ROUTE_BLOCK
    return -(-rows // ROUTE_BLOCK) * ROUTE_BLOCK


def _lane_scalar(row, lane, e):
    return jnp.sum(jnp.where(lane == e, row, 0.0)).astype(jnp.int32)


def _route_kernel(y_ref, x_ref, wout_ref, nmix_ref, npre_ref, rw_ref, rb_ref,
                  x1_ref, slot_ref, stats_ref, srt_hbm,
                  stage_ref, runv_ref, run_ref, short_ref, sem, *, region):
    i = pl.program_id(0)
    last = pl.num_programs(0) - 1
    cur = i % 2

    @pl.when(i == 0)
    def _():
        runv_ref[...] = jnp.zeros(runv_ref.shape, F32)
        stage_ref[:, ROUTE_PACK:, :] = jnp.zeros((2, ROUTE_SEG, ROUTE_W), BF16)
        for e in range(N_EXPERTS):
            run_ref[e] = 0

    x1 = x_ref[...] + _rms(_dot(y_ref[...], wout_ref[...]), nmix_ref[...])
    x1_ref[...] = x1
    h = _rms(x1, npre_ref[...]).astype(BF16)
    gates = _router_gates(h, rw_ref, rb_ref)
    sel = gates > 0.0
    ones = jnp.where(sel, 1.0, 0.0)
    trow = lax.broadcasted_iota(jnp.int32, (ROUTE_TILE, ROUTE_TILE), 0)
    tcol = lax.broadcasted_iota(jnp.int32, (ROUTE_TILE, ROUTE_TILE), 1)
    before = jnp.where(tcol < trow, 1.0, 0.0).astype(BF16)
    rank = _dot(before, ones.astype(BF16))
    cnt = jnp.sum(ones, axis=0, keepdims=True)
    cnt_pad = jnp.floor((cnt + (ROW_ALIGN - 1)) * (1.0 / ROW_ALIGN)) * ROW_ALIGN
    lrow = lax.broadcasted_iota(jnp.int32, (LANES, LANES), 0)
    lcol = lax.broadcasted_iota(jnp.int32, (LANES, LANES), 1)
    lower = jnp.where(lrow < lcol, 1.0, 0.0).astype(BF16)
    off = _dot(jnp.broadcast_to(cnt_pad, (SUBLANES, LANES)).astype(BF16), lower)[0:1, :]
    lane = lax.broadcasted_iota(jnp.int32, (1, LANES), 1)
    src_rows = [_lane_scalar(off, lane, e) for e in range(N_EXPERTS)]
    seg_lens = [_lane_scalar(cnt_pad, lane, e) for e in range(N_EXPERTS)]
    short = (jnp.max(cnt_pad) <= ROUTE_SHORT_MAX).astype(jnp.int32)
    slot_ref[...] = jnp.where(sel, rank, -1.0)
    stats_ref[...] = jnp.zeros(stats_ref.shape, F32)
    stats_ref[0:1, :] = runv_ref[...]
    stats_ref[1:2, :] = cnt
    runv_ref[...] = runv_ref[...] + cnt_pad

    pos = jnp.where(sel, rank + off, -1.0)
    pos_t = jnp.concatenate([jnp.transpose(pos[0:LANES, :]), jnp.transpose(pos[LANES:2 * LANES, :])], axis=1)
    pos_a = jnp.max(pos_t, axis=0, keepdims=True)
    pos_b = jnp.max(jnp.where(pos_t == pos_a, -1.0, pos_t), axis=0, keepdims=True)
    prow = lax.broadcasted_iota(jnp.int32, (ROUTE_PACK, ROUTE_TILE), 0).astype(F32)
    perm = jnp.where((prow == pos_a) | (prow == pos_b), 1.0, 0.0).astype(BF16)
    g_hi, g_mid, g_lo = _split3(gates)
    rows = _dot(perm, jnp.concatenate([h, g_hi, g_mid, g_lo], axis=1))
    stage_ref[cur, 0:ROUTE_PACK, :] = rows.astype(BF16)

    def segment_copy(e, src_row, dst_row, slot, rows=ROUTE_SEG):
        return pltpu.make_async_copy(
            stage_ref.at[slot, pl.ds(pl.multiple_of(src_row, ROW_ALIGN), rows), :],
            srt_hbm.at[pl.ds(pl.multiple_of(dst_row, ROW_ALIGN), rows), :],
            sem.at[e])

    def for_each_segment(is_short, action):
        for rows, flag in ((ROUTE_SEG_SHORT, 1), (ROUTE_SEG, 0)):
            @pl.when(is_short == flag)
            def _():
                for e in range(N_EXPERTS):
                    action(e, rows)

    @pl.when(i > 0)
    def _():
        for_each_segment(short_ref[0], lambda e, rows: segment_copy(e, 0, 0, 1 - cur, rows).wait())

    dst_rows = [e * region + run_ref[e] for e in range(N_EXPERTS)]
    for_each_segment(short, lambda e, rows: segment_copy(e, src_rows[e], dst_rows[e], cur, rows).start())
    for e in range(N_EXPERTS):
        run_ref[e] = run_ref[e] + seg_lens[e]
    short_ref[0] = short

    @pl.when(i == last)
    def _():
        for_each_segment(short, lambda e, rows: segment_copy(e, 0, 0, cur, rows).wait())
        stage_ref[1 - cur, 0:ROUTE_SEG, :] = jnp.zeros((ROUTE_SEG, ROUTE_W), BF16)
        for part in range(ROUTE_BLOCK // ROUTE_SEG):
            for e in range(N_EXPERTS):
                segment_copy(e, 0, e * region + run_ref[e] + part * ROUTE_SEG, 1 - cur).start()
            for e in range(N_EXPERTS):
                segment_copy(e, 0, 0, 1 - cur).wait()


def _route(y, x, w_out, nmix, npre, rw, rb):
    t = x.shape[0]
    nt = t // ROUTE_TILE
    region = _route_region(t)
    return pl.pallas_call(
        functools.partial(_route_kernel, region=region),
        grid=(nt,),
        in_specs=[pl.BlockSpec((ROUTE_TILE, D_MODEL), lambda i: (i, 0)),
                  pl.BlockSpec((ROUTE_TILE, D_MODEL), lambda i: (i, 0)),
                  pl.BlockSpec((D_MODEL, D_MODEL), lambda i: (0, 0)),
                  pl.BlockSpec((1, D_MODEL), lambda i: (0, 0)),
                  pl.BlockSpec((1, D_MODEL), lambda i: (0, 0)),
                  pl.BlockSpec((D_MODEL, LANES), lambda i: (0, 0)),
                  pl.BlockSpec((1, LANES), lambda i: (0, 0))],
        out_specs=[pl.BlockSpec((ROUTE_TILE, D_MODEL), lambda i: (i, 0)),
                   pl.BlockSpec((ROUTE_TILE, LANES), lambda i: (i, 0)),
                   pl.BlockSpec((None, SUBLANES, LANES), lambda i: (i, 0, 0)),
                   pl.BlockSpec(memory_space=pl.ANY)],
        out_shape=[jax.ShapeDtypeStruct((t, D_MODEL), F32),
                   jax.ShapeDtypeStruct((t, LANES), F32),
                   jax.ShapeDtypeStruct((nt, SUBLANES, LANES), F32),
                   jax.ShapeDtypeStruct((N_EXPERTS * region, ROUTE_W), BF16)],
        scratch_shapes=[pltpu.VMEM((2, ROUTE_PACK + ROUTE_SEG, ROUTE_W), BF16),
                        pltpu.VMEM((1, LANES), F32),
                        pltpu.SMEM((N_EXPERTS,), jnp.int32),
                        pltpu.SMEM((1,), jnp.int32),
                        pltpu.SemaphoreType.DMA((N_EXPERTS,))],
        compiler_params=_params("arbitrary"),
        name="moe_route",
    )(y, x, w_out, nmix, npre, rw, rb)


def _experts_kernel(blk_row_ref, blk_e_ref, n_used_ref, srt_ref, wg_ref, wu_ref, wd_ref, yhi_ref, ylo_ref):
    k = pl.program_id(0)

    @pl.when(k < n_used_ref[0])
    def _():
        h = srt_ref[:, 0:D_MODEL]
        gate3 = (srt_ref[:, D_MODEL:D_MODEL + LANES].astype(F32)
                 + srt_ref[:, D_MODEL + LANES:D_MODEL + 2 * LANES].astype(F32)
                 + srt_ref[:, D_MODEL + 2 * LANES:D_MODEL + 3 * LANES].astype(F32))
        lane = lax.broadcasted_iota(jnp.int32, gate3.shape, 1)
        gate_col = jnp.sum(jnp.where(lane == blk_e_ref[k], gate3, 0.0), axis=-1, keepdims=True)
        y = None
        for f0, fw in FF_EXPERT_CHUNKS:
            gate = _dot(h, wg_ref[:, f0:f0 + fw].astype(BF16))
            up = _dot(h, wu_ref[:, f0:f0 + fw].astype(BF16))
            act = (gate * jax.nn.sigmoid(gate) * up).astype(BF16)
            part = _dot(act, wd_ref[f0:f0 + fw, :].astype(BF16))
            y = part if y is None else y + part
        y = gate_col * y
        hi = y.astype(BF16)
        yhi_ref[...] = hi
        ylo_ref[...] = (y - hi.astype(F32)).astype(BF16)


def _experts(srt, blk_row, blk_e, n_used, wg, wu, wd, n_blocks):
    rows = srt.shape[0]
    grid_spec = pltpu.PrefetchScalarGridSpec(
        num_scalar_prefetch=3,
        grid=(n_blocks,),
        in_specs=[pl.BlockSpec((ROUTE_BLOCK, ROUTE_W), lambda k, br, be, nu: (br[k], 0)),
                  pl.BlockSpec((None, D_MODEL, D_FF_EXPERT), lambda k, br, be, nu: (be[k], 0, 0)),
                  pl.BlockSpec((None, D_MODEL, D_FF_EXPERT), lambda k, br, be, nu: (be[k], 0, 0)),
                  pl.BlockSpec((None, D_FF_EXPERT, D_MODEL), lambda k, br, be, nu: (be[k], 0, 0))],
        out_specs=[pl.BlockSpec((ROUTE_BLOCK, D_MODEL), lambda k, br, be, nu: (br[k], 0)),
                   pl.BlockSpec((ROUTE_BLOCK, D_MODEL), lambda k, br, be, nu: (br[k], 0))])
    return pl.pallas_call(
        _experts_kernel,
        grid_spec=grid_spec,
        out_shape=[jax.ShapeDtypeStruct((rows, D_MODEL), BF16), jax.ShapeDtypeStruct((rows, D_MODEL), BF16)],
        compiler_params=_params("arbitrary", vmem_limit=VMEM_LIMIT_EXPERT_WEIGHTS),
        name="moe_experts",
    )(blk_row, blk_e, n_used, srt, wg, wu, wd)


def _combine_kernel(src_row_ref, short_ref, x_ref, slot_ref, shift_ref, p_ref, npost_ref, nple_ref,
                    wg_ref, wp_ref, yhi_hbm, ylo_hbm, o_ref, seg_hi_ref, seg_lo_ref, y_ref, sem):
    i = pl.program_id(0)
    nt = pl.num_programs(0)
    cur = i % 2

    def segment_copies(tile, slot, e, rows):
        src = pl.ds(pl.multiple_of(src_row_ref[tile * N_EXPERTS + e], ROW_ALIGN), rows)
        dst = pl.ds(e * rows, rows)
        return (pltpu.make_async_copy(yhi_hbm.at[src, :], seg_hi_ref.at[slot, dst, :], sem.at[slot, 0, e]),
                pltpu.make_async_copy(ylo_hbm.at[src, :], seg_lo_ref.at[slot, dst, :], sem.at[slot, 1, e]))

    def for_each_segment(tile, slot, action):
        for rows, is_short in ((ROUTE_SEG_SHORT, 1), (ROUTE_SEG, 0)):
            @pl.when(short_ref[tile] == is_short)
            def _():
                for e in range(N_EXPERTS):
                    for c in segment_copies(tile, slot, e, rows):
                        action(c)

    @pl.when(i == 0)
    def _():
        for_each_segment(0, 0, lambda c: c.start())

    @pl.when(i + 1 < nt)
    def _():
        for_each_segment(i + 1, 1 - cur, lambda c: c.start())

    for_each_segment(i, cur, lambda c: c.wait())

    slot = slot_ref[...]
    where = jnp.where(slot >= 0.0, slot + shift_ref[...], -1.0)

    def gather(rows):
        seg_lane = lax.broadcasted_iota(jnp.int32, (ROUTE_TILE, rows), 1).astype(F32)
        perm = jnp.concatenate([jnp.where(where[:, e:e + 1] == seg_lane, 1.0, 0.0).astype(BF16)
                                for e in range(N_EXPERTS)], axis=1)
        k = N_EXPERTS * rows
        y_ref[...] = _dot(perm, seg_hi_ref[cur, 0:k, :]) + _dot(perm, seg_lo_ref[cur, 0:k, :])

    @pl.when(short_ref[i] == 1)
    def _():
        gather(ROUTE_SEG_SHORT)

    @pl.when(short_ref[i] == 0)
    def _():
        gather(ROUTE_SEG)

    x = x_ref[...] + _rms(y_ref[...], npost_ref[...])
    gate = jax.nn.sigmoid(_dot(_rms(x, nple_ref[...]).astype(BF16), wg_ref[...]))
    o_ref[...] = x + gate * _dot(p_ref[...].astype(BF16), wp_ref[...])


def _combine(src_row, short, x, slot, shift, p, npost, nple, wg, wp, yhi, ylo):
    t = x.shape[0]
    grid_spec = pltpu.PrefetchScalarGridSpec(
        num_scalar_prefetch=2,
        grid=(t // ROUTE_TILE,),
        in_specs=[pl.BlockSpec((ROUTE_TILE, D_MODEL), lambda i, *_: (i, 0)),
                  pl.BlockSpec((ROUTE_TILE, LANES), lambda i, *_: (i, 0)),
                  pl.BlockSpec((None, 1, LANES), lambda i, *_: (i, 0, 0)),
                  pl.BlockSpec((ROUTE_TILE, PLE_DIM), lambda i, *_: (i, 0)),
                  pl.BlockSpec((1, D_MODEL), lambda i, *_: (0, 0)),
                  pl.BlockSpec((1, D_MODEL), lambda i, *_: (0, 0)),
                  pl.BlockSpec((D_MODEL, D_MODEL), lambda i, *_: (0, 0)),
                  pl.BlockSpec((PLE_DIM, D_MODEL), lambda i, *_: (0, 0)),
                  pl.BlockSpec(memory_space=pl.ANY),
                  pl.BlockSpec(memory_space=pl.ANY)],
        out_specs=pl.BlockSpec((ROUTE_TILE, D_MODEL), lambda i, *_: (i, 0)),
        scratch_shapes=[pltpu.VMEM((2, N_EXPERTS * ROUTE_SEG, D_MODEL), BF16),
                        pltpu.VMEM((2, N_EXPERTS * ROUTE_SEG, D_MODEL), BF16),
                        pltpu.VMEM((ROUTE_TILE, D_MODEL), F32),
                        pltpu.SemaphoreType.DMA((2, 2, N_EXPERTS))])
    return pl.pallas_call(
        _combine_kernel,
        grid_spec=grid_spec,
        out_shape=jax.ShapeDtypeStruct((t, D_MODEL), F32),
        compiler_params=_params("arbitrary"),
        name="moe_combine_ple",
    )(src_row, short, x, slot, shift, p, npost, nple, wg, wp, yhi, ylo)


def _moe_layer_routed(y, x, p, w_out, nmix, npre, npost, nple, rw, rb, wg, wu, wd, ple_g, ple_p):
    t = x.shape[0]
    nt = t // ROUTE_TILE
    region = _route_region(t)
    x, slot, stats, srt = _route(y, x, w_out, nmix, npre, rw, rb)
    base = stats[:, 0, 0:N_EXPERTS].astype(jnp.int32)
    cnt = stats[:, 1, 0:N_EXPERTS].astype(jnp.int32)
    cnt_pad = (cnt + (ROW_ALIGN - 1)) // ROW_ALIGN * ROW_ALIGN
    total = base[-1] + cnt_pad[-1]
    nblk = (total + (ROUTE_BLOCK - 1)) // ROUTE_BLOCK
    cum = jnp.cumsum(nblk)
    n_used = cum[-1]
    max_rows = 2 * t + nt * N_EXPERTS * (ROW_ALIGN - 1)
    n_blocks = max_rows // ROUTE_BLOCK + N_EXPERTS
    kk = jnp.minimum(jnp.arange(n_blocks, dtype=jnp.int32), n_used - 1)
    blk_e = jnp.sum(kk[:, None] >= cum[None, :], axis=1).astype(jnp.int32)
    blk_row = blk_e * (region // ROUTE_BLOCK) + kk - (cum - nblk)[blk_e]
    yhi, ylo = _experts(srt, blk_row.astype(jnp.int32), blk_e, n_used.reshape(1).astype(jnp.int32),
                        wg, wu, wd, n_blocks)
    short = jnp.all(cnt_pad <= ROUTE_SHORT_MAX, axis=1)
    seg_rows = jnp.where(short, ROUTE_SEG_SHORT, ROUTE_SEG)[:, None]
    start = jnp.maximum(jnp.minimum(base, nblk[None, :] * ROUTE_BLOCK - seg_rows), 0)
    src_row = jnp.arange(N_EXPERTS, dtype=jnp.int32)[None, :] * region + start
    first_used = jnp.argmax(nblk > 0).astype(jnp.int32)
    src_row = jnp.where(cnt > 0, src_row, first_used * region).reshape(-1)
    shift = _pad_lanes((base - start).astype(F32)).reshape(nt, 1, LANES)
    return _combine(src_row.astype(jnp.int32), short.astype(jnp.int32), x, slot, shift, p, npost, nple,
                    ple_g, ple_p, yhi, ylo)


def _ple_kernel(x_ref, p_ref, nw_ref, wg_ref, wp_ref, o_ref):
    x = x_ref[...]
    gate = jax.nn.sigmoid(_dot(_rms(x, nw_ref[...]).astype(BF16), wg_ref[...]))
    o_ref[...] = x + gate * _dot(p_ref[...].astype(BF16), wp_ref[...])


def _ple(x, p, nw, wg, wp, tm):
    t = x.shape[0]
    return pl.pallas_call(
        _ple_kernel,
        grid=(t // tm,),
        in_specs=[pl.BlockSpec((tm, D_MODEL), lambda i: (i, 0)),
                  pl.BlockSpec((tm, PLE_DIM), lambda i: (i, 0)),
                  pl.BlockSpec((1, D_MODEL), lambda i: (0, 0)),
                  pl.BlockSpec((D_MODEL, D_MODEL), lambda i: (0, 0)),
                  pl.BlockSpec((PLE_DIM, D_MODEL), lambda i: (0, 0))],
        out_specs=pl.BlockSpec((tm, D_MODEL), lambda i: (i, 0)),
        out_shape=jax.ShapeDtypeStruct((t, D_MODEL), F32),
        compiler_params=_params("parallel"),
        name="ple",
    )(x, p, nw, wg, wp)


def _pad_lanes(a, width=LANES):
    return jnp.pad(a, [(0, 0)] * (a.ndim - 1) + [(0, width - a.shape[-1])])


def _block_diag(blocks):
    g, d, _ = blocks.shape
    out = jnp.zeros((g * d, g * d), blocks.dtype)
    for i in range(g):
        out = out.at[i * d:(i + 1) * d, i * d:(i + 1) * d].set(blocks[i])
    return out


def _row(a):
    return a.reshape(1, -1).astype(F32)


def kernel(x_prompt, x_sample, state_pool, state_mlstm_C, state_mlstm_n, state_mlstm_m, p_prompt, p_sample,
           norm_mix_pre, norm_mix_post, norm_ffn_pre, norm_ffn_post, norm_ple, w_in, pool_w, pool_scale,
           mlstm_b_i, mlstm_b_f, mlstm_norm_w, gmlp_norm_w, gmlp_ws, gmlp_bs, w_out,
           ffn_w_gate, ffn_w_up, ffn_w_down, moe_router_w, moe_router_b, moe_w_gate, moe_w_up, moe_w_down,
           ple_w_gate, ple_w_proj):
    batch, seq, _ = x_prompt.shape
    nseq = x_sample.shape[0]
    xp = x_prompt.reshape(batch * seq, D_MODEL)
    xs = x_sample.reshape(nseq, D_MODEL)
    gmean = _block_diag(jnp.full((GMLP_GROUPS, GMLP_GROUP_DIM, GMLP_GROUP_DIM), 1.0 / GMLP_GROUP_DIM, BF16))

    pools_p, cs_p, ns_p, ms_p = [], [], [], []
    pools_s, ns_s, ms_s, gvs_s = [], [], [], []
    c_new_s = None
    for i in range(DEPTH):
        w_out_b = w_out[i].astype(BF16)
        poolw = _block_diag(pool_w[i]).astype(BF16)
        shared = [poolw, _row(pool_scale[i]), _pad_lanes(_row(mlstm_b_i[i])), _pad_lanes(_row(mlstm_b_f[i])),
                  _row(mlstm_norm_w[i]), _row(gmlp_norm_w[i])]
        gbs_full = jnp.repeat(gmlp_bs[i].T, GMLP_GROUP_DIM, axis=1)
        consts_p = shared + [gmlp_ws[i], gbs_full, gmean]
        gw0 = jnp.repeat(gmlp_ws[i][:, 0, 0], GMLP_GROUP_DIM).reshape(1, GMLP_WIDTH)
        consts_s = shared + [gw0, gbs_full[0:1, :], gmean]
        ple_g = ple_w_gate[i].astype(BF16)
        ple_p = ple_w_proj[i].astype(BF16)
        j = i // 2
        if i % 2 == 0:
            ffn_g, ffn_u, ffn_d = (ffn_w_gate[j].astype(BF16), ffn_w_up[j].astype(BF16),
                                   ffn_w_down[j].astype(BF16))
        else:
            rw = _pad_lanes(moe_router_w[j]).astype(BF16)
            rb = _pad_lanes(_row(moe_router_b[j]))
            moe_g, moe_u, moe_d = moe_w_gate[j], moe_w_up[j], moe_w_down[j]

        z = _norm_matmul(xp, _row(norm_mix_pre[i]), w_in, i, TM_IN_PROJ)
        y, cn_new, m_new = _mixer_prompt(z, consts_p, batch, seq)
        pools_p.append(z.reshape(batch, seq, Z_WIDTH)[:, seq - POOL_STATE:, 0:POOL_WIDTH])
        cs_p.append(cn_new[..., 0:MLSTM_HEAD_DIM])
        ns_p.append(cn_new[..., MLSTM_HEAD_DIM])
        ms_p.append(m_new[:, 0, 0:MLSTM_HEADS])
        y = y.reshape(batch * seq, D_MODEL)
        pp = p_prompt[i].reshape(batch * seq, PLE_DIM)
        norms = (_row(norm_mix_post[i]), _row(norm_ffn_pre[i]), _row(norm_ffn_post[i]), _row(norm_ple[i]))
        if i % 2 == 0:
            xp = _dense_layer(y, xp, pp, w_out_b, *norms, ffn_g, ffn_u, ffn_d, ple_g, ple_p, TM_PROMPT)
        else:
            xp = _moe_layer_routed(y, xp, pp, w_out_b, *norms, rw, rb, moe_g, moe_u, moe_d, ple_g, ple_p)

        z = _norm_matmul(xs, _row(norm_mix_pre[i]), w_in, i, nseq)
        sp_t = jnp.transpose(state_pool[i], (1, 0, 2))
        y, c_new_s, n_new, m_new, gv = _mixer_sample(z, sp_t, state_mlstm_C, i, c_new_s,
                                                     state_mlstm_n[i].reshape(nseq, MLSTM_WIDTH),
                                                     _pad_lanes(state_mlstm_m[i]), consts_s)
        pools_s.append(jnp.concatenate([state_pool[i][:, 1:], z[:, None, 0:POOL_WIDTH]], axis=1))
        ns_s.append(n_new.reshape(nseq, MLSTM_HEADS, MLSTM_HEAD_DIM))
        ms_s.append(m_new[:, 0:MLSTM_HEADS])
        gvs_s.append(gv[:, None, :])
        ps = p_sample[i].reshape(nseq, PLE_DIM)
        if i % 2 == 0:
            xs = _dense_layer(y, xs, ps, w_out_b, *norms, ffn_g, ffn_u, ffn_d, ple_g, ple_p, nseq)
        else:
            xs = _proj_norm_res(y, xs, w_out_b, norms[0], nseq)
            xs = _ffn_moe(xs, norms[1], norms[2], rw, rb, moe_g, moe_u, moe_d, nseq)
            xs = _ple(xs, ps, norms[3], ple_g, ple_p, nseq)

    return (xp.reshape(batch, seq, D_MODEL), xs.reshape(nseq, 1, D_MODEL),
            jnp.stack(pools_p), jnp.stack(cs_p), jnp.stack(ns_p), jnp.stack(ms_p),
            jnp.stack(pools_s), c_new_s, jnp.stack(ns_s), jnp.stack(ms_s), jnp.stack(gvs_s))
```

```python
import functools

import jax
import jax.numpy as jnp
from jax import lax
from jax.experimental import pallas as pl
from jax.experimental.pallas import tpu as pltpu

F32 = jnp.float32
BF16 = jnp.bfloat16

D_MODEL = 1024
DEPTH = 2
POOL_WIDTH = 256
POOL_WINDOWS = (2, 4, 8, 16)
POOL_GROUP_DIM = 64
POOL_STATE = 15
MLSTM_WIDTH = 512
MLSTM_HEADS = 4
MLSTM_HEAD_DIM = 128
CHUNK = 128
GMLP_WIDTH = 256
GMLP_GROUPS = 4
GMLP_GROUP_DIM = 64
D_FF = 2816
N_EXPERTS = 8
D_FF_EXPERT = 1408
PLE_DIM = 256
RMS_EPS = 1e-6
PAST_LEN = 16384

LANES = 128
SUBLANES = 8
VMEM_LIMIT = 48 * 1024 * 1024
VMEM_LIMIT_EXPERT_WEIGHTS = 58 * 1024 * 1024

Z_POOL = 0
Z_Q = 256
Z_K = 768
Z_V = 1280
Z_O = 1792
Z_GU = 2304
Z_GV = 2560
Z_IG = 2816
Z_FG = 2944
Z_WIDTH = 3072
Z_CHUNK = 512
W_IN_IG = 2304
W_IN_GU = 2312
W_IN_WIDTH = 2824

TM_PROMPT = 512
FF_CHUNK = 512
FF_EXPERT_CHUNKS = ((0, 512), (512, 512), (1024, 384))
SAMPLE_BLOCK = 8
PROMPT_SEQ_PER_STEP = 2


def _params(*semantics, vmem_limit=VMEM_LIMIT):
    return pltpu.CompilerParams(dimension_semantics=semantics, vmem_limit_bytes=vmem_limit)


def _rms(x, w):
    return x * lax.rsqrt(jnp.mean(x * x, axis=-1, keepdims=True) + RMS_EPS) * w


def _log_sigmoid(x):
    return jnp.minimum(x, 0.0) - jnp.log1p(jnp.exp(-jnp.abs(x)))


def _dot(a, b):
    return jnp.dot(a, b, preferred_element_type=F32)


def _split3(x):
    hi = x.astype(BF16)
    rest = x - hi.astype(F32)
    mid = rest.astype(BF16)
    lo = (rest - mid.astype(F32)).astype(BF16)
    return hi, mid, lo


def _resident(shape):
    return pl.BlockSpec(shape, lambda i: (0,) * len(shape), pipeline_mode=pl.Buffered(1))


def _norm_matmul_kernel(x_ref, nw_ref, wt_ref, o_ref, h_ref, wz_ref):
    @pl.when(pl.program_id(0) == 0)
    def _():
        wz_ref[0:Z_GU, :] = wt_ref[0:Z_GU, :].astype(BF16)
        wz_ref[Z_GU:Z_IG, :] = wt_ref[W_IN_GU:W_IN_GU + 2 * GMLP_WIDTH, :].astype(BF16)
        gates = wt_ref[W_IN_IG:W_IN_IG + SUBLANES, :]
        row = lax.broadcasted_iota(jnp.int32, gates.shape, 0)
        zeros = jnp.zeros((LANES - SUBLANES, D_MODEL), F32)
        ig_rows = jnp.where(row < MLSTM_HEADS, gates, 0.0)
        fg_rows = jnp.where(row < MLSTM_HEADS, pltpu.roll(gates, MLSTM_HEADS, 0), 0.0)
        wz_ref[Z_IG:Z_FG, :] = jnp.concatenate([ig_rows, zeros], axis=0).astype(BF16)
        wz_ref[Z_FG:Z_WIDTH, :] = jnp.concatenate([fg_rows, zeros], axis=0).astype(BF16)

    h_ref[...] = _rms(x_ref[...], nw_ref[...]).astype(BF16)
    for n0 in range(0, Z_WIDTH, Z_CHUNK):
        o_ref[:, n0:n0 + Z_CHUNK] = lax.dot_general(h_ref[...], wz_ref[n0:n0 + Z_CHUNK, :],
                                                    (((1,), (1,)), ((), ())), preferred_element_type=F32)


def _norm_matmul(x, nw, wt_all, layer, tm):
    t = x.shape[0]
    return pl.pallas_call(
        _norm_matmul_kernel,
        grid=(t // tm,),
        in_specs=[pl.BlockSpec((tm, D_MODEL), lambda i: (i, 0)),
                  _resident((1, D_MODEL)),
                  pl.BlockSpec((None, W_IN_WIDTH, D_MODEL), lambda i: (layer, 0, 0),
                               pipeline_mode=pl.Buffered(1))],
        out_specs=pl.BlockSpec((tm, Z_WIDTH), lambda i: (i, 0)),
        out_shape=jax.ShapeDtypeStruct((t, Z_WIDTH), F32),
        scratch_shapes=[pltpu.VMEM((tm, D_MODEL), BF16), pltpu.VMEM((Z_WIDTH, D_MODEL), BF16)],
        compiler_params=_params("arbitrary"),
        name="norm_in_proj",
    )(x, nw, wt_all)


def _group_rms(v, gmean, w):
    hi, mid, lo = _split3(v * v)
    ms = _dot(hi, gmean) + _dot(mid, gmean) + _dot(lo, gmean)
    return v * lax.rsqrt(ms + RMS_EPS) * w


def _pool_tile(ext_ref, u_tile, col0, w_lo, w_hi, pos):
    acc = u_tile
    sums = {}
    for shift in range(1, w_hi):
        acc = acc + ext_ref[pl.ds(16 - shift, CHUNK), col0:col0 + LANES]
        if shift + 1 in (w_lo, w_hi):
            sums[shift + 1] = acc
    cnt_lo = jnp.minimum(w_lo, pos + 1).astype(F32)
    cnt_hi = jnp.minimum(w_hi, pos + 1).astype(F32)
    lane = lax.broadcasted_iota(jnp.int32, (CHUNK, LANES), 1)
    return jnp.where(lane < POOL_GROUP_DIM, sums[w_lo] / cnt_lo, sums[w_hi] / cnt_hi) - u_tile


def _mixer_prompt_kernel(z_ref, *refs):
    consts = refs[:9]
    y_ref, cn_ref, m_ref, ext_ref = refs[9:]

    @pl.when(pl.program_id(1) == 0)
    def _():
        ext_ref[:, 0:16, :] = jnp.zeros((PROMPT_SEQ_PER_STEP, 16, POOL_WIDTH), F32)
        cn_ref[...] = jnp.zeros(cn_ref.shape, F32)
        m_ref[...] = jnp.zeros(m_ref.shape, F32)

    for i in range(PROMPT_SEQ_PER_STEP):
        _mixer_prompt_body(z_ref.at[i], *consts, y_ref.at[i], cn_ref.at[i], m_ref.at[i], ext_ref.at[i])


def _mixer_prompt_body(z_ref, poolw_ref, pscale_ref, bi_ref, bf_ref, mnorm_ref, gnorm_ref,
                       gws_ref, gbs_ref, gmean_ref,
                       y_ref, cn_ref, m_ref, ext_ref):
    chunk = pl.program_id(1)
    row = lax.broadcasted_iota(jnp.int32, (CHUNK, CHUNK), 0)
    col = lax.broadcasted_iota(jnp.int32, (CHUNK, CHUNK), 1)
    causal = col <= row
    lane = col

    ext_ref[16:16 + CHUNK, :] = z_ref[:, Z_POOL:Z_POOL + POOL_WIDTH]
    pos = chunk * CHUNK + lax.broadcasted_iota(jnp.int32, (CHUNK, 1), 0)
    pooled = []
    for tile in range(2):
        col0 = tile * LANES
        u_tile = z_ref[:, Z_POOL + col0:Z_POOL + col0 + LANES]
        pooled.append(_pool_tile(ext_ref, u_tile, col0, POOL_WINDOWS[2 * tile],
                                 POOL_WINDOWS[2 * tile + 1], pos))
    pooled = jnp.concatenate(pooled, axis=1).astype(BF16)
    y_pool = _dot(pooled, poolw_ref[...]) * pscale_ref[...]
    y_ref[:, 0:POOL_WIDTH] = y_pool.astype(BF16)
    ext_ref[0:16, :] = ext_ref[CHUNK:CHUNK + 16, :]

    vn = _group_rms(z_ref[:, Z_GV:Z_GV + GMLP_WIDTH], gmean_ref[...], gnorm_ref[...]).astype(BF16)
    for tile in range(2):
        col0 = tile * LANES
        vt = vn[:, col0:col0 + LANES]
        w_a = jnp.where(causal, gws_ref[2 * tile], 0.0).astype(BF16)
        w_b = jnp.where(causal, gws_ref[2 * tile + 1], 0.0).astype(BF16)
        mixed = jnp.where(lane < GMLP_GROUP_DIM, _dot(w_a, vt), _dot(w_b, vt))
        gu = z_ref[:, Z_GU + col0:Z_GU + col0 + LANES]
        y_g = gu * (mixed + gbs_ref[:, col0:col0 + LANES])
        y_ref[:, 768 + col0:768 + col0 + LANES] = y_g.astype(BF16)

    ig = z_ref[:, Z_IG:Z_IG + LANES] + bi_ref[...]
    lf = _log_sigmoid(z_ref[:, Z_FG:Z_FG + LANES] + bf_ref[...])
    tri = jnp.where(causal, 1.0, 0.0).astype(BF16)
    lf_hi, lf_mid, lf_lo = _split3(lf)
    b = _dot(tri, lf_hi) + _dot(tri, lf_mid) + _dot(tri, lf_lo)
    m_prev = m_ref[...]
    g = b + m_prev
    r_t = jnp.transpose(ig - b)
    b_last = b[CHUNK - 1:CHUNK, :]
    ones_col = jnp.where(lane == 0, 1.0, 0.0).astype(BF16)
    m_new_row = m_prev
    for h in range(MLSTM_HEADS):
        c0 = h * MLSTM_HEAD_DIM
        q = z_ref[:, Z_Q + c0:Z_Q + c0 + MLSTM_HEAD_DIM].astype(BF16)
        k = z_ref[:, Z_K + c0:Z_K + c0 + MLSTM_HEAD_DIM] * (MLSTM_HEAD_DIM ** -0.5)
        v = z_ref[:, Z_V + c0:Z_V + c0 + MLSTM_HEAD_DIM].astype(BF16)
        o = z_ref[:, Z_O + c0:Z_O + c0 + MLSTM_HEAD_DIM]
        b_col = b[:, h:h + 1]
        dmat = jnp.where(causal, b_col + r_t[h:h + 1, :], -jnp.inf)
        g_col = g[:, h:h + 1]
        m_t = jnp.maximum(g_col, jnp.max(dmat, axis=1, keepdims=True))
        scores = lax.dot_general(q, k.astype(BF16), (((1,), (1,)), ((), ())),
                                 preferred_element_type=F32)
        wts = jnp.exp(dmat - m_t) * scores
        inter = jnp.exp(g_col - m_t)
        cn_h = cn_ref[h]
        q_cn = _dot(q, cn_h.astype(BF16))
        num = inter * q_cn[:, 0:MLSTM_HEAD_DIM] + _dot(wts.astype(BF16), v)
        den = inter * q_cn[:, MLSTM_HEAD_DIM:MLSTM_HEAD_DIM + 1] + jnp.sum(wts, axis=1, keepdims=True)
        hid = num / jnp.maximum(jnp.abs(den), jnp.exp(-m_t))
        hid = _rms(hid, mnorm_ref[:, c0:c0 + MLSTM_HEAD_DIM])
        y_ref[:, POOL_WIDTH + c0:POOL_WIDTH + c0 + MLSTM_HEAD_DIM] = (jax.nn.sigmoid(o) * hid).astype(BF16)
        m_new = m_t[CHUNK - 1:CHUNK, :]
        bl = b_last[:, h:h + 1]
        decay = jnp.exp(bl + m_prev[:, h:h + 1] - m_new)
        w_s = jnp.exp(bl - b_col + ig[:, h:h + 1] - m_new)
        kw = (k * w_s).astype(BF16)
        v_ext = jnp.concatenate([v, ones_col], axis=1)
        cn_ref[h] = decay * cn_h + lax.dot_general(kw, v_ext, (((0,), (0,)), ((), ())),
                                                   preferred_element_type=F32)
        m_new_row = jnp.where(lane[0:1, :] == h, m_new, m_new_row)
    m_ref[...] = m_new_row


def _mixer_prompt(z, consts, batch, seq):
    nc = seq // CHUNK
    hd = MLSTM_HEAD_DIM
    z3 = z.reshape(batch, seq, Z_WIDTH)
    ns = PROMPT_SEQ_PER_STEP
    const_specs = [pl.BlockSpec(a.shape, lambda b, c, nd=a.ndim: (0,) * nd) for a in consts]
    return pl.pallas_call(
        _mixer_prompt_kernel,
        grid=(batch // ns, nc),
        in_specs=[pl.BlockSpec((ns, CHUNK, Z_WIDTH), lambda b, c: (b, c, 0))] + const_specs,
        out_specs=[pl.BlockSpec((ns, CHUNK, D_MODEL), lambda b, c: (b, c, 0)),
                   pl.BlockSpec((ns, MLSTM_HEADS, hd, 2 * hd), lambda b, c: (b, 0, 0, 0)),
                   pl.BlockSpec((ns, 1, LANES), lambda b, c: (b, 0, 0))],
        out_shape=[jax.ShapeDtypeStruct((batch, seq, D_MODEL), BF16),
                   jax.ShapeDtypeStruct((batch, MLSTM_HEADS, hd, 2 * hd), F32),
                   jax.ShapeDtypeStruct((batch, 1, LANES), F32)],
        scratch_shapes=[pltpu.VMEM((ns, 16 + CHUNK, POOL_WIDTH), F32)],
        compiler_params=_params("parallel", "arbitrary"),
        name="mixer_prompt",
    )(z3, *consts)


def _mixer_sample_kernel(z_ref, sp_ref, c_ref, n_ref, m_ref, c_other_layers_ref,
                         poolw_ref, pscale_ref, bi_ref, bf_ref, mnorm_ref, gnorm_ref,
                         gw0_ref, gb0_ref, gmean_ref,
                         y_ref, cn_ref, nn_ref, mn_ref, gv_ref, tk_ref):
    del c_other_layers_ref
    nb = SAMPLE_BLOCK
    hd = MLSTM_HEAD_DIM
    lane = lax.broadcasted_iota(jnp.int32, (nb, LANES), 1)
    seq_id = lax.broadcasted_iota(jnp.int32, (nb, LANES), 0)

    pooled = []
    for tile in range(2):
        col0 = tile * LANES
        u_tile = z_ref[:, Z_POOL + col0:Z_POOL + col0 + LANES]
        w_lo, w_hi = POOL_WINDOWS[2 * tile], POOL_WINDOWS[2 * tile + 1]
        acc = u_tile
        sums = {}
        for shift in range(1, w_hi):
            acc = acc + sp_ref[POOL_STATE - shift, :, col0:col0 + LANES]
            if shift + 1 in (w_lo, w_hi):
                sums[shift + 1] = acc
        pooled.append(jnp.where(lane < POOL_GROUP_DIM, sums[w_lo] / float(w_lo), sums[w_hi] / float(w_hi)) - u_tile)
    pooled = jnp.concatenate(pooled, axis=1).astype(BF16)
    y_ref[:, 0:POOL_WIDTH] = (_dot(pooled, poolw_ref[...]) * pscale_ref[...]).astype(BF16)

    vn = _group_rms(z_ref[:, Z_GV:Z_GV + GMLP_WIDTH], gmean_ref[...], gnorm_ref[...])
    gv_ref[...] = vn
    y_g = z_ref[:, Z_GU:Z_GU + GMLP_WIDTH] * (gw0_ref[...] * vn + gb0_ref[...])
    y_ref[:, 768:768 + GMLP_WIDTH] = y_g.astype(BF16)

    ig = z_ref[:, Z_IG:Z_IG + LANES] + bi_ref[...]
    lf = _log_sigmoid(z_ref[:, Z_FG:Z_FG + LANES] + bf_ref[...])
    m_prev = m_ref[...]
    g = lf + m_prev
    m_t = jnp.maximum(g, ig)
    inter = jnp.exp(g - m_t)
    e_ig = jnp.exp(ig - m_t)
    floor = jnp.exp(-m_t)
    mn_ref[...] = m_t
    tk_ref[...] = jnp.zeros((LANES, LANES), F32)
    for h in range(MLSTM_HEADS):
        tk_ref[nb * h:nb * (h + 1), :] = z_ref[:, Z_K + h * hd:Z_K + (h + 1) * hd] * (hd ** -0.5)
    k_t = jnp.transpose(tk_ref[...])
    for h in range(MLSTM_HEADS):
        c0 = h * hd
        q_h = z_ref[:, Z_Q + c0:Z_Q + c0 + hd]
        k_h = tk_ref[nb * h:nb * (h + 1), :]
        v_h = z_ref[:, Z_V + c0:Z_V + c0 + hd]
        o_h = z_ref[:, Z_O + c0:Z_O + c0 + hd]
        n_h = n_ref[:, c0:c0 + hd]
        inter_b = jnp.broadcast_to(inter[:, h:h + 1], (nb, hd))
        e_b = jnp.broadcast_to(e_ig[:, h:h + 1], (nb, hd))
        floor_b = jnp.broadcast_to(floor[:, h:h + 1], (nb, hd))
        v_w = e_b * v_h
        q_b = q_h.astype(BF16)
        q_c = jnp.zeros((nb, hd), F32)
        for s in range(nb):
            c_sh = c_ref[s, h]
            q_c = jnp.where(seq_id == s, _dot(q_b, c_sh.astype(BF16)), q_c)
            col = nb * h + s
            cn_ref[s, h] = inter_b[s:s + 1, :] * c_sh + k_t[:, col:col + 1] * v_w[s:s + 1, :]
        wts = e_b * jnp.sum(q_h * k_h, axis=1, keepdims=True)
        num = inter_b * q_c + wts * v_h
        den = inter_b * jnp.sum(q_h * n_h, axis=1, keepdims=True) + wts
        hid = num / jnp.maximum(jnp.abs(den), floor_b)
        hid = _rms(hid, mnorm_ref[:, c0:c0 + hd])
        y_ref[:, POOL_WIDTH + c0:POOL_WIDTH + c0 + hd] = (jax.nn.sigmoid(o_h) * hid).astype(BF16)
        nn_ref[:, c0:c0 + hd] = inter_b * n_h + e_b * k_h


def _mixer_sample(z, sp_t, c_all, layer, c_new_all, n_state, m_pad, consts):
    nseq = z.shape[0]
    nb = SAMPLE_BLOCK
    hd = MLSTM_HEAD_DIM
    const_specs = [pl.BlockSpec(a.shape, lambda j, nd=a.ndim: (0,) * nd) for a in consts]
    c_spec = pl.BlockSpec((None, nb, MLSTM_HEADS, hd, hd), lambda j: (layer, j, 0, 0, 0))
    aliases = {} if c_new_all is None else {5: 1}
    return pl.pallas_call(
        _mixer_sample_kernel,
        grid=(nseq // nb,),
        in_specs=[pl.BlockSpec((nb, Z_WIDTH), lambda j: (j, 0)),
                  pl.BlockSpec((POOL_STATE, nb, POOL_WIDTH), lambda j: (0, j, 0)),
                  c_spec,
                  pl.BlockSpec((nb, MLSTM_WIDTH), lambda j: (j, 0)),
                  pl.BlockSpec((nb, LANES), lambda j: (j, 0)),
                  pl.BlockSpec(memory_space=pl.ANY)] + const_specs,
        out_specs=[pl.BlockSpec((nb, D_MODEL), lambda j: (j, 0)),
                   c_spec,
                   pl.BlockSpec((nb, MLSTM_WIDTH), lambda j: (j, 0)),
                   pl.BlockSpec((nb, LANES), lambda j: (j, 0)),
                   pl.BlockSpec((nb, GMLP_WIDTH), lambda j: (j, 0))],
        out_shape=[jax.ShapeDtypeStruct((nseq, D_MODEL), BF16),
                   jax.ShapeDtypeStruct(c_all.shape, F32),
                   jax.ShapeDtypeStruct((nseq, MLSTM_WIDTH), F32),
                   jax.ShapeDtypeStruct((nseq, LANES), F32),
                   jax.ShapeDtypeStruct((nseq, GMLP_WIDTH), F32)],
        scratch_shapes=[pltpu.VMEM((LANES, LANES), F32)],
        input_output_aliases=aliases,
        compiler_params=_params("parallel"),
        name="mixer_sample",
    )(z, sp_t, c_all, n_state, m_pad, c_all if c_new_all is None else c_new_all, *consts)


def _proj_norm_res_kernel(y_ref, x_ref, w_ref, nw_ref, o_ref):
    o_ref[...] = x_ref[...] + _rms(_dot(y_ref[...], w_ref[...]), nw_ref[...])


def _proj_norm_res(y, x, w, nw, tm):
    t = x.shape[0]
    return pl.pallas_call(
        _proj_norm_res_kernel,
        grid=(t // tm,),
        in_specs=[pl.BlockSpec((tm, D_MODEL), lambda i: (i, 0)),
                  pl.BlockSpec((tm, D_MODEL), lambda i: (i, 0)),
                  pl.BlockSpec((D_MODEL, D_MODEL), lambda i: (0, 0)),
                  pl.BlockSpec((1, D_MODEL), lambda i: (0, 0))],
        out_specs=pl.BlockSpec((tm, D_MODEL), lambda i: (i, 0)),
        out_shape=jax.ShapeDtypeStruct((t, D_MODEL), F32),
        compiler_params=_params("parallel"),
        name="out_proj",
    )(y, x, w, nw)


def _dense_layer_kernel(y_ref, x_ref, p_ref, wout_ref, nmix_ref, npre_ref, npost_ref, nple_ref,
                        wg_ref, wu_ref, wd_ref, pg_ref, pp_ref, o_ref):
    x1 = x_ref[...] + _rms(_dot(y_ref[...], wout_ref[...]), nmix_ref[...])
    h = _rms(x1, npre_ref[...]).astype(BF16)
    y = None
    for f0 in range(0, D_FF, FF_CHUNK):
        fw = min(FF_CHUNK, D_FF - f0)
        gate = _dot(h, wg_ref[:, f0:f0 + fw])
        up = _dot(h, wu_ref[:, f0:f0 + fw])
        act = (gate * jax.nn.sigmoid(gate) * up).astype(BF16)
        part = _dot(act, wd_ref[f0:f0 + fw, :])
        y = part if y is None else y + part
    x2 = x1 + _rms(y, npost_ref[...])
    gate = jax.nn.sigmoid(_dot(_rms(x2, nple_ref[...]).astype(BF16), pg_ref[...]))
    o_ref[...] = x2 + gate * _dot(p_ref[...].astype(BF16), pp_ref[...])


def _dense_layer(y, x, p, w_out, nmix, npre, npost, nple, wg, wu, wd, ple_g, ple_p, tm):
    t = x.shape[0]
    return pl.pallas_call(
        _dense_layer_kernel,
        grid=(t // tm,),
        in_specs=[pl.BlockSpec((tm, D_MODEL), lambda i: (i, 0)),
                  pl.BlockSpec((tm, D_MODEL), lambda i: (i, 0)),
                  pl.BlockSpec((tm, PLE_DIM), lambda i: (i, 0)),
                  _resident((D_MODEL, D_MODEL)),
                  _resident((1, D_MODEL)), _resident((1, D_MODEL)), _resident((1, D_MODEL)), _resident((1, D_MODEL)),
                  _resident((D_MODEL, D_FF)), _resident((D_MODEL, D_FF)), _resident((D_FF, D_MODEL)),
                  _resident((D_MODEL, D_MODEL)), _resident((PLE_DIM, D_MODEL))],
        out_specs=pl.BlockSpec((tm, D_MODEL), lambda i: (i, 0)),
        out_shape=jax.ShapeDtypeStruct((t, D_MODEL), F32),
        compiler_params=_params("parallel"),
        name="dense_layer",
    )(y, x, p, w_out, nmix, npre, npost, nple, wg, wu, wd, ple_g, ple_p)


def _router_gates(h, rw_ref, rb_ref):
    shape = (h.shape[0], LANES)
    lane = lax.broadcasted_iota(jnp.int32, shape, 1)
    lane_f = lane.astype(F32)
    logits = jnp.where(lane < N_EXPERTS, _dot(h, rw_ref[...]) + rb_ref[...], -jnp.inf)
    l1 = jnp.max(logits, axis=-1, keepdims=True)
    i1 = jnp.min(jnp.where(logits == l1, lane_f, float(LANES)), axis=-1, keepdims=True)
    rest = jnp.where(lane_f == i1, -jnp.inf, logits)
    l2 = jnp.max(rest, axis=-1, keepdims=True)
    i2 = jnp.min(jnp.where(rest == l2, lane_f, float(LANES)), axis=-1, keepdims=True)
    e2 = jnp.exp(l2 - l1)
    total = 1.0 + e2
    return jnp.where(lane_f == i1, 1.0 / total, 0.0) + jnp.where(lane_f == i2, e2 / total, 0.0)


def _ffn_moe_kernel(x_ref, npre_ref, npost_ref, rw_ref, rb_ref, wg_ref, wu_ref, wd_ref,
                    o_ref, h_ref, acc_ref, gates_ref):
    e = pl.program_id(1)

    @pl.when(e == 0)
    def _():
        h_ref[...] = _rms(x_ref[...], npre_ref[...]).astype(BF16)
        acc_ref[...] = jnp.zeros(acc_ref.shape, F32)
        gates_ref[...] = _router_gates(h_ref[...], rw_ref, rb_ref)

    h = h_ref[...]
    lane = lax.broadcasted_iota(jnp.int32, gates_ref.shape, 1)
    gate_col = jnp.sum(jnp.where(lane == e, gates_ref[...], 0.0), axis=-1, keepdims=True)
    y = None
    for f0, fw in FF_EXPERT_CHUNKS:
        gate = _dot(h, wg_ref[:, f0:f0 + fw].astype(BF16))
        up = _dot(h, wu_ref[:, f0:f0 + fw].astype(BF16))
        act = (gate * jax.nn.sigmoid(gate) * up).astype(BF16)
        part = _dot(act, wd_ref[f0:f0 + fw, :].astype(BF16))
        y = part if y is None else y + part
    acc_ref[...] += gate_col * y

    @pl.when(e == pl.num_programs(1) - 1)
    def _():
        o_ref[...] = x_ref[...] + _rms(acc_ref[...], npost_ref[...])


def _ffn_moe(x, npre, npost, rw, rb, wg, wu, wd, tm):
    t = x.shape[0]
    return pl.pallas_call(
        _ffn_moe_kernel,
        grid=(t // tm, N_EXPERTS),
        in_specs=[pl.BlockSpec((tm, D_MODEL), lambda i, e: (i, 0)),
                  pl.BlockSpec((1, D_MODEL), lambda i, e: (0, 0)),
                  pl.BlockSpec((1, D_MODEL), lambda i, e: (0, 0)),
                  pl.BlockSpec((D_MODEL, LANES), lambda i, e: (0, 0)),
                  pl.BlockSpec((1, LANES), lambda i, e: (0, 0)),
                  pl.BlockSpec((None, D_MODEL, D_FF_EXPERT), lambda i, e: (e, 0, 0)),
                  pl.BlockSpec((None, D_MODEL, D_FF_EXPERT), lambda i, e: (e, 0, 0)),
                  pl.BlockSpec((None, D_FF_EXPERT, D_MODEL), lambda i, e: (e, 0, 0))],
        out_specs=pl.BlockSpec((tm, D_MODEL), lambda i, e: (i, 0)),
        out_shape=jax.ShapeDtypeStruct((t, D_MODEL), F32),
        scratch_shapes=[pltpu.VMEM((tm, D_MODEL), BF16), pltpu.VMEM((tm, D_MODEL), F32),
                        pltpu.VMEM((tm, LANES), F32)],
        compiler_params=_params("parallel", "arbitrary", vmem_limit=VMEM_LIMIT_EXPERT_WEIGHTS),
        name="ffn_moe",
    )(x, npre, npost, rw, rb, wg, wu, wd)


ROUTE_TILE = 256
ROW_ALIGN = 16
ROUTE_SEG = ROUTE_TILE
ROUTE_PACK = 2 * ROUTE_TILE + N_EXPERTS * ROW_ALIGN
ROUTE_W = D_MODEL + 3 * LANES
ROUTE_BLOCK = 512
ROUTE_SEG_SHORT = 96
ROUTE_SHORT_MAX = ROUTE_SEG_SHORT


def _route_region(n_tokens):
    rows = n_tokens + (n_tokens // ROUTE_TILE) * (ROW_ALIGN - 1) + ROUTE_SEG + ROUTE_BLOCK
    return -(-rows // ROUTE_BLOCK) * ROUTE_BLOCK


def _lane_scalar(row, lane, e):
    return jnp.sum(jnp.where(lane == e, row, 0.0)).astype(jnp.int32)


def _route_kernel(y_ref, x_ref, wout_ref, nmix_ref, npre_ref, rw_ref, rb_ref,
                  x1_ref, slot_ref, stats_ref, srt_hbm,
                  stage_ref, runv_ref, run_ref, short_ref, sem, *, region):
    i = pl.program_id(0)
    last = pl.num_programs(0) - 1
    cur = i % 2

    @pl.when(i == 0)
    def _():
        runv_ref[...] = jnp.zeros(runv_ref.shape, F32)
        stage_ref[:, ROUTE_PACK:, :] = jnp.zeros((2, ROUTE_SEG, ROUTE_W), BF16)
        for e in range(N_EXPERTS):
            run_ref[e] = 0

    x1 = x_ref[...] + _rms(_dot(y_ref[...], wout_ref[...]), nmix_ref[...])
    x1_ref[...] = x1
    h = _rms(x1, npre_ref[...]).astype(BF16)
    gates = _router_gates(h, rw_ref, rb_ref)
    sel = gates > 0.0
    ones = jnp.where(sel, 1.0, 0.0)
    trow = lax.broadcasted_iota(jnp.int32, (ROUTE_TILE, ROUTE_TILE), 0)
    tcol = lax.broadcasted_iota(jnp.int32, (ROUTE_TILE, ROUTE_TILE), 1)
    before = jnp.where(tcol < trow, 1.0, 0.0).astype(BF16)
    rank = _dot(before, ones.astype(BF16))
    cnt = jnp.sum(ones, axis=0, keepdims=True)
    cnt_pad = jnp.floor((cnt + (ROW_ALIGN - 1)) * (1.0 / ROW_ALIGN)) * ROW_ALIGN
    lrow = lax.broadcasted_iota(jnp.int32, (LANES, LANES), 0)
    lcol = lax.broadcasted_iota(jnp.int32, (LANES, LANES), 1)
    lower = jnp.where(lrow < lcol, 1.0, 0.0).astype(BF16)
    off = _dot(jnp.broadcast_to(cnt_pad, (SUBLANES, LANES)).astype(BF16), lower)[0:1, :]
    lane = lax.broadcasted_iota(jnp.int32, (1, LANES), 1)
    src_rows = [_lane_scalar(off, lane, e) for e in range(N_EXPERTS)]
    seg_lens = [_lane_scalar(cnt_pad, lane, e) for e in range(N_EXPERTS)]
    short = (jnp.max(cnt_pad) <= ROUTE_SHORT_MAX).astype(jnp.int32)
    slot_ref[...] = jnp.where(sel, rank, -1.0)
    stats_ref[...] = jnp.zeros(stats_ref.shape, F32)
    stats_ref[0:1, :] = runv_ref[...]
    stats_ref[1:2, :] = cnt
    runv_ref[...] = runv_ref[...] + cnt_pad

    pos = jnp.where(sel, rank + off, -1.0)
    pos_t = jnp.concatenate([jnp.transpose(pos[0:LANES, :]), jnp.transpose(pos[LANES:2 * LANES, :])], axis=1)
    pos_a = jnp.max(pos_t, axis=0, keepdims=True)
    pos_b = jnp.max(jnp.where(pos_t == pos_a, -1.0, pos_t), axis=0, keepdims=True)
    prow = lax.broadcasted_iota(jnp.int32, (ROUTE_PACK, ROUTE_TILE), 0).astype(F32)
    perm = jnp.where((prow == pos_a) | (prow == pos_b), 1.0, 0.0).astype(BF16)
    g_hi, g_mid, g_lo = _split3(gates)
    rows = _dot(perm, jnp.concatenate([h, g_hi, g_mid, g_lo], axis=1))
    stage_ref[cur, 0:ROUTE_PACK, :] = rows.astype(BF16)

    def segment_copy(e, src_row, dst_row, slot, rows=ROUTE_SEG):
        return pltpu.make_async_copy(
            stage_ref.at[slot, pl.ds(pl.multiple_of(src_row, ROW_ALIGN), rows), :],
            srt_hbm.at[pl.ds(pl.multiple_of(dst_row, ROW_ALIGN), rows), :],
            sem.at[e])

    def for_each_segment(is_short, action):
        for rows, flag in ((ROUTE_SEG_SHORT, 1), (ROUTE_SEG, 0)):
            @pl.when(is_short == flag)
            def _():
                for e in range(N_EXPERTS):
                    action(e, rows)

    @pl.when(i > 0)
    def _():
        for_each_segment(short_ref[0], lambda e, rows: segment_copy(e, 0, 0, 1 - cur, rows).wait())

    dst_rows = [e * region + run_ref[e] for e in range(N_EXPERTS)]
    for_each_segment(short, lambda e, rows: segment_copy(e, src_rows[e], dst_rows[e], cur, rows).start())
    for e in range(N_EXPERTS):
        run_ref[e] = run_ref[e] + seg_lens[e]
    short_ref[0] = short

    @pl.when(i == last)
    def _():
        for_each_segment(short, lambda e, rows: segment_copy(e, 0, 0, cur, rows).wait())
        stage_ref[1 - cur, 0:ROUTE_SEG, :] = jnp.zeros((ROUTE_SEG, ROUTE_W), BF16)
        for part in range(ROUTE_BLOCK // ROUTE_SEG):
            for e in range(N_EXPERTS):
                segment_copy(e, 0, e * region + run_ref[e] + part * ROUTE_SEG, 1 - cur).start()
            for e in range(N_EXPERTS):
                segment_copy(e, 0, 0, 1 - cur).wait()


def _route(y, x, w_out, nmix, npre, rw, rb):
    t = x.shape[0]
    nt = t // ROUTE_TILE
    region = _route_region(t)
    return pl.pallas_call(
        functools.partial(_route_kernel, region=region),
        grid=(nt,),
        in_specs=[pl.BlockSpec((ROUTE_TILE, D_MODEL), lambda i: (i, 0)),
                  pl.BlockSpec((ROUTE_TILE, D_MODEL), lambda i: (i, 0)),
                  pl.BlockSpec((D_MODEL, D_MODEL), lambda i: (0, 0)),
                  pl.BlockSpec((1, D_MODEL), lambda i: (0, 0)),
                  pl.BlockSpec((1, D_MODEL), lambda i: (0, 0)),
                  pl.BlockSpec((D_MODEL, LANES), lambda i: (0, 0)),
                  pl.BlockSpec((1, LANES), lambda i: (0, 0))],
        out_specs=[pl.BlockSpec((ROUTE_TILE, D_MODEL), lambda i: (i, 0)),
                   pl.BlockSpec((ROUTE_TILE, LANES), lambda i: (i, 0)),
                   pl.BlockSpec((None, SUBLANES, LANES), lambda i: (i, 0, 0)),
                   pl.BlockSpec(memory_space=pl.ANY)],
        out_shape=[jax.ShapeDtypeStruct((t, D_MODEL), F32),
                   jax.ShapeDtypeStruct((t, LANES), F32),
                   jax.ShapeDtypeStruct((nt, SUBLANES, LANES), F32),
                   jax.ShapeDtypeStruct((N_EXPERTS * region, ROUTE_W), BF16)],
        scratch_shapes=[pltpu.VMEM((2, ROUTE_PACK + ROUTE_SEG, ROUTE_W), BF16),
                        pltpu.VMEM((1, LANES), F32),
                        pltpu.SMEM((N_EXPERTS,), jnp.int32),
                        pltpu.SMEM((1,), jnp.int32),
                        pltpu.SemaphoreType.DMA((N_EXPERTS,))],
        compiler_params=_params("arbitrary"),
        name="moe_route",
    )(y, x, w_out, nmix, npre, rw, rb)


def _experts_kernel(blk_row_ref, blk_e_ref, n_used_ref, srt_ref, wg_ref, wu_ref, wd_ref, yhi_ref, ylo_ref):
    k = pl.program_id(0)

    @pl.when(k < n_used_ref[0])
    def _():
        h = srt_ref[:, 0:D_MODEL]
        gate3 = (srt_ref[:, D_MODEL:D_MODEL + LANES].astype(F32)
                 + srt_ref[:, D_MODEL + LANES:D_MODEL + 2 * LANES].astype(F32)
                 + srt_ref[:, D_MODEL + 2 * LANES:D_MODEL + 3 * LANES].astype(F32))
        lane = lax.broadcasted_iota(jnp.int32, gate3.shape, 1)
        gate_col = jnp.sum(jnp.where(lane == blk_e_ref[k], gate3, 0.0), axis=-1, keepdims=True)
        y = None
        for f0, fw in FF_EXPERT_CHUNKS:
            gate = _dot(h, wg_ref[:, f0:f0 + fw].astype(BF16))
            up = _dot(h, wu_ref[:, f0:f0 + fw].astype(BF16))
            act = (gate * jax.nn.sigmoid(gate) * up).astype(BF16)
            part = _dot(act, wd_ref[f0:f0 + fw, :].astype(BF16))
            y = part if y is None else y + part
        y = gate_col * y
        hi = y.astype(BF16)
        yhi_ref[...] = hi
        ylo_ref[...] = (y - hi.astype(F32)).astype(BF16)


def _experts(srt, blk_row, blk_e, n_used, wg, wu, wd, n_blocks):
    rows = srt.shape[0]
    grid_spec = pltpu.PrefetchScalarGridSpec(
        num_scalar_prefetch=3,
        grid=(n_blocks,),
        in_specs=[pl.BlockSpec((ROUTE_BLOCK, ROUTE_W), lambda k, br, be, nu: (br[k], 0)),
                  pl.BlockSpec((None, D_MODEL, D_FF_EXPERT), lambda k, br, be, nu: (be[k], 0, 0)),
                  pl.BlockSpec((None, D_MODEL, D_FF_EXPERT), lambda k, br, be, nu: (be[k], 0, 0)),
                  pl.BlockSpec((None, D_FF_EXPERT, D_MODEL), lambda k, br, be, nu: (be[k], 0, 0))],
        out_specs=[pl.BlockSpec((ROUTE_BLOCK, D_MODEL), lambda k, br, be, nu: (br[k], 0)),
                   pl.BlockSpec((ROUTE_BLOCK, D_MODEL), lambda k, br, be, nu: (br[k], 0))])
    return pl.pallas_call(
        _experts_kernel,
        grid_spec=grid_spec,
        out_shape=[jax.ShapeDtypeStruct((rows, D_MODEL), BF16), jax.ShapeDtypeStruct((rows, D_MODEL), BF16)],
        compiler_params=_params("arbitrary", vmem_limit=VMEM_LIMIT_EXPERT_WEIGHTS),
        name="moe_experts",
    )(blk_row, blk_e, n_used, srt, wg, wu, wd)


def _combine_kernel(src_row_ref, short_ref, x_ref, slot_ref, shift_ref, p_ref, npost_ref, nple_ref,
                    wg_ref, wp_ref, yhi_hbm, ylo_hbm, o_ref, seg_hi_ref, seg_lo_ref, y_ref, sem):
    i = pl.program_id(0)
    nt = pl.num_programs(0)
    cur = i % 2

    def segment_copies(tile, slot, e, rows):
        src = pl.ds(pl.multiple_of(src_row_ref[tile * N_EXPERTS + e], ROW_ALIGN), rows)
        dst = pl.ds(e * rows, rows)
        return (pltpu.make_async_copy(yhi_hbm.at[src, :], seg_hi_ref.at[slot, dst, :], sem.at[slot, 0, e]),
                pltpu.make_async_copy(ylo_hbm.at[src, :], seg_lo_ref.at[slot, dst, :], sem.at[slot, 1, e]))

    def for_each_segment(tile, slot, action):
        for rows, is_short in ((ROUTE_SEG_SHORT, 1), (ROUTE_SEG, 0)):
            @pl.when(short_ref[tile] == is_short)
            def _():
                for e in range(N_EXPERTS):
                    for c in segment_copies(tile, slot, e, rows):
                        action(c)

    @pl.when(i == 0)
    def _():
        for_each_segment(0, 0, lambda c: c.start())

    @pl.when(i + 1 < nt)
    def _():
        for_each_segment(i + 1, 1 - cur, lambda c: c.start())

    for_each_segment(i, cur, lambda c: c.wait())

    slot = slot_ref[...]
    where = jnp.where(slot >= 0.0, slot + shift_ref[...], -1.0)

    def gather(rows):
        seg_lane = lax.broadcasted_iota(jnp.int32, (ROUTE_TILE, rows), 1).astype(F32)
        perm = jnp.concatenate([jnp.where(where[:, e:e + 1] == seg_lane, 1.0, 0.0).astype(BF16)
                                for e in range(N_EXPERTS)], axis=1)
        k = N_EXPERTS * rows
        y_ref[...] = _dot(perm, seg_hi_ref[cur, 0:k, :]) + _dot(perm, seg_lo_ref[cur, 0:k, :])

    @pl.when(short_ref[i] == 1)
    def _():
        gather(ROUTE_SEG_SHORT)

    @pl.when(short_ref[i] == 0)
    def _():
        gather(ROUTE_SEG)

    x = x_ref[...] + _rms(y_ref[...], npost_ref[...])
    gate = jax.nn.sigmoid(_dot(_rms(x, nple_ref[...]).astype(BF16), wg_ref[...]))
    o_ref[...] = x + gate * _dot(p_ref[...].astype(BF16), wp_ref[...])


def _combine(src_row, short, x, slot, shift, p, npost, nple, wg, wp, yhi, ylo):
    t = x.shape[0]
    grid_spec = pltpu.PrefetchScalarGridSpec(
        num_scalar_prefetch=2,
        grid=(t // ROUTE_TILE,),
        in_specs=[pl.BlockSpec((ROUTE_TILE, D_MODEL), lambda i, *_: (i, 0)),
                  pl.BlockSpec((ROUTE_TILE, LANES), lambda i, *_: (i, 0)),
                  pl.BlockSpec((None, 1, LANES), lambda i, *_: (i, 0, 0)),
                  pl.BlockSpec((ROUTE_TILE, PLE_DIM), lambda i, *_: (i, 0)),
                  pl.BlockSpec((1, D_MODEL), lambda i, *_: (0, 0)),
                  pl.BlockSpec((1, D_MODEL), lambda i, *_: (0, 0)),
                  pl.BlockSpec((D_MODEL, D_MODEL), lambda i, *_: (0, 0)),
                  pl.BlockSpec((PLE_DIM, D_MODEL), lambda i, *_: (0, 0)),
                  pl.BlockSpec(memory_space=pl.ANY),
                  pl.BlockSpec(memory_space=pl.ANY)],
        out_specs=pl.BlockSpec((ROUTE_TILE, D_MODEL), lambda i, *_: (i, 0)),
        scratch_shapes=[pltpu.VMEM((2, N_EXPERTS * ROUTE_SEG, D_MODEL), BF16),
                        pltpu.VMEM((2, N_EXPERTS * ROUTE_SEG, D_MODEL), BF16),
                        pltpu.VMEM((ROUTE_TILE, D_MODEL), F32),
                        pltpu.SemaphoreType.DMA((2, 2, N_EXPERTS))])
    return pl.pallas_call(
        _combine_kernel,
        grid_spec=grid_spec,
        out_shape=jax.ShapeDtypeStruct((t, D_MODEL), F32),
        compiler_params=_params("arbitrary"),
        name="moe_combine_ple",
    )(src_row, short, x, slot, shift, p, npost, nple, wg, wp, yhi, ylo)


def _moe_layer_routed(y, x, p, w_out, nmix, npre, npost, nple, rw, rb, wg, wu, wd, ple_g, ple_p):
    t = x.shape[0]
    nt = t // ROUTE_TILE
    region = _route_region(t)
    x, slot, stats, srt = _route(y, x, w_out, nmix, npre, rw, rb)
    base = stats[:, 0, 0:N_EXPERTS].astype(jnp.int32)
    cnt = stats[:, 1, 0:N_EXPERTS].astype(jnp.int32)
    cnt_pad = (cnt + (ROW_ALIGN - 1)) // ROW_ALIGN * ROW_ALIGN
    total = base[-1] + cnt_pad[-1]
    nblk = (total + (ROUTE_BLOCK - 1)) // ROUTE_BLOCK
    cum = jnp.cumsum(nblk)
    n_used = cum[-1]
    max_rows = 2 * t + nt * N_EXPERTS * (ROW_ALIGN - 1)
    n_blocks = max_rows // ROUTE_BLOCK + N_EXPERTS
    kk = jnp.minimum(jnp.arange(n_blocks, dtype=jnp.int32), n_used - 1)
    blk_e = jnp.sum(kk[:, None] >= cum[None, :], axis=1).astype(jnp.int32)
    blk_row = blk_e * (region // ROUTE_BLOCK) + kk - (cum - nblk)[blk_e]
    yhi, ylo = _experts(srt, blk_row.astype(jnp.int32), blk_e, n_used.reshape(1).astype(jnp.int32),
                        wg, wu, wd, n_blocks)
    short = jnp.all(cnt_pad <= ROUTE_SHORT_MAX, axis=1)
    seg_rows = jnp.where(short, ROUTE_SEG_SHORT, ROUTE_SEG)[:, None]
    start = jnp.maximum(jnp.minimum(base, nblk[None, :] * ROUTE_BLOCK - seg_rows), 0)
    src_row = jnp.arange(N_EXPERTS, dtype=jnp.int32)[None, :] * region + start
    first_used = jnp.argmax(nblk > 0).astype(jnp.int32)
    src_row = jnp.where(cnt > 0, src_row, first_used * region).reshape(-1)
    shift = _pad_lanes((base - start).astype(F32)).reshape(nt, 1, LANES)
    return _combine(src_row.astype(jnp.int32), short.astype(jnp.int32), x, slot, shift, p, npost, nple,
                    ple_g, ple_p, yhi, ylo)


def _ple_kernel(x_ref, p_ref, nw_ref, wg_ref, wp_ref, o_ref):
    x = x_ref[...]
    gate = jax.nn.sigmoid(_dot(_rms(x, nw_ref[...]).astype(BF16), wg_ref[...]))
    o_ref[...] = x + gate * _dot(p_ref[...].astype(BF16), wp_ref[...])


def _ple(x, p, nw, wg, wp, tm):
    t = x.shape[0]
    return pl.pallas_call(
        _ple_kernel,
        grid=(t // tm,),
        in_specs=[pl.BlockSpec((tm, D_MODEL), lambda i: (i, 0)),
                  pl.BlockSpec((tm, PLE_DIM), lambda i: (i, 0)),
                  pl.BlockSpec((1, D_MODEL), lambda i: (0, 0)),
                  pl.BlockSpec((D_MODEL, D_MODEL), lambda i: (0, 0)),
                  pl.BlockSpec((PLE_DIM, D_MODEL), lambda i: (0, 0))],
        out_specs=pl.BlockSpec((tm, D_MODEL), lambda i: (i, 0)),
        out_shape=jax.ShapeDtypeStruct((t, D_MODEL), F32),
        compiler_params=_params("parallel"),
        name="ple",
    )(x, p, nw, wg, wp)


def _pad_lanes(a, width=LANES):
    return jnp.pad(a, [(0, 0)] * (a.ndim - 1) + [(0, width - a.shape[-1])])


def _block_diag(blocks):
    g, d, _ = blocks.shape
    out = jnp.zeros((g * d, g * d), blocks.dtype)
    for i in range(g):
        out = out.at[i * d:(i + 1) * d, i * d:(i + 1) * d].set(blocks[i])
    return out


def _row(a):
    return a.reshape(1, -1).astype(F32)


def kernel(x_prompt, x_sample, state_pool, state_mlstm_C, state_mlstm_n, state_mlstm_m, p_prompt, p_sample,
           norm_mix_pre, norm_mix_post, norm_ffn_pre, norm_ffn_post, norm_ple, w_in, pool_w, pool_scale,
           mlstm_b_i, mlstm_b_f, mlstm_norm_w, gmlp_norm_w, gmlp_ws, gmlp_bs, w_out,
           ffn_w_gate, ffn_w_up, ffn_w_down, moe_router_w, moe_router_b, moe_w_gate, moe_w_up, moe_w_down,
           ple_w_gate, ple_w_proj):
    batch, seq, _ = x_prompt.shape
    nseq = x_sample.shape[0]
    xp = x_prompt.reshape(batch * seq, D_MODEL)
    xs = x_sample.reshape(nseq, D_MODEL)
    gmean = _block_diag(jnp.full((GMLP_GROUPS, GMLP_GROUP_DIM, GMLP_GROUP_DIM), 1.0 / GMLP_GROUP_DIM, BF16))

    w_in_t = jnp.swapaxes(w_in, 1, 2)

    pools_p, cs_p, ns_p, ms_p = [], [], [], []
    pools_s, ns_s, ms_s, gvs_s = [], [], [], []
    c_new_s = None
    for i in range(DEPTH):
        w_out_b = w_out[i].astype(BF16)
        poolw = _block_diag(pool_w[i]).astype(BF16)
        shared = [poolw, _row(pool_scale[i]), _pad_lanes(_row(mlstm_b_i[i])), _pad_lanes(_row(mlstm_b_f[i])),
                  _row(mlstm_norm_w[i]), _row(gmlp_norm_w[i])]
        gbs_full = jnp.repeat(gmlp_bs[i].T, GMLP_GROUP_DIM, axis=1)
        consts_p = shared + [gmlp_ws[i], gbs_full, gmean]
        gw0 = jnp.repeat(gmlp_ws[i][:, 0, 0], GMLP_GROUP_DIM).reshape(1, GMLP_WIDTH)
        consts_s = shared + [gw0, gbs_full[0:1, :], gmean]
        ple_g = ple_w_gate[i].astype(BF16)
        ple_p = ple_w_proj[i].astype(BF16)
        j = i // 2
        if i % 2 == 0:
            ffn_g, ffn_u, ffn_d = (ffn_w_gate[j].astype(BF16), ffn_w_up[j].astype(BF16),
                                   ffn_w_down[j].astype(BF16))
        else:
            rw = _pad_lanes(moe_router_w[j]).astype(BF16)
            rb = _pad_lanes(_row(moe_router_b[j]))
            moe_g, moe_u, moe_d = moe_w_gate[j], moe_w_up[j], moe_w_down[j]

        z = _norm_matmul(xp, _row(norm_mix_pre[i]), w_in_t, i, TM_PROMPT)
        y, cn_new, m_new = _mixer_prompt(z, consts_p, batch, seq)
        pools_p.append(z.reshape(batch, seq, Z_WIDTH)[:, seq - POOL_STATE:, 0:POOL_WIDTH])
        cs_p.append(cn_new[..., 0:MLSTM_HEAD_DIM])
        ns_p.append(cn_new[..., MLSTM_HEAD_DIM])
        ms_p.append(m_new[:, 0, 0:MLSTM_HEADS])
        y = y.reshape(batch * seq, D_MODEL)
        pp = p_prompt[i].reshape(batch * seq, PLE_DIM)
        norms = (_row(norm_mix_post[i]), _row(norm_ffn_pre[i]), _row(norm_ffn_post[i]), _row(norm_ple[i]))
        if i % 2 == 0:
            xp = _dense_layer(y, xp, pp, w_out_b, *norms, ffn_g, ffn_u, ffn_d, ple_g, ple_p, TM_PROMPT)
        else:
            xp = _moe_layer_routed(y, xp, pp, w_out_b, *norms, rw, rb, moe_g, moe_u, moe_d, ple_g, ple_p)

        z = _norm_matmul(xs, _row(norm_mix_pre[i]), w_in_t, i, nseq)
        sp_t = jnp.transpose(state_pool[i], (1, 0, 2))
        y, c_new_s, n_new, m_new, gv = _mixer_sample(z, sp_t, state_mlstm_C, i, c_new_s,
                                                     state_mlstm_n[i].reshape(nseq, MLSTM_WIDTH),
                                                     _pad_lanes(state_mlstm_m[i]), consts_s)
        pools_s.append(jnp.concatenate([state_pool[i][:, 1:], z[:, None, 0:POOL_WIDTH]], axis=1))
        ns_s.append(n_new.reshape(nseq, MLSTM_HEADS, MLSTM_HEAD_DIM))
        ms_s.append(m_new[:, 0:MLSTM_HEADS])
        gvs_s.append(gv[:, None, :])
        ps = p_sample[i].reshape(nseq, PLE_DIM)
        if i % 2 == 0:
            xs = _dense_layer(y, xs, ps, w_out_b, *norms, ffn_g, ffn_u, ffn_d, ple_g, ple_p, nseq)
        else:
            xs = _proj_norm_res(y, xs, w_out_b, norms[0], nseq)
            xs = _ffn_moe(xs, norms[1], norms[2], rw, rb, moe_g, moe_u, moe_d, nseq)
            xs = _ple(xs, ps, norms[3], ple_g, ple_p, nseq)

    return (xp.reshape(batch, seq, D_MODEL), xs.reshape(nseq, 1, D_MODEL),
            jnp.stack(pools_p), jnp.stack(cs_p), jnp.stack(ns_p), jnp.stack(ms_p),
            jnp.stack(pools_s), c_new_s, jnp.stack(ns_s), jnp.stack(ms_s), jnp.stack(gvs_s))
```

```python
import functools

import jax
import jax.numpy as jnp
from jax import lax
from jax.experimental import pallas as pl
from jax.experimental.pallas import tpu as pltpu

F32 = jnp.float32
BF16 = jnp.bfloat16

D_MODEL = 1024
DEPTH = 2
POOL_WIDTH = 256
POOL_WINDOWS = (2, 4, 8, 16)
POOL_GROUP_DIM = 64
POOL_STATE = 15
POOL_PREV_ROWS = 16
MLSTM_WIDTH = 512
MLSTM_HEADS = 4
MLSTM_HEAD_DIM = 128
CHUNK = 128
GMLP_WIDTH = 256
GMLP_GROUPS = 4
GMLP_GROUP_DIM = 64
D_FF = 2816
N_EXPERTS = 8
D_FF_EXPERT = 1408
PLE_DIM = 256
RMS_EPS = 1e-6
PAST_LEN = 16384

LANES = 128
SUBLANES = 8
VMEM_LIMIT = 48 * 1024 * 1024
VMEM_LIMIT_EXPERT_WEIGHTS = 58 * 1024 * 1024

Z_POOL = 0
Z_Q = 256
Z_K = 768
Z_V = 1280
Z_O = 1792
Z_GU = 2304
Z_GV = 2560
Z_GATES = 2816
Z_WIDTH = 2944
Y_GMLP = POOL_WIDTH + MLSTM_WIDTH
Z_CHUNK = 512
W_IN_IG = 2304
W_IN_GU = 2312
W_IN_WIDTH = 2824

TM_PROMPT = 512
FF_CHUNK = 512
FF_EXPERT_CHUNKS = ((0, 512), (512, 512), (1024, 384))
SAMPLE_BLOCK = 8
PROMPT_SEQ_PER_STEP = 2


def _params(*semantics, vmem_limit=VMEM_LIMIT):
    return pltpu.CompilerParams(dimension_semantics=semantics, vmem_limit_bytes=vmem_limit)


def _rms(x, w):
    return x * lax.rsqrt(jnp.mean(x * x, axis=-1, keepdims=True) + RMS_EPS) * w


def _log_sigmoid(x):
    return jnp.minimum(x, 0.0) - jnp.log1p(jnp.exp(-jnp.abs(x)))


def _dot(a, b):
    return jnp.dot(a, b, preferred_element_type=F32)


def _split3(x):
    hi = x.astype(BF16)
    rest = x - hi.astype(F32)
    mid = rest.astype(BF16)
    lo = (rest - mid.astype(F32)).astype(BF16)
    return hi, mid, lo


def _resident(shape):
    return pl.BlockSpec(shape, lambda i: (0,) * len(shape), pipeline_mode=pl.Buffered(1))


def _norm_matmul_kernel(x_ref, nw_ref, wt_ref, o_ref, h_ref, wz_ref):
    @pl.when(pl.program_id(0) == 0)
    def _():
        wz_ref[0:Z_GU, :] = wt_ref[0:Z_GU, :].astype(BF16)
        wz_ref[Z_GU:Z_GATES, :] = wt_ref[W_IN_GU:W_IN_GU + 2 * GMLP_WIDTH, :].astype(BF16)
        gates = wt_ref[W_IN_IG:W_IN_IG + 2 * MLSTM_HEADS, :]
        zeros = jnp.zeros((LANES - 2 * MLSTM_HEADS, D_MODEL), F32)
        wz_ref[Z_GATES:Z_WIDTH, :] = jnp.concatenate([gates, zeros], axis=0).astype(BF16)

    h_ref[...] = _rms(x_ref[...], nw_ref[...]).astype(BF16)
    for n0 in range(0, Z_WIDTH, Z_CHUNK):
        nw = min(Z_CHUNK, Z_WIDTH - n0)
        o_ref[:, n0:n0 + nw] = lax.dot_general(h_ref[...], wz_ref[n0:n0 + nw, :],
                                               (((1,), (1,)), ((), ())), preferred_element_type=F32)


def _norm_matmul(x, nw, wt_all, layer, tm):
    t = x.shape[0]
    return pl.pallas_call(
        _norm_matmul_kernel,
        grid=(t // tm,),
        in_specs=[pl.BlockSpec((tm, D_MODEL), lambda i: (i, 0)),
                  _resident((1, D_MODEL)),
                  pl.BlockSpec((None, W_IN_WIDTH, D_MODEL), lambda i: (layer, 0, 0),
                               pipeline_mode=pl.Buffered(1))],
        out_specs=pl.BlockSpec((tm, Z_WIDTH), lambda i: (i, 0)),
        out_shape=jax.ShapeDtypeStruct((t, Z_WIDTH), F32),
        scratch_shapes=[pltpu.VMEM((tm, D_MODEL), BF16), pltpu.VMEM((Z_WIDTH, D_MODEL), BF16)],
        compiler_params=_params("arbitrary"),
        name="norm_in_proj",
    )(x, nw, wt_all)


def _gate_terms(z_ref, bi_ref, bf_ref):
    gates = z_ref[:, Z_GATES:Z_GATES + LANES]
    forget = pltpu.roll(gates, LANES - MLSTM_HEADS, 1)
    return gates + bi_ref[...], _log_sigmoid(forget + bf_ref[...])


def _group_rms(v, gmean, w):
    hi, mid, lo = _split3(v * v)
    ms = _dot(hi, gmean) + _dot(mid, gmean) + _dot(lo, gmean)
    return v * lax.rsqrt(ms + RMS_EPS) * w


def _pool_tile(ext_ref, u_tile, col0, w_lo, w_hi, pos):
    acc = u_tile
    sums = {}
    for shift in range(1, w_hi):
        acc = acc + ext_ref[pl.ds(POOL_PREV_ROWS - shift, CHUNK), col0:col0 + LANES]
        if shift + 1 in (w_lo, w_hi):
            sums[shift + 1] = acc
    cnt_lo = jnp.minimum(w_lo, pos + 1).astype(F32)
    cnt_hi = jnp.minimum(w_hi, pos + 1).astype(F32)
    lane = lax.broadcasted_iota(jnp.int32, (CHUNK, LANES), 1)
    return jnp.where(lane < POOL_GROUP_DIM, sums[w_lo] / cnt_lo, sums[w_hi] / cnt_hi) - u_tile


def _mixer_prompt_kernel(z_ref, *refs):
    consts = refs[:9]
    y_ref, cn_ref, m_ref, ext_ref = refs[9:]

    @pl.when(pl.program_id(1) == 0)
    def _():
        ext_ref[:, 0:POOL_PREV_ROWS, :] = jnp.zeros((PROMPT_SEQ_PER_STEP, POOL_PREV_ROWS, POOL_WIDTH), F32)
        cn_ref[...] = jnp.zeros(cn_ref.shape, F32)
        m_ref[...] = jnp.zeros(m_ref.shape, F32)

    for i in range(PROMPT_SEQ_PER_STEP):
        _mixer_prompt_body(z_ref.at[i], *consts, y_ref.at[i], cn_ref.at[i], m_ref.at[i], ext_ref.at[i])


def _mixer_prompt_body(z_ref, poolw_ref, pscale_ref, bi_ref, bf_ref, mnorm_ref, gnorm_ref,
                       gws_ref, gbs_ref, gmean_ref,
                       y_ref, cn_ref, m_ref, ext_ref):
    chunk = pl.program_id(1)
    row = lax.broadcasted_iota(jnp.int32, (CHUNK, CHUNK), 0)
    col = lax.broadcasted_iota(jnp.int32, (CHUNK, CHUNK), 1)
    causal = col <= row
    lane = col

    ext_ref[POOL_PREV_ROWS:POOL_PREV_ROWS + CHUNK, :] = z_ref[:, Z_POOL:Z_POOL + POOL_WIDTH]
    pos = chunk * CHUNK + lax.broadcasted_iota(jnp.int32, (CHUNK, 1), 0)
    pooled = []
    for tile in range(2):
        col0 = tile * LANES
        u_tile = z_ref[:, Z_POOL + col0:Z_POOL + col0 + LANES]
        pooled.append(_pool_tile(ext_ref, u_tile, col0, POOL_WINDOWS[2 * tile],
                                 POOL_WINDOWS[2 * tile + 1], pos))
    pooled = jnp.concatenate(pooled, axis=1).astype(BF16)
    y_pool = _dot(pooled, poolw_ref[...]) * pscale_ref[...]
    y_ref[:, 0:POOL_WIDTH] = y_pool.astype(BF16)
    ext_ref[0:POOL_PREV_ROWS, :] = ext_ref[CHUNK:CHUNK + POOL_PREV_ROWS, :]

    vn = _group_rms(z_ref[:, Z_GV:Z_GV + GMLP_WIDTH], gmean_ref[...], gnorm_ref[...]).astype(BF16)
    for tile in range(2):
        col0 = tile * LANES
        vt = vn[:, col0:col0 + LANES]
        w_a = jnp.where(causal, gws_ref[2 * tile], 0.0).astype(BF16)
        w_b = jnp.where(causal, gws_ref[2 * tile + 1], 0.0).astype(BF16)
        mixed = jnp.where(lane < GMLP_GROUP_DIM, _dot(w_a, vt), _dot(w_b, vt))
        gu = z_ref[:, Z_GU + col0:Z_GU + col0 + LANES]
        y_g = gu * (mixed + gbs_ref[:, col0:col0 + LANES])
        y_ref[:, Y_GMLP + col0:Y_GMLP + col0 + LANES] = y_g.astype(BF16)

    ig, lf = _gate_terms(z_ref, bi_ref, bf_ref)
    tri =jnp.where(causal, 1.0, 0.0).astype(BF16)
    lf_hi, lf_mid, lf_lo = _split3(lf)
    b = _dot(tri, lf_hi) + _dot(tri, lf_mid) + _dot(tri, lf_lo)
    m_prev = m_ref[...]
    g = b + m_prev
    r_t = jnp.transpose(ig - b)
    b_last = b[CHUNK - 1:CHUNK, :]
    ones_col = jnp.where(lane == 0, 1.0, 0.0).astype(BF16)
    m_new_row = m_prev
    for h in range(MLSTM_HEADS):
        c0 = h * MLSTM_HEAD_DIM
        q = z_ref[:, Z_Q + c0:Z_Q + c0 + MLSTM_HEAD_DIM].astype(BF16)
        k = z_ref[:, Z_K + c0:Z_K + c0 + MLSTM_HEAD_DIM] * (MLSTM_HEAD_DIM ** -0.5)
        v = z_ref[:, Z_V + c0:Z_V + c0 + MLSTM_HEAD_DIM].astype(BF16)
        o = z_ref[:, Z_O + c0:Z_O + c0 + MLSTM_HEAD_DIM]
        b_col = b[:, h:h + 1]
        dmat = jnp.where(causal, b_col + r_t[h:h + 1, :], -jnp.inf)
        g_col = g[:, h:h + 1]
        m_t = jnp.maximum(g_col, jnp.max(dmat, axis=1, keepdims=True))
        scores = lax.dot_general(q, k.astype(BF16), (((1,), (1,)), ((), ())),
                                 preferred_element_type=F32)
        wts = jnp.exp(dmat - m_t) * scores
        inter = jnp.exp(g_col - m_t)
        cn_h = cn_ref[h]
        q_cn = _dot(q, cn_h.astype(BF16))
        num = inter * q_cn[:, 0:MLSTM_HEAD_DIM] + _dot(wts.astype(BF16), v)
        den = inter * q_cn[:, MLSTM_HEAD_DIM:MLSTM_HEAD_DIM + 1] + jnp.sum(wts, axis=1, keepdims=True)
        hid = num / jnp.maximum(jnp.abs(den), jnp.exp(-m_t))
        hid = _rms(hid, mnorm_ref[:, c0:c0 + MLSTM_HEAD_DIM])
        y_ref[:, POOL_WIDTH + c0:POOL_WIDTH + c0 + MLSTM_HEAD_DIM] = (jax.nn.sigmoid(o) * hid).astype(BF16)
        m_new = m_t[CHUNK - 1:CHUNK, :]
        bl = b_last[:, h:h + 1]
        decay = jnp.exp(bl + m_prev[:, h:h + 1] - m_new)
        w_s = jnp.exp(bl - b_col + ig[:, h:h + 1] - m_new)
        kw = (k * w_s).astype(BF16)
        v_ext = jnp.concatenate([v, ones_col], axis=1)
        cn_ref[h] = decay * cn_h + lax.dot_general(kw, v_ext, (((0,), (0,)), ((), ())),
                                                   preferred_element_type=F32)
        m_new_row = jnp.where(lane[0:1, :] == h, m_new, m_new_row)
    m_ref[...] = m_new_row


def _mixer_prompt(z, consts, batch, seq):
    nc = seq // CHUNK
    hd = MLSTM_HEAD_DIM
    z3 = z.reshape(batch, seq, Z_WIDTH)
    ns = PROMPT_SEQ_PER_STEP
    const_specs = [pl.BlockSpec(a.shape, lambda b, c, nd=a.ndim: (0,) * nd) for a in consts]
    return pl.pallas_call(
        _mixer_prompt_kernel,
        grid=(batch // ns, nc),
        in_specs=[pl.BlockSpec((ns, CHUNK, Z_WIDTH), lambda b, c: (b, c, 0))] + const_specs,
        out_specs=[pl.BlockSpec((ns, CHUNK, D_MODEL), lambda b, c: (b, c, 0)),
                   pl.BlockSpec((ns, MLSTM_HEADS, hd, 2 * hd), lambda b, c: (b, 0, 0, 0)),
                   pl.BlockSpec((ns, 1, LANES), lambda b, c: (b, 0, 0))],
        out_shape=[jax.ShapeDtypeStruct((batch, seq, D_MODEL), BF16),
                   jax.ShapeDtypeStruct((batch, MLSTM_HEADS, hd, 2 * hd), F32),
                   jax.ShapeDtypeStruct((batch, 1, LANES), F32)],
        scratch_shapes=[pltpu.VMEM((ns, POOL_PREV_ROWS + CHUNK, POOL_WIDTH), F32)],
        compiler_params=_params("parallel", "arbitrary"),
        name="mixer_prompt",
    )(z3, *consts)


def _mixer_sample_kernel(z_ref, sp_ref, c_ref, n_ref, m_ref, c_other_layers_ref,
                         poolw_ref, pscale_ref, bi_ref, bf_ref, mnorm_ref, gnorm_ref,
                         gw0_ref, gb0_ref, gmean_ref,
                         y_ref, cn_ref, nn_ref, mn_ref, gv_ref, tk_ref):
    del c_other_layers_ref
    nb = SAMPLE_BLOCK
    hd = MLSTM_HEAD_DIM
    lane = lax.broadcasted_iota(jnp.int32, (nb, LANES), 1)
    seq_id = lax.broadcasted_iota(jnp.int32, (nb, LANES), 0)

    pooled = []
    for tile in range(2):
        col0 = tile * LANES
        u_tile = z_ref[:, Z_POOL + col0:Z_POOL + col0 + LANES]
        w_lo, w_hi = POOL_WINDOWS[2 * tile], POOL_WINDOWS[2 * tile + 1]
        acc = u_tile
        sums = {}
        for shift in range(1, w_hi):
            acc = acc + sp_ref[POOL_STATE - shift, :, col0:col0 + LANES]
            if shift + 1 in (w_lo, w_hi):
                sums[shift + 1] = acc
        pooled.append(jnp.where(lane < POOL_GROUP_DIM, sums[w_lo] / float(w_lo), sums[w_hi] / float(w_hi)) - u_tile)
    pooled = jnp.concatenate(pooled, axis=1).astype(BF16)
    y_ref[:, 0:POOL_WIDTH] = (_dot(pooled, poolw_ref[...]) * pscale_ref[...]).astype(BF16)

    vn = _group_rms(z_ref[:, Z_GV:Z_GV + GMLP_WIDTH], gmean_ref[...], gnorm_ref[...])
    gv_ref[...] = vn
    y_g = z_ref[:, Z_GU:Z_GU + GMLP_WIDTH] * (gw0_ref[...] * vn + gb0_ref[...])
    y_ref[:, Y_GMLP:Y_GMLP + GMLP_WIDTH] = y_g.astype(BF16)

    ig, lf = _gate_terms(z_ref, bi_ref, bf_ref)
    m_prev = m_ref[...]
    g = lf + m_prev
    m_t = jnp.maximum(g, ig)
    inter = jnp.exp(g - m_t)
    e_ig = jnp.exp(ig - m_t)
    floor = jnp.exp(-m_t)
    mn_ref[...] = m_t
    tk_ref[...] = jnp.zeros((LANES, LANES), F32)
    for h in range(MLSTM_HEADS):
        tk_ref[nb * h:nb * (h + 1), :] = z_ref[:, Z_K + h * hd:Z_K + (h + 1) * hd] * (hd ** -0.5)
    k_t = jnp.transpose(tk_ref[...])
    for h in range(MLSTM_HEADS):
        c0 = h * hd
        q_h = z_ref[:, Z_Q + c0:Z_Q + c0 + hd]
        k_h = tk_ref[nb * h:nb * (h + 1), :]
        v_h = z_ref[:, Z_V + c0:Z_V + c0 + hd]
        o_h = z_ref[:, Z_O + c0:Z_O + c0 + hd]
        n_h = n_ref[:, c0:c0 + hd]
        inter_b = jnp.broadcast_to(inter[:, h:h + 1], (nb, hd))
        e_b = jnp.broadcast_to(e_ig[:, h:h + 1], (nb, hd))
        floor_b = jnp.broadcast_to(floor[:, h:h + 1], (nb, hd))
        v_w = e_b * v_h
        q_b = q_h.astype(BF16)
        q_c = jnp.zeros((nb, hd), F32)
        for s in range(nb):
            c_sh = c_ref[s, h]
            q_c = jnp.where(seq_id == s, _dot(q_b, c_sh.astype(BF16)), q_c)
            col = nb * h + s
            cn_ref[s, h] = inter_b[s:s + 1, :] * c_sh + k_t[:, col:col + 1] * v_w[s:s + 1, :]
        wts = e_b * jnp.sum(q_h * k_h, axis=1, keepdims=True)
        num = inter_b * q_c + wts * v_h
        den = inter_b * jnp.sum(q_h * n_h, axis=1, keepdims=True) + wts
        hid = num / jnp.maximum(jnp.abs(den), floor_b)
        hid = _rms(hid, mnorm_ref[:, c0:c0 + hd])
        y_ref[:, POOL_WIDTH + c0:POOL_WIDTH + c0 + hd] = (jax.nn.sigmoid(o_h) * hid).astype(BF16)
        nn_ref[:, c0:c0 + hd] = inter_b * n_h + e_b * k_h


def _mixer_sample(z, sp_t, c_all, layer, c_new_all, n_state, m_pad, consts):
    nseq = z.shape[0]
    nb = SAMPLE_BLOCK
    hd = MLSTM_HEAD_DIM
    const_specs = [pl.BlockSpec(a.shape, lambda j, nd=a.ndim: (0,) * nd) for a in consts]
    c_spec = pl.BlockSpec((None, nb, MLSTM_HEADS, hd, hd), lambda j: (layer, j, 0, 0, 0))
    aliases = {} if c_new_all is None else {5: 1}
    return pl.pallas_call(
        _mixer_sample_kernel,
        grid=(nseq // nb,),
        in_specs=[pl.BlockSpec((nb, Z_WIDTH), lambda j: (j, 0)),
                  pl.BlockSpec((POOL_STATE, nb, POOL_WIDTH), lambda j: (0, j, 0)),
                  c_spec,
                  pl.BlockSpec((nb, MLSTM_WIDTH), lambda j: (j, 0)),
                  pl.BlockSpec((nb, LANES), lambda j: (j, 0)),
                  pl.BlockSpec(memory_space=pl.ANY)] + const_specs,
        out_specs=[pl.BlockSpec((nb, D_MODEL), lambda j: (j, 0)),
                   c_spec,
                   pl.BlockSpec((nb, MLSTM_WIDTH), lambda j: (j, 0)),
                   pl.BlockSpec((nb, LANES), lambda j: (j, 0)),
                   pl.BlockSpec((nb, GMLP_WIDTH), lambda j: (j, 0))],
        out_shape=[jax.ShapeDtypeStruct((nseq, D_MODEL), BF16),
                   jax.ShapeDtypeStruct(c_all.shape, F32),
                   jax.ShapeDtypeStruct((nseq, MLSTM_WIDTH), F32),
                   jax.ShapeDtypeStruct((nseq, LANES), F32),
                   jax.ShapeDtypeStruct((nseq, GMLP_WIDTH), F32)],
        scratch_shapes=[pltpu.VMEM((LANES, LANES), F32)],
        input_output_aliases=aliases,
        compiler_params=_params("parallel"),
        name="mixer_sample",
    )(z, sp_t, c_all, n_state, m_pad, c_all if c_new_all is None else c_new_all, *consts)


def _proj_norm_res_kernel(y_ref, x_ref, w_ref, nw_ref, o_ref):
    o_ref[...] = x_ref[...] + _rms(_dot(y_ref[...], w_ref[...]), nw_ref[...])


def _proj_norm_res(y, x, w, nw, tm):
    t = x.shape[0]
    return pl.pallas_call(
        _proj_norm_res_kernel,
        grid=(t // tm,),
        in_specs=[pl.BlockSpec((tm, D_MODEL), lambda i: (i, 0)),
                  pl.BlockSpec((tm, D_MODEL), lambda i: (i, 0)),
                  pl.BlockSpec((D_MODEL, D_MODEL), lambda i: (0, 0)),
                  pl.BlockSpec((1, D_MODEL), lambda i: (0, 0))],
        out_specs=pl.BlockSpec((tm, D_MODEL), lambda i: (i, 0)),
        out_shape=jax.ShapeDtypeStruct((t, D_MODEL), F32),
        compiler_params=_params("parallel"),
        name="out_proj",
    )(y, x, w, nw)


def _dense_layer_kernel(y_ref, x_ref, p_ref, wout_ref, nmix_ref, npre_ref, npost_ref, nple_ref,
                        wg_ref, wu_ref, wd_ref, pg_ref, pp_ref, o_ref):
    x1 = x_ref[...] + _rms(_dot(y_ref[...], wout_ref[...]), nmix_ref[...])
    h = _rms(x1, npre_ref[...]).astype(BF16)
    y = None
    for f0 in range(0, D_FF, FF_CHUNK):
        fw = min(FF_CHUNK, D_FF - f0)
        gate = _dot(h, wg_ref[:, f0:f0 + fw])
        up = _dot(h, wu_ref[:, f0:f0 + fw])
        act = (gate * jax.nn.sigmoid(gate) * up).astype(BF16)
        part = _dot(act, wd_ref[f0:f0 + fw, :])
        y = part if y is None else y + part
    x2 = x1 + _rms(y, npost_ref[...])
    gate = jax.nn.sigmoid(_dot(_rms(x2, nple_ref[...]).astype(BF16), pg_ref[...]))
    o_ref[...] = x2 + gate * _dot(p_ref[...].astype(BF16), pp_ref[...])


def _dense_layer(y, x, p, w_out, nmix, npre, npost, nple, wg, wu, wd, ple_g, ple_p, tm):
    t = x.shape[0]
    return pl.pallas_call(
        _dense_layer_kernel,
        grid=(t // tm,),
        in_specs=[pl.BlockSpec((tm, D_MODEL), lambda i: (i, 0)),
                  pl.BlockSpec((tm, D_MODEL), lambda i: (i, 0)),
                  pl.BlockSpec((tm, PLE_DIM), lambda i: (i, 0)),
                  _resident((D_MODEL, D_MODEL)),
                  _resident((1, D_MODEL)), _resident((1, D_MODEL)), _resident((1, D_MODEL)), _resident((1, D_MODEL)),
                  _resident((D_MODEL, D_FF)), _resident((D_MODEL, D_FF)), _resident((D_FF, D_MODEL)),
                  _resident((D_MODEL, D_MODEL)), _resident((PLE_DIM, D_MODEL))],
        out_specs=pl.BlockSpec((tm, D_MODEL), lambda i: (i, 0)),
        out_shape=jax.ShapeDtypeStruct((t, D_MODEL), F32),
        compiler_params=_params("parallel"),
        name="dense_layer",
    )(y, x, p, w_out, nmix, npre, npost, nple, wg, wu, wd, ple_g, ple_p)


def _router_gates(h, rw_ref, rb_ref):
    shape = (h.shape[0], LANES)
    lane = lax.broadcasted_iota(jnp.int32, shape, 1)
    lane_f = lane.astype(F32)
    logits = jnp.where(lane < N_EXPERTS, _dot(h, rw_ref[...]) + rb_ref[...], -jnp.inf)
    l1 = jnp.max(logits, axis=-1, keepdims=True)
    i1 = jnp.min(jnp.where(logits == l1, lane_f, float(LANES)), axis=-1, keepdims=True)
    rest = jnp.where(lane_f == i1, -jnp.inf, logits)
    l2 = jnp.max(rest, axis=-1, keepdims=True)
    i2 = jnp.min(jnp.where(rest == l2, lane_f, float(LANES)), axis=-1, keepdims=True)
    e2 = jnp.exp(l2 - l1)
    total = 1.0 + e2
    return jnp.where(lane_f == i1, 1.0 / total, 0.0) + jnp.where(lane_f == i2, e2 / total, 0.0)


def _ffn_moe_kernel(x_ref, npre_ref, npost_ref, rw_ref, rb_ref, wg_ref, wu_ref, wd_ref,
                    o_ref, h_ref, acc_ref, gates_ref):
    e = pl.program_id(1)

    @pl.when(e == 0)
    def _():
        h_ref[...] = _rms(x_ref[...], npre_ref[...]).astype(BF16)
        acc_ref[...] = jnp.zeros(acc_ref.shape, F32)
        gates_ref[...] = _router_gates(h_ref[...], rw_ref, rb_ref)

    h = h_ref[...]
    lane = lax.broadcasted_iota(jnp.int32, gates_ref.shape, 1)
    gate_col = jnp.sum(jnp.where(lane == e, gates_ref[...], 0.0), axis=-1, keepdims=True)
    y = None
    for f0, fw in FF_EXPERT_CHUNKS:
        gate = _dot(h, wg_ref[:, f0:f0 + fw].astype(BF16))
        up = _dot(h, wu_ref[:, f0:f0 + fw].astype(BF16))
        act = (gate * jax.nn.sigmoid(gate) * up).astype(BF16)
        part = _dot(act, wd_ref[f0:f0 + fw, :].astype(BF16))
        y = part if y is None else y + part
    acc_ref[...] += gate_col * y

    @pl.when(e == pl.num_programs(1) - 1)
    def _():
        o_ref[...] = x_ref[...] + _rms(acc_ref[...], npost_ref[...])


def _ffn_moe(x, npre, npost, rw, rb, wg, wu, wd, tm):
    t = x.shape[0]
    return pl.pallas_call(
        _ffn_moe_kernel,
        grid=(t // tm, N_EXPERTS),
        in_specs=[pl.BlockSpec((tm, D_MODEL), lambda i, e: (i, 0)),
                  pl.BlockSpec((1, D_MODEL), lambda i, e: (0, 0)),
                  pl.BlockSpec((1, D_MODEL), lambda i, e: (0, 0)),
                  pl.BlockSpec((D_MODEL, LANES), lambda i, e: (0, 0)),
                  pl.BlockSpec((1, LANES), lambda i, e: (0, 0)),
                  pl.BlockSpec((None, D_MODEL, D_FF_EXPERT), lambda i, e: (e, 0, 0)),
                  pl.BlockSpec((None, D_MODEL, D_FF_EXPERT), lambda i, e: (e, 0, 0)),
                  pl.BlockSpec((None, D_FF_EXPERT, D_MODEL), lambda i, e: (e, 0, 0))],
        out_specs=pl.BlockSpec((tm, D_MODEL), lambda i, e: (i, 0)),
        out_shape=jax.ShapeDtypeStruct((t, D_MODEL), F32),
        scratch_shapes=[pltpu.VMEM((tm, D_MODEL), BF16), pltpu.VMEM((tm, D_MODEL), F32),
                        pltpu.VMEM((tm, LANES), F32)],
        compiler_params=_params("parallel", "arbitrary", vmem_limit=VMEM_LIMIT_EXPERT_WEIGHTS),
        name="ffn_moe",
    )(x, npre, npost, rw, rb, wg, wu, wd)


ROUTE_TILE = 256
ROW_ALIGN = 16
ROUTE_SEG = ROUTE_TILE
ROUTE_PACK = 2 * ROUTE_TILE + N_EXPERTS * ROW_ALIGN
ROUTE_W = D_MODEL + 3 * LANES
ROUTE_BLOCK = 512
ROUTE_SEG_SHORT = 96
ROUTE_SHORT_MAX = ROUTE_SEG_SHORT


def _route_region(n_tokens):
    rows = n_tokens + (n_tokens // ROUTE_TILE) * (ROW_ALIGN - 1) + ROUTE_SEG + ROUTE_BLOCK
    return -(-rows // ROUTE_BLOCK) * ROUTE_BLOCK


def _lane_scalar(row, lane, e):
    return jnp.sum(jnp.where(lane == e, row, 0.0)).astype(jnp.int32)


def _route_kernel(y_ref, x_ref, wout_ref, nmix_ref, npre_ref, rw_ref, rb_ref,
                  x1_ref, slot_ref, stats_ref, srt_hbm,
                  stage_ref, runv_ref, run_ref, short_ref, sem, *, region):
    i = pl.program_id(0)
    last = pl.num_programs(0) - 1
    cur = i % 2

    @pl.when(i == 0)
    def _():
        runv_ref[...] = jnp.zeros(runv_ref.shape, F32)
        stage_ref[:, ROUTE_PACK:, :] = jnp.zeros((2, ROUTE_SEG, ROUTE_W), BF16)
        for e in range(N_EXPERTS):
            run_ref[e] = 0

    x1 = x_ref[...] + _rms(_dot(y_ref[...], wout_ref[...]), nmix_ref[...])
    x1_ref[...] = x1
    h = _rms(x1, npre_ref[...]).astype(BF16)
    gates = _router_gates(h, rw_ref, rb_ref)
    sel = gates > 0.0
    ones = jnp.where(sel, 1.0, 0.0)
    trow = lax.broadcasted_iota(jnp.int32, (ROUTE_TILE, ROUTE_TILE), 0)
    tcol = lax.broadcasted_iota(jnp.int32, (ROUTE_TILE, ROUTE_TILE), 1)
    before = jnp.where(tcol < trow, 1.0, 0.0).astype(BF16)
    rank = _dot(before, ones.astype(BF16))
    cnt = jnp.sum(ones, axis=0, keepdims=True)
    cnt_pad = jnp.floor((cnt + (ROW_ALIGN - 1)) * (1.0 / ROW_ALIGN)) * ROW_ALIGN
    lrow = lax.broadcasted_iota(jnp.int32, (LANES, LANES), 0)
    lcol = lax.broadcasted_iota(jnp.int32, (LANES, LANES), 1)
    lower = jnp.where(lrow < lcol, 1.0, 0.0).astype(BF16)
    off = _dot(jnp.broadcast_to(cnt_pad, (SUBLANES, LANES)).astype(BF16), lower)[0:1, :]
    lane = lax.broadcasted_iota(jnp.int32, (1, LANES), 1)
    src_rows = [_lane_scalar(off, lane, e) for e in range(N_EXPERTS)]
    seg_lens = [_lane_scalar(cnt_pad, lane, e) for e in range(N_EXPERTS)]
    short = (jnp.max(cnt_pad) <= ROUTE_SHORT_MAX).astype(jnp.int32)
    slot_ref[...] = jnp.where(sel, rank, -1.0)
    stats_ref[...] = jnp.zeros(stats_ref.shape, F32)
    stats_ref[0:1, :] = runv_ref[...]
    stats_ref[1:2, :] = cnt
    runv_ref[...] = runv_ref[...] + cnt_pad

    pos = jnp.where(sel, rank + off, -1.0)
    pos_t = jnp.concatenate([jnp.transpose(pos[0:LANES, :]), jnp.transpose(pos[LANES:2 * LANES, :])], axis=1)
    pos_a = jnp.max(pos_t, axis=0, keepdims=True)
    pos_b = jnp.max(jnp.where(pos_t == pos_a, -1.0, pos_t), axis=0, keepdims=True)
    prow = lax.broadcasted_iota(jnp.int32, (ROUTE_PACK, ROUTE_TILE), 0).astype(F32)
    perm = jnp.where((prow == pos_a) | (prow == pos_b), 1.0, 0.0).astype(BF16)
    g_hi, g_mid, g_lo = _split3(gates)
    rows = _dot(perm, jnp.concatenate([h, g_hi, g_mid, g_lo], axis=1))
    stage_ref[cur, 0:ROUTE_PACK, :] = rows.astype(BF16)

    def segment_copy(e, src_row, dst_row, slot, rows=ROUTE_SEG):
        return pltpu.make_async_copy(
            stage_ref.at[slot, pl.ds(pl.multiple_of(src_row, ROW_ALIGN), rows), :],
            srt_hbm.at[pl.ds(pl.multiple_of(dst_row, ROW_ALIGN), rows), :],
            sem.at[e])

    def for_each_segment(is_short, action):
        for rows, flag in ((ROUTE_SEG_SHORT, 1), (ROUTE_SEG, 0)):
            @pl.when(is_short == flag)
            def _():
                for e in range(N_EXPERTS):
                    action(e, rows)

    @pl.when(i > 0)
    def _():
        for_each_segment(short_ref[0], lambda e, rows: segment_copy(e, 0, 0, 1 - cur, rows).wait())

    dst_rows = [e * region + run_ref[e] for e in range(N_EXPERTS)]
    for_each_segment(short, lambda e, rows: segment_copy(e, src_rows[e], dst_rows[e], cur, rows).start())
    for e in range(N_EXPERTS):
        run_ref[e] = run_ref[e] + seg_lens[e]
    short_ref[0] = short

    @pl.when(i == last)
    def _():
        for_each_segment(short, lambda e, rows: segment_copy(e, 0, 0, cur, rows).wait())
        stage_ref[1 - cur, 0:ROUTE_SEG, :] = jnp.zeros((ROUTE_SEG, ROUTE_W), BF16)
        for part in range(ROUTE_BLOCK // ROUTE_SEG):
            for e in range(N_EXPERTS):
                segment_copy(e, 0, e * region + run_ref[e] + part * ROUTE_SEG, 1 - cur).start()
            for e in range(N_EXPERTS):
                segment_copy(e, 0, 0, 1 - cur).wait()


def _route(y, x, w_out, nmix, npre, rw, rb):
    t = x.shape[0]
    nt = t // ROUTE_TILE
    region = _route_region(t)
    return pl.pallas_call(
        functools.partial(_route_kernel, region=region),
        grid=(nt,),
        in_specs=[pl.BlockSpec((ROUTE_TILE, D_MODEL), lambda i: (i, 0)),
                  pl.BlockSpec((ROUTE_TILE, D_MODEL), lambda i: (i, 0)),
                  pl.BlockSpec((D_MODEL, D_MODEL), lambda i: (0, 0)),
                  pl.BlockSpec((1, D_MODEL), lambda i: (0, 0)),
                  pl.BlockSpec((1, D_MODEL), lambda i: (0, 0)),
                  pl.BlockSpec((D_MODEL, LANES), lambda i: (0, 0)),
                  pl.BlockSpec((1, LANES), lambda i: (0, 0))],
        out_specs=[pl.BlockSpec((ROUTE_TILE, D_MODEL), lambda i: (i, 0)),
                   pl.BlockSpec((ROUTE_TILE, LANES), lambda i: (i, 0)),
                   pl.BlockSpec((None, SUBLANES, LANES), lambda i: (i, 0, 0)),
                   pl.BlockSpec(memory_space=pl.ANY)],
        out_shape=[jax.ShapeDtypeStruct((t, D_MODEL), F32),
                   jax.ShapeDtypeStruct((t, LANES), F32),
                   jax.ShapeDtypeStruct((nt, SUBLANES, LANES), F32),
                   jax.ShapeDtypeStruct((N_EXPERTS * region, ROUTE_W), BF16)],
        scratch_shapes=[pltpu.VMEM((2, ROUTE_PACK + ROUTE_SEG, ROUTE_W), BF16),
                        pltpu.VMEM((1, LANES), F32),
                        pltpu.SMEM((N_EXPERTS,), jnp.int32),
                        pltpu.SMEM((1,), jnp.int32),
                        pltpu.SemaphoreType.DMA((N_EXPERTS,))],
        compiler_params=_params("arbitrary"),
        name="moe_route",
    )(y, x, w_out, nmix, npre, rw, rb)


def _experts_kernel(blk_row_ref, blk_e_ref, n_used_ref, srt_ref, wg_ref, wu_ref, wd_ref, yhi_ref, ylo_ref):
    k = pl.program_id(0)

    @pl.when(k < n_used_ref[0])
    def _():
        h = srt_ref[:, 0:D_MODEL]
        gate3 = (srt_ref[:, D_MODEL:D_MODEL + LANES].astype(F32)
                 + srt_ref[:, D_MODEL + LANES:D_MODEL + 2 * LANES].astype(F32)
                 + srt_ref[:, D_MODEL + 2 * LANES:D_MODEL + 3 * LANES].astype(F32))
        lane = lax.broadcasted_iota(jnp.int32, gate3.shape, 1)
        gate_col = jnp.sum(jnp.where(lane == blk_e_ref[k], gate3, 0.0), axis=-1, keepdims=True)
        y = None
        for f0, fw in FF_EXPERT_CHUNKS:
            gate = _dot(h, wg_ref[:, f0:f0 + fw].astype(BF16))
            up = _dot(h, wu_ref[:, f0:f0 + fw].astype(BF16))
            act = (gate * jax.nn.sigmoid(gate) * up).astype(BF16)
            part = _dot(act, wd_ref[f0:f0 + fw, :].astype(BF16))
            y = part if y is None else y + part
        y = gate_col * y
        hi = y.astype(BF16)
        yhi_ref[...] = hi
        ylo_ref[...] = (y - hi.astype(F32)).astype(BF16)


def _experts(srt, blk_row, blk_e, n_used, wg, wu, wd, n_blocks):
    rows = srt.shape[0]
    grid_spec = pltpu.PrefetchScalarGridSpec(
        num_scalar_prefetch=3,
        grid=(n_blocks,),
        in_specs=[pl.BlockSpec((ROUTE_BLOCK, ROUTE_W), lambda k, br, be, nu: (br[k], 0)),
                  pl.BlockSpec((None, D_MODEL, D_FF_EXPERT), lambda k, br, be, nu: (be[k], 0, 0)),
                  pl.BlockSpec((None, D_MODEL, D_FF_EXPERT), lambda k, br, be, nu: (be[k], 0, 0)),
                  pl.BlockSpec((None, D_FF_EXPERT, D_MODEL), lambda k, br, be, nu: (be[k], 0, 0))],
        out_specs=[pl.BlockSpec((ROUTE_BLOCK, D_MODEL), lambda k, br, be, nu: (br[k], 0)),
                   pl.BlockSpec((ROUTE_BLOCK, D_MODEL), lambda k, br, be, nu: (br[k], 0))])
    return pl.pallas_call(
        _experts_kernel,
        grid_spec=grid_spec,
        out_shape=[jax.ShapeDtypeStruct((rows, D_MODEL), BF16), jax.ShapeDtypeStruct((rows, D_MODEL), BF16)],
        compiler_params=_params("arbitrary", vmem_limit=VMEM_LIMIT_EXPERT_WEIGHTS),
        name="moe_experts",
    )(blk_row, blk_e, n_used, srt, wg, wu, wd)


def _combine_kernel(src_row_ref, short_ref, x_ref, slot_ref, shift_ref, p_ref, npost_ref, nple_ref,
                    wg_ref, wp_ref, yhi_hbm, ylo_hbm, o_ref, seg_hi_ref, seg_lo_ref, y_ref, sem):
    i = pl.program_id(0)
    nt = pl.num_programs(0)
    cur = i % 2

    def segment_copies(tile, slot, e, rows):
        src = pl.ds(pl.multiple_of(src_row_ref[tile * N_EXPERTS + e], ROW_ALIGN), rows)
        dst = pl.ds(e * rows, rows)
        return (pltpu.make_async_copy(yhi_hbm.at[src, :], seg_hi_ref.at[slot, dst, :], sem.at[slot, 0, e]),
                pltpu.make_async_copy(ylo_hbm.at[src, :], seg_lo_ref.at[slot, dst, :], sem.at[slot, 1, e]))

    def for_each_segment(tile, slot, action):
        for rows, is_short in ((ROUTE_SEG_SHORT, 1), (ROUTE_SEG, 0)):
            @pl.when(short_ref[tile] == is_short)
            def _():
                for e in range(N_EXPERTS):
                    for c in segment_copies(tile, slot, e, rows):
                        action(c)

    @pl.when(i == 0)
    def _():
        for_each_segment(0, 0, lambda c: c.start())

    @pl.when(i + 1 < nt)
    def _():
        for_each_segment(i + 1, 1 - cur, lambda c: c.start())

    for_each_segment(i, cur, lambda c: c.wait())

    slot = slot_ref[...]
    where = jnp.where(slot >= 0.0, slot + shift_ref[...], -1.0)

    def gather(rows):
        seg_lane = lax.broadcasted_iota(jnp.int32, (ROUTE_TILE, rows), 1).astype(F32)
        perm = jnp.concatenate([jnp.where(where[:, e:e + 1] == seg_lane, 1.0, 0.0).astype(BF16)
                                for e in range(N_EXPERTS)], axis=1)
        k = N_EXPERTS * rows
        y_ref[...] = _dot(perm, seg_hi_ref[cur, 0:k, :]) + _dot(perm, seg_lo_ref[cur, 0:k, :])

    @pl.when(short_ref[i] == 1)
    def _():
        gather(ROUTE_SEG_SHORT)

    @pl.when(short_ref[i] == 0)
    def _():
        gather(ROUTE_SEG)

    x = x_ref[...] + _rms(y_ref[...], npost_ref[...])
    gate = jax.nn.sigmoid(_dot(_rms(x, nple_ref[...]).astype(BF16), wg_ref[...]))
    o_ref[...] = x + gate * _dot(p_ref[...].astype(BF16), wp_ref[...])


def _combine(src_row, short, x, slot, shift, p, npost, nple, wg, wp, yhi, ylo):
    t = x.shape[0]
    grid_spec = pltpu.PrefetchScalarGridSpec(
        num_scalar_prefetch=2,
        grid=(t // ROUTE_TILE,),
        in_specs=[pl.BlockSpec((ROUTE_TILE, D_MODEL), lambda i, *_: (i, 0)),
                  pl.BlockSpec((ROUTE_TILE, LANES), lambda i, *_: (i, 0)),
                  pl.BlockSpec((None, 1, LANES), lambda i, *_: (i, 0, 0)),
                  pl.BlockSpec((ROUTE_TILE, PLE_DIM), lambda i, *_: (i, 0)),
                  pl.BlockSpec((1, D_MODEL), lambda i, *_: (0, 0)),
                  pl.BlockSpec((1, D_MODEL), lambda i, *_: (0, 0)),
                  pl.BlockSpec((D_MODEL, D_MODEL), lambda i, *_: (0, 0)),
                  pl.BlockSpec((PLE_DIM, D_MODEL), lambda i, *_: (0, 0)),
                  pl.BlockSpec(memory_space=pl.ANY),
                  pl.BlockSpec(memory_space=pl.ANY)],
        out_specs=pl.BlockSpec((ROUTE_TILE, D_MODEL), lambda i, *_: (i, 0)),
        scratch_shapes=[pltpu.VMEM((2, N_EXPERTS * ROUTE_SEG, D_MODEL), BF16),
                        pltpu.VMEM((2, N_EXPERTS * ROUTE_SEG, D_MODEL), BF16),
                        pltpu.VMEM((ROUTE_TILE, D_MODEL), F32),
                        pltpu.SemaphoreType.DMA((2, 2, N_EXPERTS))])
    return pl.pallas_call(
        _combine_kernel,
        grid_spec=grid_spec,
        out_shape=jax.ShapeDtypeStruct((t, D_MODEL), F32),
        compiler_params=_params("arbitrary"),
        name="moe_combine_ple",
    )(src_row, short, x, slot, shift, p, npost, nple, wg, wp, yhi, ylo)


def _moe_layer_routed(y, x, p, w_out, nmix, npre, npost, nple, rw, rb, wg, wu, wd, ple_g, ple_p):
    t = x.shape[0]
    nt = t // ROUTE_TILE
    region = _route_region(t)
    x, slot, stats, srt = _route(y, x, w_out, nmix, npre, rw, rb)
    base = stats[:, 0, 0:N_EXPERTS].astype(jnp.int32)
    cnt = stats[:, 1, 0:N_EXPERTS].astype(jnp.int32)
    cnt_pad = (cnt + (ROW_ALIGN - 1)) // ROW_ALIGN * ROW_ALIGN
    total = base[-1] + cnt_pad[-1]
    nblk = (total + (ROUTE_BLOCK - 1)) // ROUTE_BLOCK
    cum = jnp.cumsum(nblk)
    n_used = cum[-1]
    max_rows = 2 * t + nt * N_EXPERTS * (ROW_ALIGN - 1)
    n_blocks = max_rows // ROUTE_BLOCK + N_EXPERTS
    kk = jnp.minimum(jnp.arange(n_blocks, dtype=jnp.int32), n_used - 1)
    blk_e = jnp.sum(kk[:, None] >= cum[None, :], axis=1).astype(jnp.int32)
    blk_row = blk_e * (region // ROUTE_BLOCK) + kk - (cum - nblk)[blk_e]
    yhi, ylo = _experts(srt, blk_row.astype(jnp.int32), blk_e, n_used.reshape(1).astype(jnp.int32),
                        wg, wu, wd, n_blocks)
    short = jnp.all(cnt_pad <= ROUTE_SHORT_MAX, axis=1)
    seg_rows = jnp.where(short, ROUTE_SEG_SHORT, ROUTE_SEG)[:, None]
    start = jnp.maximum(jnp.minimum(base, nblk[None, :] * ROUTE_BLOCK - seg_rows), 0)
    src_row = jnp.arange(N_EXPERTS, dtype=jnp.int32)[None, :] * region + start
    first_used = jnp.argmax(nblk > 0).astype(jnp.int32)
    src_row = jnp.where(cnt > 0, src_row, first_used * region).reshape(-1)
    shift = _pad_lanes((base - start).astype(F32)).reshape(nt, 1, LANES)
    return _combine(src_row.astype(jnp.int32), short.astype(jnp.int32), x, slot, shift, p, npost, nple,
                    ple_g, ple_p, yhi, ylo)


def _ple_kernel(x_ref, p_ref, nw_ref, wg_ref, wp_ref, o_ref):
    x = x_ref[...]
    gate = jax.nn.sigmoid(_dot(_rms(x, nw_ref[...]).astype(BF16), wg_ref[...]))
    o_ref[...] = x + gate * _dot(p_ref[...].astype(BF16), wp_ref[...])


def _ple(x, p, nw, wg, wp, tm):
    t = x.shape[0]
    return pl.pallas_call(
        _ple_kernel,
        grid=(t // tm,),
        in_specs=[pl.BlockSpec((tm, D_MODEL), lambda i: (i, 0)),
                  pl.BlockSpec((tm, PLE_DIM), lambda i: (i, 0)),
                  pl.BlockSpec((1, D_MODEL), lambda i: (0, 0)),
                  pl.BlockSpec((D_MODEL, D_MODEL), lambda i: (0, 0)),
                  pl.BlockSpec((PLE_DIM, D_MODEL), lambda i: (0, 0))],
        out_specs=pl.BlockSpec((tm, D_MODEL), lambda i: (i, 0)),
        out_shape=jax.ShapeDtypeStruct((t, D_MODEL), F32),
        compiler_params=_params("parallel"),
        name="ple",
    )(x, p, nw, wg, wp)


def _pad_lanes(a, width=LANES):
    return jnp.pad(a, [(0, 0)] * (a.ndim - 1) + [(0, width - a.shape[-1])])


def _block_diag(blocks):
    g, d, _ = blocks.shape
    out = jnp.zeros((g * d, g * d), blocks.dtype)
    for i in range(g):
        out = out.at[i * d:(i + 1) * d, i * d:(i + 1) * d].set(blocks[i])
    return out


def _row(a):
    return a.reshape(1, -1).astype(F32)


def kernel(x_prompt, x_sample, state_pool, state_mlstm_C, state_mlstm_n, state_mlstm_m, p_prompt, p_sample,
           norm_mix_pre, norm_mix_post, norm_ffn_pre, norm_ffn_post, norm_ple, w_in, pool_w, pool_scale,
           mlstm_b_i, mlstm_b_f, mlstm_norm_w, gmlp_norm_w, gmlp_ws, gmlp_bs, w_out,
           ffn_w_gate, ffn_w_up, ffn_w_down, moe_router_w, moe_router_b, moe_w_gate, moe_w_up, moe_w_down,
           ple_w_gate, ple_w_proj):
    batch, seq, _ = x_prompt.shape
    nseq = x_sample.shape[0]
    xp = x_prompt.reshape(batch * seq, D_MODEL)
    xs = x_sample.reshape(nseq, D_MODEL)
    gmean = _block_diag(jnp.full((GMLP_GROUPS, GMLP_GROUP_DIM, GMLP_GROUP_DIM), 1.0 / GMLP_GROUP_DIM, BF16))

    w_in_t = jnp.swapaxes(w_in, 1, 2)

    pools_p, cs_p, ns_p, ms_p = [], [], [], []
    pools_s, ns_s, ms_s, gvs_s = [], [], [], []
    c_new_s = None
    for i in range(DEPTH):
        w_out_b = w_out[i].astype(BF16)
        poolw = _block_diag(pool_w[i]).astype(BF16)
        shared = [poolw, _row(pool_scale[i]), _pad_lanes(_row(mlstm_b_i[i])), _pad_lanes(_row(mlstm_b_f[i])),
                  _row(mlstm_norm_w[i]), _row(gmlp_norm_w[i])]
        gbs_full = jnp.repeat(gmlp_bs[i].T, GMLP_GROUP_DIM, axis=1)
        consts_p = shared + [gmlp_ws[i], gbs_full, gmean]
        gw0 = jnp.repeat(gmlp_ws[i][:, 0, 0], GMLP_GROUP_DIM).reshape(1, GMLP_WIDTH)
        consts_s = shared + [gw0, gbs_full[0:1, :], gmean]
        ple_g = ple_w_gate[i].astype(BF16)
        ple_p = ple_w_proj[i].astype(BF16)
        j = i // 2
        if i % 2 == 0:
            ffn_g, ffn_u, ffn_d = (ffn_w_gate[j].astype(BF16), ffn_w_up[j].astype(BF16),
                                   ffn_w_down[j].astype(BF16))
        else:
            rw = _pad_lanes(moe_router_w[j]).astype(BF16)
            rb = _pad_lanes(_row(moe_router_b[j]))
            moe_g, moe_u, moe_d = moe_w_gate[j], moe_w_up[j], moe_w_down[j]

        z = _norm_matmul(xp, _row(norm_mix_pre[i]), w_in_t, i, TM_PROMPT)
        y, cn_new, m_new = _mixer_prompt(z, consts_p, batch, seq)
        pools_p.append(z.reshape(batch, seq, Z_WIDTH)[:, seq - POOL_STATE:, 0:POOL_WIDTH])
        cs_p.append(cn_new[..., 0:MLSTM_HEAD_DIM])
        ns_p.append(cn_new[..., MLSTM_HEAD_DIM])
        ms_p.append(m_new[:, 0, 0:MLSTM_HEADS])
        y = y.reshape(batch * seq, D_MODEL)
        pp = p_prompt[i].reshape(batch * seq, PLE_DIM)
        norms = (_row(norm_mix_post[i]), _row(norm_ffn_pre[i]), _row(norm_ffn_post[i]), _row(norm_ple[i]))
        if i % 2 == 0:
            xp = _dense_layer(y, xp, pp, w_out_b, *norms, ffn_g, ffn_u, ffn_d, ple_g, ple_p, TM_PROMPT)
        else:
            xp = _moe_layer_routed(y, xp, pp, w_out_b, *norms, rw, rb, moe_g, moe_u, moe_d, ple_g, ple_p)

        z = _norm_matmul(xs, _row(norm_mix_pre[i]), w_in_t, i, nseq)
        sp_t = jnp.transpose(state_pool[i], (1, 0, 2))
        y, c_new_s, n_new, m_new, gv = _mixer_sample(z, sp_t, state_mlstm_C, i, c_new_s,
                                                     state_mlstm_n[i].reshape(nseq, MLSTM_WIDTH),
                                                     _pad_lanes(state_mlstm_m[i]), consts_s)
        pools_s.append(jnp.concatenate([state_pool[i][:, 1:], z[:, None, 0:POOL_WIDTH]], axis=1))
        ns_s.append(n_new.reshape(nseq, MLSTM_HEADS, MLSTM_HEAD_DIM))
        ms_s.append(m_new[:, 0:MLSTM_HEADS])
        gvs_s.append(gv[:, None, :])
        ps = p_sample[i].reshape(nseq, PLE_DIM)
        if i % 2 == 0:
            xs = _dense_layer(y, xs, ps, w_out_b, *norms, ffn_g, ffn_u, ffn_d, ple_g, ple_p, nseq)
        else:
            xs = _proj_norm_res(y, xs, w_out_b, norms[0], nseq)
            xs = _ffn_moe(xs, norms[1], norms[2], rw, rb, moe_g, moe_u, moe_d, nseq)
            xs = _ple(xs, ps, norms[3], ple_g, ple_p, nseq)

    return (xp.reshape(batch, seq, D_MODEL), xs.reshape(nseq, 1, D_MODEL),
            jnp.stack(pools_p), jnp.stack(cs_p), jnp.stack(ns_p), jnp.stack(ms_p),
            jnp.stack(pools_s), c_new_s, jnp.stack(ns_s), jnp.stack(ms_s), jnp.stack(gvs_s))
```

```python
import functools

import jax
import jax.numpy as jnp
from jax import lax
from jax.experimental import pallas as pl
from jax.experimental.pallas import tpu as pltpu

F32 = jnp.float32
BF16 = jnp.bfloat16

D_MODEL = 1024
DEPTH = 2
POOL_WIDTH = 256
POOL_WINDOWS = (2, 4, 8, 16)
POOL_GROUP_DIM = 64
POOL_STATE = 15
POOL_PREV_ROWS = 16
MLSTM_WIDTH = 512
MLSTM_HEADS = 4
MLSTM_HEAD_DIM = 128
CHUNK = 128
GMLP_WIDTH = 256
GMLP_GROUPS = 4
GMLP_GROUP_DIM = 64
D_FF = 2816
N_EXPERTS = 8
D_FF_EXPERT = 1408
PLE_DIM = 256
RMS_EPS = 1e-6
PAST_LEN = 16384

LANES = 128
SUBLANES = 8
VMEM_LIMIT = 48 * 1024 * 1024
VMEM_LIMIT_EXPERT_WEIGHTS = 58 * 1024 * 1024

Z_POOL = 0
Z_K = 256
Z_O = 768
Z_GU = 1280
Z_GV = 1536
Z_GATES = 1792
Z_Q = 1920
Z_V = 2432
Z_WIDTH = 2944
Y_GMLP = POOL_WIDTH + MLSTM_WIDTH
Z_CHUNKS = ((0, 512), (512, 512), (1024, 512), (1536, 384), (1920, 512), (2432, 512))
W_IN_IG = 2304
W_IN_WIDTH = 2824
W_IN_MOVES = ((Z_POOL, 0, 256), (Z_Q, 256, 512), (Z_K, 768, 512), (Z_V, 1280, 512), (Z_O, 1792, 512),
              (Z_GU, 2312, 512))

TM_PROMPT = 512
FF_CHUNK = 512
FF_EXPERT_CHUNKS = ((0, 512), (512, 512), (1024, 384))
SAMPLE_BLOCK = 8
PROMPT_SEQ_PER_STEP = 2


def _params(*semantics, vmem_limit=VMEM_LIMIT):
    return pltpu.CompilerParams(dimension_semantics=semantics, vmem_limit_bytes=vmem_limit)


def _rms(x, w):
    return x * lax.rsqrt(jnp.mean(x * x, axis=-1, keepdims=True) + RMS_EPS) * w


def _log_sigmoid(x):
    return jnp.minimum(x, 0.0) - jnp.log1p(jnp.exp(-jnp.abs(x)))


def _dot(a, b):
    return jnp.dot(a, b, preferred_element_type=F32)


def _split3(x):
    hi = x.astype(BF16)
    rest = x - hi.astype(F32)
    mid = rest.astype(BF16)
    lo = (rest - mid.astype(F32)).astype(BF16)
    return hi, mid, lo


def _resident(shape):
    return pl.BlockSpec(shape, lambda i: (0,) * len(shape), pipeline_mode=pl.Buffered(1))


def _norm_matmul_kernel(x_ref, nw_ref, wt_ref, o_ref, qv_ref, h_ref, wz_ref):
    @pl.when(pl.program_id(0) == 0)
    def _():
        for dst, src, rows in W_IN_MOVES:
            wz_ref[dst:dst + rows, :] = wt_ref[src:src + rows, :].astype(BF16)
        gates = wt_ref[W_IN_IG:W_IN_IG + 2 * MLSTM_HEADS, :]
        zeros = jnp.zeros((LANES - 2 * MLSTM_HEADS, D_MODEL), F32)
        wz_ref[Z_GATES:Z_GATES + LANES, :] = jnp.concatenate([gates, zeros], axis=0).astype(BF16)

    h_ref[...] = _rms(x_ref[...], nw_ref[...]).astype(BF16)
    for n0, nw in Z_CHUNKS:
        zc = lax.dot_general(h_ref[...], wz_ref[n0:n0 + nw, :], (((1,), (1,)), ((), ())),
                             preferred_element_type=F32)
        o_ref[:, n0:n0 + nw] = zc
        if n0 >= Z_Q:
            qv_ref[:, n0 - Z_Q:n0 - Z_Q + nw] = zc.astype(BF16)


def _norm_matmul(x, nw, wt_all, layer, tm):
    t = x.shape[0]
    return pl.pallas_call(
        _norm_matmul_kernel,
        grid=(t // tm,),
        in_specs=[pl.BlockSpec((tm, D_MODEL), lambda i: (i, 0)),
                  _resident((1, D_MODEL)),
                  pl.BlockSpec((None, W_IN_WIDTH, D_MODEL), lambda i: (layer, 0, 0),
                               pipeline_mode=pl.Buffered(1))],
        out_specs=[pl.BlockSpec((tm, Z_WIDTH), lambda i: (i, 0)),
                   pl.BlockSpec((tm, Z_WIDTH - Z_Q), lambda i: (i, 0))],
        out_shape=[jax.ShapeDtypeStruct((t, Z_WIDTH), F32),
                   jax.ShapeDtypeStruct((t, Z_WIDTH - Z_Q), BF16)],
        scratch_shapes=[pltpu.VMEM((tm, D_MODEL), BF16), pltpu.VMEM((Z_WIDTH, D_MODEL), BF16)],
        compiler_params=_params("arbitrary"),
        name="norm_in_proj",
    )(x, nw, wt_all)


def _gate_terms(z_ref, bi_ref, bf_ref):
    gates = z_ref[:, Z_GATES:Z_GATES + LANES]
    forget = pltpu.roll(gates, LANES - MLSTM_HEADS, 1)
    return gates + bi_ref[...], _log_sigmoid(forget + bf_ref[...])


def _group_rms(v, gmean, w):
    hi, mid, lo = _split3(v * v)
    ms = _dot(hi, gmean) + _dot(mid, gmean) + _dot(lo, gmean)
    return v * lax.rsqrt(ms + RMS_EPS) * w


def _pool_tile(ext_ref, u_tile, col0, w_lo, w_hi, pos):
    acc = u_tile
    sums = {}
    for shift in range(1, w_hi):
        acc = acc + ext_ref[pl.ds(POOL_PREV_ROWS - shift, CHUNK), col0:col0 + LANES]
        if shift + 1 in (w_lo, w_hi):
            sums[shift + 1] = acc
    cnt_lo = jnp.minimum(w_lo, pos + 1).astype(F32)
    cnt_hi = jnp.minimum(w_hi, pos + 1).astype(F32)
    lane = lax.broadcasted_iota(jnp.int32, (CHUNK, LANES), 1)
    return jnp.where(lane < POOL_GROUP_DIM, sums[w_lo] / cnt_lo, sums[w_hi] / cnt_hi) - u_tile


def _mixer_prompt_kernel(z_ref, qv_ref, *refs):
    consts = refs[:9]
    y_ref, cn_ref, m_ref, ext_ref = refs[9:]

    @pl.when(pl.program_id(1) == 0)
    def _():
        ext_ref[:, 0:POOL_PREV_ROWS, :] = jnp.zeros((PROMPT_SEQ_PER_STEP, POOL_PREV_ROWS, POOL_WIDTH), F32)
        cn_ref[...] = jnp.zeros(cn_ref.shape, F32)
        m_ref[...] = jnp.zeros(m_ref.shape, F32)

    for i in range(PROMPT_SEQ_PER_STEP):
        _mixer_prompt_body(z_ref.at[i], qv_ref.at[i], *consts,
                           y_ref.at[i], cn_ref.at[i], m_ref.at[i], ext_ref.at[i])


def _mixer_prompt_body(z_ref, qv_ref, poolw_ref, pscale_ref, bi_ref, bf_ref, mnorm_ref, gnorm_ref,
                       gws_ref, gbs_ref, gmean_ref,
                       y_ref, cn_ref, m_ref, ext_ref):
    chunk = pl.program_id(1)
    row = lax.broadcasted_iota(jnp.int32, (CHUNK, CHUNK), 0)
    col = lax.broadcasted_iota(jnp.int32, (CHUNK, CHUNK), 1)
    causal = col <= row
    lane = col

    ext_ref[POOL_PREV_ROWS:POOL_PREV_ROWS + CHUNK, :] = z_ref[:, Z_POOL:Z_POOL + POOL_WIDTH]
    pos = chunk * CHUNK + lax.broadcasted_iota(jnp.int32, (CHUNK, 1), 0)
    pooled = []
    for tile in range(2):
        col0 = tile * LANES
        u_tile = z_ref[:, Z_POOL + col0:Z_POOL + col0 + LANES]
        pooled.append(_pool_tile(ext_ref, u_tile, col0, POOL_WINDOWS[2 * tile],
                                 POOL_WINDOWS[2 * tile + 1], pos))
    pooled = jnp.concatenate(pooled, axis=1).astype(BF16)
    y_pool = _dot(pooled, poolw_ref[...]) * pscale_ref[...]
    y_ref[:, 0:POOL_WIDTH] = y_pool.astype(BF16)
    ext_ref[0:POOL_PREV_ROWS, :] = ext_ref[CHUNK:CHUNK + POOL_PREV_ROWS, :]

    vn = _group_rms(z_ref[:, Z_GV:Z_GV + GMLP_WIDTH], gmean_ref[...], gnorm_ref[...]).astype(BF16)
    for tile in range(2):
        col0 = tile * LANES
        vt = vn[:, col0:col0 + LANES]
        w_a = jnp.where(causal, gws_ref[2 * tile], 0.0).astype(BF16)
        w_b = jnp.where(causal, gws_ref[2 * tile + 1], 0.0).astype(BF16)
        mixed = jnp.where(lane < GMLP_GROUP_DIM, _dot(w_a, vt), _dot(w_b, vt))
        gu = z_ref[:, Z_GU + col0:Z_GU + col0 + LANES]
        y_g = gu * (mixed + gbs_ref[:, col0:col0 + LANES])
        y_ref[:, Y_GMLP + col0:Y_GMLP + col0 + LANES] = y_g.astype(BF16)

    ig, lf = _gate_terms(z_ref, bi_ref, bf_ref)
    tri =jnp.where(causal, 1.0, 0.0).astype(BF16)
    lf_hi, lf_mid, lf_lo = _split3(lf)
    b = _dot(tri, lf_hi) + _dot(tri, lf_mid) + _dot(tri, lf_lo)
    m_prev = m_ref[...]
    g = b + m_prev
    r_t = jnp.transpose(ig - b)
    b_last = b[CHUNK - 1:CHUNK, :]
    ones_col = jnp.where(lane == 0, 1.0, 0.0).astype(BF16)
    m_new_row = m_prev
    for h in range(MLSTM_HEADS):
        c0 = h * MLSTM_HEAD_DIM
        q = qv_ref[:, c0:c0 + MLSTM_HEAD_DIM]
        k = z_ref[:, Z_K + c0:Z_K + c0 + MLSTM_HEAD_DIM] * (MLSTM_HEAD_DIM ** -0.5)
        v = qv_ref[:, Z_V - Z_Q + c0:Z_V - Z_Q + c0 + MLSTM_HEAD_DIM]
        o = z_ref[:, Z_O + c0:Z_O + c0 + MLSTM_HEAD_DIM]
        b_col = b[:, h:h + 1]
        dmat = jnp.where(causal, b_col + r_t[h:h + 1, :], -jnp.inf)
        g_col = g[:, h:h + 1]
        m_t = jnp.maximum(g_col, jnp.max(dmat, axis=1, keepdims=True))
        scores = lax.dot_general(q, k.astype(BF16), (((1,), (1,)), ((), ())),
                                 preferred_element_type=F32)
        wts = jnp.exp(dmat - m_t) * scores
        inter = jnp.exp(g_col - m_t)
        cn_h = cn_ref[h]
        q_cn = _dot(q, cn_h.astype(BF16))
        num = inter * q_cn[:, 0:MLSTM_HEAD_DIM] + _dot(wts.astype(BF16), v)
        den = inter * q_cn[:, MLSTM_HEAD_DIM:MLSTM_HEAD_DIM + 1] + jnp.sum(wts, axis=1, keepdims=True)
        hid = num / jnp.maximum(jnp.abs(den), jnp.exp(-m_t))
        hid = _rms(hid, mnorm_ref[:, c0:c0 + MLSTM_HEAD_DIM])
        y_ref[:, POOL_WIDTH + c0:POOL_WIDTH + c0 + MLSTM_HEAD_DIM] = (jax.nn.sigmoid(o) * hid).astype(BF16)
        m_new = m_t[CHUNK - 1:CHUNK, :]
        bl = b_last[:, h:h + 1]
        decay = jnp.exp(bl + m_prev[:, h:h + 1] - m_new)
        w_s = jnp.exp(bl - b_col + ig[:, h:h + 1] - m_new)
        kw = (k * w_s).astype(BF16)
        v_ext = jnp.concatenate([v, ones_col], axis=1)
        cn_ref[h] = decay * cn_h + lax.dot_general(kw, v_ext, (((0,), (0,)), ((), ())),
                                                   preferred_element_type=F32)
        m_new_row = jnp.where(lane[0:1, :] == h, m_new, m_new_row)
    m_ref[...] = m_new_row


def _mixer_prompt(z, qv, consts, batch, seq):
    nc = seq // CHUNK
    hd = MLSTM_HEAD_DIM
    z3 = z.reshape(batch, seq, Z_WIDTH)
    qv3 = qv.reshape(batch, seq, Z_WIDTH - Z_Q)
    ns = PROMPT_SEQ_PER_STEP
    const_specs = [pl.BlockSpec(a.shape, lambda b, c, nd=a.ndim: (0,) * nd) for a in consts]
    return pl.pallas_call(
        _mixer_prompt_kernel,
        grid=(batch // ns, nc),
        in_specs=[pl.BlockSpec((ns, CHUNK, Z_Q), lambda b, c: (b, c, 0)),
                  pl.BlockSpec((ns, CHUNK, Z_WIDTH - Z_Q), lambda b, c: (b, c, 0))] + const_specs,
        out_specs=[pl.BlockSpec((ns, CHUNK, D_MODEL), lambda b, c: (b, c, 0)),
                   pl.BlockSpec((ns, MLSTM_HEADS, hd, 2 * hd), lambda b, c: (b, 0, 0, 0)),
                   pl.BlockSpec((ns, 1, LANES), lambda b, c: (b, 0, 0))],
        out_shape=[jax.ShapeDtypeStruct((batch, seq, D_MODEL), BF16),
                   jax.ShapeDtypeStruct((batch, MLSTM_HEADS, hd, 2 * hd), F32),
                   jax.ShapeDtypeStruct((batch, 1, LANES), F32)],
        scratch_shapes=[pltpu.VMEM((ns, POOL_PREV_ROWS + CHUNK, POOL_WIDTH), F32)],
        compiler_params=_params("parallel", "arbitrary"),
        name="mixer_prompt",
    )(z3, qv3, *consts)


def _mixer_sample_kernel(z_ref, sp_ref, c_ref, n_ref, m_ref, c_other_layers_ref,
                         poolw_ref, pscale_ref, bi_ref, bf_ref, mnorm_ref, gnorm_ref,
                         gw0_ref, gb0_ref, gmean_ref,
                         y_ref, cn_ref, nn_ref, mn_ref, gv_ref, tk_ref):
    del c_other_layers_ref
    nb = SAMPLE_BLOCK
    hd = MLSTM_HEAD_DIM
    lane = lax.broadcasted_iota(jnp.int32, (nb, LANES), 1)
    seq_id = lax.broadcasted_iota(jnp.int32, (nb, LANES), 0)

    pooled = []
    for tile in range(2):
        col0 = tile * LANES
        u_tile = z_ref[:, Z_POOL + col0:Z_POOL + col0 + LANES]
        w_lo, w_hi = POOL_WINDOWS[2 * tile], POOL_WINDOWS[2 * tile + 1]
        acc = u_tile
        sums = {}
        for shift in range(1, w_hi):
            acc = acc + sp_ref[POOL_STATE - shift, :, col0:col0 + LANES]
            if shift + 1 in (w_lo, w_hi):
                sums[shift + 1] = acc
        pooled.append(jnp.where(lane < POOL_GROUP_DIM, sums[w_lo] / float(w_lo), sums[w_hi] / float(w_hi)) - u_tile)
    pooled = jnp.concatenate(pooled, axis=1).astype(BF16)
    y_ref[:, 0:POOL_WIDTH] = (_dot(pooled, poolw_ref[...]) * pscale_ref[...]).astype(BF16)

    vn = _group_rms(z_ref[:, Z_GV:Z_GV + GMLP_WIDTH], gmean_ref[...], gnorm_ref[...])
    gv_ref[...] = vn
    y_g = z_ref[:, Z_GU:Z_GU + GMLP_WIDTH] * (gw0_ref[...] * vn + gb0_ref[...])
    y_ref[:, Y_GMLP:Y_GMLP + GMLP_WIDTH] = y_g.astype(BF16)

    ig, lf = _gate_terms(z_ref, bi_ref, bf_ref)
    m_prev = m_ref[...]
    g = lf + m_prev
    m_t = jnp.maximum(g, ig)
    inter = jnp.exp(g - m_t)
    e_ig = jnp.exp(ig - m_t)
    floor = jnp.exp(-m_t)
    mn_ref[...] = m_t
    tk_ref[...] = jnp.zeros((LANES, LANES), F32)
    for h in range(MLSTM_HEADS):
        tk_ref[nb * h:nb * (h + 1), :] = z_ref[:, Z_K + h * hd:Z_K + (h + 1) * hd] * (hd ** -0.5)
    k_t = jnp.transpose(tk_ref[...])
    for h in range(MLSTM_HEADS):
        c0 = h * hd
        q_h = z_ref[:, Z_Q + c0:Z_Q + c0 + hd]
        k_h = tk_ref[nb * h:nb * (h + 1), :]
        v_h = z_ref[:, Z_V + c0:Z_V + c0 + hd]
        o_h = z_ref[:, Z_O + c0:Z_O + c0 + hd]
        n_h = n_ref[:, c0:c0 + hd]
        inter_b = jnp.broadcast_to(inter[:, h:h + 1], (nb, hd))
        e_b = jnp.broadcast_to(e_ig[:, h:h + 1], (nb, hd))
        floor_b = jnp.broadcast_to(floor[:, h:h + 1], (nb, hd))
        v_w = e_b * v_h
        q_b = q_h.astype(BF16)
        q_c = jnp.zeros((nb, hd), F32)
        for s in range(nb):
            c_sh = c_ref[s, h]
            q_c = jnp.where(seq_id == s, _dot(q_b, c_sh.astype(BF16)), q_c)
            col = nb * h + s
            cn_ref[s, h] = inter_b[s:s + 1, :] * c_sh + k_t[:, col:col + 1] * v_w[s:s + 1, :]
        wts = e_b * jnp.sum(q_h * k_h, axis=1, keepdims=True)
        num = inter_b * q_c + wts * v_h
        den = inter_b * jnp.sum(q_h * n_h, axis=1, keepdims=True) + wts
        hid = num / jnp.maximum(jnp.abs(den), floor_b)
        hid = _rms(hid, mnorm_ref[:, c0:c0 + hd])
        y_ref[:, POOL_WIDTH + c0:POOL_WIDTH + c0 + hd] = (jax.nn.sigmoid(o_h) * hid).astype(BF16)
        nn_ref[:, c0:c0 + hd] = inter_b * n_h + e_b * k_h


def _mixer_sample(z, sp_t, c_all, layer, c_new_all, n_state, m_pad, consts):
    nseq = z.shape[0]
    nb = SAMPLE_BLOCK
    hd = MLSTM_HEAD_DIM
    const_specs = [pl.BlockSpec(a.shape, lambda j, nd=a.ndim: (0,) * nd) for a in consts]
    c_spec = pl.BlockSpec((None, nb, MLSTM_HEADS, hd, hd), lambda j: (layer, j, 0, 0, 0))
    aliases = {} if c_new_all is None else {5: 1}
    return pl.pallas_call(
        _mixer_sample_kernel,
        grid=(nseq // nb,),
        in_specs=[pl.BlockSpec((nb, Z_WIDTH), lambda j: (j, 0)),
                  pl.BlockSpec((POOL_STATE, nb, POOL_WIDTH), lambda j: (0, j, 0)),
                  c_spec,
                  pl.BlockSpec((nb, MLSTM_WIDTH), lambda j: (j, 0)),
                  pl.BlockSpec((nb, LANES), lambda j: (j, 0)),
                  pl.BlockSpec(memory_space=pl.ANY)] + const_specs,
        out_specs=[pl.BlockSpec((nb, D_MODEL), lambda j: (j, 0)),
                   c_spec,
                   pl.BlockSpec((nb, MLSTM_WIDTH), lambda j: (j, 0)),
                   pl.BlockSpec((nb, LANES), lambda j: (j, 0)),
                   pl.BlockSpec((nb, GMLP_WIDTH), lambda j: (j, 0))],
        out_shape=[jax.ShapeDtypeStruct((nseq, D_MODEL), BF16),
                   jax.ShapeDtypeStruct(c_all.shape, F32),
                   jax.ShapeDtypeStruct((nseq, MLSTM_WIDTH), F32),
                   jax.ShapeDtypeStruct((nseq, LANES), F32),
                   jax.ShapeDtypeStruct((nseq, GMLP_WIDTH), F32)],
        scratch_shapes=[pltpu.VMEM((LANES, LANES), F32)],
        input_output_aliases=aliases,
        compiler_params=_params("parallel"),
        name="mixer_sample",
    )(z, sp_t, c_all, n_state, m_pad, c_all if c_new_all is None else c_new_all, *consts)


def _proj_norm_res_kernel(y_ref, x_ref, w_ref, nw_ref, o_ref):
    o_ref[...] = x_ref[...] + _rms(_dot(y_ref[...], w_ref[...]), nw_ref[...])


def _proj_norm_res(y, x, w, nw, tm):
    t = x.shape[0]
    return pl.pallas_call(
        _proj_norm_res_kernel,
        grid=(t // tm,),
        in_specs=[pl.BlockSpec((tm, D_MODEL), lambda i: (i, 0)),
                  pl.BlockSpec((tm, D_MODEL), lambda i: (i, 0)),
                  pl.BlockSpec((D_MODEL, D_MODEL), lambda i: (0, 0)),
                  pl.BlockSpec((1, D_MODEL), lambda i: (0, 0))],
        out_specs=pl.BlockSpec((tm, D_MODEL), lambda i: (i, 0)),
        out_shape=jax.ShapeDtypeStruct((t, D_MODEL), F32),
        compiler_params=_params("parallel"),
        name="out_proj",
    )(y, x, w, nw)


def _dense_layer_kernel(y_ref, x_ref, p_ref, wout_ref, nmix_ref, npre_ref, npost_ref, nple_ref,
                        wg_ref, wu_ref, wd_ref, pg_ref, pp_ref, o_ref):
    x1 = x_ref[...] + _rms(_dot(y_ref[...], wout_ref[...]), nmix_ref[...])
    h = _rms(x1, npre_ref[...]).astype(BF16)
    y = None
    for f0 in range(0, D_FF, FF_CHUNK):
        fw = min(FF_CHUNK, D_FF - f0)
        gate = _dot(h, wg_ref[:, f0:f0 + fw])
        up = _dot(h, wu_ref[:, f0:f0 + fw])
        act = (gate * jax.nn.sigmoid(gate) * up).astype(BF16)
        part = _dot(act, wd_ref[f0:f0 + fw, :])
        y = part if y is None else y + part
    x2 = x1 + _rms(y, npost_ref[...])
    gate = jax.nn.sigmoid(_dot(_rms(x2, nple_ref[...]).astype(BF16), pg_ref[...]))
    o_ref[...] = x2 + gate * _dot(p_ref[...].astype(BF16), pp_ref[...])


def _dense_layer(y, x, p, w_out, nmix, npre, npost, nple, wg, wu, wd, ple_g, ple_p, tm):
    t = x.shape[0]
    return pl.pallas_call(
        _dense_layer_kernel,
        grid=(t // tm,),
        in_specs=[pl.BlockSpec((tm, D_MODEL), lambda i: (i, 0)),
                  pl.BlockSpec((tm, D_MODEL), lambda i: (i, 0)),
                  pl.BlockSpec((tm, PLE_DIM), lambda i: (i, 0)),
                  _resident((D_MODEL, D_MODEL)),
                  _resident((1, D_MODEL)), _resident((1, D_MODEL)), _resident((1, D_MODEL)), _resident((1, D_MODEL)),
                  _resident((D_MODEL, D_FF)), _resident((D_MODEL, D_FF)), _resident((D_FF, D_MODEL)),
                  _resident((D_MODEL, D_MODEL)), _resident((PLE_DIM, D_MODEL))],
        out_specs=pl.BlockSpec((tm, D_MODEL), lambda i: (i, 0)),
        out_shape=jax.ShapeDtypeStruct((t, D_MODEL), F32),
        compiler_params=_params("parallel"),
        name="dense_layer",
    )(y, x, p, w_out, nmix, npre, npost, nple, wg, wu, wd, ple_g, ple_p)


def _router_gates(h, rw_ref, rb_ref):
    shape = (h.shape[0], LANES)
    lane = lax.broadcasted_iota(jnp.int32, shape, 1)
    lane_f = lane.astype(F32)
    logits = jnp.where(lane < N_EXPERTS, _dot(h, rw_ref[...]) + rb_ref[...], -jnp.inf)
    l1 = jnp.max(logits, axis=-1, keepdims=True)
    i1 = jnp.min(jnp.where(logits == l1, lane_f, float(LANES)), axis=-1, keepdims=True)
    rest = jnp.where(lane_f == i1, -jnp.inf, logits)
    l2 = jnp.max(rest, axis=-1, keepdims=True)
    i2 = jnp.min(jnp.where(rest == l2, lane_f, float(LANES)), axis=-1, keepdims=True)
    e2 = jnp.exp(l2 - l1)
    total = 1.0 + e2
    return jnp.where(lane_f == i1, 1.0 / total, 0.0) + jnp.where(lane_f == i2, e2 / total, 0.0)


def _ffn_moe_kernel(x_ref, npre_ref, npost_ref, rw_ref, rb_ref, wg_ref, wu_ref, wd_ref,
                    o_ref, h_ref, acc_ref, gates_ref):
    e = pl.program_id(1)

    @pl.when(e == 0)
    def _():
        h_ref[...] = _rms(x_ref[...], npre_ref[...]).astype(BF16)
        acc_ref[...] = jnp.zeros(acc_ref.shape, F32)
        gates_ref[...] = _router_gates(h_ref[...], rw_ref, rb_ref)

    h = h_ref[...]
    lane = lax.broadcasted_iota(jnp.int32, gates_ref.shape, 1)
    gate_col = jnp.sum(jnp.where(lane == e, gates_ref[...], 0.0), axis=-1, keepdims=True)
    y = None
    for f0, fw in FF_EXPERT_CHUNKS:
        gate = _dot(h, wg_ref[:, f0:f0 + fw].astype(BF16))
        up = _dot(h, wu_ref[:, f0:f0 + fw].astype(BF16))
        act = (gate * jax.nn.sigmoid(gate) * up).astype(BF16)
        part = _dot(act, wd_ref[f0:f0 + fw, :].astype(BF16))
        y = part if y is None else y + part
    acc_ref[...] += gate_col * y

    @pl.when(e == pl.num_programs(1) - 1)
    def _():
        o_ref[...] = x_ref[...] + _rms(acc_ref[...], npost_ref[...])


def _ffn_moe(x, npre, npost, rw, rb, wg, wu, wd, tm):
    t = x.shape[0]
    return pl.pallas_call(
        _ffn_moe_kernel,
        grid=(t // tm, N_EXPERTS),
        in_specs=[pl.BlockSpec((tm, D_MODEL), lambda i, e: (i, 0)),
                  pl.BlockSpec((1, D_MODEL), lambda i, e: (0, 0)),
                  pl.BlockSpec((1, D_MODEL), lambda i, e: (0, 0)),
                  pl.BlockSpec((D_MODEL, LANES), lambda i, e: (0, 0)),
                  pl.BlockSpec((1, LANES), lambda i, e: (0, 0)),
                  pl.BlockSpec((None, D_MODEL, D_FF_EXPERT), lambda i, e: (e, 0, 0)),
                  pl.BlockSpec((None, D_MODEL, D_FF_EXPERT), lambda i, e: (e, 0, 0)),
                  pl.BlockSpec((None, D_FF_EXPERT, D_MODEL), lambda i, e: (e, 0, 0))],
        out_specs=pl.BlockSpec((tm, D_MODEL), lambda i, e: (i, 0)),
        out_shape=jax.ShapeDtypeStruct((t, D_MODEL), F32),
        scratch_shapes=[pltpu.VMEM((tm, D_MODEL), BF16), pltpu.VMEM((tm, D_MODEL), F32),
                        pltpu.VMEM((tm, LANES), F32)],
        compiler_params=_params("parallel", "arbitrary", vmem_limit=VMEM_LIMIT_EXPERT_WEIGHTS),
        name="ffn_moe",
    )(x, npre, npost, rw, rb, wg, wu, wd)


ROUTE_TILE = 256
ROW_ALIGN = 16
ROUTE_SEG = ROUTE_TILE
ROUTE_PACK = 2 * ROUTE_TILE + N_EXPERTS * ROW_ALIGN
ROUTE_W = D_MODEL + 3 * LANES
ROUTE_BLOCK = 512
ROUTE_SEG_SHORT = 96
ROUTE_SHORT_MAX = ROUTE_SEG_SHORT


def _route_region(n_tokens):
    rows = n_tokens + (n_tokens // ROUTE_TILE) * (ROW_ALIGN - 1) + ROUTE_SEG + ROUTE_BLOCK
    return -(-rows // ROUTE_BLOCK) * ROUTE_BLOCK


def _lane_scalar(row, lane, e):
    return jnp.sum(jnp.where(lane == e, row, 0.0)).astype(jnp.int32)


def _route_kernel(y_ref, x_ref, wout_ref, nmix_ref, npre_ref, rw_ref, rb_ref,
                  x1_ref, slot_ref, stats_ref, srt_hbm,
                  stage_ref, runv_ref, run_ref, short_ref, sem, *, region):
    i = pl.program_id(0)
    last = pl.num_programs(0) - 1
    cur = i % 2

    @pl.when(i == 0)
    def _():
        runv_ref[...] = jnp.zeros(runv_ref.shape, F32)
        stage_ref[:, ROUTE_PACK:, :] = jnp.zeros((2, ROUTE_SEG, ROUTE_W), BF16)
        for e in range(N_EXPERTS):
            run_ref[e] = 0

    x1 = x_ref[...] + _rms(_dot(y_ref[...], wout_ref[...]), nmix_ref[...])
    x1_ref[...] = x1
    h = _rms(x1, npre_ref[...]).astype(BF16)
    gates = _router_gates(h, rw_ref, rb_ref)
    sel = gates > 0.0
    ones = jnp.where(sel, 1.0, 0.0)
    trow = lax.broadcasted_iota(jnp.int32, (ROUTE_TILE, ROUTE_TILE), 0)
    tcol = lax.broadcasted_iota(jnp.int32, (ROUTE_TILE, ROUTE_TILE), 1)
    before = jnp.where(tcol < trow, 1.0, 0.0).astype(BF16)
    rank = _dot(before, ones.astype(BF16))
    cnt = jnp.sum(ones, axis=0, keepdims=True)
    cnt_pad = jnp.floor((cnt + (ROW_ALIGN - 1)) * (1.0 / ROW_ALIGN)) * ROW_ALIGN
    lrow = lax.broadcasted_iota(jnp.int32, (LANES, LANES), 0)
    lcol = lax.broadcasted_iota(jnp.int32, (LANES, LANES), 1)
    lower = jnp.where(lrow < lcol, 1.0, 0.0).astype(BF16)
    off = _dot(jnp.broadcast_to(cnt_pad, (SUBLANES, LANES)).astype(BF16), lower)[0:1, :]
    lane = lax.broadcasted_iota(jnp.int32, (1, LANES), 1)
    src_rows = [_lane_scalar(off, lane, e) for e in range(N_EXPERTS)]
    seg_lens = [_lane_scalar(cnt_pad, lane, e) for e in range(N_EXPERTS)]
    short = (jnp.max(cnt_pad) <= ROUTE_SHORT_MAX).astype(jnp.int32)
    slot_ref[...] = jnp.where(sel, rank, -1.0)
    stats_ref[...] = jnp.zeros(stats_ref.shape, F32)
    stats_ref[0:1, :] = runv_ref[...]
    stats_ref[1:2, :] = cnt
    runv_ref[...] = runv_ref[...] + cnt_pad

    pos = jnp.where(sel, rank + off, -1.0)
    pos_t = jnp.concatenate([jnp.transpose(pos[0:LANES, :]), jnp.transpose(pos[LANES:2 * LANES, :])], axis=1)
    pos_a = jnp.max(pos_t, axis=0, keepdims=True)
    pos_b = jnp.max(jnp.where(pos_t == pos_a, -1.0, pos_t), axis=0, keepdims=True)
    prow = lax.broadcasted_iota(jnp.int32, (ROUTE_PACK, ROUTE_TILE), 0).astype(F32)
    perm = jnp.where((prow == pos_a) | (prow == pos_b), 1.0, 0.0).astype(BF16)
    g_hi, g_mid, g_lo = _split3(gates)
    rows = _dot(perm, jnp.concatenate([h, g_hi, g_mid, g_lo], axis=1))
    stage_ref[cur, 0:ROUTE_PACK, :] = rows.astype(BF16)

    def segment_copy(e, src_row, dst_row, slot, rows=ROUTE_SEG):
        return pltpu.make_async_copy(
            stage_ref.at[slot, pl.ds(pl.multiple_of(src_row, ROW_ALIGN), rows), :],
            srt_hbm.at[pl.ds(pl.multiple_of(dst_row, ROW_ALIGN), rows), :],
            sem.at[e])

    def for_each_segment(is_short, action):
        for rows, flag in ((ROUTE_SEG_SHORT, 1), (ROUTE_SEG, 0)):
            @pl.when(is_short == flag)
            def _():
                for e in range(N_EXPERTS):
                    action(e, rows)

    @pl.when(i > 0)
    def _():
        for_each_segment(short_ref[0], lambda e, rows: segment_copy(e, 0, 0, 1 - cur, rows).wait())

    dst_rows = [e * region + run_ref[e] for e in range(N_EXPERTS)]
    for_each_segment(short, lambda e, rows: segment_copy(e, src_rows[e], dst_rows[e], cur, rows).start())
    for e in range(N_EXPERTS):
        run_ref[e] = run_ref[e] + seg_lens[e]
    short_ref[0] = short

    @pl.when(i == last)
    def _():
        for_each_segment(short, lambda e, rows: segment_copy(e, 0, 0, cur, rows).wait())
        stage_ref[1 - cur, 0:ROUTE_SEG, :] = jnp.zeros((ROUTE_SEG, ROUTE_W), BF16)
        for part in range(ROUTE_BLOCK // ROUTE_SEG):
            for e in range(N_EXPERTS):
                segment_copy(e, 0, e * region + run_ref[e] + part * ROUTE_SEG, 1 - cur).start()
            for e in range(N_EXPERTS):
                segment_copy(e, 0, 0, 1 - cur).wait()


def _route(y, x, w_out, nmix, npre, rw, rb):
    t = x.shape[0]
    nt = t // ROUTE_TILE
    region = _route_region(t)
    return pl.pallas_call(
        functools.partial(_route_kernel, region=region),
        grid=(nt,),
        in_specs=[pl.BlockSpec((ROUTE_TILE, D_MODEL), lambda i: (i, 0)),
                  pl.BlockSpec((ROUTE_TILE, D_MODEL), lambda i: (i, 0)),
                  pl.BlockSpec((D_MODEL, D_MODEL), lambda i: (0, 0)),
                  pl.BlockSpec((1, D_MODEL), lambda i: (0, 0)),
                  pl.BlockSpec((1, D_MODEL), lambda i: (0, 0)),
                  pl.BlockSpec((D_MODEL, LANES), lambda i: (0, 0)),
                  pl.BlockSpec((1, LANES), lambda i: (0, 0))],
        out_specs=[pl.BlockSpec((ROUTE_TILE, D_MODEL), lambda i: (i, 0)),
                   pl.BlockSpec((ROUTE_TILE, LANES), lambda i: (i, 0)),
                   pl.BlockSpec((None, SUBLANES, LANES), lambda i: (i, 0, 0)),
                   pl.BlockSpec(memory_space=pl.ANY)],
        out_shape=[jax.ShapeDtypeStruct((t, D_MODEL), F32),
                   jax.ShapeDtypeStruct((t, LANES), F32),
                   jax.ShapeDtypeStruct((nt, SUBLANES, LANES), F32),
                   jax.ShapeDtypeStruct((N_EXPERTS * region, ROUTE_W), BF16)],
        scratch_shapes=[pltpu.VMEM((2, ROUTE_PACK + ROUTE_SEG, ROUTE_W), BF16),
                        pltpu.VMEM((1, LANES), F32),
                        pltpu.SMEM((N_EXPERTS,), jnp.int32),
                        pltpu.SMEM((1,), jnp.int32),
                        pltpu.SemaphoreType.DMA((N_EXPERTS,))],
        compiler_params=_params("arbitrary"),
        name="moe_route",
    )(y, x, w_out, nmix, npre, rw, rb)


def _experts_kernel(blk_row_ref, blk_e_ref, n_used_ref, srt_ref, wg_ref, wu_ref, wd_ref, yhi_ref, ylo_ref):
    k = pl.program_id(0)

    @pl.when(k < n_used_ref[0])
    def _():
        h = srt_ref[:, 0:D_MODEL]
        gate3 = (srt_ref[:, D_MODEL:D_MODEL + LANES].astype(F32)
                 + srt_ref[:, D_MODEL + LANES:D_MODEL + 2 * LANES].astype(F32)
                 + srt_ref[:, D_MODEL + 2 * LANES:D_MODEL + 3 * LANES].astype(F32))
        lane = lax.broadcasted_iota(jnp.int32, gate3.shape, 1)
        gate_col = jnp.sum(jnp.where(lane == blk_e_ref[k], gate3, 0.0), axis=-1, keepdims=True)
        y = None
        for f0, fw in FF_EXPERT_CHUNKS:
            gate = _dot(h, wg_ref[:, f0:f0 + fw].astype(BF16))
            up = _dot(h, wu_ref[:, f0:f0 + fw].astype(BF16))
            act = (gate * jax.nn.sigmoid(gate) * up).astype(BF16)
            part = _dot(act, wd_ref[f0:f0 + fw, :].astype(BF16))
            y = part if y is None else y + part
        y = gate_col * y
        hi = y.astype(BF16)
        yhi_ref[...] = hi
        ylo_ref[...] = (y - hi.astype(F32)).astype(BF16)


def _experts(srt, blk_row, blk_e, n_used, wg, wu, wd, n_blocks):
    rows = srt.shape[0]
    grid_spec = pltpu.PrefetchScalarGridSpec(
        num_scalar_prefetch=3,
        grid=(n_blocks,),
        in_specs=[pl.BlockSpec((ROUTE_BLOCK, ROUTE_W), lambda k, br, be, nu: (br[k], 0)),
                  pl.BlockSpec((None, D_MODEL, D_FF_EXPERT), lambda k, br, be, nu: (be[k], 0, 0)),
                  pl.BlockSpec((None, D_MODEL, D_FF_EXPERT), lambda k, br, be, nu: (be[k], 0, 0)),
                  pl.BlockSpec((None, D_FF_EXPERT, D_MODEL), lambda k, br, be, nu: (be[k], 0, 0))],
        out_specs=[pl.BlockSpec((ROUTE_BLOCK, D_MODEL), lambda k, br, be, nu: (br[k], 0)),
                   pl.BlockSpec((ROUTE_BLOCK, D_MODEL), lambda k, br, be, nu: (br[k], 0))])
    return pl.pallas_call(
        _experts_kernel,
        grid_spec=grid_spec,
        out_shape=[jax.ShapeDtypeStruct((rows, D_MODEL), BF16), jax.ShapeDtypeStruct((rows, D_MODEL), BF16)],
        compiler_params=_params("arbitrary", vmem_limit=VMEM_LIMIT_EXPERT_WEIGHTS),
        name="moe_experts",
    )(blk_row, blk_e, n_used, srt, wg, wu, wd)


def _combine_kernel(src_row_ref, short_ref, x_ref, slot_ref, shift_ref, p_ref, npost_ref, nple_ref,
                    wg_ref, wp_ref, yhi_hbm, ylo_hbm, o_ref, seg_hi_ref, seg_lo_ref, y_ref, sem):
    i = pl.program_id(0)
    nt = pl.num_programs(0)
    cur = i % 2

    def segment_copies(tile, slot, e, rows):
        src = pl.ds(pl.multiple_of(src_row_ref[tile * N_EXPERTS + e], ROW_ALIGN), rows)
        dst = pl.ds(e * rows, rows)
        return (pltpu.make_async_copy(yhi_hbm.at[src, :], seg_hi_ref.at[slot, dst, :], sem.at[slot, 0, e]),
                pltpu.make_async_copy(ylo_hbm.at[src, :], seg_lo_ref.at[slot, dst, :], sem.at[slot, 1, e]))

    def for_each_segment(tile, slot, action):
        for rows, is_short in ((ROUTE_SEG_SHORT, 1), (ROUTE_SEG, 0)):
            @pl.when(short_ref[tile] == is_short)
            def _():
                for e in range(N_EXPERTS):
                    for c in segment_copies(tile, slot, e, rows):
                        action(c)

    @pl.when(i == 0)
    def _():
        for_each_segment(0, 0, lambda c: c.start())

    @pl.when(i + 1 < nt)
    def _():
        for_each_segment(i + 1, 1 - cur, lambda c: c.start())

    for_each_segment(i, cur, lambda c: c.wait())

    slot = slot_ref[...]
    where = jnp.where(slot >= 0.0, slot + shift_ref[...], -1.0)

    def gather(rows):
        seg_lane = lax.broadcasted_iota(jnp.int32, (ROUTE_TILE, rows), 1).astype(F32)
        perm = jnp.concatenate([jnp.where(where[:, e:e + 1] == seg_lane, 1.0, 0.0).astype(BF16)
                                for e in range(N_EXPERTS)], axis=1)
        k = N_EXPERTS * rows
        y_ref[...] = _dot(perm, seg_hi_ref[cur, 0:k, :]) + _dot(perm, seg_lo_ref[cur, 0:k, :])

    @pl.when(short_ref[i] == 1)
    def _():
        gather(ROUTE_SEG_SHORT)

    @pl.when(short_ref[i] == 0)
    def _():
        gather(ROUTE_SEG)

    x = x_ref[...] + _rms(y_ref[...], npost_ref[...])
    gate = jax.nn.sigmoid(_dot(_rms(x, nple_ref[...]).astype(BF16), wg_ref[...]))
    o_ref[...] = x + gate * _dot(p_ref[...].astype(BF16), wp_ref[...])


def _combine(src_row, short, x, slot, shift, p, npost, nple, wg, wp, yhi, ylo):
    t = x.shape[0]
    grid_spec = pltpu.PrefetchScalarGridSpec(
        num_scalar_prefetch=2,
        grid=(t // ROUTE_TILE,),
        in_specs=[pl.BlockSpec((ROUTE_TILE, D_MODEL), lambda i, *_: (i, 0)),
                  pl.BlockSpec((ROUTE_TILE, LANES), lambda i, *_: (i, 0)),
                  pl.BlockSpec((None, 1, LANES), lambda i, *_: (i, 0, 0)),
                  pl.BlockSpec((ROUTE_TILE, PLE_DIM), lambda i, *_: (i, 0)),
                  pl.BlockSpec((1, D_MODEL), lambda i, *_: (0, 0)),
                  pl.BlockSpec((1, D_MODEL), lambda i, *_: (0, 0)),
                  pl.BlockSpec((D_MODEL, D_MODEL), lambda i, *_: (0, 0)),
                  pl.BlockSpec((PLE_DIM, D_MODEL), lambda i, *_: (0, 0)),
                  pl.BlockSpec(memory_space=pl.ANY),
                  pl.BlockSpec(memory_space=pl.ANY)],
        out_specs=pl.BlockSpec((ROUTE_TILE, D_MODEL), lambda i, *_: (i, 0)),
        scratch_shapes=[pltpu.VMEM((2, N_EXPERTS * ROUTE_SEG, D_MODEL), BF16),
                        pltpu.VMEM((2, N_EXPERTS * ROUTE_SEG, D_MODEL), BF16),
                        pltpu.VMEM((ROUTE_TILE, D_MODEL), F32),
                        pltpu.SemaphoreType.DMA((2, 2, N_EXPERTS))])
    return pl.pallas_call(
        _combine_kernel,
        grid_spec=grid_spec,
        out_shape=jax.ShapeDtypeStruct((t, D_MODEL), F32),
        compiler_params=_params("arbitrary"),
        name="moe_combine_ple",
    )(src_row, short, x, slot, shift, p, npost, nple, wg, wp, yhi, ylo)


def _moe_layer_routed(y, x, p, w_out, nmix, npre, npost, nple, rw, rb, wg, wu, wd, ple_g, ple_p):
    t = x.shape[0]
    nt = t // ROUTE_TILE
    region = _route_region(t)
    x, slot, stats, srt = _route(y, x, w_out, nmix, npre, rw, rb)
    base = stats[:, 0, 0:N_EXPERTS].astype(jnp.int32)
    cnt = stats[:, 1, 0:N_EXPERTS].astype(jnp.int32)
    cnt_pad = (cnt + (ROW_ALIGN - 1)) // ROW_ALIGN * ROW_ALIGN
    total = base[-1] + cnt_pad[-1]
    nblk = (total + (ROUTE_BLOCK - 1)) // ROUTE_BLOCK
    cum = jnp.cumsum(nblk)
    n_used = cum[-1]
    max_rows = 2 * t + nt * N_EXPERTS * (ROW_ALIGN - 1)
    n_blocks = max_rows // ROUTE_BLOCK + N_EXPERTS
    kk = jnp.minimum(jnp.arange(n_blocks, dtype=jnp.int32), n_used - 1)
    blk_e = jnp.sum(kk[:, None] >= cum[None, :], axis=1).astype(jnp.int32)
    blk_row = blk_e * (region // ROUTE_BLOCK) + kk - (cum - nblk)[blk_e]
    yhi, ylo = _experts(srt, blk_row.astype(jnp.int32), blk_e, n_used.reshape(1).astype(jnp.int32),
                        wg, wu, wd, n_blocks)
    short = jnp.all(cnt_pad <= ROUTE_SHORT_MAX, axis=1)
    seg_rows = jnp.where(short, ROUTE_SEG_SHORT, ROUTE_SEG)[:, None]
    start = jnp.maximum(jnp.minimum(base, nblk[None, :] * ROUTE_BLOCK - seg_rows), 0)
    src_row = jnp.arange(N_EXPERTS, dtype=jnp.int32)[None, :] * region + start
    first_used = jnp.argmax(nblk > 0).astype(jnp.int32)
    src_row = jnp.where(cnt > 0, src_row, first_used * region).reshape(-1)
    shift = _pad_lanes((base - start).astype(F32)).reshape(nt, 1, LANES)
    return _combine(src_row.astype(jnp.int32), short.astype(jnp.int32), x, slot, shift, p, npost, nple,
                    ple_g, ple_p, yhi, ylo)


def _ple_kernel(x_ref, p_ref, nw_ref, wg_ref, wp_ref, o_ref):
    x = x_ref[...]
    gate = jax.nn.sigmoid(_dot(_rms(x, nw_ref[...]).astype(BF16), wg_ref[...]))
    o_ref[...] = x + gate * _dot(p_ref[...].astype(BF16), wp_ref[...])


def _ple(x, p, nw, wg, wp, tm):
    t = x.shape[0]
    return pl.pallas_call(
        _ple_kernel,
        grid=(t // tm,),
        in_specs=[pl.BlockSpec((tm, D_MODEL), lambda i: (i, 0)),
                  pl.BlockSpec((tm, PLE_DIM), lambda i: (i, 0)),
                  pl.BlockSpec((1, D_MODEL), lambda i: (0, 0)),
                  pl.BlockSpec((D_MODEL, D_MODEL), lambda i: (0, 0)),
                  pl.BlockSpec((PLE_DIM, D_MODEL), lambda i: (0, 0))],
        out_specs=pl.BlockSpec((tm, D_MODEL), lambda i: (i, 0)),
        out_shape=jax.ShapeDtypeStruct((t, D_MODEL), F32),
        compiler_params=_params("parallel"),
        name="ple",
    )(x, p, nw, wg, wp)


def _pad_lanes(a, width=LANES):
    return jnp.pad(a, [(0, 0)] * (a.ndim - 1) + [(0, width - a.shape[-1])])


def _block_diag(blocks):
    g, d, _ = blocks.shape
    out = jnp.zeros((g * d, g * d), blocks.dtype)
    for i in range(g):
        out = out.at[i * d:(i + 1) * d, i * d:(i + 1) * d].set(blocks[i])
    return out


def _row(a):
    return a.reshape(1, -1).astype(F32)


def kernel(x_prompt, x_sample, state_pool, state_mlstm_C, state_mlstm_n, state_mlstm_m, p_prompt, p_sample,
           norm_mix_pre, norm_mix_post, norm_ffn_pre, norm_ffn_post, norm_ple, w_in, pool_w, pool_scale,
           mlstm_b_i, mlstm_b_f, mlstm_norm_w, gmlp_norm_w, gmlp_ws, gmlp_bs, w_out,
           ffn_w_gate, ffn_w_up, ffn_w_down, moe_router_w, moe_router_b, moe_w_gate, moe_w_up, moe_w_down,
           ple_w_gate, ple_w_proj):
    batch, seq, _ = x_prompt.shape
    nseq = x_sample.shape[0]
    xp = x_prompt.reshape(batch * seq, D_MODEL)
    xs = x_sample.reshape(nseq, D_MODEL)
    gmean = _block_diag(jnp.full((GMLP_GROUPS, GMLP_GROUP_DIM, GMLP_GROUP_DIM), 1.0 / GMLP_GROUP_DIM, BF16))

    w_in_t = jnp.swapaxes(w_in, 1, 2)

    pools_p, cs_p, ns_p, ms_p = [], [], [], []
    pools_s, ns_s, ms_s, gvs_s = [], [], [], []
    c_new_s = None
    for i in range(DEPTH):
        w_out_b = w_out[i].astype(BF16)
        poolw = _block_diag(pool_w[i]).astype(BF16)
        shared = [poolw, _row(pool_scale[i]), _pad_lanes(_row(mlstm_b_i[i])), _pad_lanes(_row(mlstm_b_f[i])),
                  _row(mlstm_norm_w[i]), _row(gmlp_norm_w[i])]
        gbs_full = jnp.repeat(gmlp_bs[i].T, GMLP_GROUP_DIM, axis=1)
        consts_p = shared + [gmlp_ws[i], gbs_full, gmean]
        gw0 = jnp.repeat(gmlp_ws[i][:, 0, 0], GMLP_GROUP_DIM).reshape(1, GMLP_WIDTH)
        consts_s = shared + [gw0, gbs_full[0:1, :], gmean]
        ple_g = ple_w_gate[i].astype(BF16)
        ple_p = ple_w_proj[i].astype(BF16)
        j = i // 2
        if i % 2 == 0:
            ffn_g, ffn_u, ffn_d = (ffn_w_gate[j].astype(BF16), ffn_w_up[j].astype(BF16),
                                   ffn_w_down[j].astype(BF16))
        else:
            rw = _pad_lanes(moe_router_w[j]).astype(BF16)
            rb = _pad_lanes(_row(moe_router_b[j]))
            moe_g, moe_u, moe_d = moe_w_gate[j], moe_w_up[j], moe_w_down[j]

        z, qv = _norm_matmul(xp, _row(norm_mix_pre[i]), w_in_t, i, TM_PROMPT)
        y, cn_new, m_new = _mixer_prompt(z, qv, consts_p, batch, seq)
        pools_p.append(z.reshape(batch, seq, Z_WIDTH)[:, seq - POOL_STATE:, 0:POOL_WIDTH])
        cs_p.append(cn_new[..., 0:MLSTM_HEAD_DIM])
        ns_p.append(cn_new[..., MLSTM_HEAD_DIM])
        ms_p.append(m_new[:, 0, 0:MLSTM_HEADS])
        y = y.reshape(batch * seq, D_MODEL)
        pp = p_prompt[i].reshape(batch * seq, PLE_DIM)
        norms = (_row(norm_mix_post[i]), _row(norm_ffn_pre[i]), _row(norm_ffn_post[i]), _row(norm_ple[i]))
        if i % 2 == 0:
            xp = _dense_layer(y, xp, pp, w_out_b, *norms, ffn_g, ffn_u, ffn_d, ple_g, ple_p, TM_PROMPT)
        else:
            xp = _moe_layer_routed(y, xp, pp, w_out_b, *norms, rw, rb, moe_g, moe_u, moe_d, ple_g, ple_p)

        z, _ = _norm_matmul(xs, _row(norm_mix_pre[i]), w_in_t, i, nseq)
        sp_t = jnp.transpose(state_pool[i], (1, 0, 2))
        y, c_new_s, n_new, m_new, gv = _mixer_sample(z, sp_t, state_mlstm_C, i, c_new_s,
                                                     state_mlstm_n[i].reshape(nseq, MLSTM_WIDTH),
                                                     _pad_lanes(state_mlstm_m[i]), consts_s)
        pools_s.append(jnp.concatenate([state_pool[i][:, 1:], z[:, None, 0:POOL_WIDTH]], axis=1))
        ns_s.append(n_new.reshape(nseq, MLSTM_HEADS, MLSTM_HEAD_DIM))
        ms_s.append(m_new[:, 0:MLSTM_HEADS])
        gvs_s.append(gv[:, None, :])
        ps = p_sample[i].reshape(nseq, PLE_DIM)
        if i % 2 == 0:
            xs = _dense_layer(y, xs, ps, w_out_b, *norms, ffn_g, ffn_u, ffn_d, ple_g, ple_p, nseq)
        else:
            xs = _proj_norm_res(y, xs, w_out_b, norms[0], nseq)
            xs = _ffn_moe(xs, norms[1], norms[2], rw, rb, moe_g, moe_u, moe_d, nseq)
            xs = _ple(xs, ps, norms[3], ple_g, ple_p, nseq)

    return (xp.reshape(batch, seq, D_MODEL), xs.reshape(nseq, 1, D_MODEL),
            jnp.stack(pools_p), jnp.stack(cs_p), jnp.stack(ns_p), jnp.stack(ms_p),
            jnp.stack(pools_s), c_new_s, jnp.stack(ns_s), jnp.stack(ms_s), jnp.stack(gvs_s))
```

```python
import functools

import jax
import jax.numpy as jnp
from jax import lax
from jax.experimental import pallas as pl
from jax.experimental.pallas import tpu as pltpu

F32 = jnp.float32
BF16 = jnp.bfloat16

D_MODEL = 1024
DEPTH = 2
POOL_WIDTH = 256
POOL_WINDOWS = (2, 4, 8, 16)
POOL_GROUP_DIM = 64
POOL_STATE = 15
POOL_PREV_ROWS = 16
MLSTM_WIDTH = 512
MLSTM_HEADS = 4
MLSTM_HEAD_DIM = 128
CHUNK = 128
GMLP_WIDTH = 256
GMLP_GROUPS = 4
GMLP_GROUP_DIM = 64
D_FF = 2816
N_EXPERTS = 8
D_FF_EXPERT = 1408
PLE_DIM = 256
RMS_EPS = 1e-6
PAST_LEN = 16384

LANES = 128
SUBLANES = 8
VMEM_LIMIT = 48 * 1024 * 1024
VMEM_LIMIT_EXPERT_WEIGHTS = 58 * 1024 * 1024

Z_POOL = 0
Z_Q = 256
Z_K = 768
Z_V = 1280
Z_O = 1792
Z_GU = 2304
Z_GV = 2560
Z_GATES = 2816
Z_WIDTH = 2944
Y_GMLP = POOL_WIDTH + MLSTM_WIDTH
Z_CHUNK = 512
W_IN_IG = 2304
W_IN_GU = 2312
W_IN_WIDTH = 2824

TM_PROMPT = 512
FF_CHUNK = 512
FF_EXPERT_CHUNKS = ((0, 512), (512, 512), (1024, 384))
SAMPLE_BLOCK = 16
PROMPT_SEQ_PER_STEP = 4


def _params(*semantics, vmem_limit=VMEM_LIMIT):
    return pltpu.CompilerParams(dimension_semantics=semantics, vmem_limit_bytes=vmem_limit)


def _rms(x, w):
    return x * lax.rsqrt(jnp.mean(x * x, axis=-1, keepdims=True) + RMS_EPS) * w


def _log_sigmoid(x):
    return jnp.minimum(x, 0.0) - jnp.log1p(jnp.exp(-jnp.abs(x)))


def _dot(a, b):
    return jnp.dot(a, b, preferred_element_type=F32)


def _split3(x):
    hi = x.astype(BF16)
    rest = x - hi.astype(F32)
    mid = rest.astype(BF16)
    lo = (rest - mid.astype(F32)).astype(BF16)
    return hi, mid, lo


def _resident(shape):
    return pl.BlockSpec(shape, lambda i: (0,) * len(shape), pipeline_mode=pl.Buffered(1))


def _norm_matmul_kernel(x_ref, nw_ref, wt_ref, o_ref, h_ref, wz_ref):
    @pl.when(pl.program_id(0) == 0)
    def _():
        wz_ref[0:Z_GU, :] = wt_ref[0:Z_GU, :].astype(BF16)
        wz_ref[Z_GU:Z_GATES, :] = wt_ref[W_IN_GU:W_IN_GU + 2 * GMLP_WIDTH, :].astype(BF16)
        gates = wt_ref[W_IN_IG:W_IN_IG + 2 * MLSTM_HEADS, :]
        zeros = jnp.zeros((LANES - 2 * MLSTM_HEADS, D_MODEL), F32)
        wz_ref[Z_GATES:Z_WIDTH, :] = jnp.concatenate([gates, zeros], axis=0).astype(BF16)

    h_ref[...] = _rms(x_ref[...], nw_ref[...]).astype(BF16)
    for n0 in range(0, Z_WIDTH, Z_CHUNK):
        nw = min(Z_CHUNK, Z_WIDTH - n0)
        o_ref[:, n0:n0 + nw] = lax.dot_general(h_ref[...], wz_ref[n0:n0 + nw, :],
                                               (((1,), (1,)), ((), ())), preferred_element_type=F32)


def _norm_matmul(x, nw, wt_all, layer, tm):
    t = x.shape[0]
    return pl.pallas_call(
        _norm_matmul_kernel,
        grid=(t // tm,),
        in_specs=[pl.BlockSpec((tm, D_MODEL), lambda i: (i, 0)),
                  _resident((1, D_MODEL)),
                  pl.BlockSpec((None, W_IN_WIDTH, D_MODEL), lambda i: (layer, 0, 0),
                               pipeline_mode=pl.Buffered(1))],
        out_specs=pl.BlockSpec((tm, Z_WIDTH), lambda i: (i, 0)),
        out_shape=jax.ShapeDtypeStruct((t, Z_WIDTH), F32),
        scratch_shapes=[pltpu.VMEM((tm, D_MODEL), BF16), pltpu.VMEM((Z_WIDTH, D_MODEL), BF16)],
        compiler_params=_params("arbitrary"),
        name="norm_in_proj",
    )(x, nw, wt_all)


def _gate_terms(z_ref, bi_ref, bf_ref):
    gates = z_ref[:, Z_GATES:Z_GATES + LANES]
    forget = pltpu.roll(gates, LANES - MLSTM_HEADS, 1)
    return gates + bi_ref[...], _log_sigmoid(forget + bf_ref[...])


def _group_rms(v, gmean, w):
    hi, mid, lo = _split3(v * v)
    ms = _dot(hi, gmean) + _dot(mid, gmean) + _dot(lo, gmean)
    return v * lax.rsqrt(ms + RMS_EPS) * w


def _pool_tile(ext_ref, u_tile, col0, w_lo, w_hi, pos):
    acc = u_tile
    sums = {}
    for shift in range(1, w_hi):
        acc = acc + ext_ref[pl.ds(POOL_PREV_ROWS - shift, CHUNK), col0:col0 + LANES]
        if shift + 1 in (w_lo, w_hi):
            sums[shift + 1] = acc
    cnt_lo = jnp.minimum(w_lo, pos + 1).astype(F32)
    cnt_hi = jnp.minimum(w_hi, pos + 1).astype(F32)
    lane = lax.broadcasted_iota(jnp.int32, (CHUNK, LANES), 1)
    return jnp.where(lane < POOL_GROUP_DIM, sums[w_lo] / cnt_lo, sums[w_hi] / cnt_hi) - u_tile


def _mixer_prompt_kernel(z_ref, *refs):
    consts = refs[:9]
    y_ref, cn_ref, m_ref, ext_ref = refs[9:]

    @pl.when(pl.program_id(1) == 0)
    def _():
        ext_ref[:, 0:POOL_PREV_ROWS, :] = jnp.zeros((PROMPT_SEQ_PER_STEP, POOL_PREV_ROWS, POOL_WIDTH), F32)
        cn_ref[...] = jnp.zeros(cn_ref.shape, F32)
        m_ref[...] = jnp.zeros(m_ref.shape, F32)

    for i in range(PROMPT_SEQ_PER_STEP):
        _mixer_prompt_body(z_ref.at[i], *consts, y_ref.at[i], cn_ref.at[i], m_ref.at[i], ext_ref.at[i])


def _mixer_prompt_body(z_ref, poolw_ref, pscale_ref, bi_ref, bf_ref, mnorm_ref, gnorm_ref,
                       gws_ref, gbs_ref, gmean_ref,
                       y_ref, cn_ref, m_ref, ext_ref):
    chunk = pl.program_id(1)
    row = lax.broadcasted_iota(jnp.int32, (CHUNK, CHUNK), 0)
    col = lax.broadcasted_iota(jnp.int32, (CHUNK, CHUNK), 1)
    causal = col <= row
    lane = col

    ext_ref[POOL_PREV_ROWS:POOL_PREV_ROWS + CHUNK, :] = z_ref[:, Z_POOL:Z_POOL + POOL_WIDTH]
    pos = chunk * CHUNK + lax.broadcasted_iota(jnp.int32, (CHUNK, 1), 0)
    pooled = []
    for tile in range(2):
        col0 = tile * LANES
        u_tile = z_ref[:, Z_POOL + col0:Z_POOL + col0 + LANES]
        pooled.append(_pool_tile(ext_ref, u_tile, col0, POOL_WINDOWS[2 * tile],
                                 POOL_WINDOWS[2 * tile + 1], pos))
    pooled = jnp.concatenate(pooled, axis=1).astype(BF16)
    y_pool = _dot(pooled, poolw_ref[...]) * pscale_ref[...]
    y_ref[:, 0:POOL_WIDTH] = y_pool.astype(BF16)
    ext_ref[0:POOL_PREV_ROWS, :] = ext_ref[CHUNK:CHUNK + POOL_PREV_ROWS, :]

    vn = _group_rms(z_ref[:, Z_GV:Z_GV + GMLP_WIDTH], gmean_ref[...], gnorm_ref[...]).astype(BF16)
    for tile in range(2):
        col0 = tile * LANES
        vt = vn[:, col0:col0 + LANES]
        w_a = jnp.where(causal, gws_ref[2 * tile], 0.0).astype(BF16)
        w_b = jnp.where(causal, gws_ref[2 * tile + 1], 0.0).astype(BF16)
        mixed = jnp.where(lane < GMLP_GROUP_DIM, _dot(w_a, vt), _dot(w_b, vt))
        gu = z_ref[:, Z_GU + col0:Z_GU + col0 + LANES]
        y_g = gu * (mixed + gbs_ref[:, col0:col0 + LANES])
        y_ref[:, Y_GMLP + col0:Y_GMLP + col0 + LANES] = y_g.astype(BF16)

    ig, lf = _gate_terms(z_ref, bi_ref, bf_ref)
    tri =jnp.where(causal, 1.0, 0.0).astype(BF16)
    lf_hi, lf_mid, lf_lo = _split3(lf)
    b = _dot(tri, lf_hi) + _dot(tri, lf_mid) + _dot(tri, lf_lo)
    m_prev = m_ref[...]
    g = b + m_prev
    r_t = jnp.transpose(ig - b)
    b_last = b[CHUNK - 1:CHUNK, :]
    ones_col = jnp.where(lane == 0, 1.0, 0.0).astype(BF16)
    m_new_row = m_prev
    for h in range(MLSTM_HEADS):
        c0 = h * MLSTM_HEAD_DIM
        q = z_ref[:, Z_Q + c0:Z_Q + c0 + MLSTM_HEAD_DIM].astype(BF16)
        k = z_ref[:, Z_K + c0:Z_K + c0 + MLSTM_HEAD_DIM] * (MLSTM_HEAD_DIM ** -0.5)
        v = z_ref[:, Z_V + c0:Z_V + c0 + MLSTM_HEAD_DIM].astype(BF16)
        o = z_ref[:, Z_O + c0:Z_O + c0 + MLSTM_HEAD_DIM]
        b_col = b[:, h:h + 1]
        dmat = jnp.where(causal, b_col + r_t[h:h + 1, :], -jnp.inf)
        g_col = g[:, h:h + 1]
        m_t = jnp.maximum(g_col, jnp.max(dmat, axis=1, keepdims=True))
        scores = lax.dot_general(q, k.astype(BF16), (((1,), (1,)), ((), ())),
                                 preferred_element_type=F32)
        wts = jnp.exp(dmat - m_t) * scores
        inter = jnp.exp(g_col - m_t)
        cn_h = cn_ref[h]
        q_cn = _dot(q, cn_h.astype(BF16))
        num = inter * q_cn[:, 0:MLSTM_HEAD_DIM] + _dot(wts.astype(BF16), v)
        den = inter * q_cn[:, MLSTM_HEAD_DIM:MLSTM_HEAD_DIM + 1] + jnp.sum(wts, axis=1, keepdims=True)
        hid = num / jnp.maximum(jnp.abs(den), jnp.exp(-m_t))
        hid = _rms(hid, mnorm_ref[:, c0:c0 + MLSTM_HEAD_DIM])
        y_ref[:, POOL_WIDTH + c0:POOL_WIDTH + c0 + MLSTM_HEAD_DIM] = (jax.nn.sigmoid(o) * hid).astype(BF16)
        m_new = m_t[CHUNK - 1:CHUNK, :]
        bl = b_last[:, h:h + 1]
        decay = jnp.exp(bl + m_prev[:, h:h + 1] - m_new)
        w_s = jnp.exp(bl - b_col + ig[:, h:h + 1] - m_new)
        kw = (k * w_s).astype(BF16)
        v_ext = jnp.concatenate([v, ones_col], axis=1)
        cn_ref[h] = decay * cn_h + lax.dot_general(kw, v_ext, (((0,), (0,)), ((), ())),
                                                   preferred_element_type=F32)
        m_new_row = jnp.where(lane[0:1, :] == h, m_new, m_new_row)
    m_ref[...] = m_new_row


def _mixer_prompt(z, consts, batch, seq):
    nc = seq // CHUNK
    hd = MLSTM_HEAD_DIM
    z3 = z.reshape(batch, seq, Z_WIDTH)
    ns = PROMPT_SEQ_PER_STEP
    const_specs = [pl.BlockSpec(a.shape, lambda b, c, nd=a.ndim: (0,) * nd) for a in consts]
    return pl.pallas_call(
        _mixer_prompt_kernel,
        grid=(batch // ns, nc),
        in_specs=[pl.BlockSpec((ns, CHUNK, Z_WIDTH), lambda b, c: (b, c, 0))] + const_specs,
        out_specs=[pl.BlockSpec((ns, CHUNK, D_MODEL), lambda b, c: (b, c, 0)),
                   pl.BlockSpec((ns, MLSTM_HEADS, hd, 2 * hd), lambda b, c: (b, 0, 0, 0)),
                   pl.BlockSpec((ns, 1, LANES), lambda b, c: (b, 0, 0))],
        out_shape=[jax.ShapeDtypeStruct((batch, seq, D_MODEL), BF16),
                   jax.ShapeDtypeStruct((batch, MLSTM_HEADS, hd, 2 * hd), F32),
                   jax.ShapeDtypeStruct((batch, 1, LANES), F32)],
        scratch_shapes=[pltpu.VMEM((ns, POOL_PREV_ROWS + CHUNK, POOL_WIDTH), F32)],
        compiler_params=_params("parallel", "arbitrary"),
        name="mixer_prompt",
    )(z3, *consts)


def _mixer_sample_kernel(z_ref, sp_ref, c_ref, n_ref, m_ref, c_other_layers_ref,
                         poolw_ref, pscale_ref, bi_ref, bf_ref, mnorm_ref, gnorm_ref,
                         gw0_ref, gb0_ref, gmean_ref,
                         y_ref, cn_ref, nn_ref, mn_ref, gv_ref, tk_ref):
    del c_other_layers_ref
    nb = SAMPLE_BLOCK
    hd = MLSTM_HEAD_DIM
    lane = lax.broadcasted_iota(jnp.int32, (nb, LANES), 1)
    seq_id = lax.broadcasted_iota(jnp.int32, (nb, LANES), 0)

    pooled = []
    for tile in range(2):
        col0 = tile * LANES
        u_tile = z_ref[:, Z_POOL + col0:Z_POOL + col0 + LANES]
        w_lo, w_hi = POOL_WINDOWS[2 * tile], POOL_WINDOWS[2 * tile + 1]
        acc = u_tile
        sums = {}
        for shift in range(1, w_hi):
            acc = acc + sp_ref[POOL_STATE - shift, :, col0:col0 + LANES]
            if shift + 1 in (w_lo, w_hi):
                sums[shift + 1] = acc
        pooled.append(jnp.where(lane < POOL_GROUP_DIM, sums[w_lo] / float(w_lo), sums[w_hi] / float(w_hi)) - u_tile)
    pooled = jnp.concatenate(pooled, axis=1).astype(BF16)
    y_ref[:, 0:POOL_WIDTH] = (_dot(pooled, poolw_ref[...]) * pscale_ref[...]).astype(BF16)

    vn = _group_rms(z_ref[:, Z_GV:Z_GV + GMLP_WIDTH], gmean_ref[...], gnorm_ref[...])
    gv_ref[...] = vn
    y_g = z_ref[:, Z_GU:Z_GU + GMLP_WIDTH] * (gw0_ref[...] * vn + gb0_ref[...])
    y_ref[:, Y_GMLP:Y_GMLP + GMLP_WIDTH] = y_g.astype(BF16)

    ig, lf = _gate_terms(z_ref, bi_ref, bf_ref)
    m_prev = m_ref[...]
    g = lf + m_prev
    m_t = jnp.maximum(g, ig)
    inter = jnp.exp(g - m_t)
    e_ig = jnp.exp(ig - m_t)
    floor = jnp.exp(-m_t)
    mn_ref[...] = m_t
    tk_ref[...] = jnp.zeros((LANES, LANES), F32)
    for h in range(MLSTM_HEADS):
        tk_ref[nb * h:nb * (h + 1), :] = z_ref[:, Z_K + h * hd:Z_K + (h + 1) * hd] * (hd ** -0.5)
    k_t = jnp.transpose(tk_ref[...])
    for h in range(MLSTM_HEADS):
        c0 = h * hd
        q_h = z_ref[:, Z_Q + c0:Z_Q + c0 + hd]
        k_h = tk_ref[nb * h:nb * (h + 1), :]
        v_h = z_ref[:, Z_V + c0:Z_V + c0 + hd]
        o_h = z_ref[:, Z_O + c0:Z_O + c0 + hd]
        n_h = n_ref[:, c0:c0 + hd]
        inter_b = jnp.broadcast_to(inter[:, h:h + 1], (nb, hd))
        e_b = jnp.broadcast_to(e_ig[:, h:h + 1], (nb, hd))
        floor_b = jnp.broadcast_to(floor[:, h:h + 1], (nb, hd))
        v_w = e_b * v_h
        q_b = q_h.astype(BF16)
        q_c = jnp.zeros((nb, hd), F32)
        for s in range(nb):
            c_sh = c_ref[s, h]
            q_c = jnp.where(seq_id == s, _dot(q_b, c_sh.astype(BF16)), q_c)
            col = nb * h + s
            cn_ref[s, h] = inter_b[s:s + 1, :] * c_sh + k_t[:, col:col + 1] * v_w[s:s + 1, :]
        wts = e_b * jnp.sum(q_h * k_h, axis=1, keepdims=True)
        num = inter_b * q_c + wts * v_h
        den = inter_b * jnp.sum(q_h * n_h, axis=1, keepdims=True) + wts
        hid = num / jnp.maximum(jnp.abs(den), floor_b)
        hid = _rms(hid, mnorm_ref[:, c0:c0 + hd])
        y_ref[:, POOL_WIDTH + c0:POOL_WIDTH + c0 + hd] = (jax.nn.sigmoid(o_h) * hid).astype(BF16)
        nn_ref[:, c0:c0 + hd] = inter_b * n_h + e_b * k_h


def _mixer_sample(z, sp_t, c_all, layer, c_new_all, n_state, m_pad, consts):
    nseq = z.shape[0]
    nb = SAMPLE_BLOCK
    hd = MLSTM_HEAD_DIM
    const_specs = [pl.BlockSpec(a.shape, lambda j, nd=a.ndim: (0,) * nd) for a in consts]
    c_spec = pl.BlockSpec((None, nb, MLSTM_HEADS, hd, hd), lambda j: (layer, j, 0, 0, 0))
    aliases = {} if c_new_all is None else {5: 1}
    return pl.pallas_call(
        _mixer_sample_kernel,
        grid=(nseq // nb,),
        in_specs=[pl.BlockSpec((nb, Z_WIDTH), lambda j: (j, 0)),
                  pl.BlockSpec((POOL_STATE, nb, POOL_WIDTH), lambda j: (0, j, 0)),
                  c_spec,
                  pl.BlockSpec((nb, MLSTM_WIDTH), lambda j: (j, 0)),
                  pl.BlockSpec((nb, LANES), lambda j: (j, 0)),
                  pl.BlockSpec(memory_space=pl.ANY)] + const_specs,
        out_specs=[pl.BlockSpec((nb, D_MODEL), lambda j: (j, 0)),
                   c_spec,
                   pl.BlockSpec((nb, MLSTM_WIDTH), lambda j: (j, 0)),
                   pl.BlockSpec((nb, LANES), lambda j: (j, 0)),
                   pl.BlockSpec((nb, GMLP_WIDTH), lambda j: (j, 0))],
        out_shape=[jax.ShapeDtypeStruct((nseq, D_MODEL), BF16),
                   jax.ShapeDtypeStruct(c_all.shape, F32),
                   jax.ShapeDtypeStruct((nseq, MLSTM_WIDTH), F32),
                   jax.ShapeDtypeStruct((nseq, LANES), F32),
                   jax.ShapeDtypeStruct((nseq, GMLP_WIDTH), F32)],
        scratch_shapes=[pltpu.VMEM((LANES, LANES), F32)],
        input_output_aliases=aliases,
        compiler_params=_params("parallel"),
        name="mixer_sample",
    )(z, sp_t, c_all, n_state, m_pad, c_all if c_new_all is None else c_new_all, *consts)


def _proj_norm_res_kernel(y_ref, x_ref, w_ref, nw_ref, o_ref):
    o_ref[...] = x_ref[...] + _rms(_dot(y_ref[...], w_ref[...]), nw_ref[...])


def _proj_norm_res(y, x, w, nw, tm):
    t = x.shape[0]
    return pl.pallas_call(
        _proj_norm_res_kernel,
        grid=(t // tm,),
        in_specs=[pl.BlockSpec((tm, D_MODEL), lambda i: (i, 0)),
                  pl.BlockSpec((tm, D_MODEL), lambda i: (i, 0)),
                  pl.BlockSpec((D_MODEL, D_MODEL), lambda i: (0, 0)),
                  pl.BlockSpec((1, D_MODEL), lambda i: (0, 0))],
        out_specs=pl.BlockSpec((tm, D_MODEL), lambda i: (i, 0)),
        out_shape=jax.ShapeDtypeStruct((t, D_MODEL), F32),
        compiler_params=_params("parallel"),
        name="out_proj",
    )(y, x, w, nw)


def _dense_layer_kernel(y_ref, x_ref, p_ref, wout_ref, nmix_ref, npre_ref, npost_ref, nple_ref,
                        wg_ref, wu_ref, wd_ref, pg_ref, pp_ref, o_ref):
    x1 = x_ref[...] + _rms(_dot(y_ref[...], wout_ref[...]), nmix_ref[...])
    h = _rms(x1, npre_ref[...]).astype(BF16)
    y = None
    for f0 in range(0, D_FF, FF_CHUNK):
        fw = min(FF_CHUNK, D_FF - f0)
        gate = _dot(h, wg_ref[:, f0:f0 + fw])
        up = _dot(h, wu_ref[:, f0:f0 + fw])
        act = (gate * jax.nn.sigmoid(gate) * up).astype(BF16)
        part = _dot(act, wd_ref[f0:f0 + fw, :])
        y = part if y is None else y + part
    x2 = x1 + _rms(y, npost_ref[...])
    gate = jax.nn.sigmoid(_dot(_rms(x2, nple_ref[...]).astype(BF16), pg_ref[...]))
    o_ref[...] = x2 + gate * _dot(p_ref[...].astype(BF16), pp_ref[...])


def _dense_layer(y, x, p, w_out, nmix, npre, npost, nple, wg, wu, wd, ple_g, ple_p, tm):
    t = x.shape[0]
    return pl.pallas_call(
        _dense_layer_kernel,
        grid=(t // tm,),
        in_specs=[pl.BlockSpec((tm, D_MODEL), lambda i: (i, 0)),
                  pl.BlockSpec((tm, D_MODEL), lambda i: (i, 0)),
                  pl.BlockSpec((tm, PLE_DIM), lambda i: (i, 0)),
                  _resident((D_MODEL, D_MODEL)),
                  _resident((1, D_MODEL)), _resident((1, D_MODEL)), _resident((1, D_MODEL)), _resident((1, D_MODEL)),
                  _resident((D_MODEL, D_FF)), _resident((D_MODEL, D_FF)), _resident((D_FF, D_MODEL)),
                  _resident((D_MODEL, D_MODEL)), _resident((PLE_DIM, D_MODEL))],
        out_specs=pl.BlockSpec((tm, D_MODEL), lambda i: (i, 0)),
        out_shape=jax.ShapeDtypeStruct((t, D_MODEL), F32),
        compiler_params=_params("parallel"),
        name="dense_layer",
    )(y, x, p, w_out, nmix, npre, npost, nple, wg, wu, wd, ple_g, ple_p)


def _router_gates(h, rw_ref, rb_ref):
    shape = (h.shape[0], LANES)
    lane = lax.broadcasted_iota(jnp.int32, shape, 1)
    lane_f = lane.astype(F32)
    logits = jnp.where(lane < N_EXPERTS, _dot(h, rw_ref[...]) + rb_ref[...], -jnp.inf)
    l1 = jnp.max(logits, axis=-1, keepdims=True)
    i1 = jnp.min(jnp.where(logits == l1, lane_f, float(LANES)), axis=-1, keepdims=True)
    rest = jnp.where(lane_f == i1, -jnp.inf, logits)
    l2 = jnp.max(rest, axis=-1, keepdims=True)
    i2 = jnp.min(jnp.where(rest == l2, lane_f, float(LANES)), axis=-1, keepdims=True)
    e2 = jnp.exp(l2 - l1)
    total = 1.0 + e2
    return jnp.where(lane_f == i1, 1.0 / total, 0.0) + jnp.where(lane_f == i2, e2 / total, 0.0)


def _ffn_moe_kernel(x_ref, npre_ref, npost_ref, rw_ref, rb_ref, wg_ref, wu_ref, wd_ref,
                    o_ref, h_ref, acc_ref, gates_ref):
    e = pl.program_id(1)

    @pl.when(e == 0)
    def _():
        h_ref[...] = _rms(x_ref[...], npre_ref[...]).astype(BF16)
        acc_ref[...] = jnp.zeros(acc_ref.shape, F32)
        gates_ref[...] = _router_gates(h_ref[...], rw_ref, rb_ref)

    h = h_ref[...]
    lane = lax.broadcasted_iota(jnp.int32, gates_ref.shape, 1)
    gate_col = jnp.sum(jnp.where(lane == e, gates_ref[...], 0.0), axis=-1, keepdims=True)
    y = None
    for f0, fw in FF_EXPERT_CHUNKS:
        gate = _dot(h, wg_ref[:, f0:f0 + fw].astype(BF16))
        up = _dot(h, wu_ref[:, f0:f0 + fw].astype(BF16))
        act = (gate * jax.nn.sigmoid(gate) * up).astype(BF16)
        part = _dot(act, wd_ref[f0:f0 + fw, :].astype(BF16))
        y = part if y is None else y + part
    acc_ref[...] += gate_col * y

    @pl.when(e == pl.num_programs(1) - 1)
    def _():
        o_ref[...] = x_ref[...] + _rms(acc_ref[...], npost_ref[...])


def _ffn_moe(x, npre, npost, rw, rb, wg, wu, wd, tm):
    t = x.shape[0]
    return pl.pallas_call(
        _ffn_moe_kernel,
        grid=(t // tm, N_EXPERTS),
        in_specs=[pl.BlockSpec((tm, D_MODEL), lambda i, e: (i, 0)),
                  pl.BlockSpec((1, D_MODEL), lambda i, e: (0, 0)),
                  pl.BlockSpec((1, D_MODEL), lambda i, e: (0, 0)),
                  pl.BlockSpec((D_MODEL, LANES), lambda i, e: (0, 0)),
                  pl.BlockSpec((1, LANES), lambda i, e: (0, 0)),
                  pl.BlockSpec((None, D_MODEL, D_FF_EXPERT), lambda i, e: (e, 0, 0)),
                  pl.BlockSpec((None, D_MODEL, D_FF_EXPERT), lambda i, e: (e, 0, 0)),
                  pl.BlockSpec((None, D_FF_EXPERT, D_MODEL), lambda i, e: (e, 0, 0))],
        out_specs=pl.BlockSpec((tm, D_MODEL), lambda i, e: (i, 0)),
        out_shape=jax.ShapeDtypeStruct((t, D_MODEL), F32),
        scratch_shapes=[pltpu.VMEM((tm, D_MODEL), BF16), pltpu.VMEM((tm, D_MODEL), F32),
                        pltpu.VMEM((tm, LANES), F32)],
        compiler_params=_params("parallel", "arbitrary", vmem_limit=VMEM_LIMIT_EXPERT_WEIGHTS),
        name="ffn_moe",
    )(x, npre, npost, rw, rb, wg, wu, wd)


ROUTE_TILE = 256
ROW_ALIGN = 16
ROUTE_SEG = ROUTE_TILE
ROUTE_PACK = 2 * ROUTE_TILE + N_EXPERTS * ROW_ALIGN
ROUTE_W = D_MODEL + 3 * LANES
ROUTE_BLOCK = 512
ROUTE_SEG_SHORT = 96
ROUTE_SHORT_MAX = ROUTE_SEG_SHORT


def _route_region(n_tokens):
    rows = n_tokens + (n_tokens // ROUTE_TILE) * (ROW_ALIGN - 1) + ROUTE_SEG + ROUTE_BLOCK
    return -(-rows // ROUTE_BLOCK) * ROUTE_BLOCK


def _lane_scalar(row, lane, e):
    return jnp.sum(jnp.where(lane == e, row, 0.0)).astype(jnp.int32)


def _route_kernel(y_ref, x_ref, wout_ref, nmix_ref, npre_ref, rw_ref, rb_ref,
                  x1_ref, slot_ref, stats_ref, srt_hbm,
                  stage_ref, runv_ref, run_ref, short_ref, sem, *, region):
    i = pl.program_id(0)
    last = pl.num_programs(0) - 1
    cur = i % 2

    @pl.when(i == 0)
    def _():
        runv_ref[...] = jnp.zeros(runv_ref.shape, F32)
        stage_ref[:, ROUTE_PACK:, :] = jnp.zeros((2, ROUTE_SEG, ROUTE_W), BF16)
        for e in range(N_EXPERTS):
            run_ref[e] = 0

    x1 = x_ref[...] + _rms(_dot(y_ref[...], wout_ref[...]), nmix_ref[...])
    x1_ref[...] = x1
    h = _rms(x1, npre_ref[...]).astype(BF16)
    gates = _router_gates(h, rw_ref, rb_ref)
    sel = gates > 0.0
    ones = jnp.where(sel, 1.0, 0.0)
    trow = lax.broadcasted_iota(jnp.int32, (ROUTE_TILE, ROUTE_TILE), 0)
    tcol = lax.broadcasted_iota(jnp.int32, (ROUTE_TILE, ROUTE_TILE), 1)
    before = jnp.where(tcol < trow, 1.0, 0.0).astype(BF16)
    rank = _dot(before, ones.astype(BF16))
    cnt = jnp.sum(ones, axis=0, keepdims=True)
    cnt_pad = jnp.floor((cnt + (ROW_ALIGN - 1)) * (1.0 / ROW_ALIGN)) * ROW_ALIGN
    lrow = lax.broadcasted_iota(jnp.int32, (LANES, LANES), 0)
    lcol = lax.broadcasted_iota(jnp.int32, (LANES, LANES), 1)
    lower = jnp.where(lrow < lcol, 1.0, 0.0).astype(BF16)
    off = _dot(jnp.broadcast_to(cnt_pad, (SUBLANES, LANES)).astype(BF16), lower)[0:1, :]
    lane = lax.broadcasted_iota(jnp.int32, (1, LANES), 1)
    src_rows = [_lane_scalar(off, lane, e) for e in range(N_EXPERTS)]
    seg_lens = [_lane_scalar(cnt_pad, lane, e) for e in range(N_EXPERTS)]
    short = (jnp.max(cnt_pad) <= ROUTE_SHORT_MAX).astype(jnp.int32)
    slot_ref[...] = jnp.where(sel, rank, -1.0)
    stats_ref[...] = jnp.zeros(stats_ref.shape, F32)
    stats_ref[0:1, :] = runv_ref[...]
    stats_ref[1:2, :] = cnt
    runv_ref[...] = runv_ref[...] + cnt_pad

    pos = jnp.where(sel, rank + off, -1.0)
    pos_t = jnp.concatenate([jnp.transpose(pos[0:LANES, :]), jnp.transpose(pos[LANES:2 * LANES, :])], axis=1)
    pos_a = jnp.max(pos_t, axis=0, keepdims=True)
    pos_b = jnp.max(jnp.where(pos_t == pos_a, -1.0, pos_t), axis=0, keepdims=True)
    prow = lax.broadcasted_iota(jnp.int32, (ROUTE_PACK, ROUTE_TILE), 0).astype(F32)
    perm = jnp.where((prow == pos_a) | (prow == pos_b), 1.0, 0.0).astype(BF16)
    g_hi, g_mid, g_lo = _split3(gates)
    rows = _dot(perm, jnp.concatenate([h, g_hi, g_mid, g_lo], axis=1))
    stage_ref[cur, 0:ROUTE_PACK, :] = rows.astype(BF16)

    def segment_copy(e, src_row, dst_row, slot, rows=ROUTE_SEG):
        return pltpu.make_async_copy(
            stage_ref.at[slot, pl.ds(pl.multiple_of(src_row, ROW_ALIGN), rows), :],
            srt_hbm.at[pl.ds(pl.multiple_of(dst_row, ROW_ALIGN), rows), :],
            sem.at[e])

    def for_each_segment(is_short, action):
        for rows, flag in ((ROUTE_SEG_SHORT, 1), (ROUTE_SEG, 0)):
            @pl.when(is_short == flag)
            def _():
                for e in range(N_EXPERTS):
                    action(e, rows)

    @pl.when(i > 0)
    def _():
        for_each_segment(short_ref[0], lambda e, rows: segment_copy(e, 0, 0, 1 - cur, rows).wait())

    dst_rows = [e * region + run_ref[e] for e in range(N_EXPERTS)]
    for_each_segment(short, lambda e, rows: segment_copy(e, src_rows[e], dst_rows[e], cur, rows).start())
    for e in range(N_EXPERTS):
        run_ref[e] = run_ref[e] + seg_lens[e]
    short_ref[0] = short

    @pl.when(i == last)
    def _():
        for_each_segment(short, lambda e, rows: segment_copy(e, 0, 0, cur, rows).wait())
        stage_ref[1 - cur, 0:ROUTE_SEG, :] = jnp.zeros((ROUTE_SEG, ROUTE_W), BF16)
        for part in range(ROUTE_BLOCK // ROUTE_SEG):
            for e in range(N_EXPERTS):
                segment_copy(e, 0, e * region + run_ref[e] + part * ROUTE_SEG, 1 - cur).start()
            for e in range(N_EXPERTS):
                segment_copy(e, 0, 0, 1 - cur).wait()


def _route(y, x, w_out, nmix, npre, rw, rb):
    t = x.shape[0]
    nt = t // ROUTE_TILE
    region = _route_region(t)
    return pl.pallas_call(
        functools.partial(_route_kernel, region=region),
        grid=(nt,),
        in_specs=[pl.BlockSpec((ROUTE_TILE, D_MODEL), lambda i: (i, 0)),
                  pl.BlockSpec((ROUTE_TILE, D_MODEL), lambda i: (i, 0)),
                  pl.BlockSpec((D_MODEL, D_MODEL), lambda i: (0, 0)),
                  pl.BlockSpec((1, D_MODEL), lambda i: (0, 0)),
                  pl.BlockSpec((1, D_MODEL), lambda i: (0, 0)),
                  pl.BlockSpec((D_MODEL, LANES), lambda i: (0, 0)),
                  pl.BlockSpec((1, LANES), lambda i: (0, 0))],
        out_specs=[pl.BlockSpec((ROUTE_TILE, D_MODEL), lambda i: (i, 0)),
                   pl.BlockSpec((ROUTE_TILE, LANES), lambda i: (i, 0)),
                   pl.BlockSpec((None, SUBLANES, LANES), lambda i: (i, 0, 0)),
                   pl.BlockSpec(memory_space=pl.ANY)],
        out_shape=[jax.ShapeDtypeStruct((t, D_MODEL), F32),
                   jax.ShapeDtypeStruct((t, LANES), F32),
                   jax.ShapeDtypeStruct((nt, SUBLANES, LANES), F32),
                   jax.ShapeDtypeStruct((N_EXPERTS * region, ROUTE_W), BF16)],
        scratch_shapes=[pltpu.VMEM((2, ROUTE_PACK + ROUTE_SEG, ROUTE_W), BF16),
                        pltpu.VMEM((1, LANES), F32),
                        pltpu.SMEM((N_EXPERTS,), jnp.int32),
                        pltpu.SMEM((1,), jnp.int32),
                        pltpu.SemaphoreType.DMA((N_EXPERTS,))],
        compiler_params=_params("arbitrary"),
        name="moe_route",
    )(y, x, w_out, nmix, npre, rw, rb)


def _experts_kernel(blk_row_ref, blk_e_ref, n_used_ref, srt_ref, wg_ref, wu_ref, wd_ref, yhi_ref, ylo_ref):
    k = pl.program_id(0)

    @pl.when(k < n_used_ref[0])
    def _():
        h = srt_ref[:, 0:D_MODEL]
        gate3 = (srt_ref[:, D_MODEL:D_MODEL + LANES].astype(F32)
                 + srt_ref[:, D_MODEL + LANES:D_MODEL + 2 * LANES].astype(F32)
                 + srt_ref[:, D_MODEL + 2 * LANES:D_MODEL + 3 * LANES].astype(F32))
        lane = lax.broadcasted_iota(jnp.int32, gate3.shape, 1)
        gate_col = jnp.sum(jnp.where(lane == blk_e_ref[k], gate3, 0.0), axis=-1, keepdims=True)
        y = None
        for f0, fw in FF_EXPERT_CHUNKS:
            gate = _dot(h, wg_ref[:, f0:f0 + fw].astype(BF16))
            up = _dot(h, wu_ref[:, f0:f0 + fw].astype(BF16))
            act = (gate * jax.nn.sigmoid(gate) * up).astype(BF16)
            part = _dot(act, wd_ref[f0:f0 + fw, :].astype(BF16))
            y = part if y is None else y + part
        y = gate_col * y
        hi = y.astype(BF16)
        yhi_ref[...] = hi
        ylo_ref[...] = (y - hi.astype(F32)).astype(BF16)


def _experts(srt, blk_row, blk_e, n_used, wg, wu, wd, n_blocks):
    rows = srt.shape[0]
    grid_spec = pltpu.PrefetchScalarGridSpec(
        num_scalar_prefetch=3,
        grid=(n_blocks,),
        in_specs=[pl.BlockSpec((ROUTE_BLOCK, ROUTE_W), lambda k, br, be, nu: (br[k], 0)),
                  pl.BlockSpec((None, D_MODEL, D_FF_EXPERT), lambda k, br, be, nu: (be[k], 0, 0)),
                  pl.BlockSpec((None, D_MODEL, D_FF_EXPERT), lambda k, br, be, nu: (be[k], 0, 0)),
                  pl.BlockSpec((None, D_FF_EXPERT, D_MODEL), lambda k, br, be, nu: (be[k], 0, 0))],
        out_specs=[pl.BlockSpec((ROUTE_BLOCK, D_MODEL), lambda k, br, be, nu: (br[k], 0)),
                   pl.BlockSpec((ROUTE_BLOCK, D_MODEL), lambda k, br, be, nu: (br[k], 0))])
    return pl.pallas_call(
        _experts_kernel,
        grid_spec=grid_spec,
        out_shape=[jax.ShapeDtypeStruct((rows, D_MODEL), BF16), jax.ShapeDtypeStruct((rows, D_MODEL), BF16)],
        compiler_params=_params("arbitrary", vmem_limit=VMEM_LIMIT_EXPERT_WEIGHTS),
        name="moe_experts",
    )(blk_row, blk_e, n_used, srt, wg, wu, wd)


def _combine_kernel(src_row_ref, short_ref, x_ref, slot_ref, shift_ref, p_ref, npost_ref, nple_ref,
                    wg_ref, wp_ref, yhi_hbm, ylo_hbm, o_ref, seg_hi_ref, seg_lo_ref, y_ref, sem):
    i = pl.program_id(0)
    nt = pl.num_programs(0)
    cur = i % 2

    def segment_copies(tile, slot, e, rows):
        src = pl.ds(pl.multiple_of(src_row_ref[tile * N_EXPERTS + e], ROW_ALIGN), rows)
        dst = pl.ds(e * rows, rows)
        return (pltpu.make_async_copy(yhi_hbm.at[src, :], seg_hi_ref.at[slot, dst, :], sem.at[slot, 0, e]),
                pltpu.make_async_copy(ylo_hbm.at[src, :], seg_lo_ref.at[slot, dst, :], sem.at[slot, 1, e]))

    def for_each_segment(tile, slot, action):
        for rows, is_short in ((ROUTE_SEG_SHORT, 1), (ROUTE_SEG, 0)):
            @pl.when(short_ref[tile] == is_short)
            def _():
                for e in range(N_EXPERTS):
                    for c in segment_copies(tile, slot, e, rows):
                        action(c)

    @pl.when(i == 0)
    def _():
        for_each_segment(0, 0, lambda c: c.start())

    @pl.when(i + 1 < nt)
    def _():
        for_each_segment(i + 1, 1 - cur, lambda c: c.start())

    for_each_segment(i, cur, lambda c: c.wait())

    slot = slot_ref[...]
    where = jnp.where(slot >= 0.0, slot + shift_ref[...], -1.0)

    def gather(rows):
        seg_lane = lax.broadcasted_iota(jnp.int32, (ROUTE_TILE, rows), 1).astype(F32)
        perm = jnp.concatenate([jnp.where(where[:, e:e + 1] == seg_lane, 1.0, 0.0).astype(BF16)
                                for e in range(N_EXPERTS)], axis=1)
        k = N_EXPERTS * rows
        y_ref[...] = _dot(perm, seg_hi_ref[cur, 0:k, :]) + _dot(perm, seg_lo_ref[cur, 0:k, :])

    @pl.when(short_ref[i] == 1)
    def _():
        gather(ROUTE_SEG_SHORT)

    @pl.when(short_ref[i] == 0)
    def _():
        gather(ROUTE_SEG)

    x = x_ref[...] + _rms(y_ref[...], npost_ref[...])
    gate = jax.nn.sigmoid(_dot(_rms(x, nple_ref[...]).astype(BF16), wg_ref[...]))
    o_ref[...] = x + gate * _dot(p_ref[...].astype(BF16), wp_ref[...])


def _combine(src_row, short, x, slot, shift, p, npost, nple, wg, wp, yhi, ylo):
    t = x.shape[0]
    grid_spec = pltpu.PrefetchScalarGridSpec(
        num_scalar_prefetch=2,
        grid=(t // ROUTE_TILE,),
        in_specs=[pl.BlockSpec((ROUTE_TILE, D_MODEL), lambda i, *_: (i, 0)),
                  pl.BlockSpec((ROUTE_TILE, LANES), lambda i, *_: (i, 0)),
                  pl.BlockSpec((None, 1, LANES), lambda i, *_: (i, 0, 0)),
                  pl.BlockSpec((ROUTE_TILE, PLE_DIM), lambda i, *_: (i, 0)),
                  pl.BlockSpec((1, D_MODEL), lambda i, *_: (0, 0)),
                  pl.BlockSpec((1, D_MODEL), lambda i, *_: (0, 0)),
                  pl.BlockSpec((D_MODEL, D_MODEL), lambda i, *_: (0, 0)),
                  pl.BlockSpec((PLE_DIM, D_MODEL), lambda i, *_: (0, 0)),
                  pl.BlockSpec(memory_space=pl.ANY),
                  pl.BlockSpec(memory_space=pl.ANY)],
        out_specs=pl.BlockSpec((ROUTE_TILE, D_MODEL), lambda i, *_: (i, 0)),
        scratch_shapes=[pltpu.VMEM((2, N_EXPERTS * ROUTE_SEG, D_MODEL), BF16),
                        pltpu.VMEM((2, N_EXPERTS * ROUTE_SEG, D_MODEL), BF16),
                        pltpu.VMEM((ROUTE_TILE, D_MODEL), F32),
                        pltpu.SemaphoreType.DMA((2, 2, N_EXPERTS))])
    return pl.pallas_call(
        _combine_kernel,
        grid_spec=grid_spec,
        out_shape=jax.ShapeDtypeStruct((t, D_MODEL), F32),
        compiler_params=_params("arbitrary"),
        name="moe_combine_ple",
    )(src_row, short, x, slot, shift, p, npost, nple, wg, wp, yhi, ylo)


def _moe_layer_routed(y, x, p, w_out, nmix, npre, npost, nple, rw, rb, wg, wu, wd, ple_g, ple_p):
    t = x.shape[0]
    nt = t // ROUTE_TILE
    region = _route_region(t)
    x, slot, stats, srt = _route(y, x, w_out, nmix, npre, rw, rb)
    base = stats[:, 0, 0:N_EXPERTS].astype(jnp.int32)
    cnt = stats[:, 1, 0:N_EXPERTS].astype(jnp.int32)
    cnt_pad = (cnt + (ROW_ALIGN - 1)) // ROW_ALIGN * ROW_ALIGN
    total = base[-1] + cnt_pad[-1]
    nblk = (total + (ROUTE_BLOCK - 1)) // ROUTE_BLOCK
    cum = jnp.cumsum(nblk)
    n_used = cum[-1]
    max_rows = 2 * t + nt * N_EXPERTS * (ROW_ALIGN - 1)
    n_blocks = max_rows // ROUTE_BLOCK + N_EXPERTS
    kk = jnp.minimum(jnp.arange(n_blocks, dtype=jnp.int32), n_used - 1)
    blk_e = jnp.sum(kk[:, None] >= cum[None, :], axis=1).astype(jnp.int32)
    blk_row = blk_e * (region // ROUTE_BLOCK) + kk - (cum - nblk)[blk_e]
    yhi, ylo = _experts(srt, blk_row.astype(jnp.int32), blk_e, n_used.reshape(1).astype(jnp.int32),
                        wg, wu, wd, n_blocks)
    short = jnp.all(cnt_pad <= ROUTE_SHORT_MAX, axis=1)
    seg_rows = jnp.where(short, ROUTE_SEG_SHORT, ROUTE_SEG)[:, None]
    start = jnp.maximum(jnp.minimum(base, nblk[None, :] * ROUTE_BLOCK - seg_rows), 0)
    src_row = jnp.arange(N_EXPERTS, dtype=jnp.int32)[None, :] * region + start
    first_used = jnp.argmax(nblk > 0).astype(jnp.int32)
    src_row = jnp.where(cnt > 0, src_row, first_used * region).reshape(-1)
    shift = _pad_lanes((base - start).astype(F32)).reshape(nt, 1, LANES)
    return _combine(src_row.astype(jnp.int32), short.astype(jnp.int32), x, slot, shift, p, npost, nple,
                    ple_g, ple_p, yhi, ylo)


def _ple_kernel(x_ref, p_ref, nw_ref, wg_ref, wp_ref, o_ref):
    x = x_ref[...]
    gate = jax.nn.sigmoid(_dot(_rms(x, nw_ref[...]).astype(BF16), wg_ref[...]))
    o_ref[...] = x + gate * _dot(p_ref[...].astype(BF16), wp_ref[...])


def _ple(x, p, nw, wg, wp, tm):
    t = x.shape[0]
    return pl.pallas_call(
        _ple_kernel,
        grid=(t // tm,),
        in_specs=[pl.BlockSpec((tm, D_MODEL), lambda i: (i, 0)),
                  pl.BlockSpec((tm, PLE_DIM), lambda i: (i, 0)),
                  pl.BlockSpec((1, D_MODEL), lambda i: (0, 0)),
                  pl.BlockSpec((D_MODEL, D_MODEL), lambda i: (0, 0)),
                  pl.BlockSpec((PLE_DIM, D_MODEL), lambda i: (0, 0))],
        out_specs=pl.BlockSpec((tm, D_MODEL), lambda i: (i, 0)),
        out_shape=jax.ShapeDtypeStruct((t, D_MODEL), F32),
        compiler_params=_params("parallel"),
        name="ple",
    )(x, p, nw, wg, wp)


def _pad_lanes(a, width=LANES):
    return jnp.pad(a, [(0, 0)] * (a.ndim - 1) + [(0, width - a.shape[-1])])


def _block_diag(blocks):
    g, d, _ = blocks.shape
    out = jnp.zeros((g * d, g * d), blocks.dtype)
    for i in range(g):
        out = out.at[i * d:(i + 1) * d, i * d:(i + 1) * d].set(blocks[i])
    return out


def _row(a):
    return a.reshape(1, -1).astype(F32)


def kernel(x_prompt, x_sample, state_pool, state_mlstm_C, state_mlstm_n, state_mlstm_m, p_prompt, p_sample,
           norm_mix_pre, norm_mix_post, norm_ffn_pre, norm_ffn_post, norm_ple, w_in, pool_w, pool_scale,
           mlstm_b_i, mlstm_b_f, mlstm_norm_w, gmlp_norm_w, gmlp_ws, gmlp_bs, w_out,
           ffn_w_gate, ffn_w_up, ffn_w_down, moe_router_w, moe_router_b, moe_w_gate, moe_w_up, moe_w_down,
           ple_w_gate, ple_w_proj):
    batch, seq, _ = x_prompt.shape
    nseq = x_sample.shape[0]
    xp = x_prompt.reshape(batch * seq, D_MODEL)
    xs = x_sample.reshape(nseq, D_MODEL)
    gmean = _block_diag(jnp.full((GMLP_GROUPS, GMLP_GROUP_DIM, GMLP_GROUP_DIM), 1.0 / GMLP_GROUP_DIM, BF16))

    w_in_t = jnp.swapaxes(w_in, 1, 2)

    pools_p, cs_p, ns_p, ms_p = [], [], [], []
    pools_s, ns_s, ms_s, gvs_s = [], [], [], []
    c_new_s = None
    for i in range(DEPTH):
        w_out_b = w_out[i].astype(BF16)
        poolw = _block_diag(pool_w[i]).astype(BF16)
        shared = [poolw, _row(pool_scale[i]), _pad_lanes(_row(mlstm_b_i[i])), _pad_lanes(_row(mlstm_b_f[i])),
                  _row(mlstm_norm_w[i]), _row(gmlp_norm_w[i])]
        gbs_full = jnp.repeat(gmlp_bs[i].T, GMLP_GROUP_DIM, axis=1)
        consts_p = shared + [gmlp_ws[i], gbs_full, gmean]
        gw0 = jnp.repeat(gmlp_ws[i][:, 0, 0], GMLP_GROUP_DIM).reshape(1, GMLP_WIDTH)
        consts_s = shared + [gw0, gbs_full[0:1, :], gmean]
        ple_g = ple_w_gate[i].astype(BF16)
        ple_p = ple_w_proj[i].astype(BF16)
        j = i // 2
        if i % 2 == 0:
            ffn_g, ffn_u, ffn_d = (ffn_w_gate[j].astype(BF16), ffn_w_up[j].astype(BF16),
                                   ffn_w_down[j].astype(BF16))
        else:
            rw = _pad_lanes(moe_router_w[j]).astype(BF16)
            rb = _pad_lanes(_row(moe_router_b[j]))
            moe_g, moe_u, moe_d = moe_w_gate[j], moe_w_up[j], moe_w_down[j]

        z = _norm_matmul(xp, _row(norm_mix_pre[i]), w_in_t, i, TM_PROMPT)
        y, cn_new, m_new = _mixer_prompt(z, consts_p, batch, seq)
        pools_p.append(z.reshape(batch, seq, Z_WIDTH)[:, seq - POOL_STATE:, 0:POOL_WIDTH])
        cs_p.append(cn_new[..., 0:MLSTM_HEAD_DIM])
        ns_p.append(cn_new[..., MLSTM_HEAD_DIM])
        ms_p.append(m_new[:, 0, 0:MLSTM_HEADS])
        y = y.reshape(batch * seq, D_MODEL)
        pp = p_prompt[i].reshape(batch * seq, PLE_DIM)
        norms = (_row(norm_mix_post[i]), _row(norm_ffn_pre[i]), _row(norm_ffn_post[i]), _row(norm_ple[i]))
        if i % 2 == 0:
            xp = _dense_layer(y, xp, pp, w_out_b, *norms, ffn_g, ffn_u, ffn_d, ple_g, ple_p, TM_PROMPT)
        else:
            xp = _moe_layer_routed(y, xp, pp, w_out_b, *norms, rw, rb, moe_g, moe_u, moe_d, ple_g, ple_p)

        z = _norm_matmul(xs, _row(norm_mix_pre[i]), w_in_t, i, nseq)
        sp_t = jnp.transpose(state_pool[i], (1, 0, 2))
        y, c_new_s, n_new, m_new, gv = _mixer_sample(z, sp_t, state_mlstm_C, i, c_new_s,
                                                     state_mlstm_n[i].reshape(nseq, MLSTM_WIDTH),
                                                     _pad_lanes(state_mlstm_m[i]), consts_s)
        pools_s.append(jnp.concatenate([state_pool[i][:, 1:], z[:, None, 0:POOL_WIDTH]], axis=1))
        ns_s.append(n_new.reshape(nseq, MLSTM_HEADS, MLSTM_HEAD_DIM))
        ms_s.append(m_new[:, 0:MLSTM_HEADS])
        gvs_s.append(gv[:, None, :])
        ps = p_sample[i].reshape(nseq, PLE_DIM)
        if i % 2 == 0:
            xs = _dense_layer(y, xs, ps, w_out_b, *norms, ffn_g, ffn_u, ffn_d, ple_g, ple_p, nseq)
        else:
            xs = _proj_norm_res(y, xs, w_out_b, norms[0], nseq)
            xs = _ffn_moe(xs, norms[1], norms[2], rw, rb, moe_g, moe_u, moe_d, nseq)
            xs = _ple(xs, ps, norms[3], ple_g, ple_p, nseq)

    return (xp.reshape(batch, seq, D_MODEL), xs.reshape(nseq, 1, D_MODEL),
            jnp.stack(pools_p), jnp.stack(cs_p), jnp.stack(ns_p), jnp.stack(ms_p),
            jnp.stack(pools_s), c_new_s, jnp.stack(ns_s), jnp.stack(ms_s), jnp.stack(gvs_s))
```

```python
import functools

import jax
import jax.numpy as jnp
from jax import lax
from jax.experimental import pallas as pl
from jax.experimental.pallas import tpu as pltpu

F32 = jnp.float32
BF16 = jnp.bfloat16

D_MODEL = 1024
DEPTH = 2
POOL_WIDTH = 256
POOL_WINDOWS = (2, 4, 8, 16)
POOL_GROUP_DIM = 64
POOL_STATE = 15
POOL_PREV_ROWS = 16
MLSTM_WIDTH = 512
MLSTM_HEADS = 4
MLSTM_HEAD_DIM = 128
CHUNK = 128
GMLP_WIDTH = 256
GMLP_GROUPS = 4
GMLP_GROUP_DIM = 64
D_FF = 2816
N_EXPERTS = 8
D_FF_EXPERT = 1408
PLE_DIM = 256
RMS_EPS = 1e-6
PAST_LEN = 16384

LANES = 128
SUBLANES = 8
VMEM_LIMIT = 48 * 1024 * 1024
VMEM_LIMIT_EXPERT_WEIGHTS = 58 * 1024 * 1024

Z_POOL = 0
Z_Q = 256
Z_K = 768
Z_V = 1280
Z_O = 1792
Z_GU = 2304
Z_GV = 2560
Z_GATES = 2816
Z_WIDTH = 2944
Y_GMLP = POOL_WIDTH + MLSTM_WIDTH
Z_CHUNK = 512
W_IN_IG = 2304
W_IN_GU = 2312
W_IN_WIDTH = 2824

TM_PROMPT = 512
FF_CHUNK = 512
FF_EXPERT_CHUNKS = ((0, 512), (512, 512), (1024, 384))
SAMPLE_BLOCK = 16
PROMPT_SEQ_PER_STEP = 2


def _params(*semantics, vmem_limit=VMEM_LIMIT):
    return pltpu.CompilerParams(dimension_semantics=semantics, vmem_limit_bytes=vmem_limit)


def _rms(x, w):
    return x * lax.rsqrt(jnp.mean(x * x, axis=-1, keepdims=True) + RMS_EPS) * w


def _log_sigmoid(x):
    return jnp.minimum(x, 0.0) - jnp.log1p(jnp.exp(-jnp.abs(x)))


def _dot(a, b):
    return jnp.dot(a, b, preferred_element_type=F32)


def _split3(x):
    hi = x.astype(BF16)
    rest = x - hi.astype(F32)
    mid = rest.astype(BF16)
    lo = (rest - mid.astype(F32)).astype(BF16)
    return hi, mid, lo


def _resident(shape):
    return pl.BlockSpec(shape, lambda i: (0,) * len(shape), pipeline_mode=pl.Buffered(1))


def _norm_matmul_kernel(x_ref, nw_ref, wt_ref, o_ref, h_ref, wz_ref):
    @pl.when(pl.program_id(0) == 0)
    def _():
        wz_ref[0:Z_GU, :] = wt_ref[0:Z_GU, :].astype(BF16)
        wz_ref[Z_GU:Z_GATES, :] = wt_ref[W_IN_GU:W_IN_GU + 2 * GMLP_WIDTH, :].astype(BF16)
        gates = wt_ref[W_IN_IG:W_IN_IG + 2 * MLSTM_HEADS, :]
        zeros = jnp.zeros((LANES - 2 * MLSTM_HEADS, D_MODEL), F32)
        wz_ref[Z_GATES:Z_WIDTH, :] = jnp.concatenate([gates, zeros], axis=0).astype(BF16)

    h_ref[...] = _rms(x_ref[...], nw_ref[...]).astype(BF16)
    for n0 in range(0, Z_WIDTH, Z_CHUNK):
        nw = min(Z_CHUNK, Z_WIDTH - n0)
        o_ref[:, n0:n0 + nw] = lax.dot_general(h_ref[...], wz_ref[n0:n0 + nw, :],
                                               (((1,), (1,)), ((), ())), preferred_element_type=F32)


def _norm_matmul(x, nw, wt_all, layer, tm):
    t = x.shape[0]
    return pl.pallas_call(
        _norm_matmul_kernel,
        grid=(t // tm,),
        in_specs=[pl.BlockSpec((tm, D_MODEL), lambda i: (i, 0)),
                  _resident((1, D_MODEL)),
                  pl.BlockSpec((None, W_IN_WIDTH, D_MODEL), lambda i: (layer, 0, 0),
                               pipeline_mode=pl.Buffered(1))],
        out_specs=pl.BlockSpec((tm, Z_WIDTH), lambda i: (i, 0)),
        out_shape=jax.ShapeDtypeStruct((t, Z_WIDTH), F32),
        scratch_shapes=[pltpu.VMEM((tm, D_MODEL), BF16), pltpu.VMEM((Z_WIDTH, D_MODEL), BF16)],
        compiler_params=_params("arbitrary"),
        name="norm_in_proj",
    )(x, nw, wt_all)


def _gate_terms(z_ref, bi_ref, bf_ref):
    gates = z_ref[:, Z_GATES:Z_GATES + LANES]
    forget = pltpu.roll(gates, LANES - MLSTM_HEADS, 1)
    return gates + bi_ref[...], _log_sigmoid(forget + bf_ref[...])


def _group_rms(v, gmean, w):
    hi, mid, lo = _split3(v * v)
    ms = _dot(hi, gmean) + _dot(mid, gmean) + _dot(lo, gmean)
    return v * lax.rsqrt(ms + RMS_EPS) * w


def _pool_tile(ext_ref, u_tile, col0, w_lo, w_hi, pos):
    acc = u_tile
    sums = {}
    for shift in range(1, w_hi):
        acc = acc + ext_ref[pl.ds(POOL_PREV_ROWS - shift, CHUNK), col0:col0 + LANES]
        if shift + 1 in (w_lo, w_hi):
            sums[shift + 1] = acc
    cnt_lo = jnp.minimum(w_lo, pos + 1).astype(F32)
    cnt_hi = jnp.minimum(w_hi, pos + 1).astype(F32)
    lane = lax.broadcasted_iota(jnp.int32, (CHUNK, LANES), 1)
    return jnp.where(lane < POOL_GROUP_DIM, sums[w_lo] / cnt_lo, sums[w_hi] / cnt_hi) - u_tile


def _mixer_prompt_kernel(z_ref, *refs):
    consts = refs[:9]
    y_ref, cn_ref, m_ref, ext_ref = refs[9:]

    @pl.when(pl.program_id(1) == 0)
    def _():
        ext_ref[:, 0:POOL_PREV_ROWS, :] = jnp.zeros((PROMPT_SEQ_PER_STEP, POOL_PREV_ROWS, POOL_WIDTH), F32)
        cn_ref[...] = jnp.zeros(cn_ref.shape, F32)
        m_ref[...] = jnp.zeros(m_ref.shape, F32)

    for i in range(PROMPT_SEQ_PER_STEP):
        _mixer_prompt_body(z_ref.at[i], *consts, y_ref.at[i], cn_ref.at[i], m_ref.at[i], ext_ref.at[i])


def _mixer_prompt_body(z_ref, poolw_ref, pscale_ref, bi_ref, bf_ref, mnorm_ref, gnorm_ref,
                       gws_ref, gbs_ref, gmean_ref,
                       y_ref, cn_ref, m_ref, ext_ref):
    chunk = pl.program_id(1)
    row = lax.broadcasted_iota(jnp.int32, (CHUNK, CHUNK), 0)
    col = lax.broadcasted_iota(jnp.int32, (CHUNK, CHUNK), 1)
    causal = col <= row
    lane = col

    ext_ref[POOL_PREV_ROWS:POOL_PREV_ROWS + CHUNK, :] = z_ref[:, Z_POOL:Z_POOL + POOL_WIDTH]
    pos = chunk * CHUNK + lax.broadcasted_iota(jnp.int32, (CHUNK, 1), 0)
    pooled = []
    for tile in range(2):
        col0 = tile * LANES
        u_tile = z_ref[:, Z_POOL + col0:Z_POOL + col0 + LANES]
        pooled.append(_pool_tile(ext_ref, u_tile, col0, POOL_WINDOWS[2 * tile],
                                 POOL_WINDOWS[2 * tile + 1], pos))
    pooled = jnp.concatenate(pooled, axis=1).astype(BF16)
    y_pool = _dot(pooled, poolw_ref[...]) * pscale_ref[...]
    y_ref[:, 0:POOL_WIDTH] = y_pool.astype(BF16)
    ext_ref[0:POOL_PREV_ROWS, :] = ext_ref[CHUNK:CHUNK + POOL_PREV_ROWS, :]

    vn = _group_rms(z_ref[:, Z_GV:Z_GV + GMLP_WIDTH], gmean_ref[...], gnorm_ref[...]).astype(BF16)
    for tile in range(2):
        col0 = tile * LANES
        vt = vn[:, col0:col0 + LANES]
        w_a = jnp.where(causal, gws_ref[2 * tile], 0.0).astype(BF16)
        w_b = jnp.where(causal, gws_ref[2 * tile + 1], 0.0).astype(BF16)
        mixed = jnp.where(lane < GMLP_GROUP_DIM, _dot(w_a, vt), _dot(w_b, vt))
        gu = z_ref[:, Z_GU + col0:Z_GU + col0 + LANES]
        y_g = gu * (mixed + gbs_ref[:, col0:col0 + LANES])
        y_ref[:, Y_GMLP + col0:Y_GMLP + col0 + LANES] = y_g.astype(BF16)

    ig, lf = _gate_terms(z_ref, bi_ref, bf_ref)
    tri =jnp.where(causal, 1.0, 0.0).astype(BF16)
    lf_hi, lf_mid, lf_lo = _split3(lf)
    b = _dot(tri, lf_hi) + _dot(tri, lf_mid) + _dot(tri, lf_lo)
    m_prev = m_ref[...]
    g = b + m_prev
    r_t = jnp.transpose(ig - b)
    b_last = b[CHUNK - 1:CHUNK, :]
    ones_col = jnp.where(lane == 0, 1.0, 0.0).astype(BF16)
    m_new_row = m_prev
    for h in range(MLSTM_HEADS):
        c0 = h * MLSTM_HEAD_DIM
        q = z_ref[:, Z_Q + c0:Z_Q + c0 + MLSTM_HEAD_DIM].astype(BF16)
        k = z_ref[:, Z_K + c0:Z_K + c0 + MLSTM_HEAD_DIM] * (MLSTM_HEAD_DIM ** -0.5)
        v = z_ref[:, Z_V + c0:Z_V + c0 + MLSTM_HEAD_DIM].astype(BF16)
        o = z_ref[:, Z_O + c0:Z_O + c0 + MLSTM_HEAD_DIM]
        b_col = b[:, h:h + 1]
        dmat = jnp.where(causal, b_col + r_t[h:h + 1, :], -jnp.inf)
        g_col = g[:, h:h + 1]
        m_t = jnp.maximum(g_col, jnp.max(dmat, axis=1, keepdims=True))
        scores = lax.dot_general(q, k.astype(BF16), (((1,), (1,)), ((), ())),
                                 preferred_element_type=F32)
        wts = jnp.exp(dmat - m_t) * scores
        inter = jnp.exp(g_col - m_t)
        cn_h = cn_ref[h]
        q_cn = _dot(q, cn_h.astype(BF16))
        num = inter * q_cn[:, 0:MLSTM_HEAD_DIM] + _dot(wts.astype(BF16), v)
        den = inter * q_cn[:, MLSTM_HEAD_DIM:MLSTM_HEAD_DIM + 1] + jnp.sum(wts, axis=1, keepdims=True)
        hid = num / jnp.maximum(jnp.abs(den), jnp.exp(-m_t))
        hid = _rms(hid, mnorm_ref[:, c0:c0 + MLSTM_HEAD_DIM])
        y_ref[:, POOL_WIDTH + c0:POOL_WIDTH + c0 + MLSTM_HEAD_DIM] = (jax.nn.sigmoid(o) * hid).astype(BF16)
        m_new = m_t[CHUNK - 1:CHUNK, :]
        bl = b_last[:, h:h + 1]
        decay = jnp.exp(bl + m_prev[:, h:h + 1] - m_new)
        w_s = jnp.exp(bl - b_col + ig[:, h:h + 1] - m_new)
        kw = (k * w_s).astype(BF16)
        v_ext = jnp.concatenate([v, ones_col], axis=1)
        cn_ref[h] = decay * cn_h + lax.dot_general(kw, v_ext, (((0,), (0,)), ((), ())),
                                                   preferred_element_type=F32)
        m_new_row = jnp.where(lane[0:1, :] == h, m_new, m_new_row)
    m_ref[...] = m_new_row


def _mixer_prompt(z, consts, batch, seq):
    nc = seq // CHUNK
    hd = MLSTM_HEAD_DIM
    z3 = z.reshape(batch, seq, Z_WIDTH)
    ns = PROMPT_SEQ_PER_STEP
    const_specs = [pl.BlockSpec(a.shape, lambda b, c, nd=a.ndim: (0,) * nd) for a in consts]
    return pl.pallas_call(
        _mixer_prompt_kernel,
        grid=(batch // ns, nc),
        in_specs=[pl.BlockSpec((ns, CHUNK, Z_WIDTH), lambda b, c: (b, c, 0))] + const_specs,
        out_specs=[pl.BlockSpec((ns, CHUNK, D_MODEL), lambda b, c: (b, c, 0)),
                   pl.BlockSpec((ns, MLSTM_HEADS, hd, 2 * hd), lambda b, c: (b, 0, 0, 0)),
                   pl.BlockSpec((ns, 1, LANES), lambda b, c: (b, 0, 0))],
        out_shape=[jax.ShapeDtypeStruct((batch, seq, D_MODEL), BF16),
                   jax.ShapeDtypeStruct((batch, MLSTM_HEADS, hd, 2 * hd), F32),
                   jax.ShapeDtypeStruct((batch, 1, LANES), F32)],
        scratch_shapes=[pltpu.VMEM((ns, POOL_PREV_ROWS + CHUNK, POOL_WIDTH), F32)],
        compiler_params=_params("parallel", "arbitrary"),
        name="mixer_prompt",
    )(z3, *consts)


def _mixer_sample_kernel(z_ref, sp_ref, c_ref, n_ref, m_ref, c_other_layers_ref,
                         poolw_ref, pscale_ref, bi_ref, bf_ref, mnorm_ref, gnorm_ref,
                         gw0_ref, gb0_ref, gmean_ref,
                         y_ref, cn_ref, nn_ref, mn_ref, gv_ref, tk_ref):
    del c_other_layers_ref
    nb = SAMPLE_BLOCK
    hd = MLSTM_HEAD_DIM
    lane = lax.broadcasted_iota(jnp.int32, (nb, LANES), 1)
    seq_id = lax.broadcasted_iota(jnp.int32, (nb, LANES), 0)

    pooled = []
    for tile in range(2):
        col0 = tile * LANES
        u_tile = z_ref[:, Z_POOL + col0:Z_POOL + col0 + LANES]
        w_lo, w_hi = POOL_WINDOWS[2 * tile], POOL_WINDOWS[2 * tile + 1]
        acc = u_tile
        sums = {}
        for shift in range(1, w_hi):
            acc = acc + sp_ref[POOL_STATE - shift, :, col0:col0 + LANES]
            if shift + 1 in (w_lo, w_hi):
                sums[shift + 1] = acc
        pooled.append(jnp.where(lane < POOL_GROUP_DIM, sums[w_lo] / float(w_lo), sums[w_hi] / float(w_hi)) - u_tile)
    pooled = jnp.concatenate(pooled, axis=1).astype(BF16)
    y_ref[:, 0:POOL_WIDTH] = (_dot(pooled, poolw_ref[...]) * pscale_ref[...]).astype(BF16)

    vn = _group_rms(z_ref[:, Z_GV:Z_GV + GMLP_WIDTH], gmean_ref[...], gnorm_ref[...])
    gv_ref[...] = vn
    y_g = z_ref[:, Z_GU:Z_GU + GMLP_WIDTH] * (gw0_ref[...] * vn + gb0_ref[...])
    y_ref[:, Y_GMLP:Y_GMLP + GMLP_WIDTH] = y_g.astype(BF16)

    ig, lf = _gate_terms(z_ref, bi_ref, bf_ref)
    m_prev = m_ref[...]
    g = lf + m_prev
    m_t = jnp.maximum(g, ig)
    inter = jnp.exp(g - m_t)
    e_ig = jnp.exp(ig - m_t)
    floor = jnp.exp(-m_t)
    mn_ref[...] = m_t
    tk_ref[...] = jnp.zeros((LANES, LANES), F32)
    for h in range(MLSTM_HEADS):
        tk_ref[nb * h:nb * (h + 1), :] = z_ref[:, Z_K + h * hd:Z_K + (h + 1) * hd] * (hd ** -0.5)
    k_t = jnp.transpose(tk_ref[...])
    for h in range(MLSTM_HEADS):
        c0 = h * hd
        q_h = z_ref[:, Z_Q + c0:Z_Q + c0 + hd]
        k_h = tk_ref[nb * h:nb * (h + 1), :]
        v_h = z_ref[:, Z_V + c0:Z_V + c0 + hd]
        o_h = z_ref[:, Z_O + c0:Z_O + c0 + hd]
        n_h = n_ref[:, c0:c0 + hd]
        inter_b = jnp.broadcast_to(inter[:, h:h + 1], (nb, hd))
        e_b = jnp.broadcast_to(e_ig[:, h:h + 1], (nb, hd))
        floor_b = jnp.broadcast_to(floor[:, h:h + 1], (nb, hd))
        v_w = e_b * v_h
        q_b = q_h.astype(BF16)
        q_c = jnp.zeros((nb, hd), F32)
        for s in range(nb):
            c_sh = c_ref[s, h]
            q_c = jnp.where(seq_id == s, _dot(q_b, c_sh.astype(BF16)), q_c)
            col = nb * h + s
            cn_ref[s, h] = inter_b[s:s + 1, :] * c_sh + k_t[:, col:col + 1] * v_w[s:s + 1, :]
        wts = e_b * jnp.sum(q_h * k_h, axis=1, keepdims=True)
        num = inter_b * q_c + wts * v_h
        den = inter_b * jnp.sum(q_h * n_h, axis=1, keepdims=True) + wts
        hid = num / jnp.maximum(jnp.abs(den), floor_b)
        hid = _rms(hid, mnorm_ref[:, c0:c0 + hd])
        y_ref[:, POOL_WIDTH + c0:POOL_WIDTH + c0 + hd] = (jax.nn.sigmoid(o_h) * hid).astype(BF16)
        nn_ref[:, c0:c0 + hd] = inter_b * n_h + e_b * k_h


def _mixer_sample(z, sp_t, c_all, layer, c_new_all, n_state, m_pad, consts):
    nseq = z.shape[0]
    nb = SAMPLE_BLOCK
    hd = MLSTM_HEAD_DIM
    const_specs = [pl.BlockSpec(a.shape, lambda j, nd=a.ndim: (0,) * nd) for a in consts]
    c_spec = pl.BlockSpec((None, nb, MLSTM_HEADS, hd, hd), lambda j: (layer, j, 0, 0, 0))
    aliases = {} if c_new_all is None else {5: 1}
    return pl.pallas_call(
        _mixer_sample_kernel,
        grid=(nseq // nb,),
        in_specs=[pl.BlockSpec((nb, Z_WIDTH), lambda j: (j, 0)),
                  pl.BlockSpec((POOL_STATE, nb, POOL_WIDTH), lambda j: (0, j, 0)),
                  c_spec,
                  pl.BlockSpec((nb, MLSTM_WIDTH), lambda j: (j, 0)),
                  pl.BlockSpec((nb, LANES), lambda j: (j, 0)),
                  pl.BlockSpec(memory_space=pl.ANY)] + const_specs,
        out_specs=[pl.BlockSpec((nb, D_MODEL), lambda j: (j, 0)),
                   c_spec,
                   pl.BlockSpec((nb, MLSTM_WIDTH), lambda j: (j, 0)),
                   pl.BlockSpec((nb, LANES), lambda j: (j, 0)),
                   pl.BlockSpec((nb, GMLP_WIDTH), lambda j: (j, 0))],
        out_shape=[jax.ShapeDtypeStruct((nseq, D_MODEL), BF16),
                   jax.ShapeDtypeStruct(c_all.shape, F32),
                   jax.ShapeDtypeStruct((nseq, MLSTM_WIDTH), F32),
                   jax.ShapeDtypeStruct((nseq, LANES), F32),
                   jax.ShapeDtypeStruct((nseq, GMLP_WIDTH), F32)],
        scratch_shapes=[pltpu.VMEM((LANES, LANES), F32)],
        input_output_aliases=aliases,
        compiler_params=_params("parallel"),
        name="mixer_sample",
    )(z, sp_t, c_all, n_state, m_pad, c_all if c_new_all is None else c_new_all, *consts)


def _proj_norm_res_kernel(y_ref, x_ref, w_ref, nw_ref, o_ref):
    o_ref[...] = x_ref[...] + _rms(_dot(y_ref[...], w_ref[...]), nw_ref[...])


def _proj_norm_res(y, x, w, nw, tm):
    t = x.shape[0]
    return pl.pallas_call(
        _proj_norm_res_kernel,
        grid=(t // tm,),
        in_specs=[pl.BlockSpec((tm, D_MODEL), lambda i: (i, 0)),
                  pl.BlockSpec((tm, D_MODEL), lambda i: (i, 0)),
                  pl.BlockSpec((D_MODEL, D_MODEL), lambda i: (0, 0)),
                  pl.BlockSpec((1, D_MODEL), lambda i: (0, 0))],
        out_specs=pl.BlockSpec((tm, D_MODEL), lambda i: (i, 0)),
        out_shape=jax.ShapeDtypeStruct((t, D_MODEL), F32),
        compiler_params=_params("parallel"),
        name="out_proj",
    )(y, x, w, nw)


def _dense_layer_kernel(y_ref, x_ref, p_ref, wout_ref, nmix_ref, npre_ref, npost_ref, nple_ref,
                        wg_ref, wu_ref, wd_ref, pg_ref, pp_ref, o_ref):
    x1 = x_ref[...] + _rms(_dot(y_ref[...], wout_ref[...]), nmix_ref[...])
    h = _rms(x1, npre_ref[...]).astype(BF16)
    y = None
    for f0 in range(0, D_FF, FF_CHUNK):
        fw = min(FF_CHUNK, D_FF - f0)
        gate = _dot(h, wg_ref[:, f0:f0 + fw])
        up = _dot(h, wu_ref[:, f0:f0 + fw])
        act = (gate * jax.nn.sigmoid(gate) * up).astype(BF16)
        part = _dot(act, wd_ref[f0:f0 + fw, :])
        y = part if y is None else y + part
    x2 = x1 + _rms(y, npost_ref[...])
    gate = jax.nn.sigmoid(_dot(_rms(x2, nple_ref[...]).astype(BF16), pg_ref[...]))
    o_ref[...] = x2 + gate * _dot(p_ref[...].astype(BF16), pp_ref[...])


def _dense_layer(y, x, p, w_out, nmix, npre, npost, nple, wg, wu, wd, ple_g, ple_p, tm):
    t = x.shape[0]
    return pl.pallas_call(
        _dense_layer_kernel,
        grid=(t // tm,),
        in_specs=[pl.BlockSpec((tm, D_MODEL), lambda i: (i, 0)),
                  pl.BlockSpec((tm, D_MODEL), lambda i: (i, 0)),
                  pl.BlockSpec((tm, PLE_DIM), lambda i: (i, 0)),
                  _resident((D_MODEL, D_MODEL)),
                  _resident((1, D_MODEL)), _resident((1, D_MODEL)), _resident((1, D_MODEL)), _resident((1, D_MODEL)),
                  _resident((D_MODEL, D_FF)), _resident((D_MODEL, D_FF)), _resident((D_FF, D_MODEL)),
                  _resident((D_MODEL, D_MODEL)), _resident((PLE_DIM, D_MODEL))],
        out_specs=pl.BlockSpec((tm, D_MODEL), lambda i: (i, 0)),
        out_shape=jax.ShapeDtypeStruct((t, D_MODEL), F32),
        compiler_params=_params("parallel"),
        name="dense_layer",
    )(y, x, p, w_out, nmix, npre, npost, nple, wg, wu, wd, ple_g, ple_p)


def _router_gates(h, rw_ref, rb_ref):
    shape = (h.shape[0], LANES)
    lane = lax.broadcasted_iota(jnp.int32, shape, 1)
    lane_f = lane.astype(F32)
    logits = jnp.where(lane < N_EXPERTS, _dot(h, rw_ref[...]) + rb_ref[...], -jnp.inf)
    l1 = jnp.max(logits, axis=-1, keepdims=True)
    i1 = jnp.min(jnp.where(logits == l1, lane_f, float(LANES)), axis=-1, keepdims=True)
    rest = jnp.where(lane_f == i1, -jnp.inf, logits)
    l2 = jnp.max(rest, axis=-1, keepdims=True)
    i2 = jnp.min(jnp.where(rest == l2, lane_f, float(LANES)), axis=-1, keepdims=True)
    e2 = jnp.exp(l2 - l1)
    total = 1.0 + e2
    return jnp.where(lane_f == i1, 1.0 / total, 0.0) + jnp.where(lane_f == i2, e2 / total, 0.0)


def _ffn_moe_kernel(x_ref, npre_ref, npost_ref, rw_ref, rb_ref, wg_ref, wu_ref, wd_ref,
                    o_ref, h_ref, acc_ref, gates_ref):
    e = pl.program_id(1)

    @pl.when(e == 0)
    def _():
        h_ref[...] = _rms(x_ref[...], npre_ref[...]).astype(BF16)
        acc_ref[...] = jnp.zeros(acc_ref.shape, F32)
        gates_ref[...] = _router_gates(h_ref[...], rw_ref, rb_ref)

    h = h_ref[...]
    lane = lax.broadcasted_iota(jnp.int32, gates_ref.shape, 1)
    gate_col = jnp.sum(jnp.where(lane == e, gates_ref[...], 0.0), axis=-1, keepdims=True)
    y = None
    for f0, fw in FF_EXPERT_CHUNKS:
        gate = _dot(h, wg_ref[:, f0:f0 + fw].astype(BF16))
        up = _dot(h, wu_ref[:, f0:f0 + fw].astype(BF16))
        act = (gate * jax.nn.sigmoid(gate) * up).astype(BF16)
        part = _dot(act, wd_ref[f0:f0 + fw, :].astype(BF16))
        y = part if y is None else y + part
    acc_ref[...] += gate_col * y

    @pl.when(e == pl.num_programs(1) - 1)
    def _():
        o_ref[...] = x_ref[...] + _rms(acc_ref[...], npost_ref[...])


def _ffn_moe(x, npre, npost, rw, rb, wg, wu, wd, tm):
    t = x.shape[0]
    return pl.pallas_call(
        _ffn_moe_kernel,
        grid=(t // tm, N_EXPERTS),
        in_specs=[pl.BlockSpec((tm, D_MODEL), lambda i, e: (i, 0)),
                  pl.BlockSpec((1, D_MODEL), lambda i, e: (0, 0)),
                  pl.BlockSpec((1, D_MODEL), lambda i, e: (0, 0)),
                  pl.BlockSpec((D_MODEL, LANES), lambda i, e: (0, 0)),
                  pl.BlockSpec((1, LANES), lambda i, e: (0, 0)),
                  pl.BlockSpec((None, D_MODEL, D_FF_EXPERT), lambda i, e: (e, 0, 0)),
                  pl.BlockSpec((None, D_MODEL, D_FF_EXPERT), lambda i, e: (e, 0, 0)),
                  pl.BlockSpec((None, D_FF_EXPERT, D_MODEL), lambda i, e: (e, 0, 0))],
        out_specs=pl.BlockSpec((tm, D_MODEL), lambda i, e: (i, 0)),
        out_shape=jax.ShapeDtypeStruct((t, D_MODEL), F32),
        scratch_shapes=[pltpu.VMEM((tm, D_MODEL), BF16), pltpu.VMEM((tm, D_MODEL), F32),
                        pltpu.VMEM((tm, LANES), F32)],
        compiler_params=_params("parallel", "arbitrary", vmem_limit=VMEM_LIMIT_EXPERT_WEIGHTS),
        name="ffn_moe",
    )(x, npre, npost, rw, rb, wg, wu, wd)


ROUTE_TILE = 256
ROW_ALIGN = 16
ROUTE_SEG = ROUTE_TILE
ROUTE_PACK = 2 * ROUTE_TILE + N_EXPERTS * ROW_ALIGN
ROUTE_W = D_MODEL + 3 * LANES
ROUTE_BLOCK = 512
ROUTE_SEG_SHORT = 96
ROUTE_SHORT_MAX = ROUTE_SEG_SHORT


def _route_region(n_tokens):
    rows = n_tokens + (n_tokens // ROUTE_TILE) * (ROW_ALIGN - 1) + ROUTE_SEG + ROUTE_BLOCK
    return -(-rows // ROUTE_BLOCK) * ROUTE_BLOCK


def _lane_scalar(row, lane, e):
    return jnp.sum(jnp.where(lane == e, row, 0.0)).astype(jnp.int32)


def _route_kernel(y_ref, x_ref, wout_ref, nmix_ref, npre_ref, rw_ref, rb_ref,
                  x1_ref, slot_ref, stats_ref, srt_hbm,
                  stage_ref, runv_ref, run_ref, short_ref, sem, *, region):
    i = pl.program_id(0)
    last = pl.num_programs(0) - 1
    cur = i % 2

    @pl.when(i == 0)
    def _():
        runv_ref[...] = jnp.zeros(runv_ref.shape, F32)
        stage_ref[:, ROUTE_PACK:, :] = jnp.zeros((2, ROUTE_SEG, ROUTE_W), BF16)
        for e in range(N_EXPERTS):
            run_ref[e] = 0

    x1 = x_ref[...] + _rms(_dot(y_ref[...], wout_ref[...]), nmix_ref[...])
    x1_ref[...] = x1
    h = _rms(x1, npre_ref[...]).astype(BF16)
    gates = _router_gates(h, rw_ref, rb_ref)
    sel = gates > 0.0
    ones = jnp.where(sel, 1.0, 0.0)
    trow = lax.broadcasted_iota(jnp.int32, (ROUTE_TILE, ROUTE_TILE), 0)
    tcol = lax.broadcasted_iota(jnp.int32, (ROUTE_TILE, ROUTE_TILE), 1)
    before = jnp.where(tcol < trow, 1.0, 0.0).astype(BF16)
    rank = _dot(before, ones.astype(BF16))
    cnt = jnp.sum(ones, axis=0, keepdims=True)
    cnt_pad = jnp.floor((cnt + (ROW_ALIGN - 1)) * (1.0 / ROW_ALIGN)) * ROW_ALIGN
    lrow = lax.broadcasted_iota(jnp.int32, (LANES, LANES), 0)
    lcol = lax.broadcasted_iota(jnp.int32, (LANES, LANES), 1)
    lower = jnp.where(lrow < lcol, 1.0, 0.0).astype(BF16)
    off = _dot(jnp.broadcast_to(cnt_pad, (SUBLANES, LANES)).astype(BF16), lower)[0:1, :]
    lane = lax.broadcasted_iota(jnp.int32, (1, LANES), 1)
    src_rows = [_lane_scalar(off, lane, e) for e in range(N_EXPERTS)]
    seg_lens = [_lane_scalar(cnt_pad, lane, e) for e in range(N_EXPERTS)]
    short = (jnp.max(cnt_pad) <= ROUTE_SHORT_MAX).astype(jnp.int32)
    slot_ref[...] = jnp.where(sel, rank, -1.0)
    stats_ref[...] = jnp.zeros(stats_ref.shape, F32)
    stats_ref[0:1, :] = runv_ref[...]
    stats_ref[1:2, :] = cnt
    runv_ref[...] = runv_ref[...] + cnt_pad

    pos = jnp.where(sel, rank + off, -1.0)
    pos_t = jnp.concatenate([jnp.transpose(pos[0:LANES, :]), jnp.transpose(pos[LANES:2 * LANES, :])], axis=1)
    pos_a = jnp.max(pos_t, axis=0, keepdims=True)
    pos_b = jnp.max(jnp.where(pos_t == pos_a, -1.0, pos_t), axis=0, keepdims=True)
    prow = lax.broadcasted_iota(jnp.int32, (ROUTE_PACK, ROUTE_TILE), 0).astype(F32)
    perm = jnp.where((prow == pos_a) | (prow == pos_b), 1.0, 0.0).astype(BF16)
    g_hi, g_mid, g_lo = _split3(gates)
    rows = _dot(perm, jnp.concatenate([h, g_hi, g_mid, g_lo], axis=1))
    stage_ref[cur, 0:ROUTE_PACK, :] = rows.astype(BF16)

    def segment_copy(e, src_row, dst_row, slot, rows=ROUTE_SEG):
        return pltpu.make_async_copy(
            stage_ref.at[slot, pl.ds(pl.multiple_of(src_row, ROW_ALIGN), rows), :],
            srt_hbm.at[pl.ds(pl.multiple_of(dst_row, ROW_ALIGN), rows), :],
            sem.at[e])

    def for_each_segment(is_short, action):
        for rows, flag in ((ROUTE_SEG_SHORT, 1), (ROUTE_SEG, 0)):
            @pl.when(is_short == flag)
            def _():
                for e in range(N_EXPERTS):
                    action(e, rows)

    @pl.when(i > 0)
    def _():
        for_each_segment(short_ref[0], lambda e, rows: segment_copy(e, 0, 0, 1 - cur, rows).wait())

    dst_rows = [e * region + run_ref[e] for e in range(N_EXPERTS)]
    for_each_segment(short, lambda e, rows: segment_copy(e, src_rows[e], dst_rows[e], cur, rows).start())
    for e in range(N_EXPERTS):
        run_ref[e] = run_ref[e] + seg_lens[e]
    short_ref[0] = short

    @pl.when(i == last)
    def _():
        for_each_segment(short, lambda e, rows: segment_copy(e, 0, 0, cur, rows).wait())
        stage_ref[1 - cur, 0:ROUTE_SEG, :] = jnp.zeros((ROUTE_SEG, ROUTE_W), BF16)
        for part in range(ROUTE_BLOCK // ROUTE_SEG):
            for e in range(N_EXPERTS):
                segment_copy(e, 0, e * region + run_ref[e] + part * ROUTE_SEG, 1 - cur).start()
            for e in range(N_EXPERTS):
                segment_copy(e, 0, 0, 1 - cur).wait()


def _route(y, x, w_out, nmix, npre, rw, rb):
    t = x.shape[0]
    nt = t // ROUTE_TILE
    region = _route_region(t)
    return pl.pallas_call(
        functools.partial(_route_kernel, region=region),
        grid=(nt,),
        in_specs=[pl.BlockSpec((ROUTE_TILE, D_MODEL), lambda i: (i, 0)),
                  pl.BlockSpec((ROUTE_TILE, D_MODEL), lambda i: (i, 0)),
                  pl.BlockSpec((D_MODEL, D_MODEL), lambda i: (0, 0)),
                  pl.BlockSpec((1, D_MODEL), lambda i: (0, 0)),
                  pl.BlockSpec((1, D_MODEL), lambda i: (0, 0)),
                  pl.BlockSpec((D_MODEL, LANES), lambda i: (0, 0)),
                  pl.BlockSpec((1, LANES), lambda i: (0, 0))],
        out_specs=[pl.BlockSpec((ROUTE_TILE, D_MODEL), lambda i: (i, 0)),
                   pl.BlockSpec((ROUTE_TILE, LANES), lambda i: (i, 0)),
                   pl.BlockSpec((None, SUBLANES, LANES), lambda i: (i, 0, 0)),
                   pl.BlockSpec(memory_space=pl.ANY)],
        out_shape=[jax.ShapeDtypeStruct((t, D_MODEL), F32),
                   jax.ShapeDtypeStruct((t, LANES), F32),
                   jax.ShapeDtypeStruct((nt, SUBLANES, LANES), F32),
                   jax.ShapeDtypeStruct((N_EXPERTS * region, ROUTE_W), BF16)],
        scratch_shapes=[pltpu.VMEM((2, ROUTE_PACK + ROUTE_SEG, ROUTE_W), BF16),
                        pltpu.VMEM((1, LANES), F32),
                        pltpu.SMEM((N_EXPERTS,), jnp.int32),
                        pltpu.SMEM((1,), jnp.int32),
                        pltpu.SemaphoreType.DMA((N_EXPERTS,))],
        compiler_params=_params("arbitrary"),
        name="moe_route",
    )(y, x, w_out, nmix, npre, rw, rb)


def _experts_kernel(blk_row_ref, blk_e_ref, n_used_ref, srt_ref, wg_ref, wu_ref, wd_ref, yhi_ref, ylo_ref):
    k = pl.program_id(0)

    @pl.when(k < n_used_ref[0])
    def _():
        h = srt_ref[:, 0:D_MODEL]
        gate3 = (srt_ref[:, D_MODEL:D_MODEL + LANES].astype(F32)
                 + srt_ref[:, D_MODEL + LANES:D_MODEL + 2 * LANES].astype(F32)
                 + srt_ref[:, D_MODEL + 2 * LANES:D_MODEL + 3 * LANES].astype(F32))
        lane = lax.broadcasted_iota(jnp.int32, gate3.shape, 1)
        gate_col = jnp.sum(jnp.where(lane == blk_e_ref[k], gate3, 0.0), axis=-1, keepdims=True)
        y = None
        for f0, fw in FF_EXPERT_CHUNKS:
            gate = _dot(h, wg_ref[:, f0:f0 + fw].astype(BF16))
            up = _dot(h, wu_ref[:, f0:f0 + fw].astype(BF16))
            act = (gate * jax.nn.sigmoid(gate) * up).astype(BF16)
            part = _dot(act, wd_ref[f0:f0 + fw, :].astype(BF16))
            y = part if y is None else y + part
        y = gate_col * y
        hi = y.astype(BF16)
        yhi_ref[...] = hi
        ylo_ref[...] = (y - hi.astype(F32)).astype(BF16)


def _experts(srt, blk_row, blk_e, n_used, wg, wu, wd, n_blocks):
    rows = srt.shape[0]
    grid_spec = pltpu.PrefetchScalarGridSpec(
        num_scalar_prefetch=3,
        grid=(n_blocks,),
        in_specs=[pl.BlockSpec((ROUTE_BLOCK, ROUTE_W), lambda k, br, be, nu: (br[k], 0)),
                  pl.BlockSpec((None, D_MODEL, D_FF_EXPERT), lambda k, br, be, nu: (be[k], 0, 0)),
                  pl.BlockSpec((None, D_MODEL, D_FF_EXPERT), lambda k, br, be, nu: (be[k], 0, 0)),
                  pl.BlockSpec((None, D_FF_EXPERT, D_MODEL), lambda k, br, be, nu: (be[k], 0, 0))],
        out_specs=[pl.BlockSpec((ROUTE_BLOCK, D_MODEL), lambda k, br, be, nu: (br[k], 0)),
                   pl.BlockSpec((ROUTE_BLOCK, D_MODEL), lambda k, br, be, nu: (br[k], 0))])
    return pl.pallas_call(
        _experts_kernel,
        grid_spec=grid_spec,
        out_shape=[jax.ShapeDtypeStruct((rows, D_MODEL), BF16), jax.ShapeDtypeStruct((rows, D_MODEL), BF16)],
        compiler_params=_params("arbitrary", vmem_limit=VMEM_LIMIT_EXPERT_WEIGHTS),
        name="moe_experts",
    )(blk_row, blk_e, n_used, srt, wg, wu, wd)


def _combine_kernel(src_row_ref, short_ref, x_ref, slot_ref, shift_ref, p_ref, npost_ref, nple_ref,
                    wg_ref, wp_ref, yhi_hbm, ylo_hbm, o_ref, seg_hi_ref, seg_lo_ref, y_ref, sem):
    i = pl.program_id(0)
    nt = pl.num_programs(0)
    cur = i % 2

    def segment_copies(tile, slot, e, rows):
        src = pl.ds(pl.multiple_of(src_row_ref[tile * N_EXPERTS + e], ROW_ALIGN), rows)
        dst = pl.ds(e * rows, rows)
        return (pltpu.make_async_copy(yhi_hbm.at[src, :], seg_hi_ref.at[slot, dst, :], sem.at[slot, 0, e]),
                pltpu.make_async_copy(ylo_hbm.at[src, :], seg_lo_ref.at[slot, dst, :], sem.at[slot, 1, e]))

    def for_each_segment(tile, slot, action):
        for rows, is_short in ((ROUTE_SEG_SHORT, 1), (ROUTE_SEG, 0)):
            @pl.when(short_ref[tile] == is_short)
            def _():
                for e in range(N_EXPERTS):
                    for c in segment_copies(tile, slot, e, rows):
                        action(c)

    @pl.when(i == 0)
    def _():
        for_each_segment(0, 0, lambda c: c.start())

    @pl.when(i + 1 < nt)
    def _():
        for_each_segment(i + 1, 1 - cur, lambda c: c.start())

    for_each_segment(i, cur, lambda c: c.wait())

    slot = slot_ref[...]
    where = jnp.where(slot >= 0.0, slot + shift_ref[...], -1.0)

    def gather(rows):
        seg_lane = lax.broadcasted_iota(jnp.int32, (ROUTE_TILE, rows), 1).astype(F32)
        perm = jnp.concatenate([jnp.where(where[:, e:e + 1] == seg_lane, 1.0, 0.0).astype(BF16)
                                for e in range(N_EXPERTS)], axis=1)
        k = N_EXPERTS * rows
        y_ref[...] = _dot(perm, seg_hi_ref[cur, 0:k, :]) + _dot(perm, seg_lo_ref[cur, 0:k, :])

    @pl.when(short_ref[i] == 1)
    def _():
        gather(ROUTE_SEG_SHORT)

    @pl.when(short_ref[i] == 0)
    def _():
        gather(ROUTE_SEG)

    x = x_ref[...] + _rms(y_ref[...], npost_ref[...])
    gate = jax.nn.sigmoid(_dot(_rms(x, nple_ref[...]).astype(BF16), wg_ref[...]))
    o_ref[...] = x + gate * _dot(p_ref[...].astype(BF16), wp_ref[...])


def _combine(src_row, short, x, slot, shift, p, npost, nple, wg, wp, yhi, ylo):
    t = x.shape[0]
    grid_spec = pltpu.PrefetchScalarGridSpec(
        num_scalar_prefetch=2,
        grid=(t // ROUTE_TILE,),
        in_specs=[pl.BlockSpec((ROUTE_TILE, D_MODEL), lambda i, *_: (i, 0)),
                  pl.BlockSpec((ROUTE_TILE, LANES), lambda i, *_: (i, 0)),
                  pl.BlockSpec((None, 1, LANES), lambda i, *_: (i, 0, 0)),
                  pl.BlockSpec((ROUTE_TILE, PLE_DIM), lambda i, *_: (i, 0)),
                  pl.BlockSpec((1, D_MODEL), lambda i, *_: (0, 0)),
                  pl.BlockSpec((1, D_MODEL), lambda i, *_: (0, 0)),
                  pl.BlockSpec((D_MODEL, D_MODEL), lambda i, *_: (0, 0)),
                  pl.BlockSpec((PLE_DIM, D_MODEL), lambda i, *_: (0, 0)),
                  pl.BlockSpec(memory_space=pl.ANY),
                  pl.BlockSpec(memory_space=pl.ANY)],
        out_specs=pl.BlockSpec((ROUTE_TILE, D_MODEL), lambda i, *_: (i, 0)),
        scratch_shapes=[pltpu.VMEM((2, N_EXPERTS * ROUTE_SEG, D_MODEL), BF16),
                        pltpu.VMEM((2, N_EXPERTS * ROUTE_SEG, D_MODEL), BF16),
                        pltpu.VMEM((ROUTE_TILE, D_MODEL), F32),
                        pltpu.SemaphoreType.DMA((2, 2, N_EXPERTS))])
    return pl.pallas_call(
        _combine_kernel,
        grid_spec=grid_spec,
        out_shape=jax.ShapeDtypeStruct((t, D_MODEL), F32),
        compiler_params=_params("arbitrary"),
        name="moe_combine_ple",
    )(src_row, short, x, slot, shift, p, npost, nple, wg, wp, yhi, ylo)


def _moe_layer_routed(y, x, p, w_out, nmix, npre, npost, nple, rw, rb, wg, wu, wd, ple_g, ple_p):
    t = x.shape[0]
    nt = t // ROUTE_TILE
    region = _route_region(t)
    x, slot, stats, srt = _route(y, x, w_out, nmix, npre, rw, rb)
    base = stats[:, 0, 0:N_EXPERTS].astype(jnp.int32)
    cnt = stats[:, 1, 0:N_EXPERTS].astype(jnp.int32)
    cnt_pad = (cnt + (ROW_ALIGN - 1)) // ROW_ALIGN * ROW_ALIGN
    total = base[-1] + cnt_pad[-1]
    nblk = (total + (ROUTE_BLOCK - 1)) // ROUTE_BLOCK
    cum = jnp.cumsum(nblk)
    n_used = cum[-1]
    max_rows = 2 * t + nt * N_EXPERTS * (ROW_ALIGN - 1)
    n_blocks = max_rows // ROUTE_BLOCK + N_EXPERTS
    kk = jnp.minimum(jnp.arange(n_blocks, dtype=jnp.int32), n_used - 1)
    blk_e = jnp.sum(kk[:, None] >= cum[None, :], axis=1).astype(jnp.int32)
    blk_row = blk_e * (region // ROUTE_BLOCK) + kk - (cum - nblk)[blk_e]
    yhi, ylo = _experts(srt, blk_row.astype(jnp.int32), blk_e, n_used.reshape(1).astype(jnp.int32),
                        wg, wu, wd, n_blocks)
    short = jnp.all(cnt_pad <= ROUTE_SHORT_MAX, axis=1)
    seg_rows = jnp.where(short, ROUTE_SEG_SHORT, ROUTE_SEG)[:, None]
    start = jnp.maximum(jnp.minimum(base, nblk[None, :] * ROUTE_BLOCK - seg_rows), 0)
    src_row = jnp.arange(N_EXPERTS, dtype=jnp.int32)[None, :] * region + start
    first_used = jnp.argmax(nblk > 0).astype(jnp.int32)
    src_row = jnp.where(cnt > 0, src_row, first_used * region).reshape(-1)
    shift = _pad_lanes((base - start).astype(F32)).reshape(nt, 1, LANES)
    return _combine(src_row.astype(jnp.int32), short.astype(jnp.int32), x, slot, shift, p, npost, nple,
                    ple_g, ple_p, yhi, ylo)


def _ple_kernel(x_ref, p_ref, nw_ref, wg_ref, wp_ref, o_ref):
    x = x_ref[...]
    gate = jax.nn.sigmoid(_dot(_rms(x, nw_ref[...]).astype(BF16), wg_ref[...]))
    o_ref[...] = x + gate * _dot(p_ref[...].astype(BF16), wp_ref[...])


def _ple(x, p, nw, wg, wp, tm):
    t = x.shape[0]
    return pl.pallas_call(
        _ple_kernel,
        grid=(t // tm,),
        in_specs=[pl.BlockSpec((tm, D_MODEL), lambda i: (i, 0)),
                  pl.BlockSpec((tm, PLE_DIM), lambda i: (i, 0)),
                  pl.BlockSpec((1, D_MODEL), lambda i: (0, 0)),
                  pl.BlockSpec((D_MODEL, D_MODEL), lambda i: (0, 0)),
                  pl.BlockSpec((PLE_DIM, D_MODEL), lambda i: (0, 0))],
        out_specs=pl.BlockSpec((tm, D_MODEL), lambda i: (i, 0)),
        out_shape=jax.ShapeDtypeStruct((t, D_MODEL), F32),
        compiler_params=_params("parallel"),
        name="ple",
    )(x, p, nw, wg, wp)


def _pad_lanes(a, width=LANES):
    return jnp.pad(a, [(0, 0)] * (a.ndim - 1) + [(0, width - a.shape[-1])])


def _block_diag(blocks):
    g, d, _ = blocks.shape
    out = jnp.zeros((g * d, g * d), blocks.dtype)
    for i in range(g):
        out = out.at[i * d:(i + 1) * d, i * d:(i + 1) * d].set(blocks[i])
    return out


def _row(a):
    return a.reshape(1, -1).astype(F32)


def kernel(x_prompt, x_sample, state_pool, state_mlstm_C, state_mlstm_n, state_mlstm_m, p_prompt, p_sample,
           norm_mix_pre, norm_mix_post, norm_ffn_pre, norm_ffn_post, norm_ple, w_in, pool_w, pool_scale,
           mlstm_b_i, mlstm_b_f, mlstm_norm_w, gmlp_norm_w, gmlp_ws, gmlp_bs, w_out,
           ffn_w_gate, ffn_w_up, ffn_w_down, moe_router_w, moe_router_b, moe_w_gate, moe_w_up, moe_w_down,
           ple_w_gate, ple_w_proj):
    batch, seq, _ = x_prompt.shape
    nseq = x_sample.shape[0]
    xp = x_prompt.reshape(batch * seq, D_MODEL)
    xs = x_sample.reshape(nseq, D_MODEL)
    gmean = _block_diag(jnp.full((GMLP_GROUPS, GMLP_GROUP_DIM, GMLP_GROUP_DIM), 1.0 / GMLP_GROUP_DIM, BF16))

    w_in_t = jnp.swapaxes(w_in, 1, 2)

    pools_p, cs_p, ns_p, ms_p = [], [], [], []
    pools_s, ns_s, ms_s, gvs_s = [], [], [], []
    c_new_s = None
    for i in range(DEPTH):
        w_out_b = w_out[i].astype(BF16)
        poolw = _block_diag(pool_w[i]).astype(BF16)
        shared = [poolw, _row(pool_scale[i]), _pad_lanes(_row(mlstm_b_i[i])), _pad_lanes(_row(mlstm_b_f[i])),
                  _row(mlstm_norm_w[i]), _row(gmlp_norm_w[i])]
        gbs_full = jnp.repeat(gmlp_bs[i].T, GMLP_GROUP_DIM, axis=1)
        consts_p = shared + [gmlp_ws[i], gbs_full, gmean]
        gw0 = jnp.repeat(gmlp_ws[i][:, 0, 0], GMLP_GROUP_DIM).reshape(1, GMLP_WIDTH)
        consts_s = shared + [gw0, gbs_full[0:1, :], gmean]
        ple_g = ple_w_gate[i].astype(BF16)
        ple_p = ple_w_proj[i].astype(BF16)
        j = i // 2
        if i % 2 == 0:
            ffn_g, ffn_u, ffn_d = (ffn_w_gate[j].astype(BF16), ffn_w_up[j].astype(BF16),
                                   ffn_w_down[j].astype(BF16))
        else:
            rw = _pad_lanes(moe_router_w[j]).astype(BF16)
            rb = _pad_lanes(_row(moe_router_b[j]))
            moe_g, moe_u, moe_d = moe_w_gate[j], moe_w_up[j], moe_w_down[j]

        z = _norm_matmul(xp, _row(norm_mix_pre[i]), w_in_t, i, TM_PROMPT)
        y, cn_new, m_new = _mixer_prompt(z, consts_p, batch, seq)
        pools_p.append(z.reshape(batch, seq, Z_WIDTH)[:, seq - POOL_STATE:, 0:POOL_WIDTH])
        cs_p.append(cn_new[..., 0:MLSTM_HEAD_DIM])
        ns_p.append(cn_new[..., MLSTM_HEAD_DIM])
        ms_p.append(m_new[:, 0, 0:MLSTM_HEADS])
        y = y.reshape(batch * seq, D_MODEL)
        pp = p_prompt[i].reshape(batch * seq, PLE_DIM)
        norms = (_row(norm_mix_post[i]), _row(norm_ffn_pre[i]), _row(norm_ffn_post[i]), _row(norm_ple[i]))
        if i % 2 == 0:
            xp = _dense_layer(y, xp, pp, w_out_b, *norms, ffn_g, ffn_u, ffn_d, ple_g, ple_p, TM_PROMPT)
        else:
            xp = _moe_layer_routed(y, xp, pp, w_out_b, *norms, rw, rb, moe_g, moe_u, moe_d, ple_g, ple_p)

        z = _norm_matmul(xs, _row(norm_mix_pre[i]), w_in_t, i, nseq)
        sp_t = jnp.transpose(state_pool[i], (1, 0, 2))
        y, c_new_s, n_new, m_new, gv = _mixer_sample(z, sp_t, state_mlstm_C, i, c_new_s,
                                                     state_mlstm_n[i].reshape(nseq, MLSTM_WIDTH),
                                                     _pad_lanes(state_mlstm_m[i]), consts_s)
        pools_s.append(jnp.concatenate([state_pool[i][:, 1:], z[:, None, 0:POOL_WIDTH]], axis=1))
        ns_s.append(n_new.reshape(nseq, MLSTM_HEADS, MLSTM_HEAD_DIM))
        ms_s.append(m_new[:, 0:MLSTM_HEADS])
        gvs_s.append(gv[:, None, :])
        ps = p_sample[i].reshape(nseq, PLE_DIM)
        if i % 2 == 0:
            xs = _dense_layer(y, xs, ps, w_out_b, *norms, ffn_g, ffn_u, ffn_d, ple_g, ple_p, nseq)
        else:
            xs = _proj_norm_res(y, xs, w_out_b, norms[0], nseq)
            xs = _ffn_moe(xs, norms[1], norms[2], rw, rb, moe_g, moe_u, moe_d, nseq)
            xs = _ple(xs, ps, norms[3], ple_g, ple_p, nseq)

    return (xp.reshape(batch, seq, D_MODEL), xs.reshape(nseq, 1, D_MODEL),
            jnp.stack(pools_p), jnp.stack(cs_p), jnp.stack(ns_p), jnp.stack(ms_p),
            jnp.stack(pools_s), c_new_s, jnp.stack(ns_s), jnp.stack(ms_s), jnp.stack(gvs_s))
```

```python
import functools

import jax
import jax.numpy as jnp
from jax import lax
from jax.experimental import pallas as pl
from jax.experimental.pallas import tpu as pltpu

F32 = jnp.float32
BF16 = jnp.bfloat16

D_MODEL = 1024
DEPTH = 2
POOL_WIDTH = 256
POOL_WINDOWS = (2, 4, 8, 16)
POOL_GROUP_DIM = 64
POOL_STATE = 15
POOL_PREV_ROWS = 16
MLSTM_WIDTH = 512
MLSTM_HEADS = 4
MLSTM_HEAD_DIM = 128
CHUNK = 128
GMLP_WIDTH = 256
GMLP_GROUPS = 4
GMLP_GROUP_DIM = 64
D_FF = 2816
N_EXPERTS = 8
D_FF_EXPERT = 1408
PLE_DIM = 256
RMS_EPS = 1e-6
PAST_LEN = 16384

LANES = 128
SUBLANES = 8
VMEM_LIMIT = 48 * 1024 * 1024
VMEM_LIMIT_EXPERT_WEIGHTS = 58 * 1024 * 1024

Z_POOL = 0
Z_Q = 256
Z_K = 768
Z_V = 1280
Z_O = 1792
Z_GU = 2304
Z_GV = 2560
Z_GATES = 2816
Z_WIDTH = 2944
Y_GMLP = POOL_WIDTH + MLSTM_WIDTH
Z_CHUNK = 512
W_IN_IG = 2304
W_IN_GU = 2312
W_IN_WIDTH = 2824

TM_PROMPT = 512
FF_CHUNK = 512
FF_EXPERT_CHUNKS = ((0, 512), (512, 512), (1024, 384))
SAMPLE_BLOCK = 16
PROMPT_SEQ_PER_STEP = 2


def _params(*semantics, vmem_limit=VMEM_LIMIT):
    return pltpu.CompilerParams(dimension_semantics=semantics, vmem_limit_bytes=vmem_limit)


def _rms(x, w):
    return x * lax.rsqrt(jnp.mean(x * x, axis=-1, keepdims=True) + RMS_EPS) * w


def _log_sigmoid(x):
    return jnp.minimum(x, 0.0) - jnp.log1p(jnp.exp(-jnp.abs(x)))


def _dot(a, b):
    return jnp.dot(a, b, preferred_element_type=F32)


def _split3(x):
    hi = x.astype(BF16)
    rest = x - hi.astype(F32)
    mid = rest.astype(BF16)
    lo = (rest - mid.astype(F32)).astype(BF16)
    return hi, mid, lo


def _resident(shape):
    return pl.BlockSpec(shape, lambda i: (0,) * len(shape), pipeline_mode=pl.Buffered(1))


def _norm_matmul_kernel(x_ref, nw_ref, wt_ref, o_ref, h_ref, wz_ref):
    @pl.when(pl.program_id(0) == 0)
    def _():
        wz_ref[0:Z_GU, :] = wt_ref[0:Z_GU, :].astype(BF16)
        wz_ref[Z_GU:Z_GATES, :] = wt_ref[W_IN_GU:W_IN_GU + 2 * GMLP_WIDTH, :].astype(BF16)
        gates = wt_ref[W_IN_IG:W_IN_IG + 2 * MLSTM_HEADS, :]
        zeros = jnp.zeros((LANES - 2 * MLSTM_HEADS, D_MODEL), F32)
        wz_ref[Z_GATES:Z_WIDTH, :] = jnp.concatenate([gates, zeros], axis=0).astype(BF16)

    h_ref[...] = _rms(x_ref[...], nw_ref[...]).astype(BF16)
    for n0 in range(0, Z_WIDTH, Z_CHUNK):
        nw = min(Z_CHUNK, Z_WIDTH - n0)
        o_ref[:, n0:n0 + nw] = lax.dot_general(h_ref[...], wz_ref[n0:n0 + nw, :],
                                               (((1,), (1,)), ((), ())), preferred_element_type=F32)


def _norm_matmul(x, nw, wt_all, layer, tm):
    t = x.shape[0]
    return pl.pallas_call(
        _norm_matmul_kernel,
        grid=(t // tm,),
        in_specs=[pl.BlockSpec((tm, D_MODEL), lambda i: (i, 0)),
                  _resident((1, D_MODEL)),
                  pl.BlockSpec((None, W_IN_WIDTH, D_MODEL), lambda i: (layer, 0, 0),
                               pipeline_mode=pl.Buffered(1))],
        out_specs=pl.BlockSpec((tm, Z_WIDTH), lambda i: (i, 0)),
        out_shape=jax.ShapeDtypeStruct((t, Z_WIDTH), F32),
        scratch_shapes=[pltpu.VMEM((tm, D_MODEL), BF16), pltpu.VMEM((Z_WIDTH, D_MODEL), BF16)],
        compiler_params=_params("arbitrary"),
        name="norm_in_proj",
    )(x, nw, wt_all)


def _gate_terms(z_ref, bi_ref, bf_ref):
    gates = z_ref[:, Z_GATES:Z_GATES + LANES]
    forget = pltpu.roll(gates, LANES - MLSTM_HEADS, 1)
    return gates + bi_ref[...], _log_sigmoid(forget + bf_ref[...])


def _group_rms(v, gmean, w):
    hi, mid, lo = _split3(v * v)
    ms = _dot(hi, gmean) + _dot(mid, gmean) + _dot(lo, gmean)
    return v * lax.rsqrt(ms + RMS_EPS) * w


def _pool_tile(ext_ref, u_tile, col0, w_lo, w_hi, pos):
    acc = u_tile
    sums = {}
    for shift in range(1, w_hi):
        acc = acc + ext_ref[pl.ds(POOL_PREV_ROWS - shift, CHUNK), col0:col0 + LANES]
        if shift + 1 in (w_lo, w_hi):
            sums[shift + 1] = acc
    cnt_lo = jnp.minimum(w_lo, pos + 1).astype(F32)
    cnt_hi = jnp.minimum(w_hi, pos + 1).astype(F32)
    lane = lax.broadcasted_iota(jnp.int32, (CHUNK, LANES), 1)
    return jnp.where(lane < POOL_GROUP_DIM, sums[w_lo] / cnt_lo, sums[w_hi] / cnt_hi) - u_tile


def _mixer_prompt_kernel(z_ref, *refs):
    consts = refs[:9]
    y_ref, cn_ref, m_ref, ext_ref = refs[9:]

    @pl.when(pl.program_id(1) == 0)
    def _():
        ext_ref[:, 0:POOL_PREV_ROWS, :] = jnp.zeros((PROMPT_SEQ_PER_STEP, POOL_PREV_ROWS, POOL_WIDTH), F32)
        cn_ref[...] = jnp.zeros(cn_ref.shape, F32)
        m_ref[...] = jnp.zeros(m_ref.shape, F32)

    for i in range(PROMPT_SEQ_PER_STEP):
        _mixer_prompt_body(z_ref.at[i], *consts, y_ref.at[i], cn_ref.at[i], m_ref.at[i], ext_ref.at[i])


def _mixer_prompt_body(z_ref, poolw_ref, pscale_ref, bi_ref, bf_ref, mnorm_ref, gnorm_ref,
                       gws_ref, gbs_ref, gmean_ref,
                       y_ref, cn_ref, m_ref, ext_ref):
    chunk = pl.program_id(1)
    row = lax.broadcasted_iota(jnp.int32, (CHUNK, CHUNK), 0)
    col = lax.broadcasted_iota(jnp.int32, (CHUNK, CHUNK), 1)
    causal = col <= row
    lane = col

    ext_ref[POOL_PREV_ROWS:POOL_PREV_ROWS + CHUNK, :] = z_ref[:, Z_POOL:Z_POOL + POOL_WIDTH]
    pos = chunk * CHUNK + lax.broadcasted_iota(jnp.int32, (CHUNK, 1), 0)
    pooled = []
    for tile in range(2):
        col0 = tile * LANES
        u_tile = z_ref[:, Z_POOL + col0:Z_POOL + col0 + LANES]
        pooled.append(_pool_tile(ext_ref, u_tile, col0, POOL_WINDOWS[2 * tile],
                                 POOL_WINDOWS[2 * tile + 1], pos))
    pooled = jnp.concatenate(pooled, axis=1).astype(BF16)
    y_pool = _dot(pooled, poolw_ref[...]) * pscale_ref[...]
    y_ref[:, 0:POOL_WIDTH] = y_pool.astype(BF16)
    ext_ref[0:POOL_PREV_ROWS, :] = ext_ref[CHUNK:CHUNK + POOL_PREV_ROWS, :]

    vn = _group_rms(z_ref[:, Z_GV:Z_GV + GMLP_WIDTH], gmean_ref[...], gnorm_ref[...]).astype(BF16)
    for tile in range(2):
        col0 = tile * LANES
        vt = vn[:, col0:col0 + LANES]
        w_a = jnp.where(causal, gws_ref[2 * tile], 0.0).astype(BF16)
        w_b = jnp.where(causal, gws_ref[2 * tile + 1], 0.0).astype(BF16)
        mixed = jnp.where(lane < GMLP_GROUP_DIM, _dot(w_a, vt), _dot(w_b, vt))
        gu = z_ref[:, Z_GU + col0:Z_GU + col0 + LANES]
        y_g = gu * (mixed + gbs_ref[:, col0:col0 + LANES])
        y_ref[:, Y_GMLP + col0:Y_GMLP + col0 + LANES] = y_g.astype(BF16)

    ig, lf = _gate_terms(z_ref, bi_ref, bf_ref)
    tri =jnp.where(causal, 1.0, 0.0).astype(BF16)
    lf_hi, lf_mid, lf_lo = _split3(lf)
    b = _dot(tri, lf_hi) + _dot(tri, lf_mid) + _dot(tri, lf_lo)
    m_prev = m_ref[...]
    g = b + m_prev
    r_t = jnp.transpose(ig - b)
    b_last = b[CHUNK - 1:CHUNK, :]
    ones_col = jnp.where(lane == 0, 1.0, 0.0).astype(BF16)
    m_new_row = m_prev
    for h in range(MLSTM_HEADS):
        c0 = h * MLSTM_HEAD_DIM
        q = z_ref[:, Z_Q + c0:Z_Q + c0 + MLSTM_HEAD_DIM].astype(BF16)
        k = z_ref[:, Z_K + c0:Z_K + c0 + MLSTM_HEAD_DIM] * (MLSTM_HEAD_DIM ** -0.5)
        v = z_ref[:, Z_V + c0:Z_V + c0 + MLSTM_HEAD_DIM].astype(BF16)
        o = z_ref[:, Z_O + c0:Z_O + c0 + MLSTM_HEAD_DIM]
        b_col = b[:, h:h + 1]
        dmat = jnp.where(causal, b_col + r_t[h:h + 1, :], -jnp.inf)
        g_col = g[:, h:h + 1]
        m_t = jnp.maximum(g_col, jnp.max(dmat, axis=1, keepdims=True))
        scores = lax.dot_general(q, k.astype(BF16), (((1,), (1,)), ((), ())),
                                 preferred_element_type=F32)
        wts = jnp.exp(dmat - m_t) * scores
        inter = jnp.exp(g_col - m_t)
        cn_h = cn_ref[h]
        q_cn = _dot(q, cn_h.astype(BF16))
        num = inter * q_cn[:, 0:MLSTM_HEAD_DIM] + _dot(wts.astype(BF16), v)
        den = inter * q_cn[:, MLSTM_HEAD_DIM:MLSTM_HEAD_DIM + 1] + jnp.sum(wts, axis=1, keepdims=True)
        hid = num / jnp.maximum(jnp.abs(den), jnp.exp(-m_t))
        hid = _rms(hid, mnorm_ref[:, c0:c0 + MLSTM_HEAD_DIM])
        y_ref[:, POOL_WIDTH + c0:POOL_WIDTH + c0 + MLSTM_HEAD_DIM] = (jax.nn.sigmoid(o) * hid).astype(BF16)
        m_new = m_t[CHUNK - 1:CHUNK, :]
        bl = b_last[:, h:h + 1]
        decay = jnp.exp(bl + m_prev[:, h:h + 1] - m_new)
        w_s = jnp.exp(bl - b_col + ig[:, h:h + 1] - m_new)
        kw = (k * w_s).astype(BF16)
        v_ext = jnp.concatenate([v, ones_col], axis=1)
        cn_ref[h] = decay * cn_h + lax.dot_general(kw, v_ext, (((0,), (0,)), ((), ())),
                                                   preferred_element_type=F32)
        m_new_row = jnp.where(lane[0:1, :] == h, m_new, m_new_row)
    m_ref[...] = m_new_row


def _mixer_prompt(z, consts, batch, seq):
    nc = seq // CHUNK
    hd = MLSTM_HEAD_DIM
    z3 = z.reshape(batch, seq, Z_WIDTH)
    ns = PROMPT_SEQ_PER_STEP
    const_specs = [pl.BlockSpec(a.shape, lambda b, c, nd=a.ndim: (0,) * nd) for a in consts]
    return pl.pallas_call(
        _mixer_prompt_kernel,
        grid=(batch // ns, nc),
        in_specs=[pl.BlockSpec((ns, CHUNK, Z_WIDTH), lambda b, c: (b, c, 0))] + const_specs,
        out_specs=[pl.BlockSpec((ns, CHUNK, D_MODEL), lambda b, c: (b, c, 0)),
                   pl.BlockSpec((ns, MLSTM_HEADS, hd, 2 * hd), lambda b, c: (b, 0, 0, 0)),
                   pl.BlockSpec((ns, 1, LANES), lambda b, c: (b, 0, 0))],
        out_shape=[jax.ShapeDtypeStruct((batch, seq, D_MODEL), BF16),
                   jax.ShapeDtypeStruct((batch, MLSTM_HEADS, hd, 2 * hd), F32),
                   jax.ShapeDtypeStruct((batch, 1, LANES), F32)],
        scratch_shapes=[pltpu.VMEM((ns, POOL_PREV_ROWS + CHUNK, POOL_WIDTH), F32)],
        compiler_params=_params("parallel", "arbitrary"),
        name="mixer_prompt",
    )(z3, *consts)


def _mixer_sample_kernel(z_ref, sp_ref, c_ref, n_ref, m_ref, c_other_layers_ref,
                         poolw_ref, pscale_ref, bi_ref, bf_ref, mnorm_ref, gnorm_ref,
                         gw0_ref, gb0_ref, gmean_ref,
                         y_ref, cn_ref, nn_ref, mn_ref, gv_ref, tk_ref):
    del c_other_layers_ref
    nb = SAMPLE_BLOCK
    hd = MLSTM_HEAD_DIM
    lane = lax.broadcasted_iota(jnp.int32, (nb, LANES), 1)
    seq_id = lax.broadcasted_iota(jnp.int32, (nb, LANES), 0)

    pooled = []
    for tile in range(2):
        col0 = tile * LANES
        u_tile = z_ref[:, Z_POOL + col0:Z_POOL + col0 + LANES]
        w_lo, w_hi = POOL_WINDOWS[2 * tile], POOL_WINDOWS[2 * tile + 1]
        acc = u_tile
        sums = {}
        for shift in range(1, w_hi):
            acc = acc + sp_ref[POOL_STATE - shift, :, col0:col0 + LANES]
            if shift + 1 in (w_lo, w_hi):
                sums[shift + 1] = acc
        pooled.append(jnp.where(lane < POOL_GROUP_DIM, sums[w_lo] / float(w_lo), sums[w_hi] / float(w_hi)) - u_tile)
    pooled = jnp.concatenate(pooled, axis=1).astype(BF16)
    y_ref[:, 0:POOL_WIDTH] = (_dot(pooled, poolw_ref[...]) * pscale_ref[...]).astype(BF16)

    vn = _group_rms(z_ref[:, Z_GV:Z_GV + GMLP_WIDTH], gmean_ref[...], gnorm_ref[...])
    gv_ref[...] = vn
    y_g = z_ref[:, Z_GU:Z_GU + GMLP_WIDTH] * (gw0_ref[...] * vn + gb0_ref[...])
    y_ref[:, Y_GMLP:Y_GMLP + GMLP_WIDTH] = y_g.astype(BF16)

    ig, lf = _gate_terms(z_ref, bi_ref, bf_ref)
    m_prev = m_ref[...]
    g = lf + m_prev
    m_t = jnp.maximum(g, ig)
    inter = jnp.exp(g - m_t)
    e_ig = jnp.exp(ig - m_t)
    floor = jnp.exp(-m_t)
    mn_ref[...] = m_t
    tk_ref[...] = jnp.zeros((LANES, LANES), F32)
    for h in range(MLSTM_HEADS):
        tk_ref[nb * h:nb * (h + 1), :] = z_ref[:, Z_K + h * hd:Z_K + (h + 1) * hd] * (hd ** -0.5)
    k_t = jnp.transpose(tk_ref[...])
    for h in range(MLSTM_HEADS):
        c0 = h * hd
        q_h = z_ref[:, Z_Q + c0:Z_Q + c0 + hd]
        k_h = tk_ref[nb * h:nb * (h + 1), :]
        v_h = z_ref[:, Z_V + c0:Z_V + c0 + hd]
        o_h = z_ref[:, Z_O + c0:Z_O + c0 + hd]
        n_h = n_ref[:, c0:c0 + hd]
        inter_b = jnp.broadcast_to(inter[:, h:h + 1], (nb, hd))
        e_b = jnp.broadcast_to(e_ig[:, h:h + 1], (nb, hd))
        floor_b = jnp.broadcast_to(floor[:, h:h + 1], (nb, hd))
        v_w = e_b * v_h
        q_b = q_h.astype(BF16)
        q_c = jnp.zeros((nb, hd), F32)
        for s in range(nb):
            c_sh = c_ref[s, h]
            q_c = jnp.where(seq_id == s, _dot(q_b, c_sh.astype(BF16)), q_c)
            col = nb * h + s
            cn_ref[s, h] = inter_b[s:s + 1, :] * c_sh + k_t[:, col:col + 1] * v_w[s:s + 1, :]
        wts = e_b * jnp.sum(q_h * k_h, axis=1, keepdims=True)
        num = inter_b * q_c + wts * v_h
        den = inter_b * jnp.sum(q_h * n_h, axis=1, keepdims=True) + wts
        hid = num / jnp.maximum(jnp.abs(den), floor_b)
        hid = _rms(hid, mnorm_ref[:, c0:c0 + hd])
        y_ref[:, POOL_WIDTH + c0:POOL_WIDTH + c0 + hd] = (jax.nn.sigmoid(o_h) * hid).astype(BF16)
        nn_ref[:, c0:c0 + hd] = inter_b * n_h + e_b * k_h


def _mixer_sample(z, sp_t, c_all, layer, c_new_all, n_state, m_pad, consts):
    nseq = z.shape[0]
    nb = SAMPLE_BLOCK
    hd = MLSTM_HEAD_DIM
    const_specs = [pl.BlockSpec(a.shape, lambda j, nd=a.ndim: (0,) * nd) for a in consts]
    c_spec = pl.BlockSpec((None, nb, MLSTM_HEADS, hd, hd), lambda j: (layer, j, 0, 0, 0))
    aliases = {} if c_new_all is None else {5: 1}
    return pl.pallas_call(
        _mixer_sample_kernel,
        grid=(nseq // nb,),
        in_specs=[pl.BlockSpec((nb, Z_WIDTH), lambda j: (j, 0)),
                  pl.BlockSpec((POOL_STATE, nb, POOL_WIDTH), lambda j: (0, j, 0)),
                  c_spec,
                  pl.BlockSpec((nb, MLSTM_WIDTH), lambda j: (j, 0)),
                  pl.BlockSpec((nb, LANES), lambda j: (j, 0)),
                  pl.BlockSpec(memory_space=pl.ANY)] + const_specs,
        out_specs=[pl.BlockSpec((nb, D_MODEL), lambda j: (j, 0)),
                   c_spec,
                   pl.BlockSpec((nb, MLSTM_WIDTH), lambda j: (j, 0)),
                   pl.BlockSpec((nb, LANES), lambda j: (j, 0)),
                   pl.BlockSpec((nb, GMLP_WIDTH), lambda j: (j, 0))],
        out_shape=[jax.ShapeDtypeStruct((nseq, D_MODEL), BF16),
                   jax.ShapeDtypeStruct(c_all.shape, F32),
                   jax.ShapeDtypeStruct((nseq, MLSTM_WIDTH), F32),
                   jax.ShapeDtypeStruct((nseq, LANES), F32),
                   jax.ShapeDtypeStruct((nseq, GMLP_WIDTH), F32)],
        scratch_shapes=[pltpu.VMEM((LANES, LANES), F32)],
        input_output_aliases=aliases,
        compiler_params=_params("parallel"),
        name="mixer_sample",
    )(z, sp_t, c_all, n_state, m_pad, c_all if c_new_all is None else c_new_all, *consts)


def _proj_norm_res_kernel(y_ref, x_ref, w_ref, nw_ref, o_ref):
    o_ref[...] = x_ref[...] + _rms(_dot(y_ref[...], w_ref[...]), nw_ref[...])


def _proj_norm_res(y, x, w, nw, tm):
    t = x.shape[0]
    return pl.pallas_call(
        _proj_norm_res_kernel,
        grid=(t // tm,),
        in_specs=[pl.BlockSpec((tm, D_MODEL), lambda i: (i, 0)),
                  pl.BlockSpec((tm, D_MODEL), lambda i: (i, 0)),
                  pl.BlockSpec((D_MODEL, D_MODEL), lambda i: (0, 0)),
                  pl.BlockSpec((1, D_MODEL), lambda i: (0, 0))],
        out_specs=pl.BlockSpec((tm, D_MODEL), lambda i: (i, 0)),
        out_shape=jax.ShapeDtypeStruct((t, D_MODEL), F32),
        compiler_params=_params("parallel"),
        name="out_proj",
    )(y, x, w, nw)


def _dense_layer_kernel(y_ref, x_ref, p_ref, wout_ref, nmix_ref, npre_ref, npost_ref, nple_ref,
                        wg_ref, wu_ref, wd_ref, pg_ref, pp_ref, o_ref):
    x1 = x_ref[...] + _rms(_dot(y_ref[...], wout_ref[...]), nmix_ref[...])
    h = _rms(x1, npre_ref[...]).astype(BF16)
    y = None
    for f0 in range(0, D_FF, FF_CHUNK):
        fw = min(FF_CHUNK, D_FF - f0)
        gate = _dot(h, wg_ref[:, f0:f0 + fw])
        up = _dot(h, wu_ref[:, f0:f0 + fw])
        act = (gate * jax.nn.sigmoid(gate) * up).astype(BF16)
        part = _dot(act, wd_ref[f0:f0 + fw, :])
        y = part if y is None else y + part
    x2 = x1 + _rms(y, npost_ref[...])
    gate = jax.nn.sigmoid(_dot(_rms(x2, nple_ref[...]).astype(BF16), pg_ref[...]))
    o_ref[...] = x2 + gate * _dot(p_ref[...].astype(BF16), pp_ref[...])


def _dense_layer(y, x, p, w_out, nmix, npre, npost, nple, wg, wu, wd, ple_g, ple_p, tm):
    t = x.shape[0]
    return pl.pallas_call(
        _dense_layer_kernel,
        grid=(t // tm,),
        in_specs=[pl.BlockSpec((tm, D_MODEL), lambda i: (i, 0)),
                  pl.BlockSpec((tm, D_MODEL), lambda i: (i, 0)),
                  pl.BlockSpec((tm, PLE_DIM), lambda i: (i, 0)),
                  _resident((D_MODEL, D_MODEL)),
                  _resident((1, D_MODEL)), _resident((1, D_MODEL)), _resident((1, D_MODEL)), _resident((1, D_MODEL)),
                  _resident((D_MODEL, D_FF)), _resident((D_MODEL, D_FF)), _resident((D_FF, D_MODEL)),
                  _resident((D_MODEL, D_MODEL)), _resident((PLE_DIM, D_MODEL))],
        out_specs=pl.BlockSpec((tm, D_MODEL), lambda i: (i, 0)),
        out_shape=jax.ShapeDtypeStruct((t, D_MODEL), F32),
        compiler_params=_params("parallel"),
        name="dense_layer",
    )(y, x, p, w_out, nmix, npre, npost, nple, wg, wu, wd, ple_g, ple_p)


def _router_gates(h, rw_ref, rb_ref):
    shape = (h.shape[0], LANES)
    lane = lax.broadcasted_iota(jnp.int32, shape, 1)
    lane_f = lane.astype(F32)
    logits = jnp.where(lane < N_EXPERTS, _dot(h, rw_ref[...]) + rb_ref[...], -jnp.inf)
    l1 = jnp.max(logits, axis=-1, keepdims=True)
    i1 = jnp.min(jnp.where(logits == l1, lane_f, float(LANES)), axis=-1, keepdims=True)
    rest = jnp.where(lane_f == i1, -jnp.inf, logits)
    l2 = jnp.max(rest, axis=-1, keepdims=True)
    i2 = jnp.min(jnp.where(rest == l2, lane_f, float(LANES)), axis=-1, keepdims=True)
    e2 = jnp.exp(l2 - l1)
    total = 1.0 + e2
    return jnp.where(lane_f == i1, 1.0 / total, 0.0) + jnp.where(lane_f == i2, e2 / total, 0.0)


def _ffn_moe_kernel(x_ref, npre_ref, npost_ref, rw_ref, rb_ref, wg_ref, wu_ref, wd_ref,
                    o_ref, h_ref, acc_ref, gates_ref):
    e = pl.program_id(1)

    @pl.when(e == 0)
    def _():
        h_ref[...] = _rms(x_ref[...], npre_ref[...]).astype(BF16)
        acc_ref[...] = jnp.zeros(acc_ref.shape, F32)
        gates_ref[...] = _router_gates(h_ref[...], rw_ref, rb_ref)

    h = h_ref[...]
    lane = lax.broadcasted_iota(jnp.int32, gates_ref.shape, 1)
    gate_col = jnp.sum(jnp.where(lane == e, gates_ref[...], 0.0), axis=-1, keepdims=True)
    y = None
    for f0, fw in FF_EXPERT_CHUNKS:
        gate = _dot(h, wg_ref[:, f0:f0 + fw].astype(BF16))
        up = _dot(h, wu_ref[:, f0:f0 + fw].astype(BF16))
        act = (gate * jax.nn.sigmoid(gate) * up).astype(BF16)
        part = _dot(act, wd_ref[f0:f0 + fw, :].astype(BF16))
        y = part if y is None else y + part
    acc_ref[...] += gate_col * y

    @pl.when(e == pl.num_programs(1) - 1)
    def _():
        o_ref[...] = x_ref[...] + _rms(acc_ref[...], npost_ref[...])


def _ffn_moe(x, npre, npost, rw, rb, wg, wu, wd, tm):
    t = x.shape[0]
    return pl.pallas_call(
        _ffn_moe_kernel,
        grid=(t // tm, N_EXPERTS),
        in_specs=[pl.BlockSpec((tm, D_MODEL), lambda i, e: (i, 0)),
                  pl.BlockSpec((1, D_MODEL), lambda i, e: (0, 0)),
                  pl.BlockSpec((1, D_MODEL), lambda i, e: (0, 0)),
                  pl.BlockSpec((D_MODEL, LANES), lambda i, e: (0, 0)),
                  pl.BlockSpec((1, LANES), lambda i, e: (0, 0)),
                  pl.BlockSpec((None, D_MODEL, D_FF_EXPERT), lambda i, e: (e, 0, 0)),
                  pl.BlockSpec((None, D_MODEL, D_FF_EXPERT), lambda i, e: (e, 0, 0)),
                  pl.BlockSpec((None, D_FF_EXPERT, D_MODEL), lambda i, e: (e, 0, 0))],
        out_specs=pl.BlockSpec((tm, D_MODEL), lambda i, e: (i, 0)),
        out_shape=jax.ShapeDtypeStruct((t, D_MODEL), F32),
        scratch_shapes=[pltpu.VMEM((tm, D_MODEL), BF16), pltpu.VMEM((tm, D_MODEL), F32),
                        pltpu.VMEM((tm, LANES), F32)],
        compiler_params=_params("parallel", "arbitrary", vmem_limit=VMEM_LIMIT_EXPERT_WEIGHTS),
        name="ffn_moe",
    )(x, npre, npost, rw, rb, wg, wu, wd)


ROUTE_TILE = 256
ROW_ALIGN = 16
ROUTE_SEG = ROUTE_TILE
ROUTE_PACK = 2 * ROUTE_TILE + N_EXPERTS * ROW_ALIGN
ROUTE_W = D_MODEL + 3 * LANES
ROUTE_BLOCK = 512
ROUTE_SEG_SHORT = 96
ROUTE_SHORT_MAX = ROUTE_SEG_SHORT


def _route_region(n_tokens):
    rows = n_tokens + (n_tokens // ROUTE_TILE) * (ROW_ALIGN - 1) + ROUTE_SEG + ROUTE_BLOCK
    return -(-rows // ROUTE_BLOCK) * ROUTE_BLOCK


def _lane_scalar(row, lane, e):
    return jnp.sum(jnp.where(lane == e, row, 0.0)).astype(jnp.int32)


def _route_kernel(y_ref, x_ref, wout_ref, nmix_ref, npre_ref, rw_ref, rb_ref,
                  x1_ref, slot_ref, stats_ref, srt_hbm,
                  stage_ref, runv_ref, run_ref, short_ref, sem, *, region):
    i = pl.program_id(0)
    last = pl.num_programs(0) - 1
    cur = i % 2

    @pl.when(i == 0)
    def _():
        runv_ref[...] = jnp.zeros(runv_ref.shape, F32)
        stage_ref[:, ROUTE_PACK:, :] = jnp.zeros((2, ROUTE_SEG, ROUTE_W), BF16)
        for e in range(N_EXPERTS):
            run_ref[e] = 0

    x1 = x_ref[...] + _rms(_dot(y_ref[...], wout_ref[...]), nmix_ref[...])
    x1_ref[...] = x1
    h = _rms(x1, npre_ref[...]).astype(BF16)
    gates = _router_gates(h, rw_ref, rb_ref)
    sel = gates > 0.0
    ones = jnp.where(sel, 1.0, 0.0)
    trow = lax.broadcasted_iota(jnp.int32, (ROUTE_TILE, ROUTE_TILE), 0)
    tcol = lax.broadcasted_iota(jnp.int32, (ROUTE_TILE, ROUTE_TILE), 1)
    before = jnp.where(tcol < trow, 1.0, 0.0).astype(BF16)
    rank = _dot(before, ones.astype(BF16))
    cnt = jnp.sum(ones, axis=0, keepdims=True)
    cnt_pad = jnp.floor((cnt + (ROW_ALIGN - 1)) * (1.0 / ROW_ALIGN)) * ROW_ALIGN
    lrow = lax.broadcasted_iota(jnp.int32, (LANES, LANES), 0)
    lcol = lax.broadcasted_iota(jnp.int32, (LANES, LANES), 1)
    lower = jnp.where(lrow < lcol, 1.0, 0.0).astype(BF16)
    off = _dot(jnp.broadcast_to(cnt_pad, (SUBLANES, LANES)).astype(BF16), lower)[0:1, :]
    lane = lax.broadcasted_iota(jnp.int32, (1, LANES), 1)
    src_rows = [_lane_scalar(off, lane, e) for e in range(N_EXPERTS)]
    seg_lens = [_lane_scalar(cnt_pad, lane, e) for e in range(N_EXPERTS)]
    short = (jnp.max(cnt_pad) <= ROUTE_SHORT_MAX).astype(jnp.int32)
    slot_ref[...] = jnp.where(sel, rank, -1.0)
    stats_ref[...] = jnp.zeros(stats_ref.shape, F32)
    stats_ref[0:1, :] = runv_ref[...]
    stats_ref[1:2, :] = cnt
    runv_ref[...] = runv_ref[...] + cnt_pad

    pos = jnp.where(sel, rank + off, -1.0)
    pos_t = jnp.concatenate([jnp.transpose(pos[0:LANES, :]), jnp.transpose(pos[LANES:2 * LANES, :])], axis=1)
    pos_a = jnp.max(pos_t, axis=0, keepdims=True)
    pos_b = jnp.max(jnp.where(pos_t == pos_a, -1.0, pos_t), axis=0, keepdims=True)
    prow = lax.broadcasted_iota(jnp.int32, (ROUTE_PACK, ROUTE_TILE), 0).astype(F32)
    perm = jnp.where((prow == pos_a) | (prow == pos_b), 1.0, 0.0).astype(BF16)
    g_hi, g_mid, g_lo = _split3(gates)
    rows = _dot(perm, jnp.concatenate([h, g_hi, g_mid, g_lo], axis=1))
    stage_ref[cur, 0:ROUTE_PACK, :] = rows.astype(BF16)

    def segment_copy(e, src_row, dst_row, slot, rows=ROUTE_SEG):
        return pltpu.make_async_copy(
            stage_ref.at[slot, pl.ds(pl.multiple_of(src_row, ROW_ALIGN), rows), :],
            srt_hbm.at[pl.ds(pl.multiple_of(dst_row, ROW_ALIGN), rows), :],
            sem.at[e])

    def for_each_segment(is_short, action):
        for rows, flag in ((ROUTE_SEG_SHORT, 1), (ROUTE_SEG, 0)):
            @pl.when(is_short == flag)
            def _():
                for e in range(N_EXPERTS):
                    action(e, rows)

    @pl.when(i > 0)
    def _():
        for_each_segment(short_ref[0], lambda e, rows: segment_copy(e, 0, 0, 1 - cur, rows).wait())

    dst_rows = [e * region + run_ref[e] for e in range(N_EXPERTS)]
    for_each_segment(short, lambda e, rows: segment_copy(e, src_rows[e], dst_rows[e], cur, rows).start())
    for e in range(N_EXPERTS):
        run_ref[e] = run_ref[e] + seg_lens[e]
    short_ref[0] = short

    @pl.when(i == last)
    def _():
        for_each_segment(short, lambda e, rows: segment_copy(e, 0, 0, cur, rows).wait())
        stage_ref[1 - cur, 0:ROUTE_SEG, :] = jnp.zeros((ROUTE_SEG, ROUTE_W), BF16)
        for part in range(ROUTE_BLOCK // ROUTE_SEG):
            for e in range(N_EXPERTS):
                segment_copy(e, 0, e * region + run_ref[e] + part * ROUTE_SEG, 1 - cur).start()
            for e in range(N_EXPERTS):
                segment_copy(e, 0, 0, 1 - cur).wait()


def _route(y, x, w_out, nmix, npre, rw, rb):
    t = x.shape[0]
    nt = t // ROUTE_TILE
    region = _route_region(t)
    return pl.pallas_call(
        functools.partial(_route_kernel, region=region),
        grid=(nt,),
        in_specs=[pl.BlockSpec((ROUTE_TILE, D_MODEL), lambda i: (i, 0)),
                  pl.BlockSpec((ROUTE_TILE, D_MODEL), lambda i: (i, 0)),
                  pl.BlockSpec((D_MODEL, D_MODEL), lambda i: (0, 0)),
                  pl.BlockSpec((1, D_MODEL), lambda i: (0, 0)),
                  pl.BlockSpec((1, D_MODEL), lambda i: (0, 0)),
                  pl.BlockSpec((D_MODEL, LANES), lambda i: (0, 0)),
                  pl.BlockSpec((1, LANES), lambda i: (0, 0))],
        out_specs=[pl.BlockSpec((ROUTE_TILE, D_MODEL), lambda i: (i, 0)),
                   pl.BlockSpec((ROUTE_TILE, LANES), lambda i: (i, 0)),
                   pl.BlockSpec((None, SUBLANES, LANES), lambda i: (i, 0, 0)),
                   pl.BlockSpec(memory_space=pl.ANY)],
        out_shape=[jax.ShapeDtypeStruct((t, D_MODEL), F32),
                   jax.ShapeDtypeStruct((t, LANES), F32),
                   jax.ShapeDtypeStruct((nt, SUBLANES, LANES), F32),
                   jax.ShapeDtypeStruct((N_EXPERTS * region, ROUTE_W), BF16)],
        scratch_shapes=[pltpu.VMEM((2, ROUTE_PACK + ROUTE_SEG, ROUTE_W), BF16),
                        pltpu.VMEM((1, LANES), F32),
                        pltpu.SMEM((N_EXPERTS,), jnp.int32),
                        pltpu.SMEM((1,), jnp.int32),
                        pltpu.SemaphoreType.DMA((N_EXPERTS,))],
        compiler_params=_params("arbitrary"),
        name="moe_route",
    )(y, x, w_out, nmix, npre, rw, rb)


def _experts_kernel(blk_row_ref, blk_e_ref, blk_half_ref, n_used_ref,
                    srt_ref, wg_ref, wu_ref, wd_ref, yhi_ref, ylo_ref):
    k = pl.program_id(0)
    half = ROUTE_BLOCK // 2

    def swiglu_rows(rows):
        h = srt_ref[0:rows, 0:D_MODEL]
        gate3 = (srt_ref[0:rows, D_MODEL:D_MODEL + LANES].astype(F32)
                 + srt_ref[0:rows, D_MODEL + LANES:D_MODEL + 2 * LANES].astype(F32)
                 + srt_ref[0:rows, D_MODEL + 2 * LANES:D_MODEL + 3 * LANES].astype(F32))
        lane = lax.broadcasted_iota(jnp.int32, gate3.shape, 1)
        gate_col = jnp.sum(jnp.where(lane == blk_e_ref[k], gate3, 0.0), axis=-1, keepdims=True)
        y = None
        for f0, fw in FF_EXPERT_CHUNKS:
            gate = _dot(h, wg_ref[:, f0:f0 + fw].astype(BF16))
            up = _dot(h, wu_ref[:, f0:f0 + fw].astype(BF16))
            act = (gate * jax.nn.sigmoid(gate) * up).astype(BF16)
            part = _dot(act, wd_ref[f0:f0 + fw, :].astype(BF16))
            y = part if y is None else y + part
        y = gate_col * y
        hi = y.astype(BF16)
        yhi_ref[0:rows, :] = hi
        ylo_ref[0:rows, :] = (y - hi.astype(F32)).astype(BF16)

    @pl.when((k < n_used_ref[0]) & (blk_half_ref[k] == 0))
    def _():
        swiglu_rows(ROUTE_BLOCK)

    @pl.when((k < n_used_ref[0]) & (blk_half_ref[k] == 1))
    def _():
        swiglu_rows(half)
        yhi_ref[half:, :] = jnp.zeros((ROUTE_BLOCK - half, D_MODEL), BF16)
        ylo_ref[half:, :] = jnp.zeros((ROUTE_BLOCK - half, D_MODEL), BF16)


def _experts(srt, blk_row, blk_e, blk_half, n_used, wg, wu, wd, n_blocks):
    rows = srt.shape[0]
    grid_spec = pltpu.PrefetchScalarGridSpec(
        num_scalar_prefetch=4,
        grid=(n_blocks,),
        in_specs=[pl.BlockSpec((ROUTE_BLOCK, ROUTE_W), lambda k, br, *_: (br[k], 0)),
                  pl.BlockSpec((None, D_MODEL, D_FF_EXPERT), lambda k, br, be, *_: (be[k], 0, 0)),
                  pl.BlockSpec((None, D_MODEL, D_FF_EXPERT), lambda k, br, be, *_: (be[k], 0, 0)),
                  pl.BlockSpec((None, D_FF_EXPERT, D_MODEL), lambda k, br, be, *_: (be[k], 0, 0))],
        out_specs=[pl.BlockSpec((ROUTE_BLOCK, D_MODEL), lambda k, br, *_: (br[k], 0)),
                   pl.BlockSpec((ROUTE_BLOCK, D_MODEL), lambda k, br, *_: (br[k], 0))])
    return pl.pallas_call(
        _experts_kernel,
        grid_spec=grid_spec,
        out_shape=[jax.ShapeDtypeStruct((rows, D_MODEL), BF16), jax.ShapeDtypeStruct((rows, D_MODEL), BF16)],
        compiler_params=_params("arbitrary", vmem_limit=VMEM_LIMIT_EXPERT_WEIGHTS),
        name="moe_experts",
    )(blk_row, blk_e, blk_half, n_used, srt, wg, wu, wd)


def _combine_kernel(src_row_ref, short_ref, x_ref, slot_ref, shift_ref, p_ref, npost_ref, nple_ref,
                    wg_ref, wp_ref, yhi_hbm, ylo_hbm, o_ref, seg_hi_ref, seg_lo_ref, y_ref, sem):
    i = pl.program_id(0)
    nt = pl.num_programs(0)
    cur = i % 2

    def segment_copies(tile, slot, e, rows):
        src = pl.ds(pl.multiple_of(src_row_ref[tile * N_EXPERTS + e], ROW_ALIGN), rows)
        dst = pl.ds(e * rows, rows)
        return (pltpu.make_async_copy(yhi_hbm.at[src, :], seg_hi_ref.at[slot, dst, :], sem.at[slot, 0, e]),
                pltpu.make_async_copy(ylo_hbm.at[src, :], seg_lo_ref.at[slot, dst, :], sem.at[slot, 1, e]))

    def for_each_segment(tile, slot, action):
        for rows, is_short in ((ROUTE_SEG_SHORT, 1), (ROUTE_SEG, 0)):
            @pl.when(short_ref[tile] == is_short)
            def _():
                for e in range(N_EXPERTS):
                    for c in segment_copies(tile, slot, e, rows):
                        action(c)

    @pl.when(i == 0)
    def _():
        for_each_segment(0, 0, lambda c: c.start())

    @pl.when(i + 1 < nt)
    def _():
        for_each_segment(i + 1, 1 - cur, lambda c: c.start())

    for_each_segment(i, cur, lambda c: c.wait())

    slot = slot_ref[...]
    where = jnp.where(slot >= 0.0, slot + shift_ref[...], -1.0)

    def gather(rows):
        seg_lane = lax.broadcasted_iota(jnp.int32, (ROUTE_TILE, rows), 1).astype(F32)
        perm = jnp.concatenate([jnp.where(where[:, e:e + 1] == seg_lane, 1.0, 0.0).astype(BF16)
                                for e in range(N_EXPERTS)], axis=1)
        k = N_EXPERTS * rows
        y_ref[...] = _dot(perm, seg_hi_ref[cur, 0:k, :]) + _dot(perm, seg_lo_ref[cur, 0:k, :])

    @pl.when(short_ref[i] == 1)
    def _():
        gather(ROUTE_SEG_SHORT)

    @pl.when(short_ref[i] == 0)
    def _():
        gather(ROUTE_SEG)

    x = x_ref[...] + _rms(y_ref[...], npost_ref[...])
    gate = jax.nn.sigmoid(_dot(_rms(x, nple_ref[...]).astype(BF16), wg_ref[...]))
    o_ref[...] = x + gate * _dot(p_ref[...].astype(BF16), wp_ref[...])


def _combine(src_row, short, x, slot, shift, p, npost, nple, wg, wp, yhi, ylo):
    t = x.shape[0]
    grid_spec = pltpu.PrefetchScalarGridSpec(
        num_scalar_prefetch=2,
        grid=(t // ROUTE_TILE,),
        in_specs=[pl.BlockSpec((ROUTE_TILE, D_MODEL), lambda i, *_: (i, 0)),
                  pl.BlockSpec((ROUTE_TILE, LANES), lambda i, *_: (i, 0)),
                  pl.BlockSpec((None, 1, LANES), lambda i, *_: (i, 0, 0)),
                  pl.BlockSpec((ROUTE_TILE, PLE_DIM), lambda i, *_: (i, 0)),
                  pl.BlockSpec((1, D_MODEL), lambda i, *_: (0, 0)),
                  pl.BlockSpec((1, D_MODEL), lambda i, *_: (0, 0)),
                  pl.BlockSpec((D_MODEL, D_MODEL), lambda i, *_: (0, 0)),
                  pl.BlockSpec((PLE_DIM, D_MODEL), lambda i, *_: (0, 0)),
                  pl.BlockSpec(memory_space=pl.ANY),
                  pl.BlockSpec(memory_space=pl.ANY)],
        out_specs=pl.BlockSpec((ROUTE_TILE, D_MODEL), lambda i, *_: (i, 0)),
        scratch_shapes=[pltpu.VMEM((2, N_EXPERTS * ROUTE_SEG, D_MODEL), BF16),
                        pltpu.VMEM((2, N_EXPERTS * ROUTE_SEG, D_MODEL), BF16),
                        pltpu.VMEM((ROUTE_TILE, D_MODEL), F32),
                        pltpu.SemaphoreType.DMA((2, 2, N_EXPERTS))])
    return pl.pallas_call(
        _combine_kernel,
        grid_spec=grid_spec,
        out_shape=jax.ShapeDtypeStruct((t, D_MODEL), F32),
        compiler_params=_params("arbitrary"),
        name="moe_combine_ple",
    )(src_row, short, x, slot, shift, p, npost, nple, wg, wp, yhi, ylo)


def _moe_layer_routed(y, x, p, w_out, nmix, npre, npost, nple, rw, rb, wg, wu, wd, ple_g, ple_p):
    t = x.shape[0]
    nt = t // ROUTE_TILE
    region = _route_region(t)
    x, slot, stats, srt = _route(y, x, w_out, nmix, npre, rw, rb)
    base = stats[:, 0, 0:N_EXPERTS].astype(jnp.int32)
    cnt = stats[:, 1, 0:N_EXPERTS].astype(jnp.int32)
    cnt_pad = (cnt + (ROW_ALIGN - 1)) // ROW_ALIGN * ROW_ALIGN
    total = base[-1] + cnt_pad[-1]
    nblk = (total + (ROUTE_BLOCK - 1)) // ROUTE_BLOCK
    cum = jnp.cumsum(nblk)
    n_used = cum[-1]
    max_rows = 2 * t + nt * N_EXPERTS * (ROW_ALIGN - 1)
    n_blocks = max_rows // ROUTE_BLOCK + N_EXPERTS
    kk = jnp.minimum(jnp.arange(n_blocks, dtype=jnp.int32), n_used - 1)
    blk_e = jnp.sum(kk[:, None] >= cum[None, :], axis=1).astype(jnp.int32)
    blk_local = kk - (cum - nblk)[blk_e]
    blk_row = blk_e * (region // ROUTE_BLOCK) + blk_local
    blk_half = (total[blk_e] - blk_local * ROUTE_BLOCK <= ROUTE_BLOCK // 2).astype(jnp.int32)
    yhi, ylo = _experts(srt, blk_row.astype(jnp.int32), blk_e, blk_half, n_used.reshape(1).astype(jnp.int32),
                        wg, wu, wd, n_blocks)
    short = jnp.all(cnt_pad <= ROUTE_SHORT_MAX, axis=1)
    seg_rows = jnp.where(short, ROUTE_SEG_SHORT, ROUTE_SEG)[:, None]
    start = jnp.maximum(jnp.minimum(base, nblk[None, :] * ROUTE_BLOCK - seg_rows), 0)
    src_row = jnp.arange(N_EXPERTS, dtype=jnp.int32)[None, :] * region + start
    first_used = jnp.argmax(nblk > 0).astype(jnp.int32)
    src_row = jnp.where(cnt > 0, src_row, first_used * region).reshape(-1)
    shift = _pad_lanes((base - start).astype(F32)).reshape(nt, 1, LANES)
    return _combine(src_row.astype(jnp.int32), short.astype(jnp.int32), x, slot, shift, p, npost, nple,
                    ple_g, ple_p, yhi, ylo)


def _ple_kernel(x_ref, p_ref, nw_ref, wg_ref, wp_ref, o_ref):
    x = x_ref[...]
    gate = jax.nn.sigmoid(_dot(_rms(x, nw_ref[...]).astype(BF16), wg_ref[...]))
    o_ref[...] = x + gate * _dot(p_ref[...].astype(BF16), wp_ref[...])


def _ple(x, p, nw, wg, wp, tm):
    t = x.shape[0]
    return pl.pallas_call(
        _ple_kernel,
        grid=(t // tm,),
        in_specs=[pl.BlockSpec((tm, D_MODEL), lambda i: (i, 0)),
                  pl.BlockSpec((tm, PLE_DIM), lambda i: (i, 0)),
                  pl.BlockSpec((1, D_MODEL), lambda i: (0, 0)),
                  pl.BlockSpec((D_MODEL, D_MODEL), lambda i: (0, 0)),
                  pl.BlockSpec((PLE_DIM, D_MODEL), lambda i: (0, 0))],
        out_specs=pl.BlockSpec((tm, D_MODEL), lambda i: (i, 0)),
        out_shape=jax.ShapeDtypeStruct((t, D_MODEL), F32),
        compiler_params=_params("parallel"),
        name="ple",
    )(x, p, nw, wg, wp)


def _pad_lanes(a, width=LANES):
    return jnp.pad(a, [(0, 0)] * (a.ndim - 1) + [(0, width - a.shape[-1])])


def _block_diag(blocks):
    g, d, _ = blocks.shape
    out = jnp.zeros((g * d, g * d), blocks.dtype)
    for i in range(g):
        out = out.at[i * d:(i + 1) * d, i * d:(i + 1) * d].set(blocks[i])
    return out


def _row(a):
    return a.reshape(1, -1).astype(F32)


def kernel(x_prompt, x_sample, state_pool, state_mlstm_C, state_mlstm_n, state_mlstm_m, p_prompt, p_sample,
           norm_mix_pre, norm_mix_post, norm_ffn_pre, norm_ffn_post, norm_ple, w_in, pool_w, pool_scale,
           mlstm_b_i, mlstm_b_f, mlstm_norm_w, gmlp_norm_w, gmlp_ws, gmlp_bs, w_out,
           ffn_w_gate, ffn_w_up, ffn_w_down, moe_router_w, moe_router_b, moe_w_gate, moe_w_up, moe_w_down,
           ple_w_gate, ple_w_proj):
    batch, seq, _ = x_prompt.shape
    nseq = x_sample.shape[0]
    xp = x_prompt.reshape(batch * seq, D_MODEL)
    xs = x_sample.reshape(nseq, D_MODEL)
    gmean = _block_diag(jnp.full((GMLP_GROUPS, GMLP_GROUP_DIM, GMLP_GROUP_DIM), 1.0 / GMLP_GROUP_DIM, BF16))

    w_in_t = jnp.swapaxes(w_in, 1, 2)

    pools_p, cs_p, ns_p, ms_p = [], [], [], []
    pools_s, ns_s, ms_s, gvs_s = [], [], [], []
    c_new_s = None
    for i in range(DEPTH):
        w_out_b = w_out[i].astype(BF16)
        poolw = _block_diag(pool_w[i]).astype(BF16)
        shared = [poolw, _row(pool_scale[i]), _pad_lanes(_row(mlstm_b_i[i])), _pad_lanes(_row(mlstm_b_f[i])),
                  _row(mlstm_norm_w[i]), _row(gmlp_norm_w[i])]
        gbs_full = jnp.repeat(gmlp_bs[i].T, GMLP_GROUP_DIM, axis=1)
        consts_p = shared + [gmlp_ws[i], gbs_full, gmean]
        gw0 = jnp.repeat(gmlp_ws[i][:, 0, 0], GMLP_GROUP_DIM).reshape(1, GMLP_WIDTH)
        consts_s = shared + [gw0, gbs_full[0:1, :], gmean]
        ple_g = ple_w_gate[i].astype(BF16)
        ple_p = ple_w_proj[i].astype(BF16)
        j = i // 2
        if i % 2 == 0:
            ffn_g, ffn_u, ffn_d = (ffn_w_gate[j].astype(BF16), ffn_w_up[j].astype(BF16),
                                   ffn_w_down[j].astype(BF16))
        else:
            rw = _pad_lanes(moe_router_w[j]).astype(BF16)
            rb = _pad_lanes(_row(moe_router_b[j]))
            moe_g, moe_u, moe_d = moe_w_gate[j], moe_w_up[j], moe_w_down[j]

        z = _norm_matmul(xp, _row(norm_mix_pre[i]), w_in_t, i, TM_PROMPT)
        y, cn_new, m_new = _mixer_prompt(z, consts_p, batch, seq)
        pools_p.append(z.reshape(batch, seq, Z_WIDTH)[:, seq - POOL_STATE:, 0:POOL_WIDTH])
        cs_p.append(cn_new[..., 0:MLSTM_HEAD_DIM])
        ns_p.append(cn_new[..., MLSTM_HEAD_DIM])
        ms_p.append(m_new[:, 0, 0:MLSTM_HEADS])
        y = y.reshape(batch * seq, D_MODEL)
        pp = p_prompt[i].reshape(batch * seq, PLE_DIM)
        norms = (_row(norm_mix_post[i]), _row(norm_ffn_pre[i]), _row(norm_ffn_post[i]), _row(norm_ple[i]))
        if i % 2 == 0:
            xp = _dense_layer(y, xp, pp, w_out_b, *norms, ffn_g, ffn_u, ffn_d, ple_g, ple_p, TM_PROMPT)
        else:
            xp = _moe_layer_routed(y, xp, pp, w_out_b, *norms, rw, rb, moe_g, moe_u, moe_d, ple_g, ple_p)

        z = _norm_matmul(xs, _row(norm_mix_pre[i]), w_in_t, i, nseq)
        sp_t = jnp.transpose(state_pool[i], (1, 0, 2))
        y, c_new_s, n_new, m_new, gv = _mixer_sample(z, sp_t, state_mlstm_C, i, c_new_s,
                                                     state_mlstm_n[i].reshape(nseq, MLSTM_WIDTH),
                                                     _pad_lanes(state_mlstm_m[i]), consts_s)
        pools_s.append(jnp.concatenate([state_pool[i][:, 1:], z[:, None, 0:POOL_WIDTH]], axis=1))
        ns_s.append(n_new.reshape(nseq, MLSTM_HEADS, MLSTM_HEAD_DIM))
        ms_s.append(m_new[:, 0:MLSTM_HEADS])
        gvs_s.append(gv[:, None, :])
        ps = p_sample[i].reshape(nseq, PLE_DIM)
        if i % 2 == 0:
            xs = _dense_layer(y, xs, ps, w_out_b, *norms, ffn_g, ffn_u, ffn_d, ple_g, ple_p, nseq)
        else:
            xs = _proj_norm_res(y, xs, w_out_b, norms[0], nseq)
            xs = _ffn_moe(xs, norms[1], norms[2], rw, rb, moe_g, moe_u, moe_d, nseq)
            xs = _ple(xs, ps, norms[3], ple_g, ple_p, nseq)

    return (xp.reshape(batch, seq, D_MODEL), xs.reshape(nseq, 1, D_MODEL),
            jnp.stack(pools_p), jnp.stack(cs_p), jnp.stack(ns_p), jnp.stack(ms_p),
            jnp.stack(pools_s), c_new_s, jnp.stack(ns_s), jnp.stack(ms_s), jnp.stack(gvs_s))
```

```python
import functools

import jax
import jax.numpy as jnp
from jax import lax
from jax.experimental import pallas as pl
from jax.experimental.pallas import tpu as pltpu

F32 = jnp.float32
BF16 = jnp.bfloat16

D_MODEL = 1024
DEPTH = 2
POOL_WIDTH = 256
POOL_WINDOWS = (2, 4, 8, 16)
POOL_GROUP_DIM = 64
POOL_STATE = 15
POOL_PREV_ROWS = 16
MLSTM_WIDTH = 512
MLSTM_HEADS = 4
MLSTM_HEAD_DIM = 128
CHUNK = 128
GMLP_WIDTH = 256
GMLP_GROUPS = 4
GMLP_GROUP_DIM = 64
D_FF = 2816
N_EXPERTS = 8
D_FF_EXPERT = 1408
PLE_DIM = 256
RMS_EPS = 1e-6
PAST_LEN = 16384

LANES = 128
SUBLANES = 8
VMEM_LIMIT = 48 * 1024 * 1024
VMEM_LIMIT_EXPERT_WEIGHTS = 58 * 1024 * 1024

Z_POOL = 0
Z_Q = 256
Z_K = 768
Z_V = 1280
Z_O = 1792
Z_GU = 2304
Z_GV = 2560
Z_GATES = 2816
Z_WIDTH = 2944
Y_GMLP = POOL_WIDTH + MLSTM_WIDTH
Z_CHUNK = 512
W_IN_IG = 2304
W_IN_GU = 2312
W_IN_WIDTH = 2824

TM_PROMPT = 512
FF_CHUNK = 512
FF_EXPERT_CHUNKS = ((0, 512), (512, 512), (1024, 384))
SAMPLE_BLOCK = 32
PROMPT_SEQ_PER_STEP = 2


def _params(*semantics, vmem_limit=VMEM_LIMIT):
    return pltpu.CompilerParams(dimension_semantics=semantics, vmem_limit_bytes=vmem_limit)


def _rms(x, w):
    return x * lax.rsqrt(jnp.mean(x * x, axis=-1, keepdims=True) + RMS_EPS) * w


def _log_sigmoid(x):
    return jnp.minimum(x, 0.0) - jnp.log1p(jnp.exp(-jnp.abs(x)))


def _dot(a, b):
    return jnp.dot(a, b, preferred_element_type=F32)


def _split3(x):
    hi = x.astype(BF16)
    rest = x - hi.astype(F32)
    mid = rest.astype(BF16)
    lo = (rest - mid.astype(F32)).astype(BF16)
    return hi, mid, lo


def _resident(shape):
    return pl.BlockSpec(shape, lambda i: (0,) * len(shape), pipeline_mode=pl.Buffered(1))


def _norm_matmul_kernel(x_ref, nw_ref, wt_ref, o_ref, h_ref, wz_ref):
    @pl.when(pl.program_id(0) == 0)
    def _():
        wz_ref[0:Z_GU, :] = wt_ref[0:Z_GU, :].astype(BF16)
        wz_ref[Z_GU:Z_GATES, :] = wt_ref[W_IN_GU:W_IN_GU + 2 * GMLP_WIDTH, :].astype(BF16)
        gates = wt_ref[W_IN_IG:W_IN_IG + 2 * MLSTM_HEADS, :]
        zeros = jnp.zeros((LANES - 2 * MLSTM_HEADS, D_MODEL), F32)
        wz_ref[Z_GATES:Z_WIDTH, :] = jnp.concatenate([gates, zeros], axis=0).astype(BF16)

    h_ref[...] = _rms(x_ref[...], nw_ref[...]).astype(BF16)
    for n0 in range(0, Z_WIDTH, Z_CHUNK):
        nw = min(Z_CHUNK, Z_WIDTH - n0)
        o_ref[:, n0:n0 + nw] = lax.dot_general(h_ref[...], wz_ref[n0:n0 + nw, :],
                                               (((1,), (1,)), ((), ())), preferred_element_type=F32)


def _norm_matmul(x, nw, wt_all, layer, tm):
    t = x.shape[0]
    return pl.pallas_call(
        _norm_matmul_kernel,
        grid=(t // tm,),
        in_specs=[pl.BlockSpec((tm, D_MODEL), lambda i: (i, 0)),
                  _resident((1, D_MODEL)),
                  pl.BlockSpec((None, W_IN_WIDTH, D_MODEL), lambda i: (layer, 0, 0),
                               pipeline_mode=pl.Buffered(1))],
        out_specs=pl.BlockSpec((tm, Z_WIDTH), lambda i: (i, 0)),
        out_shape=jax.ShapeDtypeStruct((t, Z_WIDTH), F32),
        scratch_shapes=[pltpu.VMEM((tm, D_MODEL), BF16), pltpu.VMEM((Z_WIDTH, D_MODEL), BF16)],
        compiler_params=_params("arbitrary"),
        name="norm_in_proj",
    )(x, nw, wt_all)


def _gate_terms(z_ref, bi_ref, bf_ref):
    gates = z_ref[:, Z_GATES:Z_GATES + LANES]
    forget = pltpu.roll(gates, LANES - MLSTM_HEADS, 1)
    return gates + bi_ref[...], _log_sigmoid(forget + bf_ref[...])


def _group_rms(v, gmean, w):
    hi, mid, lo = _split3(v * v)
    ms = _dot(hi, gmean) + _dot(mid, gmean) + _dot(lo, gmean)
    return v * lax.rsqrt(ms + RMS_EPS) * w


def _pool_tile(ext_ref, u_tile, col0, w_lo, w_hi, pos):
    acc = u_tile
    sums = {}
    for shift in range(1, w_hi):
        acc = acc + ext_ref[pl.ds(POOL_PREV_ROWS - shift, CHUNK), col0:col0 + LANES]
        if shift + 1 in (w_lo, w_hi):
            sums[shift + 1] = acc
    cnt_lo = jnp.minimum(w_lo, pos + 1).astype(F32)
    cnt_hi = jnp.minimum(w_hi, pos + 1).astype(F32)
    lane = lax.broadcasted_iota(jnp.int32, (CHUNK, LANES), 1)
    return jnp.where(lane < POOL_GROUP_DIM, sums[w_lo] / cnt_lo, sums[w_hi] / cnt_hi) - u_tile


def _mixer_prompt_kernel(z_ref, *refs):
    consts = refs[:9]
    y_ref, cn_ref, m_ref, ext_ref = refs[9:]

    @pl.when(pl.program_id(1) == 0)
    def _():
        ext_ref[:, 0:POOL_PREV_ROWS, :] = jnp.zeros((PROMPT_SEQ_PER_STEP, POOL_PREV_ROWS, POOL_WIDTH), F32)
        cn_ref[...] = jnp.zeros(cn_ref.shape, F32)
        m_ref[...] = jnp.zeros(m_ref.shape, F32)

    for i in range(PROMPT_SEQ_PER_STEP):
        _mixer_prompt_body(z_ref.at[i], *consts, y_ref.at[i], cn_ref.at[i], m_ref.at[i], ext_ref.at[i])


def _mixer_prompt_body(z_ref, poolw_ref, pscale_ref, bi_ref, bf_ref, mnorm_ref, gnorm_ref,
                       gws_ref, gbs_ref, gmean_ref,
                       y_ref, cn_ref, m_ref, ext_ref):
    chunk = pl.program_id(1)
    row = lax.broadcasted_iota(jnp.int32, (CHUNK, CHUNK), 0)
    col = lax.broadcasted_iota(jnp.int32, (CHUNK, CHUNK), 1)
    causal = col <= row
    lane = col

    ext_ref[POOL_PREV_ROWS:POOL_PREV_ROWS + CHUNK, :] = z_ref[:, Z_POOL:Z_POOL + POOL_WIDTH]
    pos = chunk * CHUNK + lax.broadcasted_iota(jnp.int32, (CHUNK, 1), 0)
    pooled = []
    for tile in range(2):
        col0 = tile * LANES
        u_tile = z_ref[:, Z_POOL + col0:Z_POOL + col0 + LANES]
        pooled.append(_pool_tile(ext_ref, u_tile, col0, POOL_WINDOWS[2 * tile],
                                 POOL_WINDOWS[2 * tile + 1], pos))
    pooled = jnp.concatenate(pooled, axis=1).astype(BF16)
    y_pool = _dot(pooled, poolw_ref[...]) * pscale_ref[...]
    y_ref[:, 0:POOL_WIDTH] = y_pool.astype(BF16)
    ext_ref[0:POOL_PREV_ROWS, :] = ext_ref[CHUNK:CHUNK + POOL_PREV_ROWS, :]

    vn = _group_rms(z_ref[:, Z_GV:Z_GV + GMLP_WIDTH], gmean_ref[...], gnorm_ref[...]).astype(BF16)
    for tile in range(2):
        col0 = tile * LANES
        vt = vn[:, col0:col0 + LANES]
        w_a = jnp.where(causal, gws_ref[2 * tile], 0.0).astype(BF16)
        w_b = jnp.where(causal, gws_ref[2 * tile + 1], 0.0).astype(BF16)
        mixed = jnp.where(lane < GMLP_GROUP_DIM, _dot(w_a, vt), _dot(w_b, vt))
        gu = z_ref[:, Z_GU + col0:Z_GU + col0 + LANES]
        y_g = gu * (mixed + gbs_ref[:, col0:col0 + LANES])
        y_ref[:, Y_GMLP + col0:Y_GMLP + col0 + LANES] = y_g.astype(BF16)

    ig, lf = _gate_terms(z_ref, bi_ref, bf_ref)
    tri =jnp.where(causal, 1.0, 0.0).astype(BF16)
    lf_hi, lf_mid, lf_lo = _split3(lf)
    b = _dot(tri, lf_hi) + _dot(tri, lf_mid) + _dot(tri, lf_lo)
    m_prev = m_ref[...]
    g = b + m_prev
    r_t = jnp.transpose(ig - b)
    b_last = b[CHUNK - 1:CHUNK, :]
    ones_col = jnp.where(lane == 0, 1.0, 0.0).astype(BF16)
    m_new_row = m_prev
    for h in range(MLSTM_HEADS):
        c0 = h * MLSTM_HEAD_DIM
        q = z_ref[:, Z_Q + c0:Z_Q + c0 + MLSTM_HEAD_DIM].astype(BF16)
        k = z_ref[:, Z_K + c0:Z_K + c0 + MLSTM_HEAD_DIM] * (MLSTM_HEAD_DIM ** -0.5)
        v = z_ref[:, Z_V + c0:Z_V + c0 + MLSTM_HEAD_DIM].astype(BF16)
        o = z_ref[:, Z_O + c0:Z_O + c0 + MLSTM_HEAD_DIM]
        b_col = b[:, h:h + 1]
        dmat = jnp.where(causal, b_col + r_t[h:h + 1, :], -jnp.inf)
        g_col = g[:, h:h + 1]
        m_t = jnp.maximum(g_col, jnp.max(dmat, axis=1, keepdims=True))
        scores = lax.dot_general(q, k.astype(BF16), (((1,), (1,)), ((), ())),
                                 preferred_element_type=F32)
        wts = jnp.exp(dmat - m_t) * scores
        inter = jnp.exp(g_col - m_t)
        cn_h = cn_ref[h]
        q_cn = _dot(q, cn_h.astype(BF16))
        num = inter * q_cn[:, 0:MLSTM_HEAD_DIM] + _dot(wts.astype(BF16), v)
        den = inter * q_cn[:, MLSTM_HEAD_DIM:MLSTM_HEAD_DIM + 1] + jnp.sum(wts, axis=1, keepdims=True)
        hid = num / jnp.maximum(jnp.abs(den), jnp.exp(-m_t))
        hid = _rms(hid, mnorm_ref[:, c0:c0 + MLSTM_HEAD_DIM])
        y_ref[:, POOL_WIDTH + c0:POOL_WIDTH + c0 + MLSTM_HEAD_DIM] = (jax.nn.sigmoid(o) * hid).astype(BF16)
        m_new = m_t[CHUNK - 1:CHUNK, :]
        bl = b_last[:, h:h + 1]
        decay = jnp.exp(bl + m_prev[:, h:h + 1] - m_new)
        w_s = jnp.exp(bl - b_col + ig[:, h:h + 1] - m_new)
        kw = (k * w_s).astype(BF16)
        v_ext = jnp.concatenate([v, ones_col], axis=1)
        cn_ref[h] = decay * cn_h + lax.dot_general(kw, v_ext, (((0,), (0,)), ((), ())),
                                                   preferred_element_type=F32)
        m_new_row = jnp.where(lane[0:1, :] == h, m_new, m_new_row)
    m_ref[...] = m_new_row


def _mixer_prompt(z, consts, batch, seq):
    nc = seq // CHUNK
    hd = MLSTM_HEAD_DIM
    z3 = z.reshape(batch, seq, Z_WIDTH)
    ns = PROMPT_SEQ_PER_STEP
    const_specs = [pl.BlockSpec(a.shape, lambda b, c, nd=a.ndim: (0,) * nd) for a in consts]
    return pl.pallas_call(
        _mixer_prompt_kernel,
        grid=(batch // ns, nc),
        in_specs=[pl.BlockSpec((ns, CHUNK, Z_WIDTH), lambda b, c: (b, c, 0))] + const_specs,
        out_specs=[pl.BlockSpec((ns, CHUNK, D_MODEL), lambda b, c: (b, c, 0)),
                   pl.BlockSpec((ns, MLSTM_HEADS, hd, 2 * hd), lambda b, c: (b, 0, 0, 0)),
                   pl.BlockSpec((ns, 1, LANES), lambda b, c: (b, 0, 0))],
        out_shape=[jax.ShapeDtypeStruct((batch, seq, D_MODEL), BF16),
                   jax.ShapeDtypeStruct((batch, MLSTM_HEADS, hd, 2 * hd), F32),
                   jax.ShapeDtypeStruct((batch, 1, LANES), F32)],
        scratch_shapes=[pltpu.VMEM((ns, POOL_PREV_ROWS + CHUNK, POOL_WIDTH), F32)],
        compiler_params=_params("parallel", "arbitrary"),
        name="mixer_prompt",
    )(z3, *consts)


def _mixer_sample_kernel(z_ref, sp_ref, c_ref, n_ref, m_ref, c_other_layers_ref,
                         poolw_ref, pscale_ref, bi_ref, bf_ref, mnorm_ref, gnorm_ref,
                         gw0_ref, gb0_ref, gmean_ref,
                         y_ref, cn_ref, nn_ref, mn_ref, gv_ref, tk_ref):
    del c_other_layers_ref
    nb = SAMPLE_BLOCK
    hd = MLSTM_HEAD_DIM
    lane = lax.broadcasted_iota(jnp.int32, (nb, LANES), 1)
    seq_id = lax.broadcasted_iota(jnp.int32, (nb, LANES), 0)

    pooled = []
    for tile in range(2):
        col0 = tile * LANES
        u_tile = z_ref[:, Z_POOL + col0:Z_POOL + col0 + LANES]
        w_lo, w_hi = POOL_WINDOWS[2 * tile], POOL_WINDOWS[2 * tile + 1]
        acc = u_tile
        sums = {}
        for shift in range(1, w_hi):
            acc = acc + sp_ref[POOL_STATE - shift, :, col0:col0 + LANES]
            if shift + 1 in (w_lo, w_hi):
                sums[shift + 1] = acc
        pooled.append(jnp.where(lane < POOL_GROUP_DIM, sums[w_lo] / float(w_lo), sums[w_hi] / float(w_hi)) - u_tile)
    pooled = jnp.concatenate(pooled, axis=1).astype(BF16)
    y_ref[:, 0:POOL_WIDTH] = (_dot(pooled, poolw_ref[...]) * pscale_ref[...]).astype(BF16)

    vn = _group_rms(z_ref[:, Z_GV:Z_GV + GMLP_WIDTH], gmean_ref[...], gnorm_ref[...])
    gv_ref[...] = vn
    y_g = z_ref[:, Z_GU:Z_GU + GMLP_WIDTH] * (gw0_ref[...] * vn + gb0_ref[...])
    y_ref[:, Y_GMLP:Y_GMLP + GMLP_WIDTH] = y_g.astype(BF16)

    ig, lf = _gate_terms(z_ref, bi_ref, bf_ref)
    m_prev = m_ref[...]
    g = lf + m_prev
    m_t = jnp.maximum(g, ig)
    inter = jnp.exp(g - m_t)
    e_ig = jnp.exp(ig - m_t)
    floor = jnp.exp(-m_t)
    mn_ref[...] = m_t
    tk_ref[...] = jnp.zeros((LANES, LANES), F32)
    for h in range(MLSTM_HEADS):
        tk_ref[nb * h:nb * (h + 1), :] = z_ref[:, Z_K + h * hd:Z_K + (h + 1) * hd] * (hd ** -0.5)
    k_t = jnp.transpose(tk_ref[...])
    for h in range(MLSTM_HEADS):
        c0 = h * hd
        q_h = z_ref[:, Z_Q + c0:Z_Q + c0 + hd]
        k_h = tk_ref[nb * h:nb * (h + 1), :]
        v_h = z_ref[:, Z_V + c0:Z_V + c0 + hd]
        o_h = z_ref[:, Z_O + c0:Z_O + c0 + hd]
        n_h = n_ref[:, c0:c0 + hd]
        inter_b = jnp.broadcast_to(inter[:, h:h + 1], (nb, hd))
        e_b = jnp.broadcast_to(e_ig[:, h:h + 1], (nb, hd))
        floor_b = jnp.broadcast_to(floor[:, h:h + 1], (nb, hd))
        v_w = e_b * v_h
        q_b = q_h.astype(BF16)
        q_c = jnp.zeros((nb, hd), F32)
        for s in range(nb):
            c_sh = c_ref[s, h]
            q_c = jnp.where(seq_id == s, _dot(q_b, c_sh.astype(BF16)), q_c)
            col = nb * h + s
            cn_ref[s, h] = inter_b[s:s + 1, :] * c_sh + k_t[:, col:col + 1] * v_w[s:s + 1, :]
        wts = e_b * jnp.sum(q_h * k_h, axis=1, keepdims=True)
        num = inter_b * q_c + wts * v_h
        den = inter_b * jnp.sum(q_h * n_h, axis=1, keepdims=True) + wts
        hid = num / jnp.maximum(jnp.abs(den), floor_b)
        hid = _rms(hid, mnorm_ref[:, c0:c0 + hd])
        y_ref[:, POOL_WIDTH + c0:POOL_WIDTH + c0 + hd] = (jax.nn.sigmoid(o_h) * hid).astype(BF16)
        nn_ref[:, c0:c0 + hd] = inter_b * n_h + e_b * k_h


def _mixer_sample(z, sp_t, c_all, layer, c_new_all, n_state, m_pad, consts):
    nseq = z.shape[0]
    nb = SAMPLE_BLOCK
    hd = MLSTM_HEAD_DIM
    const_specs = [pl.BlockSpec(a.shape, lambda j, nd=a.ndim: (0,) * nd) for a in consts]
    c_spec = pl.BlockSpec((None, nb, MLSTM_HEADS, hd, hd), lambda j: (layer, j, 0, 0, 0))
    aliases = {} if c_new_all is None else {5: 1}
    return pl.pallas_call(
        _mixer_sample_kernel,
        grid=(nseq // nb,),
        in_specs=[pl.BlockSpec((nb, Z_WIDTH), lambda j: (j, 0)),
                  pl.BlockSpec((POOL_STATE, nb, POOL_WIDTH), lambda j: (0, j, 0)),
                  c_spec,
                  pl.BlockSpec((nb, MLSTM_WIDTH), lambda j: (j, 0)),
                  pl.BlockSpec((nb, LANES), lambda j: (j, 0)),
                  pl.BlockSpec(memory_space=pl.ANY)] + const_specs,
        out_specs=[pl.BlockSpec((nb, D_MODEL), lambda j: (j, 0)),
                   c_spec,
                   pl.BlockSpec((nb, MLSTM_WIDTH), lambda j: (j, 0)),
                   pl.BlockSpec((nb, LANES), lambda j: (j, 0)),
                   pl.BlockSpec((nb, GMLP_WIDTH), lambda j: (j, 0))],
        out_shape=[jax.ShapeDtypeStruct((nseq, D_MODEL), BF16),
                   jax.ShapeDtypeStruct(c_all.shape, F32),
                   jax.ShapeDtypeStruct((nseq, MLSTM_WIDTH), F32),
                   jax.ShapeDtypeStruct((nseq, LANES), F32),
                   jax.ShapeDtypeStruct((nseq, GMLP_WIDTH), F32)],
        scratch_shapes=[pltpu.VMEM((LANES, LANES), F32)],
        input_output_aliases=aliases,
        compiler_params=_params("parallel"),
        name="mixer_sample",
    )(z, sp_t, c_all, n_state, m_pad, c_all if c_new_all is None else c_new_all, *consts)


def _proj_norm_res_kernel(y_ref, x_ref, w_ref, nw_ref, o_ref):
    o_ref[...] = x_ref[...] + _rms(_dot(y_ref[...], w_ref[...]), nw_ref[...])


def _proj_norm_res(y, x, w, nw, tm):
    t = x.shape[0]
    return pl.pallas_call(
        _proj_norm_res_kernel,
        grid=(t // tm,),
        in_specs=[pl.BlockSpec((tm, D_MODEL), lambda i: (i, 0)),
                  pl.BlockSpec((tm, D_MODEL), lambda i: (i, 0)),
                  pl.BlockSpec((D_MODEL, D_MODEL), lambda i: (0, 0)),
                  pl.BlockSpec((1, D_MODEL), lambda i: (0, 0))],
        out_specs=pl.BlockSpec((tm, D_MODEL), lambda i: (i, 0)),
        out_shape=jax.ShapeDtypeStruct((t, D_MODEL), F32),
        compiler_params=_params("parallel"),
        name="out_proj",
    )(y, x, w, nw)


def _dense_layer_kernel(y_ref, x_ref, p_ref, wout_ref, nmix_ref, npre_ref, npost_ref, nple_ref,
                        wg_ref, wu_ref, wd_ref, pg_ref, pp_ref, o_ref):
    x1 = x_ref[...] + _rms(_dot(y_ref[...], wout_ref[...]), nmix_ref[...])
    h = _rms(x1, npre_ref[...]).astype(BF16)
    y = None
    for f0 in range(0, D_FF, FF_CHUNK):
        fw = min(FF_CHUNK, D_FF - f0)
        gate = _dot(h, wg_ref[:, f0:f0 + fw])
        up = _dot(h, wu_ref[:, f0:f0 + fw])
        act = (gate * jax.nn.sigmoid(gate) * up).astype(BF16)
        part = _dot(act, wd_ref[f0:f0 + fw, :])
        y = part if y is None else y + part
    x2 = x1 + _rms(y, npost_ref[...])
    gate = jax.nn.sigmoid(_dot(_rms(x2, nple_ref[...]).astype(BF16), pg_ref[...]))
    o_ref[...] = x2 + gate * _dot(p_ref[...].astype(BF16), pp_ref[...])


def _dense_layer(y, x, p, w_out, nmix, npre, npost, nple, wg, wu, wd, ple_g, ple_p, tm):
    t = x.shape[0]
    return pl.pallas_call(
        _dense_layer_kernel,
        grid=(t // tm,),
        in_specs=[pl.BlockSpec((tm, D_MODEL), lambda i: (i, 0)),
                  pl.BlockSpec((tm, D_MODEL), lambda i: (i, 0)),
                  pl.BlockSpec((tm, PLE_DIM), lambda i: (i, 0)),
                  _resident((D_MODEL, D_MODEL)),
                  _resident((1, D_MODEL)), _resident((1, D_MODEL)), _resident((1, D_MODEL)), _resident((1, D_MODEL)),
                  _resident((D_MODEL, D_FF)), _resident((D_MODEL, D_FF)), _resident((D_FF, D_MODEL)),
                  _resident((D_MODEL, D_MODEL)), _resident((PLE_DIM, D_MODEL))],
        out_specs=pl.BlockSpec((tm, D_MODEL), lambda i: (i, 0)),
        out_shape=jax.ShapeDtypeStruct((t, D_MODEL), F32),
        compiler_params=_params("parallel"),
        name="dense_layer",
    )(y, x, p, w_out, nmix, npre, npost, nple, wg, wu, wd, ple_g, ple_p)


def _router_gates(h, rw_ref, rb_ref):
    shape = (h.shape[0], LANES)
    lane = lax.broadcasted_iota(jnp.int32, shape, 1)
    lane_f = lane.astype(F32)
    logits = jnp.where(lane < N_EXPERTS, _dot(h, rw_ref[...]) + rb_ref[...], -jnp.inf)
    l1 = jnp.max(logits, axis=-1, keepdims=True)
    i1 = jnp.min(jnp.where(logits == l1, lane_f, float(LANES)), axis=-1, keepdims=True)
    rest = jnp.where(lane_f == i1, -jnp.inf, logits)
    l2 = jnp.max(rest, axis=-1, keepdims=True)
    i2 = jnp.min(jnp.where(rest == l2, lane_f, float(LANES)), axis=-1, keepdims=True)
    e2 = jnp.exp(l2 - l1)
    total = 1.0 + e2
    return jnp.where(lane_f == i1, 1.0 / total, 0.0) + jnp.where(lane_f == i2, e2 / total, 0.0)


def _ffn_moe_kernel(x_ref, npre_ref, npost_ref, rw_ref, rb_ref, wg_ref, wu_ref, wd_ref,
                    o_ref, h_ref, acc_ref, gates_ref):
    e = pl.program_id(1)

    @pl.when(e == 0)
    def _():
        h_ref[...] = _rms(x_ref[...], npre_ref[...]).astype(BF16)
        acc_ref[...] = jnp.zeros(acc_ref.shape, F32)
        gates_ref[...] = _router_gates(h_ref[...], rw_ref, rb_ref)

    h = h_ref[...]
    lane = lax.broadcasted_iota(jnp.int32, gates_ref.shape, 1)
    gate_col = jnp.sum(jnp.where(lane == e, gates_ref[...], 0.0), axis=-1, keepdims=True)
    y = None
    for f0, fw in FF_EXPERT_CHUNKS:
        gate = _dot(h, wg_ref[:, f0:f0 + fw].astype(BF16))
        up = _dot(h, wu_ref[:, f0:f0 + fw].astype(BF16))
        act = (gate * jax.nn.sigmoid(gate) * up).astype(BF16)
        part = _dot(act, wd_ref[f0:f0 + fw, :].astype(BF16))
        y = part if y is None else y + part
    acc_ref[...] += gate_col * y

    @pl.when(e == pl.num_programs(1) - 1)
    def _():
        o_ref[...] = x_ref[...] + _rms(acc_ref[...], npost_ref[...])


def _ffn_moe(x, npre, npost, rw, rb, wg, wu, wd, tm):
    t = x.shape[0]
    return pl.pallas_call(
        _ffn_moe_kernel,
        grid=(t // tm, N_EXPERTS),
        in_specs=[pl.BlockSpec((tm, D_MODEL), lambda i, e: (i, 0)),
                  pl.BlockSpec((1, D_MODEL), lambda i, e: (0, 0)),
                  pl.BlockSpec((1, D_MODEL), lambda i, e: (0, 0)),
                  pl.BlockSpec((D_MODEL, LANES), lambda i, e: (0, 0)),
                  pl.BlockSpec((1, LANES), lambda i, e: (0, 0)),
                  pl.BlockSpec((None, D_MODEL, D_FF_EXPERT), lambda i, e: (e, 0, 0)),
                  pl.BlockSpec((None, D_MODEL, D_FF_EXPERT), lambda i, e: (e, 0, 0)),
                  pl.BlockSpec((None, D_FF_EXPERT, D_MODEL), lambda i, e: (e, 0, 0))],
        out_specs=pl.BlockSpec((tm, D_MODEL), lambda i, e: (i, 0)),
        out_shape=jax.ShapeDtypeStruct((t, D_MODEL), F32),
        scratch_shapes=[pltpu.VMEM((tm, D_MODEL), BF16), pltpu.VMEM((tm, D_MODEL), F32),
                        pltpu.VMEM((tm, LANES), F32)],
        compiler_params=_params("parallel", "arbitrary", vmem_limit=VMEM_LIMIT_EXPERT_WEIGHTS),
        name="ffn_moe",
    )(x, npre, npost, rw, rb, wg, wu, wd)


ROUTE_TILE = 256
ROW_ALIGN = 16
ROUTE_SEG = ROUTE_TILE
ROUTE_PACK = 2 * ROUTE_TILE + N_EXPERTS * ROW_ALIGN
ROUTE_W = D_MODEL + 3 * LANES
ROUTE_BLOCK = 512
ROUTE_SEG_SHORT = 96
ROUTE_SHORT_MAX = ROUTE_SEG_SHORT


def _route_region(n_tokens):
    rows = n_tokens + (n_tokens // ROUTE_TILE) * (ROW_ALIGN - 1) + ROUTE_SEG + ROUTE_BLOCK
    return -(-rows // ROUTE_BLOCK) * ROUTE_BLOCK


def _lane_scalar(row, lane, e):
    return jnp.sum(jnp.where(lane == e, row, 0.0)).astype(jnp.int32)


def _route_kernel(y_ref, x_ref, wout_ref, nmix_ref, npre_ref, rw_ref, rb_ref,
                  x1_ref, slot_ref, stats_ref, srt_hbm,
                  stage_ref, runv_ref, run_ref, short_ref, sem, *, region):
    i = pl.program_id(0)
    last = pl.num_programs(0) - 1
    cur = i % 2

    @pl.when(i == 0)
    def _():
        runv_ref[...] = jnp.zeros(runv_ref.shape, F32)
        stage_ref[:, ROUTE_PACK:, :] = jnp.zeros((2, ROUTE_SEG, ROUTE_W), BF16)
        for e in range(N_EXPERTS):
            run_ref[e] = 0

    x1 = x_ref[...] + _rms(_dot(y_ref[...], wout_ref[...]), nmix_ref[...])
    x1_ref[...] = x1
    h = _rms(x1, npre_ref[...]).astype(BF16)
    gates = _router_gates(h, rw_ref, rb_ref)
    sel = gates > 0.0
    ones = jnp.where(sel, 1.0, 0.0)
    trow = lax.broadcasted_iota(jnp.int32, (ROUTE_TILE, ROUTE_TILE), 0)
    tcol = lax.broadcasted_iota(jnp.int32, (ROUTE_TILE, ROUTE_TILE), 1)
    before = jnp.where(tcol < trow, 1.0, 0.0).astype(BF16)
    rank = _dot(before, ones.astype(BF16))
    cnt = jnp.sum(ones, axis=0, keepdims=True)
    cnt_pad = jnp.floor((cnt + (ROW_ALIGN - 1)) * (1.0 / ROW_ALIGN)) * ROW_ALIGN
    lrow = lax.broadcasted_iota(jnp.int32, (LANES, LANES), 0)
    lcol = lax.broadcasted_iota(jnp.int32, (LANES, LANES), 1)
    lower = jnp.where(lrow < lcol, 1.0, 0.0).astype(BF16)
    off = _dot(jnp.broadcast_to(cnt_pad, (SUBLANES, LANES)).astype(BF16), lower)[0:1, :]
    lane = lax.broadcasted_iota(jnp.int32, (1, LANES), 1)
    src_rows = [_lane_scalar(off, lane, e) for e in range(N_EXPERTS)]
    seg_lens = [_lane_scalar(cnt_pad, lane, e) for e in range(N_EXPERTS)]
    short = (jnp.max(cnt_pad) <= ROUTE_SHORT_MAX).astype(jnp.int32)
    slot_ref[...] = jnp.where(sel, rank, -1.0)
    stats_ref[...] = jnp.zeros(stats_ref.shape, F32)
    stats_ref[0:1, :] = runv_ref[...]
    stats_ref[1:2, :] = cnt
    runv_ref[...] = runv_ref[...] + cnt_pad

    pos = jnp.where(sel, rank + off, -1.0)
    pos_t = jnp.concatenate([jnp.transpose(pos[0:LANES, :]), jnp.transpose(pos[LANES:2 * LANES, :])], axis=1)
    pos_a = jnp.max(pos_t, axis=0, keepdims=True)
    pos_b = jnp.max(jnp.where(pos_t == pos_a, -1.0, pos_t), axis=0, keepdims=True)
    prow = lax.broadcasted_iota(jnp.int32, (ROUTE_PACK, ROUTE_TILE), 0).astype(F32)
    perm = jnp.where((prow == pos_a) | (prow == pos_b), 1.0, 0.0).astype(BF16)
    g_hi, g_mid, g_lo = _split3(gates)
    rows = _dot(perm, jnp.concatenate([h, g_hi, g_mid, g_lo], axis=1))
    stage_ref[cur, 0:ROUTE_PACK, :] = rows.astype(BF16)

    def segment_copy(e, src_row, dst_row, slot, rows=ROUTE_SEG):
        return pltpu.make_async_copy(
            stage_ref.at[slot, pl.ds(pl.multiple_of(src_row, ROW_ALIGN), rows), :],
            srt_hbm.at[pl.ds(pl.multiple_of(dst_row, ROW_ALIGN), rows), :],
            sem.at[e])

    def for_each_segment(is_short, action):
        for rows, flag in ((ROUTE_SEG_SHORT, 1), (ROUTE_SEG, 0)):
            @pl.when(is_short == flag)
            def _():
                for e in range(N_EXPERTS):
                    action(e, rows)

    @pl.when(i > 0)
    def _():
        for_each_segment(short_ref[0], lambda e, rows: segment_copy(e, 0, 0, 1 - cur, rows).wait())

    dst_rows = [e * region + run_ref[e] for e in range(N_EXPERTS)]
    for_each_segment(short, lambda e, rows: segment_copy(e, src_rows[e], dst_rows[e], cur, rows).start())
    for e in range(N_EXPERTS):
        run_ref[e] = run_ref[e] + seg_lens[e]
    short_ref[0] = short

    @pl.when(i == last)
    def _():
        for_each_segment(short, lambda e, rows: segment_copy(e, 0, 0, cur, rows).wait())
        stage_ref[1 - cur, 0:ROUTE_SEG, :] = jnp.zeros((ROUTE_SEG, ROUTE_W), BF16)
        for part in range(ROUTE_BLOCK // ROUTE_SEG):
            for e in range(N_EXPERTS):
                segment_copy(e, 0, e * region + run_ref[e] + part * ROUTE_SEG, 1 - cur).start()
            for e in range(N_EXPERTS):
                segment_copy(e, 0, 0, 1 - cur).wait()


def _route(y, x, w_out, nmix, npre, rw, rb):
    t = x.shape[0]
    nt = t // ROUTE_TILE
    region = _route_region(t)
    return pl.pallas_call(
        functools.partial(_route_kernel, region=region),
        grid=(nt,),
        in_specs=[pl.BlockSpec((ROUTE_TILE, D_MODEL), lambda i: (i, 0)),
                  pl.BlockSpec((ROUTE_TILE, D_MODEL), lambda i: (i, 0)),
                  pl.BlockSpec((D_MODEL, D_MODEL), lambda i: (0, 0)),
                  pl.BlockSpec((1, D_MODEL), lambda i: (0, 0)),
                  pl.BlockSpec((1, D_MODEL), lambda i: (0, 0)),
                  pl.BlockSpec((D_MODEL, LANES), lambda i: (0, 0)),
                  pl.BlockSpec((1, LANES), lambda i: (0, 0))],
        out_specs=[pl.BlockSpec((ROUTE_TILE, D_MODEL), lambda i: (i, 0)),
                   pl.BlockSpec((ROUTE_TILE, LANES), lambda i: (i, 0)),
                   pl.BlockSpec((None, SUBLANES, LANES), lambda i: (i, 0, 0)),
                   pl.BlockSpec(memory_space=pl.ANY)],
        out_shape=[jax.ShapeDtypeStruct((t, D_MODEL), F32),
                   jax.ShapeDtypeStruct((t, LANES), F32),
                   jax.ShapeDtypeStruct((nt, SUBLANES, LANES), F32),
                   jax.ShapeDtypeStruct((N_EXPERTS * region, ROUTE_W), BF16)],
        scratch_shapes=[pltpu.VMEM((2, ROUTE_PACK + ROUTE_SEG, ROUTE_W), BF16),
                        pltpu.VMEM((1, LANES), F32),
                        pltpu.SMEM((N_EXPERTS,), jnp.int32),
                        pltpu.SMEM((1,), jnp.int32),
                        pltpu.SemaphoreType.DMA((N_EXPERTS,))],
        compiler_params=_params("arbitrary"),
        name="moe_route",
    )(y, x, w_out, nmix, npre, rw, rb)


def _experts_kernel(blk_row_ref, blk_e_ref, n_used_ref, srt_ref, wg_ref, wu_ref, wd_ref, yhi_ref, ylo_ref):
    k = pl.program_id(0)

    @pl.when(k < n_used_ref[0])
    def _():
        h = srt_ref[:, 0:D_MODEL]
        gate3 = (srt_ref[:, D_MODEL:D_MODEL + LANES].astype(F32)
                 + srt_ref[:, D_MODEL + LANES:D_MODEL + 2 * LANES].astype(F32)
                 + srt_ref[:, D_MODEL + 2 * LANES:D_MODEL + 3 * LANES].astype(F32))
        lane = lax.broadcasted_iota(jnp.int32, gate3.shape, 1)
        gate_col = jnp.sum(jnp.where(lane == blk_e_ref[k], gate3, 0.0), axis=-1, keepdims=True)
        y = None
        for f0, fw in FF_EXPERT_CHUNKS:
            gate = _dot(h, wg_ref[:, f0:f0 + fw].astype(BF16))
            up = _dot(h, wu_ref[:, f0:f0 + fw].astype(BF16))
            act = (gate * jax.nn.sigmoid(gate) * up).astype(BF16)
            part = _dot(act, wd_ref[f0:f0 + fw, :].astype(BF16))
            y = part if y is None else y + part
        y = gate_col * y
        hi = y.astype(BF16)
        yhi_ref[...] = hi
        ylo_ref[...] = (y - hi.astype(F32)).astype(BF16)


def _experts(srt, blk_row, blk_e, n_used, wg, wu, wd, n_blocks):
    rows = srt.shape[0]
    grid_spec = pltpu.PrefetchScalarGridSpec(
        num_scalar_prefetch=3,
        grid=(n_blocks,),
        in_specs=[pl.BlockSpec((ROUTE_BLOCK, ROUTE_W), lambda k, br, be, nu: (br[k], 0)),
                  pl.BlockSpec((None, D_MODEL, D_FF_EXPERT), lambda k, br, be, nu: (be[k], 0, 0)),
                  pl.BlockSpec((None, D_MODEL, D_FF_EXPERT), lambda k, br, be, nu: (be[k], 0, 0)),
                  pl.BlockSpec((None, D_FF_EXPERT, D_MODEL), lambda k, br, be, nu: (be[k], 0, 0))],
        out_specs=[pl.BlockSpec((ROUTE_BLOCK, D_MODEL), lambda k, br, be, nu: (br[k], 0)),
                   pl.BlockSpec((ROUTE_BLOCK, D_MODEL), lambda k, br, be, nu: (br[k], 0))])
    return pl.pallas_call(
        _experts_kernel,
        grid_spec=grid_spec,
        out_shape=[jax.ShapeDtypeStruct((rows, D_MODEL), BF16), jax.ShapeDtypeStruct((rows, D_MODEL), BF16)],
        compiler_params=_params("arbitrary", vmem_limit=VMEM_LIMIT_EXPERT_WEIGHTS),
        name="moe_experts",
    )(blk_row, blk_e, n_used, srt, wg, wu, wd)


def _combine_kernel(src_row_ref, short_ref, x_ref, slot_ref, shift_ref, p_ref, npost_ref, nple_ref,
                    wg_ref, wp_ref, yhi_hbm, ylo_hbm, o_ref, seg_hi_ref, seg_lo_ref, y_ref, sem):
    i = pl.program_id(0)
    nt = pl.num_programs(0)
    cur = i % 2

    def segment_copies(tile, slot, e, rows):
        src = pl.ds(pl.multiple_of(src_row_ref[tile * N_EXPERTS + e], ROW_ALIGN), rows)
        dst = pl.ds(e * rows, rows)
        return (pltpu.make_async_copy(yhi_hbm.at[src, :], seg_hi_ref.at[slot, dst, :], sem.at[slot, 0, e]),
                pltpu.make_async_copy(ylo_hbm.at[src, :], seg_lo_ref.at[slot, dst, :], sem.at[slot, 1, e]))

    def for_each_segment(tile, slot, action):
        for rows, is_short in ((ROUTE_SEG_SHORT, 1), (ROUTE_SEG, 0)):
            @pl.when(short_ref[tile] == is_short)
            def _():
                for e in range(N_EXPERTS):
                    for c in segment_copies(tile, slot, e, rows):
                        action(c)

    @pl.when(i == 0)
    def _():
        for_each_segment(0, 0, lambda c: c.start())

    @pl.when(i + 1 < nt)
    def _():
        for_each_segment(i + 1, 1 - cur, lambda c: c.start())

    for_each_segment(i, cur, lambda c: c.wait())

    slot = slot_ref[...]
    where = jnp.where(slot >= 0.0, slot + shift_ref[...], -1.0)

    def gather(rows):
        seg_lane = lax.broadcasted_iota(jnp.int32, (ROUTE_TILE, rows), 1).astype(F32)
        perm = jnp.concatenate([jnp.where(where[:, e:e + 1] == seg_lane, 1.0, 0.0).astype(BF16)
                                for e in range(N_EXPERTS)], axis=1)
        k = N_EXPERTS * rows
        y_ref[...] = _dot(perm, seg_hi_ref[cur, 0:k, :]) + _dot(perm, seg_lo_ref[cur, 0:k, :])

    @pl.when(short_ref[i] == 1)
    def _():
        gather(ROUTE_SEG_SHORT)

    @pl.when(short_ref[i] == 0)
    def _():
        gather(ROUTE_SEG)

    x = x_ref[...] + _rms(y_ref[...], npost_ref[...])
    gate = jax.nn.sigmoid(_dot(_rms(x, nple_ref[...]).astype(BF16), wg_ref[...]))
    o_ref[...] = x + gate * _dot(p_ref[...].astype(BF16), wp_ref[...])


def _combine(src_row, short, x, slot, shift, p, npost, nple, wg, wp, yhi, ylo):
    t = x.shape[0]
    grid_spec = pltpu.PrefetchScalarGridSpec(
        num_scalar_prefetch=2,
        grid=(t // ROUTE_TILE,),
        in_specs=[pl.BlockSpec((ROUTE_TILE, D_MODEL), lambda i, *_: (i, 0)),
                  pl.BlockSpec((ROUTE_TILE, LANES), lambda i, *_: (i, 0)),
                  pl.BlockSpec((None, 1, LANES), lambda i, *_: (i, 0, 0)),
                  pl.BlockSpec((ROUTE_TILE, PLE_DIM), lambda i, *_: (i, 0)),
                  pl.BlockSpec((1, D_MODEL), lambda i, *_: (0, 0)),
                  pl.BlockSpec((1, D_MODEL), lambda i, *_: (0, 0)),
                  pl.BlockSpec((D_MODEL, D_MODEL), lambda i, *_: (0, 0)),
                  pl.BlockSpec((PLE_DIM, D_MODEL), lambda i, *_: (0, 0)),
                  pl.BlockSpec(memory_space=pl.ANY),
                  pl.BlockSpec(memory_space=pl.ANY)],
        out_specs=pl.BlockSpec((ROUTE_TILE, D_MODEL), lambda i, *_: (i, 0)),
        scratch_shapes=[pltpu.VMEM((2, N_EXPERTS * ROUTE_SEG, D_MODEL), BF16),
                        pltpu.VMEM((2, N_EXPERTS * ROUTE_SEG, D_MODEL), BF16),
                        pltpu.VMEM((ROUTE_TILE, D_MODEL), F32),
                        pltpu.SemaphoreType.DMA((2, 2, N_EXPERTS))])
    return pl.pallas_call(
        _combine_kernel,
        grid_spec=grid_spec,
        out_shape=jax.ShapeDtypeStruct((t, D_MODEL), F32),
        compiler_params=_params("arbitrary"),
        name="moe_combine_ple",
    )(src_row, short, x, slot, shift, p, npost, nple, wg, wp, yhi, ylo)


def _moe_layer_routed(y, x, p, w_out, nmix, npre, npost, nple, rw, rb, wg, wu, wd, ple_g, ple_p):
    t = x.shape[0]
    nt = t // ROUTE_TILE
    region = _route_region(t)
    x, slot, stats, srt = _route(y, x, w_out, nmix, npre, rw, rb)
    base = stats[:, 0, 0:N_EXPERTS].astype(jnp.int32)
    cnt = stats[:, 1, 0:N_EXPERTS].astype(jnp.int32)
    cnt_pad = (cnt + (ROW_ALIGN - 1)) // ROW_ALIGN * ROW_ALIGN
    total = base[-1] + cnt_pad[-1]
    nblk = (total + (ROUTE_BLOCK - 1)) // ROUTE_BLOCK
    cum = jnp.cumsum(nblk)
    n_used = cum[-1]
    max_rows = 2 * t + nt * N_EXPERTS * (ROW_ALIGN - 1)
    n_blocks = max_rows // ROUTE_BLOCK + N_EXPERTS
    kk = jnp.minimum(jnp.arange(n_blocks, dtype=jnp.int32), n_used - 1)
    blk_e = jnp.sum(kk[:, None] >= cum[None, :], axis=1).astype(jnp.int32)
    blk_row = blk_e * (region // ROUTE_BLOCK) + kk - (cum - nblk)[blk_e]
    yhi, ylo = _experts(srt, blk_row.astype(jnp.int32), blk_e, n_used.reshape(1).astype(jnp.int32),
                        wg, wu, wd, n_blocks)
    short = jnp.all(cnt_pad <= ROUTE_SHORT_MAX, axis=1)
    seg_rows = jnp.where(short, ROUTE_SEG_SHORT, ROUTE_SEG)[:, None]
    start = jnp.maximum(jnp.minimum(base, nblk[None, :] * ROUTE_BLOCK - seg_rows), 0)
    src_row = jnp.arange(N_EXPERTS, dtype=jnp.int32)[None, :] * region + start
    first_used = jnp.argmax(nblk > 0).astype(jnp.int32)
    src_row = jnp.where(cnt > 0, src_row, first_used * region).reshape(-1)
    shift = _pad_lanes((base - start).astype(F32)).reshape(nt, 1, LANES)
    return _combine(src_row.astype(jnp.int32), short.astype(jnp.int32), x, slot, shift, p, npost, nple,
                    ple_g, ple_p, yhi, ylo)


def _ple_kernel(x_ref, p_ref, nw_ref, wg_ref, wp_ref, o_ref):
    x = x_ref[...]
    gate = jax.nn.sigmoid(_dot(_rms(x, nw_ref[...]).astype(BF16), wg_ref[...]))
    o_ref[...] = x + gate * _dot(p_ref[...].astype(BF16), wp_ref[...])


def _ple(x, p, nw, wg, wp, tm):
    t = x.shape[0]
    return pl.pallas_call(
        _ple_kernel,
        grid=(t // tm,),
        in_specs=[pl.BlockSpec((tm, D_MODEL), lambda i: (i, 0)),
                  pl.BlockSpec((tm, PLE_DIM), lambda i: (i, 0)),
                  pl.BlockSpec((1, D_MODEL), lambda i: (0, 0)),
                  pl.BlockSpec((D_MODEL, D_MODEL), lambda i: (0, 0)),
                  pl.BlockSpec((PLE_DIM, D_MODEL), lambda i: (0, 0))],
        out_specs=pl.BlockSpec((tm, D_MODEL), lambda i: (i, 0)),
        out_shape=jax.ShapeDtypeStruct((t, D_MODEL), F32),
        compiler_params=_params("parallel"),
        name="ple",
    )(x, p, nw, wg, wp)


def _pad_lanes(a, width=LANES):
    return jnp.pad(a, [(0, 0)] * (a.ndim - 1) + [(0, width - a.shape[-1])])


def _block_diag(blocks):
    g, d, _ = blocks.shape
    out = jnp.zeros((g * d, g * d), blocks.dtype)
    for i in range(g):
        out = out.at[i * d:(i + 1) * d, i * d:(i + 1) * d].set(blocks[i])
    return out


def _row(a):
    return a.reshape(1, -1).astype(F32)


def kernel(x_prompt, x_sample, state_pool, state_mlstm_C, state_mlstm_n, state_mlstm_m, p_prompt, p_sample,
           norm_mix_pre, norm_mix_post, norm_ffn_pre, norm_ffn_post, norm_ple, w_in, pool_w, pool_scale,
           mlstm_b_i, mlstm_b_f, mlstm_norm_w, gmlp_norm_w, gmlp_ws, gmlp_bs, w_out,
           ffn_w_gate, ffn_w_up, ffn_w_down, moe_router_w, moe_router_b, moe_w_gate, moe_w_up, moe_w_down,
           ple_w_gate, ple_w_proj):
    batch, seq, _ = x_prompt.shape
    nseq = x_sample.shape[0]
    xp = x_prompt.reshape(batch * seq, D_MODEL)
    xs = x_sample.reshape(nseq, D_MODEL)
    gmean = _block_diag(jnp.full((GMLP_GROUPS, GMLP_GROUP_DIM, GMLP_GROUP_DIM), 1.0 / GMLP_GROUP_DIM, BF16))

    w_in_t = jnp.swapaxes(w_in, 1, 2)

    pools_p, cs_p, ns_p, ms_p = [], [], [], []
    pools_s, ns_s, ms_s, gvs_s = [], [], [], []
    c_new_s = None
    for i in range(DEPTH):
        w_out_b = w_out[i].astype(BF16)
        poolw = _block_diag(pool_w[i]).astype(BF16)
        shared = [poolw, _row(pool_scale[i]), _pad_lanes(_row(mlstm_b_i[i])), _pad_lanes(_row(mlstm_b_f[i])),
                  _row(mlstm_norm_w[i]), _row(gmlp_norm_w[i])]
        gbs_full = jnp.repeat(gmlp_bs[i].T, GMLP_GROUP_DIM, axis=1)
        consts_p = shared + [gmlp_ws[i], gbs_full, gmean]
        gw0 = jnp.repeat(gmlp_ws[i][:, 0, 0], GMLP_GROUP_DIM).reshape(1, GMLP_WIDTH)
        consts_s = shared + [gw0, gbs_full[0:1, :], gmean]
        ple_g = ple_w_gate[i].astype(BF16)
        ple_p = ple_w_proj[i].astype(BF16)
        j = i // 2
        if i % 2 == 0:
            ffn_g, ffn_u, ffn_d = (ffn_w_gate[j].astype(BF16), ffn_w_up[j].astype(BF16),
                                   ffn_w_down[j].astype(BF16))
        else:
            rw = _pad_lanes(moe_router_w[j]).astype(BF16)
            rb = _pad_lanes(_row(moe_router_b[j]))
            moe_g, moe_u, moe_d = moe_w_gate[j], moe_w_up[j], moe_w_down[j]

        z = _norm_matmul(xp, _row(norm_mix_pre[i]), w_in_t, i, TM_PROMPT)
        y, cn_new, m_new = _mixer_prompt(z, consts_p, batch, seq)
        pools_p.append(z.reshape(batch, seq, Z_WIDTH)[:, seq - POOL_STATE:, 0:POOL_WIDTH])
        cs_p.append(cn_new[..., 0:MLSTM_HEAD_DIM])
        ns_p.append(cn_new[..., MLSTM_HEAD_DIM])
        ms_p.append(m_new[:, 0, 0:MLSTM_HEADS])
        y = y.reshape(batch * seq, D_MODEL)
        pp = p_prompt[i].reshape(batch * seq, PLE_DIM)
        norms = (_row(norm_mix_post[i]), _row(norm_ffn_pre[i]), _row(norm_ffn_post[i]), _row(norm_ple[i]))
        if i % 2 == 0:
            xp = _dense_layer(y, xp, pp, w_out_b, *norms, ffn_g, ffn_u, ffn_d, ple_g, ple_p, TM_PROMPT)
        else:
            xp = _moe_layer_routed(y, xp, pp, w_out_b, *norms, rw, rb, moe_g, moe_u, moe_d, ple_g, ple_p)

        z = _norm_matmul(xs, _row(norm_mix_pre[i]), w_in_t, i, nseq)
        sp_t = jnp.transpose(state_pool[i], (1, 0, 2))
        y, c_new_s, n_new, m_new, gv = _mixer_sample(z, sp_t, state_mlstm_C, i, c_new_s,
                                                     state_mlstm_n[i].reshape(nseq, MLSTM_WIDTH),
                                                     _pad_lanes(state_mlstm_m[i]), consts_s)
        pools_s.append(jnp.concatenate([state_pool[i][:, 1:], z[:, None, 0:POOL_WIDTH]], axis=1))
        ns_s.append(n_new.reshape(nseq, MLSTM_HEADS, MLSTM_HEAD_DIM))
        ms_s.append(m_new[:, 0:MLSTM_HEADS])
        gvs_s.append(gv[:, None, :])
        ps = p_sample[i].reshape(nseq, PLE_DIM)
        if i % 2 == 0:
            xs = _dense_layer(y, xs, ps, w_out_b, *norms, ffn_g, ffn_u, ffn_d, ple_g, ple_p, nseq)
        else:
            xs = _proj_norm_res(y, xs, w_out_b, norms[0], nseq)
            xs = _ffn_moe(xs, norms[1], norms[2], rw, rb, moe_g, moe_u, moe_d, nseq)
            xs = _ple(xs, ps, norms[3], ple_g, ple_p, nseq)

    return (xp.reshape(batch, seq, D_MODEL), xs.reshape(nseq, 1, D_MODEL),
            jnp.stack(pools_p), jnp.stack(cs_p), jnp.stack(ns_p), jnp.stack(ms_p),
            jnp.stack(pools_s), c_new_s, jnp.stack(ns_s), jnp.stack(ms_s), jnp.stack(gvs_s))
```
